```python
import math
import jax
import jax.numpy as jnp
from jax import lax
import numpy as np

D_MODEL = 1024
BATCH = 4
SEQ = 4096
DEPTH = 2

GRID_W = 64
CTX_LEN = 256

N_BRANCH = 4
BRANCH_W = 256

S5_WIDTH = BRANCH_W
S5_GROUP = 16
S5_GROUPS = S5_WIDTH // S5_GROUP
S5_STATE = 64
S5_DT_MIN = 1e-3
S5_DT_MAX = 1e-1

HY_WIDTH = BRANCH_W
HY_ORDER = 2
HY_BANDS = 16
HY_EMB = 2 * HY_BANDS + 1
HY_FFN = 64
HY_SHORT = 3
HY_DECAY_TARGET = 1e-2
HY_FAST_DECAY_PCT = 0.3
HY_SLOW_DECAY_PCT = 1.5
HY_MIN_DECAY = math.log(HY_DECAY_TARGET) / HY_SLOW_DECAY_PCT
HY_MAX_DECAY = math.log(HY_DECAY_TARGET) / HY_FAST_DECAY_PCT

NA_HEADS = 4
NA_HEAD_DIM = BRANCH_W // NA_HEADS
NA_WIN_H = 8
NA_WIN_W = 16

SW_HEADS = 4
SW_KV_HEADS = 2
SW_HEAD_DIM = BRANCH_W // SW_HEADS
SW_WINDOW = 128
SW_BLOCK = 128

MLP_HIDDEN = 4 * D_MODEL
ROPE_BASE = 10000.0
LN_EPS = 1e-6
NEG_INF = -1e30
DEEPNORM_ALPHA = (2 * DEPTH) ** 0.25
DEEPNORM_BETA = (8 * DEPTH) ** -0.25

IN_WIDTHS = (S5_WIDTH, 3 * HY_WIDTH, 3 * BRANCH_W, (SW_HEADS + 2 * SW_KV_HEADS) * SW_HEAD_DIM, N_BRANCH * D_MODEL)
IN_TOTAL = sum(IN_WIDTHS)
IN_CUTS = tuple(int(v) for v in np.cumsum(IN_WIDTHS)[:-1])

F32 = jnp.float32

kernel_name = 'hybrid_s5_hyena_natten_swa_dit_trunk'


def _layernorm(x, gain=None, bias=None):
    xf = x.astype(F32)
    mu = jnp.mean(xf, axis=-1, keepdims=True)
    var = jnp.mean(jnp.square(xf - mu), axis=-1, keepdims=True)
    y = (xf - mu) * lax.rsqrt(var + LN_EPS)
    if gain is not None:
        y = y * gain.astype(F32) + bias.astype(F32)
    return y.astype(x.dtype)


def _modulate(x, shift, scale):
    return x * (1.0 + scale) + shift


def _softmax(s):
    return jax.nn.softmax(s.astype(F32), axis=-1)


def _axial_rope(n_tokens, head_dim):
    t = jnp.arange(n_tokens, dtype=jnp.int32)
    row = (t // GRID_W).astype(F32)
    col = (t % GRID_W).astype(F32)
    half = head_dim // 2
    inv = ROPE_BASE ** (-(jnp.arange(0, half, 2, dtype=F32) / half))
    ang = jnp.concatenate([row[:, None] * inv, col[:, None] * inv], axis=-1)
    return jnp.cos(ang), jnp.sin(ang)


def _apply_rope(x, cos, sin):
    xf = x.astype(F32)
    x1, x2 = jnp.split(xf, 2, axis=-1)
    c = cos[None, :, None, :]
    s = sin[None, :, None, :]
    return jnp.concatenate([x1 * c - x2 * s, x1 * s + x2 * c], axis=-1).astype(x.dtype)


def _s5_discretise(lam_re, lam_im, log_dt, b_re, b_im):
    lam_re = lam_re.astype(F32)
    lam_im = lam_im.astype(F32)
    dt = jnp.exp(log_dt.astype(F32))[:, None]
    mag = jnp.exp(lam_re * dt)
    a_re = mag * jnp.cos(lam_im * dt)
    a_im = mag * jnp.sin(lam_im * dt)
    den = lam_re ** 2 + lam_im ** 2
    f_re = ((a_re - 1.0) * lam_re + a_im * lam_im) / den
    f_im = (a_im * lam_re - (a_re - 1.0) * lam_im) / den
    b_re = b_re.astype(F32)
    b_im = b_im.astype(F32)
    bb_re = f_re[..., None] * b_re - f_im[..., None] * b_im
    bb_im = f_re[..., None] * b_im + f_im[..., None] * b_re
    return a_re, a_im, bb_re, bb_im


def _diag_scan(a_re, a_im, bu_re, bu_im, h0_re, h0_im, reverse):
    if reverse:
        bu_re = jnp.flip(bu_re, axis=1)
        bu_im = jnp.flip(bu_im, axis=1)
    bu_re = bu_re.at[:, 0].add(a_re * h0_re - a_im * h0_im)
    bu_im = bu_im.at[:, 0].add(a_re * h0_im + a_im * h0_re)
    ar = jnp.broadcast_to(a_re, bu_re.shape)
    ai = jnp.broadcast_to(a_im, bu_im.shape)

    def combine(e1, e2):
        a1r, a1i, b1r, b1i = e1
        a2r, a2i, b2r, b2i = e2
        return (a2r * a1r - a2i * a1i, a2r * a1i + a2i * a1r,
                a2r * b1r - a2i * b1i + b2r, a2r * b1i + a2i * b1r + b2i)

    _, _, h_re, h_im = lax.associative_scan(combine, (ar, ai, bu_re, bu_im), axis=1)
    if reverse:
        h_re = jnp.flip(h_re, axis=1)
        h_im = jnp.flip(h_im, axis=1)
    return h_re, h_im


def _s5_readout(h_re, h_im, c_re, c_im):
    b_, l_ = h_re.shape[:2]
    y = jnp.einsum('blgp,gnp->blgn', h_re, c_re) - jnp.einsum('blgp,gnp->blgn', h_im, c_im)
    return y.reshape(b_, l_, S5_WIDTH)


def _s5_glu(y, w_glu):
    g = jax.nn.gelu(y)
    return g * jax.nn.sigmoid(g @ w_glu.astype(F32))


def _s5_branch(u_lat, u_ctx, lam_re, lam_im, log_dt, b_re, b_im, c_re, c_im, d, w_glu, need_ctx_out):
    b_, n_lat, _ = u_lat.shape
    n_ctx = u_ctx.shape[1]
    ul = u_lat.astype(F32).reshape(b_, n_lat, S5_GROUPS, S5_GROUP)
    uc = u_ctx.astype(F32).reshape(b_, n_ctx, S5_GROUPS, S5_GROUP)
    zero = jnp.zeros((b_, S5_GROUPS, S5_STATE), F32)
    d = d.astype(F32)
    y_lat = d * u_lat.astype(F32)
    y_ctx = d * u_ctx.astype(F32) if need_ctx_out else None
    for direction in range(2):
        reverse = direction == 1
        a_re, a_im, bb_re, bb_im = _s5_discretise(lam_re[direction], lam_im[direction], log_dt[direction],
                                                  b_re[direction], b_im[direction])
        cr = c_re[direction].astype(F32)
        ci = c_im[direction].astype(F32)
        hc_re, hc_im = _diag_scan(a_re, a_im, jnp.einsum('blgn,gpn->blgp', uc, bb_re),
                                  jnp.einsum('blgn,gpn->blgp', uc, bb_im), zero, zero, reverse)
        end = 0 if reverse else n_ctx - 1
        hl_re, hl_im = _diag_scan(a_re, a_im, jnp.einsum('blgn,gpn->blgp', ul, bb_re),
                                  jnp.einsum('blgn,gpn->blgp', ul, bb_im), hc_re[:, end], hc_im[:, end], reverse)
        y_lat = y_lat + _s5_readout(hl_re, hl_im, cr, ci)
        if need_ctx_out:
            y_ctx = y_ctx + _s5_readout(hc_re, hc_im, cr, ci)
    out_lat = _s5_glu(y_lat, w_glu).astype(u_lat.dtype)
    out_ctx = _s5_glu(y_ctx, w_glu).astype(u_ctx.dtype) if need_ctx_out else None
    return out_lat, out_ctx


def _depthwise_conv(x, w, b):
    y = lax.conv_general_dilated(x, w.astype(x.dtype), window_strides=(1,), padding='SAME',
                                 dimension_numbers=('NWC', 'WIO', 'NWC'), feature_group_count=x.shape[-1])
    return y + b.astype(x.dtype)


def _hyena_filters(n_tokens, freq, w1, b1, w2, b2, w3):
    pos = jnp.arange(n_tokens, dtype=F32)[:, None]
    t = pos / max(n_tokens - 1, 1)
    bands = jnp.linspace(1e-4, HY_BANDS - 1, HY_BANDS, dtype=F32)[None]
    ang = 2.0 * math.pi * bands * pos / n_tokens
    feats = jnp.concatenate([t, jnp.cos(ang), -jnp.sin(ang)], axis=-1)
    freq = freq.astype(F32)
    h = jnp.sin(freq[0] * (feats @ w1.astype(F32) + b1.astype(F32)))
    h = jnp.sin(freq[1] * (h @ w2.astype(F32) + b2.astype(F32)))
    h = (h @ w3.astype(F32)).reshape(n_tokens, 2, HY_ORDER, HY_WIDTH)
    deltas = jnp.abs(jnp.linspace(HY_MIN_DECAY, HY_MAX_DECAY, HY_WIDTH, dtype=F32))
    return h * jnp.exp(-t[:, :, None, None] * deltas)


def _two_sided_fftconv(u, k_fwd, k_bwd):
    n = u.shape[1]
    k = jnp.concatenate([k_fwd, jnp.zeros_like(k_fwd[:1]), jnp.flip(k_bwd[1:], axis=0)], axis=0)
    k_f = jnp.fft.rfft(k, axis=0)
    u_f = jnp.fft.rfft(u.astype(F32), n=2 * n, axis=1)
    y = jnp.fft.irfft(u_f * k_f[None], n=2 * n, axis=1)[:, :n]
    return y.astype(u.dtype)


def _hyena_seq(z, conv_w, conv_b, freq, w1, b1, w2, b2, w3, bias):
    n = z.shape[1]
    z = _depthwise_conv(z, conv_w, conv_b)
    v, x1, x2 = jnp.split(z, 3, axis=-1)
    filt = _hyena_filters(n, freq, w1, b1, w2, b2, w3)
    for o, gate in enumerate((x1, x2)):
        v = gate * (_two_sided_fftconv(v, filt[:, 0, o], filt[:, 1, o]) + bias[o].astype(v.dtype) * v)
    return v


def _hyena_branch(z_lat, z_ctx, conv_w, conv_b, freq, w1, b1, w2, b2, w3, bias, need_ctx_out):
    out_lat = _hyena_seq(z_lat, conv_w, conv_b, freq, w1, b1, w2, b2, w3, bias)
    out_ctx = _hyena_seq(z_ctx, conv_w, conv_b, freq, w1, b1, w2, b2, w3, bias) if need_ctx_out else None
    return out_lat, out_ctx


def _ctx_attention(q, k, v, sink):
    b_, n, n_heads, dh = q.shape
    n_kv = k.shape[2]
    qg = q.reshape(b_, n, n_kv, n_heads // n_kv, dh)
    s = jnp.einsum('bqkgd,bckd->bkgqc', qg, k).astype(F32) * dh ** -0.5
    if sink is not None:
        s_sink = jnp.broadcast_to(sink.astype(F32).reshape(n_kv, n_heads // n_kv)[None, :, :, None, None],
                                  s.shape[:-1] + (1,))
        s = jnp.concatenate([s, s_sink], axis=-1)
    p = _softmax(s)[..., :n].astype(v.dtype)
    o = jnp.einsum('bkgqc,bckd->bqkgd', p, v)
    return o.reshape(b_, n, n_heads * dh)


def _neighbourhood_attention(q, k, v, k_ctx, v_ctx, rpb):
    b_, n, n_heads, dh = q.shape
    rows = n // GRID_W
    kh = min(NA_WIN_H, rows)
    scale = dh ** -0.5
    qg = q.reshape(b_, rows, GRID_W, n_heads, dh)
    kg = k.reshape(b_, rows, GRID_W, n_heads, dh)
    vg = v.reshape(b_, rows, GRID_W, n_heads, dh)
    r = jnp.arange(rows)
    row_idx = jnp.clip(r - kh // 2, 0, rows - kh)[:, None] + jnp.arange(kh)[None]
    kb = kg[:, row_idx]
    vb = vg[:, row_idx]
    col = jnp.arange(GRID_W)
    col_start = jnp.clip(col - NA_WIN_W // 2, 0, GRID_W - NA_WIN_W)
    col_ok = (col[None] >= col_start[:, None]) & (col[None] < col_start[:, None] + NA_WIN_W)
    off_r = row_idx - r[:, None] + (NA_WIN_H - 1)
    off_c = jnp.clip(col[None] - col[:, None], -(NA_WIN_W - 1), NA_WIN_W - 1) + (NA_WIN_W - 1)
    bias = rpb[:, off_r[:, None, :, None], off_c[None, :, None, :]]
    s_lat = jnp.einsum('brqhd,brkwhd->bhrqkw', qg, kb).astype(F32) * scale + bias.astype(F32)[None]
    s_lat = jnp.where(col_ok[:, None, :], s_lat, NEG_INF)
    n_lat = kh * GRID_W
    s_ctx = jnp.einsum('brqhd,bchd->bhrqc', qg, k_ctx).astype(F32) * scale
    s = jnp.concatenate([s_lat.reshape(b_, n_heads, rows, GRID_W, n_lat), s_ctx], axis=-1)
    p = _softmax(s)
    p_lat = p[..., :n_lat].reshape(b_, n_heads, rows, GRID_W, kh, GRID_W).astype(v.dtype)
    p_ctx = p[..., n_lat:].astype(v.dtype)
    o = jnp.einsum('bhrqkw,brkwhd->brqhd', p_lat, vb) + jnp.einsum('bhrqc,bchd->brqhd', p_ctx, v_ctx)
    return o.reshape(b_, n, n_heads * dh)


def _na_branch(z_lat, z_ctx, rpb, need_ctx_out):
    b_, n, _ = z_lat.shape
    n_ctx = z_ctx.shape[1]
    q, k, v = [t.reshape(b_, n, NA_HEADS, NA_HEAD_DIM) for t in jnp.split(z_lat, 3, axis=-1)]
    qc, kc, vc = [t.reshape(b_, n_ctx, NA_HEADS, NA_HEAD_DIM) for t in jnp.split(z_ctx, 3, axis=-1)]
    out_lat = _neighbourhood_attention(q, k, v, kc, vc, rpb)
    out_ctx = _ctx_attention(qc, kc, vc, None) if need_ctx_out else None
    return out_lat, out_ctx


def _window_attention(q, k, v, k_ctx, v_ctx, sink):
    b_, n, n_heads, dh = q.shape
    n_kv = k.shape[2]
    g_ = n_heads // n_kv
    nb = n // SW_BLOCK
    span = SW_BLOCK + 2 * SW_WINDOW
    scale = dh ** -0.5
    qb = q.reshape(b_, nb, SW_BLOCK, n_kv, g_, dh)
    pad = ((0, 0), (SW_WINDOW, SW_WINDOW), (0, 0), (0, 0))
    idx = jnp.arange(nb)[:, None] * SW_BLOCK + jnp.arange(span)[None]
    kb = jnp.pad(k, pad)[:, idx]
    vb = jnp.pad(v, pad)[:, idx]
    qpos = jnp.arange(nb)[:, None] * SW_BLOCK + jnp.arange(SW_BLOCK)[None]
    kpos = (idx - SW_WINDOW)[:, None, :]
    ok = (jnp.abs(kpos - qpos[:, :, None]) <= SW_WINDOW) & (kpos >= 0) & (kpos < n)
    s_lat = jnp.where(ok, jnp.einsum('bnqkgd,bnckd->bkgnqc', qb, kb).astype(F32) * scale, NEG_INF)
    s_ctx = jnp.einsum('bnqkgd,bckd->bkgnqc', qb, k_ctx).astype(F32) * scale
    s_sink = jnp.broadcast_to(sink.astype(F32).reshape(n_kv, g_)[None, :, :, None, None, None],
                              s_lat.shape[:-1] + (1,))
    p = _softmax(jnp.concatenate([s_lat, s_ctx, s_sink], axis=-1))
    n_ctx = k_ctx.shape[1]
    p_lat = p[..., :span].astype(v.dtype)
    p_ctx = p[..., span:span + n_ctx].astype(v.dtype)
    o = jnp.einsum('bkgnqc,bnckd->bnqkgd', p_lat, vb) + jnp.einsum('bkgnqc,bckd->bnqkgd', p_ctx, v_ctx)
    return o.reshape(b_, n, n_heads * dh)


def _split_gqa(z):
    b_, n, _ = z.shape
    qw = SW_HEADS * SW_HEAD_DIM
    kw = SW_KV_HEADS * SW_HEAD_DIM
    q = z[..., :qw].reshape(b_, n, SW_HEADS, SW_HEAD_DIM)
    k = z[..., qw:qw + kw].reshape(b_, n, SW_KV_HEADS, SW_HEAD_DIM)
    v = z[..., qw + kw:].reshape(b_, n, SW_KV_HEADS, SW_HEAD_DIM)
    return q, k, v


def _sw_branch(z_lat, z_ctx, sink, need_ctx_out):
    n = z_lat.shape[1]
    q, k, v = _split_gqa(z_lat)
    cos, sin = _axial_rope(n, SW_HEAD_DIM)
    q = _apply_rope(q, cos, sin)
    k = _apply_rope(k, cos, sin)
    qc, kc, vc = _split_gqa(z_ctx)
    out_lat = _window_attention(q, k, v, kc, vc, sink)
    out_ctx = _ctx_attention(qc, kc, vc, sink) if need_ctx_out else None
    return out_lat, out_ctx


def _merge(branches, gate_logits, w_branch, w_out):
    stacked = jnp.stack(branches, axis=2)
    proj = jnp.einsum('blnw,nwd->blnd', stacked, w_branch)
    gates = jax.nn.sigmoid(gate_logits.reshape(proj.shape))
    return jnp.einsum('blnd,blnd->bld', gates, proj) @ w_out


def _token_mixer(h, hc, w_in, s5_p, hy_p, na_rpb, sw_sink, w_branch, w_out, need_ctx_out):
    z = h @ w_in
    zc = hc @ w_in
    u_s5, z_hy, z_na, z_sw, z_gate = jnp.split(z, IN_CUTS, axis=-1)
    uc_s5, zc_hy, zc_na, zc_sw, zc_gate = jnp.split(zc, IN_CUTS, axis=-1)
    s5_l, s5_c = _s5_branch(u_s5, uc_s5, *s5_p, need_ctx_out)
    hy_l, hy_c = _hyena_branch(z_hy, zc_hy, *hy_p, need_ctx_out)
    na_l, na_c = _na_branch(z_na, zc_na, na_rpb, need_ctx_out)
    sw_l, sw_c = _sw_branch(z_sw, zc_sw, sw_sink, need_ctx_out)
    out = _merge((s5_l, hy_l, na_l, sw_l), z_gate, w_branch, w_out)
    out_c = _merge((s5_c, hy_c, na_c, sw_c), zc_gate, w_branch, w_out) if need_ctx_out else None
    return out, out_c


def _sqrelu_mlp(h, w1, w2):
    return jnp.square(jax.nn.relu(h @ w1)) @ w2


def _layer(x, xc, c, c_ctx, w_ada, b_ada, w_in, s5_p, hy_p, na_rpb, sw_sink, w_branch, w_out,
           ln1_g, ln1_b, w_mlp1, w_mlp2, ln2_g, ln2_b, need_ctx_out):
    mod = (jax.nn.silu(c) @ w_ada + b_ada)[:, None, :]
    mod_c = (jax.nn.silu(c_ctx) @ w_ada + b_ada)[None, None, :]
    sh_a, sc_a, g_a, sh_m, sc_m, g_m = jnp.split(mod, 6, axis=-1)
    csh_a, csc_a, cg_a, csh_m, csc_m, cg_m = jnp.split(mod_c, 6, axis=-1)
    h = _modulate(_layernorm(x), sh_a, sc_a)
    hc = _modulate(_layernorm(xc), csh_a, csc_a)
    mix, mix_c = _token_mixer(h, hc, w_in, s5_p, hy_p, na_rpb, sw_sink, w_branch, w_out, need_ctx_out)
    x = _layernorm(DEEPNORM_ALPHA * x + g_a * mix, ln1_g, ln1_b)
    x = _layernorm(DEEPNORM_ALPHA * x + g_m * _sqrelu_mlp(_modulate(_layernorm(x), sh_m, sc_m), w_mlp1, w_mlp2),
                   ln2_g, ln2_b)
    if not need_ctx_out:
        return x, None
    xc = _layernorm(DEEPNORM_ALPHA * xc + cg_a * mix_c, ln1_g, ln1_b)
    xc = _layernorm(DEEPNORM_ALPHA * xc + cg_m * _sqrelu_mlp(_modulate(_layernorm(xc), csh_m, csc_m), w_mlp1, w_mlp2),
                    ln2_g, ln2_b)
    return x, xc


def setup_inputs(seed: int = 0) -> dict:
    key = jax.random.key(seed)
    keys = iter(jax.random.split(key, 64))

    def nrm(shape, std):
        return std * jax.random.normal(next(keys), shape, F32)

    n_idx = jnp.arange(S5_STATE, dtype=F32)
    return {
        'x': nrm((BATCH, SEQ, D_MODEL), 1.0),
        'c': nrm((BATCH, D_MODEL), 1.0),
        'ctx': nrm((BATCH, CTX_LEN, D_MODEL), 1.0),
        'c_ctx': nrm((D_MODEL,), 1.0),
        'w_ada': nrm((DEPTH, D_MODEL, 6 * D_MODEL), 0.5 * D_MODEL ** -0.5),
        'b_ada': nrm((DEPTH, 6 * D_MODEL), 0.01),
        'w_in': nrm((DEPTH, D_MODEL, IN_TOTAL), D_MODEL ** -0.5),
        's5_lambda_re': -0.5 + nrm((DEPTH, 2, S5_GROUPS, S5_STATE), 0.01),
        's5_lambda_im': jnp.broadcast_to(math.pi * n_idx, (DEPTH, 2, S5_GROUPS, S5_STATE)),
        's5_log_dt': jax.random.uniform(next(keys), (DEPTH, 2, S5_GROUPS), F32,
                                        math.log(S5_DT_MIN), math.log(S5_DT_MAX)),
        's5_b_re': nrm((DEPTH, 2, S5_GROUPS, S5_STATE, S5_GROUP), (2 * S5_GROUP) ** -0.5),
        's5_b_im': nrm((DEPTH, 2, S5_GROUPS, S5_STATE, S5_GROUP), (2 * S5_GROUP) ** -0.5),
        's5_c_re': nrm((DEPTH, 2, S5_GROUPS, S5_GROUP, S5_STATE), S5_STATE ** -0.5),
        's5_c_im': nrm((DEPTH, 2, S5_GROUPS, S5_GROUP, S5_STATE), S5_STATE ** -0.5),
        's5_d': nrm((DEPTH, S5_WIDTH), 1.0),
        's5_w_glu': nrm((DEPTH, S5_WIDTH, S5_WIDTH), S5_WIDTH ** -0.5),
        'hy_conv_w': nrm((DEPTH, HY_SHORT, 1, 3 * HY_WIDTH), HY_SHORT ** -0.5),
        'hy_conv_b': nrm((DEPTH, 3 * HY_WIDTH), 0.01),
        'hy_freq': 1.0 + nrm((DEPTH, 2, HY_FFN), 0.01),
        'hy_w1': nrm((DEPTH, HY_EMB, HY_FFN), HY_EMB ** -0.5),
        'hy_b1': nrm((DEPTH, HY_FFN), 0.01),
        'hy_w2': nrm((DEPTH, HY_FFN, HY_FFN), HY_FFN ** -0.5),
        'hy_b2': nrm((DEPTH, HY_FFN), 0.01),
        'hy_w3': nrm((DEPTH, HY_FFN, 2 * HY_ORDER * HY_WIDTH), 0.02 * HY_FFN ** -0.5),
        'hy_bias': nrm((DEPTH, HY_ORDER, HY_WIDTH), 1.0),
        'na_rpb': nrm((DEPTH, NA_HEADS, 2 * NA_WIN_H - 1, 2 * NA_WIN_W - 1), 0.02),
        'sw_sink': nrm((DEPTH, SW_HEADS), 1.0),
        'w_branch': nrm((DEPTH, N_BRANCH, BRANCH_W, D_MODEL), BRANCH_W ** -0.5),
        'w_out': nrm((DEPTH, D_MODEL, D_MODEL), DEEPNORM_BETA * D_MODEL ** -0.5),
        'ln1_g': 1.0 + nrm((DEPTH, D_MODEL), 0.01),
        'ln1_b': nrm((DEPTH, D_MODEL), 0.01),
        'w_mlp1': nrm((DEPTH, D_MODEL, MLP_HIDDEN), D_MODEL ** -0.5),
        'w_mlp2': nrm((DEPTH, MLP_HIDDEN, D_MODEL), DEEPNORM_BETA * MLP_HIDDEN ** -0.5),
        'ln2_g': 1.0 + nrm((DEPTH, D_MODEL), 0.01),
        'ln2_b': nrm((DEPTH, D_MODEL), 0.01),
    }


def reference(x, c, ctx, c_ctx, w_ada, b_ada, w_in, s5_lambda_re, s5_lambda_im, s5_log_dt, s5_b_re, s5_b_im,
              s5_c_re, s5_c_im, s5_d, s5_w_glu, hy_conv_w, hy_conv_b, hy_freq, hy_w1, hy_b1, hy_w2, hy_b2, hy_w3,
              hy_bias, na_rpb, sw_sink, w_branch, w_out, ln1_g, ln1_b, w_mlp1, w_mlp2, ln2_g, ln2_b):
    xc = ctx
    for l in range(DEPTH):
        need_ctx_out = l < DEPTH - 1
        s5_p = (s5_lambda_re[l], s5_lambda_im[l], s5_log_dt[l], s5_b_re[l], s5_b_im[l],
                s5_c_re[l], s5_c_im[l], s5_d[l], s5_w_glu[l])
        hy_p = (hy_conv_w[l], hy_conv_b[l], hy_freq[l], hy_w1[l], hy_b1[l], hy_w2[l], hy_b2[l], hy_w3[l], hy_bias[l])
        x, xc = _layer(x, xc, c, c_ctx, w_ada[l], b_ada[l], w_in[l], s5_p, hy_p, na_rpb[l], sw_sink[l],
                       w_branch[l], w_out[l], ln1_g[l], ln1_b[l], w_mlp1[l], w_mlp2[l], ln2_g[l], ln2_b[l],
                       need_ctx_out)
    return x
```

```python
import functools
import math

import numpy as np
import jax
import jax.numpy as jnp
from jax import lax
from jax.experimental import pallas as pl
from jax.experimental.pallas import tpu as pltpu

F32 = jnp.float32
BF16 = jnp.bfloat16

D_MODEL = 1024
GRID_W = 64
BRANCH_W = 256
N_BRANCH = 4
S5_GROUP = 16
S5_GROUPS = 16
S5_STATE = 64
HY_WIDTH = 256
HY_ORDER = 2
HY_BANDS = 16
HY_EMB = 2 * HY_BANDS + 1
HY_FFN = 64
HY_MIN_DECAY = math.log(1e-2) / 1.5
HY_MAX_DECAY = math.log(1e-2) / 0.3
NA_HEADS = 4
NA_HEAD_DIM = 64
NA_WIN_H = 8
NA_WIN_W = 16
SW_HEADS = 4
SW_KV_HEADS = 2
SW_HEAD_DIM = 64
SW_WINDOW = 128
SW_BLOCK = 128
MLP_HIDDEN = 4 * D_MODEL
ROPE_BASE = 10000.0
LN_EPS = 1e-6
NEG_INF = -1e30
DEPTH = 2
DEEPNORM_ALPHA = (2 * DEPTH) ** 0.25

MIX_W = 2304
GATE_W = N_BRANCH * D_MODEL
S5_T = 8
S5_STATE_W = 4 * S5_GROUPS * S5_STATE
SUBLANES = 8
LANES = 128
VMEM_LIMIT = 56 * 1024 * 1024

HI = lax.Precision.HIGHEST


def _cparams(sem):
    return pltpu.CompilerParams(dimension_semantics=sem, vmem_limit_bytes=VMEM_LIMIT)


def _layernorm(x):
    mu = jnp.mean(x, axis=-1, keepdims=True)
    xc = x - mu
    var = jnp.mean(xc * xc, axis=-1, keepdims=True)
    return xc * lax.rsqrt(var + LN_EPS)


def _ada_kernel(c_ref, w_ref, b_ref, o_ref):
    c = c_ref[...]
    a = c * jax.nn.sigmoid(c)
    o_ref[0] = jnp.dot(a, w_ref[0], preferred_element_type=F32, precision=HI) + b_ref[0]


def _ada(cc, w_ada, b_ada):
    depth, d, n = w_ada.shape
    tn = 512
    return pl.pallas_call(
        _ada_kernel,
        grid=(depth, n // tn),
        in_specs=[pl.BlockSpec((8, d), lambda l, j: (0, 0)),
                  pl.BlockSpec((1, d, tn), lambda l, j: (l, 0, j)),
                  pl.BlockSpec((1, 1, tn), lambda l, j: (l, 0, j))],
        out_specs=pl.BlockSpec((1, 8, tn), lambda l, j: (l, 0, j)),
        out_shape=jax.ShapeDtypeStruct((depth, 8, n), F32),
        compiler_params=_cparams(("parallel", "parallel")),
    )(cc, w_ada, b_ada.reshape(depth, 1, n))


def _ln_mod_mm_kernel(x_ref, sh_ref, sc_ref, w_ref, o_ref, h_scr, *, act):
    @pl.when(pl.program_id(2) == 0)
    def _():
        h = _layernorm(x_ref[0]) * (1.0 + sc_ref[0]) + sh_ref[0]
        h_scr[...] = h.astype(BF16)

    r = jnp.dot(h_scr[...], w_ref[...], preferred_element_type=F32)
    if act == "sigmoid":
        r = jax.nn.sigmoid(r)
    o_ref[0] = r


def _ln_mod_mm(x, sh, sc, w, *, tm, tn, act=None):
    b, l, d = x.shape
    n = w.shape[1]
    return pl.pallas_call(
        functools.partial(_ln_mod_mm_kernel, act=act),
        grid=(b, l // tm, n // tn),
        in_specs=[pl.BlockSpec((1, tm, d), lambda bi, i, j: (bi, i, 0)),
                  pl.BlockSpec((1, 1, d), lambda bi, i, j: (bi, 0, 0)),
                  pl.BlockSpec((1, 1, d), lambda bi, i, j: (bi, 0, 0)),
                  pl.BlockSpec((d, tn), lambda bi, i, j: (0, j))],
        out_specs=pl.BlockSpec((1, tm, tn), lambda bi, i, j: (bi, i, j)),
        out_shape=jax.ShapeDtypeStruct((b, l, n), F32),
        scratch_shapes=[pltpu.VMEM((tm, d), BF16)],
        compiler_params=_cparams(("parallel", "parallel", "arbitrary")),
    )(x, sh, sc, w)


def _mm_sum_kernel(*refs, n_pairs):
    o_ref = refs[-1]
    acc = None
    for p in range(n_pairs):
        t = jnp.dot(refs[p][...].astype(BF16), refs[n_pairs + p][...], preferred_element_type=F32)
        acc = t if acc is None else acc + t
    o_ref[...] = acc


def _mm_sum(a_list, b_list, *, tm, tn):
    m = a_list[0].shape[0]
    n = b_list[0].shape[1]
    n_pairs = len(a_list)
    in_specs = [pl.BlockSpec((tm, a.shape[1]), lambda i, j: (i, 0)) for a in a_list]
    in_specs += [pl.BlockSpec((bm.shape[0], tn), lambda i, j: (0, j)) for bm in b_list]
    return pl.pallas_call(
        functools.partial(_mm_sum_kernel, n_pairs=n_pairs),
        grid=(m // tm, n // tn),
        in_specs=in_specs,
        out_specs=pl.BlockSpec((tm, tn), lambda i, j: (i, j)),
        out_shape=jax.ShapeDtypeStruct((m, n), F32),
        compiler_params=_cparams(("parallel", "parallel")),
    )(*a_list, *b_list)


def _s5_scan_kernel(gfr, gfi, gbr, gbi, afr, afi, abr, abi, h0fr, h0fi, h0br, h0bi,
                    hfr, hfi, hbr, hbi, efr, efi, ebr, ebi, *, n_chunks):
    a_fr = afr[...][None]
    a_fi = afi[...][None]
    a_br = abr[...][None]
    a_bi = abi[...][None]

    def body(k, carry):
        sfr, sfi, sbr, sbi = carry
        kb = n_chunks - 1 - k
        hfr[:, pl.ds(k, 1), :] = sfr
        hfi[:, pl.ds(k, 1), :] = sfi
        hbr[:, pl.ds(kb, 1), :] = sbr
        hbi[:, pl.ds(kb, 1), :] = sbi
        nfr = a_fr * sfr - a_fi * sfi + gfr[:, pl.ds(k, 1), :]
        nfi = a_fr * sfi + a_fi * sfr + gfi[:, pl.ds(k, 1), :]
        nbr = a_br * sbr - a_bi * sbi + gbr[:, pl.ds(kb, 1), :]
        nbi = a_br * sbi + a_bi * sbr + gbi[:, pl.ds(kb, 1), :]
        return nfr, nfi, nbr, nbi

    sfr, sfi, sbr, sbi = lax.fori_loop(0, n_chunks, body, (h0fr[...], h0fi[...], h0br[...], h0bi[...]))
    efr[...] = sfr
    efi[...] = sfi
    ebr[...] = sbr
    ebi[...] = sbi


def _s5_scan(g, a_t, h0):
    b, k, w4 = g.shape
    w = LANES
    q = w4 // 4
    nb = q // w

    def comp(c):
        return pl.BlockSpec((b, k, w), lambda j, c=c: (0, 0, c * nb + j))

    def comp_a(c):
        return pl.BlockSpec((1, w), lambda j, c=c: (0, c * nb + j))

    state = pl.BlockSpec((b, k, w), lambda j: (0, 0, j))
    edge = pl.BlockSpec((b, 1, w), lambda j: (0, 0, j))
    outs = pl.pallas_call(
        functools.partial(_s5_scan_kernel, n_chunks=k),
        grid=(nb,),
        in_specs=[comp(c) for c in range(4)] + [comp_a(c) for c in range(4)] + [edge] * 4,
        out_specs=[state] * 4 + [edge] * 4,
        out_shape=[jax.ShapeDtypeStruct((b, k, q), F32)] * 4 + [jax.ShapeDtypeStruct((b, 1, q), F32)] * 4,
        compiler_params=_cparams(("parallel",)),
    )(g, g, g, g, a_t, a_t, a_t, a_t, *h0)
    return outs[:4], outs[4:]


def _cmul(ar, ai, br, bi):
    return ar * br - ai * bi, ar * bi + ai * br


def _s5_operators(lam_re, lam_im, log_dt, b_re, b_im, c_re, c_im, d):
    t_ = S5_T
    g_, p_, w_ = S5_GROUPS, S5_STATE, S5_GROUP * S5_GROUPS
    eye = jnp.eye(g_, dtype=F32)
    ii = np.arange(t_)[None, :]
    jj = np.arange(t_)[:, None]
    m_tot = None
    q_parts, n_parts, a_parts = [], [], []
    for direction in range(2):
        lr = lam_re[direction].astype(F32)
        li = lam_im[direction].astype(F32)
        dt = jnp.exp(log_dt[direction].astype(F32))[:, None]
        ks = jnp.arange(t_ + 1, dtype=F32)[:, None, None]
        mag = jnp.exp(ks * lr * dt)
        pw_re = mag * jnp.cos(ks * li * dt)
        pw_im = mag * jnp.sin(ks * li * dt)
        a_re, a_im = pw_re[1], pw_im[1]
        den = lr ** 2 + li ** 2
        f_re = ((a_re - 1.0) * lr + a_im * li) / den
        f_im = (a_im * lr - (a_re - 1.0) * li) / den
        br = b_re[direction].astype(F32)
        bi = b_im[direction].astype(F32)
        bb_re = f_re[..., None] * br - f_im[..., None] * bi
        bb_im = f_re[..., None] * bi + f_im[..., None] * br
        cr = c_re[direction].astype(F32)
        ci = c_im[direction].astype(F32)
        wb_re, wb_im = _cmul(pw_re[..., None], pw_im[..., None], bb_re[None], bb_im[None])
        kern = jnp.einsum('gnp,kgpm->kgnm', cr, wb_re[:t_], precision=HI) \
            - jnp.einsum('gnp,kgpm->kgnm', ci, wb_im[:t_], precision=HI)
        kbd = jnp.einsum('kgnm,gh->kgmhn', kern, eye).reshape(t_, w_, w_)
        if direction == 0:
            sel = np.clip(ii - jj, 0, t_ - 1)
            ok = (ii >= jj)
        else:
            sel = np.clip(jj - ii, 0, t_ - 1)
            ok = (jj >= ii)
        m4 = kbd[sel] * jnp.asarray(ok, F32)[:, :, None, None]
        m4 = m4.transpose(0, 2, 1, 3).reshape(t_ * w_, t_ * w_)
        m_tot = m4 if m_tot is None else m_tot + m4
        e_q = (t_ - 1 - np.arange(t_)) if direction == 0 else np.arange(t_)
        qd_re = jnp.einsum('jgpm,gh->jhmgp', wb_re[e_q], eye).reshape(t_ * w_, g_ * p_)
        qd_im = jnp.einsum('jgpm,gh->jhmgp', wb_im[e_q], eye).reshape(t_ * w_, g_ * p_)
        q_parts += [qd_re, qd_im]
        e_n = (np.arange(t_) + 1) if direction == 0 else (t_ - np.arange(t_))
        ca_re, ca_im = _cmul(cr[None], ci[None], pw_re[e_n][:, :, None, :], pw_im[e_n][:, :, None, :])
        nd_re = jnp.einsum('ignp,gh->gpihn', ca_re, eye).reshape(g_ * p_, t_ * w_)
        nd_im = jnp.einsum('ignp,gh->gpihn', ca_im, eye).reshape(g_ * p_, t_ * w_)
        n_parts += [nd_re, -nd_im]
        a_parts += [pw_re[t_].reshape(1, g_ * p_), pw_im[t_].reshape(1, g_ * p_)]
    dd = jnp.tile(d.astype(F32), t_)
    m_tot = m_tot + jnp.diag(dd)
    return (m_tot.astype(BF16), jnp.concatenate(q_parts, axis=1).astype(BF16),
            [n.astype(BF16) for n in n_parts], jnp.concatenate(a_parts, axis=1))


def _s5_mix(u, ops, h0):
    m_tot, q_all, n_parts, a_t = ops
    b, n, w = u.shape
    k = n // S5_T
    rows = b * k
    uu = u.reshape(rows, S5_T * w)
    tm = min(rows, 256)
    g = _mm_sum([uu], [q_all], tm=tm, tn=1024)
    h, e = _s5_scan(g.reshape(b, k, S5_STATE_W), a_t, h0)
    y = _mm_sum([uu] + [hc.reshape(rows, -1) for hc in h], [m_tot] + n_parts, tm=tm, tn=512)
    return y.reshape(b, n, w), e


def _hy_filter_kernel(f_ref, w1_ref, b1_ref, w2_ref, b2_ref, w3_ref, fr_ref, dl_ref, o_ref, *, half_len, tile):
    feats = f_ref[...]
    h = jnp.dot(feats, w1_ref[...], preferred_element_type=F32, precision=HI) + b1_ref[...]
    h = jnp.sin(fr_ref[0:1, :] * h)
    h = jnp.dot(h, w2_ref[...], preferred_element_type=F32, precision=HI) + b2_ref[...]
    h = jnp.sin(fr_ref[1:2, :] * h)
    o = jnp.dot(h, w3_ref[...], preferred_element_type=F32, precision=HI)
    t = feats[:, 0:1]
    o = o * jnp.exp(-t * dl_ref[...])
    nw = HY_ORDER * HY_WIDTH
    n = pl.program_id(0) * tile + lax.broadcasted_iota(jnp.int32, (tile, 1), 0)
    o_ref[...] = jnp.where(n < half_len, o[:, :nw], jnp.where(n == half_len, 0.0, o[:, nw:]))


def _hy_filters(n_tokens, freq, w1, b1, w2, b2, w3):
    n2 = 2 * n_tokens
    idx = jnp.arange(n2, dtype=F32)
    pos = jnp.where(idx <= n_tokens, idx, n2 - idx)[:, None]
    t = pos / max(n_tokens - 1, 1)
    bands = jnp.linspace(1e-4, HY_BANDS - 1, HY_BANDS, dtype=F32)[None]
    ang = 2.0 * math.pi * bands * pos / n_tokens
    feats = jnp.concatenate([t, jnp.cos(ang), -jnp.sin(ang)], axis=-1)
    kpad = LANES - HY_EMB
    feats = jnp.pad(feats, ((0, 0), (0, kpad)))
    w1p = jnp.pad(w1.astype(F32), ((0, kpad), (0, 0)))
    deltas = jnp.abs(jnp.linspace(HY_MIN_DECAY, HY_MAX_DECAY, HY_WIDTH, dtype=F32))
    dl = jnp.tile(deltas, 2 * HY_ORDER)[None]
    tile = min(n2, 1024)
    nw = 2 * HY_ORDER * HY_WIDTH
    full = lambda s: pl.BlockSpec(s, lambda i: (0,) * len(s))
    return pl.pallas_call(
        functools.partial(_hy_filter_kernel, half_len=n_tokens, tile=tile),
        grid=(n2 // tile,),
        in_specs=[pl.BlockSpec((tile, LANES), lambda i: (i, 0)),
                  full((LANES, HY_FFN)), full((1, HY_FFN)), full((HY_FFN, HY_FFN)), full((1, HY_FFN)),
                  full((HY_FFN, nw)), full((2, HY_FFN)), full((1, nw))],
        out_specs=pl.BlockSpec((tile, HY_ORDER * HY_WIDTH), lambda i: (i, 0)),
        out_shape=jax.ShapeDtypeStruct((n2, HY_ORDER * HY_WIDTH), F32),
        compiler_params=_cparams(("parallel",)),
    )(feats, w1p, b1.astype(F32)[None], w2.astype(F32), b2.astype(F32)[None], w3.astype(F32),
      freq.astype(F32), dl)


FFT_G = 4


def _fft_tables(n):
    s = n // SUBLANES
    nst = int(round(math.log2(s)))
    m = np.arange(max(s // 2, 1))
    tw_slab = np.stack([np.cos(-2 * np.pi * m / s), np.sin(-2 * np.pi * m / s)]).astype(np.float32)
    pos = np.arange(s)
    rev = np.zeros(s, np.int64)
    for bit in range(nst):
        rev |= ((pos >> bit) & 1) << (nst - 1 - bit)
    ang = -2 * np.pi * (rev[:, None] * np.arange(SUBLANES)[None, :]) / n
    tw_mid = np.stack([np.cos(ang), np.sin(ang)]).astype(np.float32)
    tw_mid = np.broadcast_to(tw_mid[..., None], (2, s, SUBLANES, LANES)).copy()
    t = np.arange(n)
    ang2 = -2 * np.pi * t / (2 * n)
    mod = np.stack([np.cos(ang2), np.sin(ang2)]).astype(np.float32).reshape(2, s, SUBLANES)
    mod = np.broadcast_to(mod[..., None], (2, s, SUBLANES, LANES)).copy()
    return jnp.asarray(tw_slab), jnp.asarray(tw_mid), jnp.asarray(mod)


def _sub_patterns():
    sub = lax.broadcasted_iota(jnp.int32, (SUBLANES, LANES), 0)

    def table(vals):
        out = jnp.full((SUBLANES, LANES), vals[0], F32)
        for k in range(1, SUBLANES):
            out = jnp.where(sub == k, np.float32(vals[k]), out)
        return out

    pats = {}
    for dist in (4, 2, 1):
        lo = (sub & dist) == 0
        sgn = jnp.where(lo, 1.0, -1.0).astype(F32)
        wr = [1.0] * SUBLANES
        wi = [0.0] * SUBLANES
        for k in range(SUBLANES):
            if k & dist:
                e = (k % dist) * (SUBLANES // (2 * dist))
                wr[k] = math.cos(-2 * math.pi * e / SUBLANES)
                wi[k] = math.sin(-2 * math.pi * e / SUBLANES)
        pats[dist] = (lo, sgn, table(wr), table(wi))
    return pats


def _dft8_fwd(vr, vi, pats):
    for dist in (4, 2, 1):
        lo, sgn, wr, wi = pats[dist]
        up_r = pltpu.roll(vr, SUBLANES - dist, 0)
        up_i = pltpu.roll(vi, SUBLANES - dist, 0)
        if dist == 4:
            pr, pi_ = up_r, up_i
        else:
            pr = jnp.where(lo, up_r, pltpu.roll(vr, dist, 0))
            pi_ = jnp.where(lo, up_i, pltpu.roll(vi, dist, 0))
        tr = pr + sgn * vr
        ti = pi_ + sgn * vi
        if dist == 1:
            vr, vi = tr, ti
        else:
            vr, vi = _cmul(tr, ti, wr, wi)
    return vr, vi


def _dft8_inv(vr, vi, pats):
    for dist in (1, 2, 4):
        lo, sgn, wr, wi = pats[dist]
        if dist != 1:
            vr, vi = _cmul(vr, vi, wr, -wi)
        up_r = pltpu.roll(vr, SUBLANES - dist, 0)
        up_i = pltpu.roll(vi, SUBLANES - dist, 0)
        if dist == 4:
            pr, pi_ = up_r, up_i
        else:
            pr = jnp.where(lo, up_r, pltpu.roll(vr, dist, 0))
            pi_ = jnp.where(lo, up_i, pltpu.roll(vi, dist, 0))
        vr = pr + sgn * vr
        vi = pi_ + sgn * vi
    return vr, vi


def _slab_stage(re, im, tw, half, slabs, inverse):
    per_block = half // FFT_G
    step = slabs // (2 * half)

    def body(c, carry):
        blk = c // per_block
        jc = c % per_block
        i0 = blk * (2 * half) + jc * FFT_G
        i1 = i0 + half
        for g in range(FFT_G):
            wr = tw[0, (jc * FFT_G + g) * step]
            wi = tw[1, (jc * FFT_G + g) * step]
            ar, ai = re[i0 + g], im[i0 + g]
            br, bi = re[i1 + g], im[i1 + g]
            if inverse:
                br, bi = br * wr + bi * wi, bi * wr - br * wi
                re[i0 + g] = ar + br
                im[i0 + g] = ai + bi
                re[i1 + g] = ar - br
                im[i1 + g] = ai - bi
            else:
                re[i0 + g] = ar + br
                im[i0 + g] = ai + bi
                dr, di = ar - br, ai - bi
                re[i1 + g] = dr * wr - di * wi
                im[i1 + g] = dr * wi + di * wr
        return carry

    lax.fori_loop(0, slabs // (2 * FFT_G), body, 0)


def _fft_forward_big(re, im, tw, slabs):
    half = slabs // 2
    while half >= 4:
        _slab_stage(re, im, tw, half, slabs, inverse=False)
        half //= 2


def _fft_inverse_big(re, im, tw, slabs):
    half = 4
    while half <= slabs // 2:
        _slab_stage(re, im, tw, half, slabs, inverse=True)
        half *= 2


def _radix4_fwd(x):
    (x0r, x0i), (x1r, x1i), (x2r, x2i), (x3r, x3i) = x
    y0r, y0i = x0r + x2r, x0i + x2i
    y2r, y2i = x0r - x2r, x0i - x2i
    y1r, y1i = x1r + x3r, x1i + x3i
    dr, di = x1r - x3r, x1i - x3i
    y3r, y3i = di, -dr
    return [(y0r + y1r, y0i + y1i), (y0r - y1r, y0i - y1i), (y2r + y3r, y2i + y3i), (y2r - y3r, y2i - y3i)]


def _radix4_inv(z):
    (z0r, z0i), (z1r, z1i), (z2r, z2i), (z3r, z3i) = z
    y0r, y0i = z0r + z1r, z0i + z1i
    y1r, y1i = z0r - z1r, z0i - z1i
    y2r, y2i = z2r + z3r, z2i + z3i
    y3r, y3i = z2r - z3r, z2i - z3i
    qr, qi = -y3i, y3r
    return [(y0r + y2r, y0i + y2i), (y1r + qr, y1i + qi), (y0r - y2r, y0i - y2i), (y1r - qr, y1i - qi)]


def _fft_middle(re, im, twm_ref, slabs, pats, spec=None, out=None):
    def body(q, carry):
        p0 = q * FFT_G
        x = [(re[p0 + g], im[p0 + g]) for g in range(FFT_G)]
        z = _radix4_fwd(x)
        res = []
        for g in range(FFT_G):
            twr = twm_ref[0, p0 + g]
            twi = twm_ref[1, p0 + g]
            vr, vi = _cmul(z[g][0], z[g][1], twr, twi)
            vr, vi = _dft8_fwd(vr, vi, pats)
            if spec is None:
                out[0][0, 0, p0 + g] = vr * out[2]
                out[1][0, 0, p0 + g] = vi * out[2]
            else:
                vr, vi = _cmul(vr, vi, spec[0][0, 0, p0 + g], spec[1][0, 0, p0 + g])
                vr, vi = _dft8_inv(vr, vi, pats)
                res.append(_cmul(vr, vi, twr, -twi))
        if spec is not None:
            x = _radix4_inv(res)
            for g in range(FFT_G):
                re[p0 + g] = x[g][0]
                im[p0 + g] = x[g][1]
        return carry

    lax.fori_loop(0, slabs // FFT_G, body, 0)


def _hy_spec_kernel(tw_ref, k_ref, twm_ref, mod_ref, sr_ref, si_ref, re, im, *, slabs):
    h = pl.program_id(2)
    pats = _sub_patterns()
    a = k_ref[0:slabs]
    b = k_ref[slabs:2 * slabs]

    @pl.when(h == 0)
    def _():
        re[...] = a + b
        im[...] = jnp.zeros_like(a)

    @pl.when(h == 1)
    def _():
        dlt = a - b
        re[...] = dlt * mod_ref[0]
        im[...] = dlt * mod_ref[1]

    _fft_forward_big(re, im, tw_ref, slabs)
    scale = np.float32(1.0 / (2 * SUBLANES * slabs))
    _fft_middle(re, im, twm_ref, slabs, pats, out=(sr_ref, si_ref, scale))


def _hy_spectra(k, n_tokens):
    slabs = n_tokens // SUBLANES
    tw, twm, mod = _fft_tables(n_tokens)
    k4 = k.reshape(2 * slabs, SUBLANES, HY_ORDER * HY_WIDTH)
    nt = HY_WIDTH // LANES
    shape = jax.ShapeDtypeStruct((HY_ORDER, 2, slabs, SUBLANES, HY_WIDTH), F32)
    spec_out = pl.BlockSpec((1, 1, slabs, SUBLANES, LANES), lambda o, j, h: (o, h, 0, 0, j))
    tab = pl.BlockSpec((2, slabs, SUBLANES, LANES), lambda o, j, h: (0, 0, 0, 0))
    return pl.pallas_call(
        functools.partial(_hy_spec_kernel, slabs=slabs),
        grid=(HY_ORDER, nt, 2),
        in_specs=[pl.BlockSpec(memory_space=pltpu.SMEM),
                  pl.BlockSpec((2 * slabs, SUBLANES, LANES), lambda o, j, h: (0, 0, o * nt + j)),
                  tab, tab],
        out_specs=[spec_out, spec_out],
        out_shape=[shape, shape],
        scratch_shapes=[pltpu.VMEM((slabs, SUBLANES, LANES), F32)] * 2,
        compiler_params=_cparams(("parallel", "parallel", "arbitrary")),
    )(tw, k4, twm, mod)


def _hy_conv_kernel(tw_ref, u_ref, sr_ref, si_ref, twm_ref, mod_ref, o_ref, re, im, *, slabs):
    h = pl.program_id(2)
    pats = _sub_patterns()

    @pl.when(h == 0)
    def _():
        re[...] = u_ref[0]
        im[...] = u_ref[1]

    @pl.when(h == 1)
    def _():
        ur, ui = u_ref[0], u_ref[1]
        mr, mi = mod_ref[0], mod_ref[1]
        re[...] = ur * mr - ui * mi
        im[...] = ur * mi + ui * mr

    _fft_forward_big(re, im, tw_ref, slabs)
    _fft_middle(re, im, twm_ref, slabs, pats, spec=(sr_ref, si_ref))
    _fft_inverse_big(re, im, tw_ref, slabs)

    @pl.when(h == 0)
    def _():
        o_ref[0] = re[...]
        o_ref[1] = im[...]

    @pl.when(h == 1)
    def _():
        yr, yi = re[...], im[...]
        mr, mi = mod_ref[0], mod_ref[1]
        o_ref[0] = o_ref[0] + (yr * mr + yi * mi)
        o_ref[1] = o_ref[1] + (yi * mr - yr * mi)


def _hy_conv(u, spec_re, spec_im, order):
    b, n, w = u.shape
    slabs = n // SUBLANES
    tw, twm, mod = _fft_tables(n)
    u4 = u.reshape(b, slabs, SUBLANES, w)
    nt = w // LANES
    tab = pl.BlockSpec((2, slabs, SUBLANES, LANES), lambda j, p, h: (0, 0, 0, 0))
    spec_in = pl.BlockSpec((1, 1, slabs, SUBLANES, LANES), lambda j, p, h: (order, h, 0, 0, j))
    io = pl.BlockSpec((2, slabs, SUBLANES, LANES), lambda j, p, h: (p, 0, 0, j))
    y = pl.pallas_call(
        functools.partial(_hy_conv_kernel, slabs=slabs),
        grid=(nt, b // 2, 2),
        in_specs=[pl.BlockSpec(memory_space=pltpu.SMEM), io, spec_in, spec_in, tab, tab],
        out_specs=io,
        out_shape=jax.ShapeDtypeStruct((b, slabs, SUBLANES, w), F32),
        scratch_shapes=[pltpu.VMEM((slabs, SUBLANES, LANES), F32)] * 2,
        compiler_params=_cparams(("parallel", "parallel", "arbitrary")),
    )(tw, u4, spec_re, spec_im, twm, mod)
    return y.reshape(b, n, w)


def _shift_rows(x, n):
    row = lax.broadcasted_iota(jnp.int32, x.shape, 0)
    prev = jnp.where(row == 0, 0.0, pltpu.roll(x, 1, 0))
    nxt = jnp.where(row == n - 1, 0.0, pltpu.roll(x, n - 1, 0))
    return prev, nxt


def _hy_short_kernel(z_ref, w_ref, b_ref, o_ref, *, n):
    x = z_ref[0]
    prev, nxt = _shift_rows(x, n)
    o_ref[0] = prev * w_ref[0:1, :] + x * w_ref[1:2, :] + nxt * w_ref[2:3, :] + b_ref[...]


def _hy_short(z_mix, conv_w, conv_b):
    b, n, _ = z_mix.shape
    c3 = 3 * HY_WIDTH
    w = conv_w.reshape(3, c3).astype(F32)
    tc = 256
    col0 = BRANCH_W // tc
    return pl.pallas_call(
        functools.partial(_hy_short_kernel, n=n),
        grid=(b, c3 // tc),
        in_specs=[pl.BlockSpec((1, n, tc), lambda bi, j: (bi, 0, col0 + j)),
                  pl.BlockSpec((3, tc), lambda bi, j: (0, j)),
                  pl.BlockSpec((1, tc), lambda bi, j: (0, j))],
        out_specs=pl.BlockSpec((1, n, tc), lambda bi, j: (bi, 0, j)),
        out_shape=jax.ShapeDtypeStruct((b, n, c3), F32),
        compiler_params=_cparams(("parallel", "parallel")),
    )(z_mix, w, conv_b.astype(F32)[None])


def _hy_gate_kernel(g_ref, y_ref, v_ref, b_ref, o_ref):
    v = v_ref[0]
    o_ref[0] = g_ref[0] * (y_ref[0] + b_ref[...] * v)


def _hy_gate(zc, gate_block, y, v, v_block, bias):
    b, n, w = y.shape
    tn = min(n, 1024)
    return pl.pallas_call(
        _hy_gate_kernel,
        grid=(b, n // tn),
        in_specs=[pl.BlockSpec((1, tn, w), lambda bi, i: (bi, i, gate_block)),
                  pl.BlockSpec((1, tn, w), lambda bi, i: (bi, i, 0)),
                  pl.BlockSpec((1, tn, w), lambda bi, i: (bi, i, v_block)),
                  pl.BlockSpec((1, w), lambda bi, i: (0, 0))],
        out_specs=pl.BlockSpec((1, tn, w), lambda bi, i: (bi, i, 0)),
        out_shape=jax.ShapeDtypeStruct((b, n, w), F32),
        compiler_params=_cparams(("parallel", "parallel")),
    )(zc, y, v, bias.astype(F32)[None])


def _hyena(z_mix, conv_w, conv_b, spec_re, spec_im, bias):
    zc = _hy_short(z_mix, conv_w, conv_b)
    y1 = _hy_conv(zc[:, :, :HY_WIDTH], spec_re, spec_im, 0)
    v1 = _hy_gate(zc, 1, y1, zc, 0, bias[0])
    y2 = _hy_conv(v1, spec_re, spec_im, 1)
    return _hy_gate(zc, 2, y2, v1, 0, bias[1])


NA_QROWS = 8
NA_KROWS = 16
NA_KPART = 256


def _softmax_parts(parts):
    m = None
    for s in parts:
        mm = jnp.max(s, axis=-1, keepdims=True)
        m = mm if m is None else jnp.maximum(m, mm)
    return m


def _na_kernel(q_ref, k0, k1, k2, k3, v0, v1, v2, v3, kc_ref, vc_ref, bias_ref, o_ref):
    q = q_ref[0] * np.float32(NA_HEAD_DIM ** -0.5)
    k = jnp.concatenate([k0[0], k1[0], k2[0], k3[0]], axis=0).astype(BF16)
    v = jnp.concatenate([v0[0], v1[0], v2[0], v3[0]], axis=0).astype(BF16)
    kc = kc_ref[0].astype(BF16)
    vc = vc_ref[0].astype(BF16)
    nt = (((1,), (1,)), ((), ()))
    outs = []
    for h in range(NA_HEADS):
        sl = slice(h * NA_HEAD_DIM, (h + 1) * NA_HEAD_DIM)
        qh = q[:, sl].astype(BF16)
        s_lat = lax.dot_general(qh, k[:, sl], nt, preferred_element_type=F32) + bias_ref[0, h]
        s_ctx = lax.dot_general(qh, kc[:, sl], nt, preferred_element_type=F32)
        m = _softmax_parts([s_lat, s_ctx])
        p_lat = jnp.exp(s_lat - m)
        p_ctx = jnp.exp(s_ctx - m)
        den = jnp.sum(p_lat, axis=-1, keepdims=True) + jnp.sum(p_ctx, axis=-1, keepdims=True)
        o = jnp.dot(p_lat.astype(BF16), v[:, sl], preferred_element_type=F32) \
            + jnp.dot(p_ctx.astype(BF16), vc[:, sl], preferred_element_type=F32)
        outs.append(o / den)
    o_ref[0] = jnp.concatenate(outs, axis=-1)


def _na_bias(rpb, rows):
    kh = min(NA_WIN_H, rows)
    col = np.arange(GRID_W)
    col_start = np.clip(col - NA_WIN_W // 2, 0, GRID_W - NA_WIN_W)
    col_ok = (col[None] >= col_start[:, None]) & (col[None] < col_start[:, None] + NA_WIN_W)
    off_c = np.clip(col[None] - col[:, None], -(NA_WIN_W - 1), NA_WIN_W - 1) + (NA_WIN_W - 1)
    nblk = rows // NA_QROWS
    out = []
    for j in (0, 1, nblk - 1):
        qr = j * NA_QROWS + np.arange(NA_QROWS)
        ws = int(np.clip(j * NA_QROWS - NA_WIN_H // 2, 0, rows - NA_KROWS))
        kr = ws + np.arange(NA_KROWS)
        start = np.clip(qr - kh // 2, 0, rows - kh)
        row_ok = (kr[None] >= start[:, None]) & (kr[None] < start[:, None] + kh)
        off_r = np.clip(kr[None] - qr[:, None] + (NA_WIN_H - 1), 0, 2 * NA_WIN_H - 2)
        bias = rpb.astype(F32)[:, off_r[:, None, :, None], off_c[None, :, None, :]]
        ok = row_ok[:, None, :, None] & col_ok[None, :, None, :]
        bias = jnp.where(jnp.asarray(ok)[None], bias, NEG_INF)
        out.append(bias.reshape(NA_HEADS, NA_QROWS * GRID_W, NA_KROWS * GRID_W))
    return jnp.stack(out)


def _na_attention(z_mix, zc_mix, rpb):
    b, n, _ = z_mix.shape
    n_ctx = zc_mix.shape[1]
    rows = n // GRID_W
    nblk = rows // NA_QROWS
    tq = NA_QROWS * GRID_W
    bias = _na_bias(rpb, rows)
    qb, kb, vb = 1024 // BRANCH_W, 1024 // BRANCH_W + 1, 1024 // BRANCH_W + 2
    kparts = NA_KROWS * GRID_W // NA_KPART
    max_k0 = (rows - NA_KROWS) * GRID_W // NA_KPART

    def k0_of(j):
        return jnp.clip(2 * j - 1, 0, max_k0)

    def kspec(part, blk):
        return pl.BlockSpec((1, NA_KPART, BRANCH_W), lambda bi, j: (bi, k0_of(j) + part, blk))

    def cls(j):
        return jnp.where(j == 0, 0, jnp.where(j == nblk - 1, 2, 1))

    return pl.pallas_call(
        _na_kernel,
        grid=(b, nblk),
        in_specs=[pl.BlockSpec((1, tq, BRANCH_W), lambda bi, j: (bi, j, qb))]
        + [kspec(p, kb) for p in range(kparts)] + [kspec(p, vb) for p in range(kparts)]
        + [pl.BlockSpec((1, n_ctx, BRANCH_W), lambda bi, j: (bi, 0, kb)),
           pl.BlockSpec((1, n_ctx, BRANCH_W), lambda bi, j: (bi, 0, vb)),
           pl.BlockSpec((1, NA_HEADS, tq, NA_KROWS * GRID_W), lambda bi, j: (cls(j), 0, 0, 0))],
        out_specs=pl.BlockSpec((1, tq, BRANCH_W), lambda bi, j: (bi, j, 0)),
        out_shape=jax.ShapeDtypeStruct((b, n, BRANCH_W), F32),
        compiler_params=_cparams(("parallel", "arbitrary")),
    )(z_mix, *([z_mix] * (2 * kparts)), zc_mix, zc_mix, bias)


def _rope_tables(n_tokens, head_dim, heads):
    t = jnp.arange(n_tokens, dtype=jnp.int32)
    row = (t // GRID_W).astype(F32)
    col = (t % GRID_W).astype(F32)
    half = head_dim // 2
    inv = ROPE_BASE ** (-(jnp.arange(0, half, 2, dtype=F32) / half))
    ang = jnp.concatenate([row[:, None] * inv, col[:, None] * inv], axis=-1)
    cos, sin = jnp.cos(ang), jnp.sin(ang)
    cos_t = jnp.tile(jnp.concatenate([cos, cos], axis=-1), (1, heads))
    sin_t = jnp.tile(jnp.concatenate([-sin, sin], axis=-1), (1, heads))
    return cos_t, sin_t


def _rope(x, cos_t, sin_t):
    w = x.shape[-1]
    half = SW_HEAD_DIM // 2
    lane = lax.broadcasted_iota(jnp.int32, x.shape, 1)
    first = (lane % SW_HEAD_DIM) < half
    partner = jnp.where(first, pltpu.roll(x, w - half, 1), pltpu.roll(x, half, 1))
    return x * cos_t + partner * sin_t


def _sw_kernel(q_ref, kp_ref, kn_ref, kx_ref, vp_ref, vn_ref, vx_ref, kc_ref, vc_ref,
               cq_ref, sq_ref, cp_ref, sp_ref, cn_ref, sn_ref, cx_ref, sx_ref, sink_ref, o_ref, *, nblk):
    i = pl.program_id(1)
    kvw = SW_KV_HEADS * SW_HEAD_DIM
    q = _rope(q_ref[0], cq_ref[...], sq_ref[...]) * np.float32(SW_HEAD_DIM ** -0.5)
    k = jnp.concatenate([_rope(kp_ref[0], cp_ref[:, :kvw], sp_ref[:, :kvw]),
                         _rope(kn_ref[0], cn_ref[:, :kvw], sn_ref[:, :kvw]),
                         _rope(kx_ref[0], cx_ref[:, :kvw], sx_ref[:, :kvw])], axis=0).astype(BF16)
    v = jnp.concatenate([vp_ref[0], vn_ref[0], vx_ref[0]], axis=0).astype(BF16)
    kc = kc_ref[0].astype(BF16)
    vc = vc_ref[0].astype(BF16)
    g_ = SW_HEADS // SW_KV_HEADS
    rows = g_ * SW_BLOCK
    span = 3 * SW_BLOCK
    r = lax.broadcasted_iota(jnp.int32, (rows, span), 0) % SW_BLOCK
    c = lax.broadcasted_iota(jnp.int32, (rows, span), 1)
    diff = c - SW_BLOCK - r
    blk = i + c // SW_BLOCK - 1
    ok = (jnp.abs(diff) <= SW_WINDOW) & (blk >= 0) & (blk < nblk)
    nt = (((1,), (1,)), ((), ()))
    outs = []
    for kv in range(SW_KV_HEADS):
        ksl = slice(kv * SW_HEAD_DIM, (kv + 1) * SW_HEAD_DIM)
        qg = jnp.concatenate([q[:, (kv * g_ + g) * SW_HEAD_DIM:(kv * g_ + g + 1) * SW_HEAD_DIM] for g in range(g_)],
                             axis=0).astype(BF16)
        s_lat = jnp.where(ok, lax.dot_general(qg, k[:, ksl], nt, preferred_element_type=F32), NEG_INF)
        s_ctx = lax.dot_general(qg, kc[:, ksl], nt, preferred_element_type=F32)
        rr = lax.broadcasted_iota(jnp.int32, (rows, 1), 0)
        s_sink = jnp.zeros((rows, 1), F32)
        for g in range(g_):
            s_sink = jnp.where(rr // SW_BLOCK == g, sink_ref[kv * g_ + g], s_sink)
        m = jnp.maximum(_softmax_parts([s_lat, s_ctx]), s_sink)
        p_lat = jnp.exp(s_lat - m)
        p_ctx = jnp.exp(s_ctx - m)
        den = jnp.sum(p_lat, axis=-1, keepdims=True) + jnp.sum(p_ctx, axis=-1, keepdims=True) + jnp.exp(s_sink - m)
        o = jnp.dot(p_lat.astype(BF16), v[:, ksl], preferred_element_type=F32) \
            + jnp.dot(p_ctx.astype(BF16), vc[:, ksl], preferred_element_type=F32)
        o = o / den
        outs += [o[g * SW_BLOCK:(g + 1) * SW_BLOCK] for g in range(g_)]
    o_ref[0] = jnp.concatenate(outs, axis=-1)


def _sw_attention(z_mix, zc_mix, sink):
    b, n, _ = z_mix.shape
    n_ctx = zc_mix.shape[1]
    nblk = n // SW_BLOCK
    cos_t, sin_t = _rope_tables(n, SW_HEAD_DIM, SW_HEADS)
    qw = SW_HEADS * SW_HEAD_DIM
    kvw = SW_KV_HEADS * SW_HEAD_DIM
    q_blk = 1792 // qw
    k_blk = (1792 + qw) // kvw
    v_blk = k_blk + 1

    def prev(i):
        return jnp.maximum(i - 1, 0)

    def nxt(i):
        return jnp.minimum(i + 1, nblk - 1)

    def tok(fn, blk):
        return pl.BlockSpec((1, SW_BLOCK, kvw), lambda bi, i: (bi, fn(i), blk))

    def tab(fn):
        return pl.BlockSpec((SW_BLOCK, qw), lambda bi, i: (fn(i), 0))

    ident = lambda i: i
    return pl.pallas_call(
        functools.partial(_sw_kernel, nblk=nblk),
        grid=(b, nblk),
        in_specs=[pl.BlockSpec((1, SW_BLOCK, qw), lambda bi, i: (bi, i, q_blk)),
                  tok(prev, k_blk), tok(ident, k_blk), tok(nxt, k_blk),
                  tok(prev, v_blk), tok(ident, v_blk), tok(nxt, v_blk),
                  pl.BlockSpec((1, n_ctx, kvw), lambda bi, i: (bi, 0, k_blk)),
                  pl.BlockSpec((1, n_ctx, kvw), lambda bi, i: (bi, 0, v_blk)),
                  tab(ident), tab(ident), tab(prev), tab(prev), tab(ident), tab(ident), tab(nxt), tab(nxt),
                  pl.BlockSpec(memory_space=pltpu.SMEM)],
        out_specs=pl.BlockSpec((1, SW_BLOCK, qw), lambda bi, i: (bi, i, 0)),
        out_shape=jax.ShapeDtypeStruct((b, n, qw), F32),
        compiler_params=_cparams(("parallel", "arbitrary")),
    )(z_mix, *([z_mix] * 6), zc_mix, zc_mix, cos_t, sin_t, cos_t, sin_t, cos_t, sin_t, cos_t, sin_t,
      sink.astype(F32))


def _ctx_attn_kernel(q_ref, k_ref, v_ref, sink_ref, o_ref, *, heads, kv_heads, dh, use_sink):
    q = q_ref[0] * np.float32(dh ** -0.5)
    k = k_ref[0].astype(BF16)
    v = v_ref[0].astype(BF16)
    g_ = heads // kv_heads
    nt = (((1,), (1,)), ((), ()))
    outs = []
    for h in range(heads):
        kv = h // g_
        s = lax.dot_general(q[:, h * dh:(h + 1) * dh].astype(BF16), k[:, kv * dh:(kv + 1) * dh], nt,
                            preferred_element_type=F32)
        m = jnp.max(s, axis=-1, keepdims=True)
        if use_sink:
            m = jnp.maximum(m, sink_ref[h])
        p = jnp.exp(s - m)
        den = jnp.sum(p, axis=-1, keepdims=True)
        if use_sink:
            den = den + jnp.exp(sink_ref[h] - m)
        outs.append(jnp.dot(p.astype(BF16), v[:, kv * dh:(kv + 1) * dh], preferred_element_type=F32) / den)
    o_ref[0] = jnp.concatenate(outs, axis=-1)


def _ctx_attention(zc_mix, col0, heads, kv_heads, dh, sink):
    b, n, _ = zc_mix.shape
    qw, kvw = heads * dh, kv_heads * dh
    use_sink = sink is not None
    sink_arr = sink.astype(F32) if use_sink else jnp.zeros((heads,), F32)
    return pl.pallas_call(
        functools.partial(_ctx_attn_kernel, heads=heads, kv_heads=kv_heads, dh=dh, use_sink=use_sink),
        grid=(b,),
        in_specs=[pl.BlockSpec((1, n, qw), lambda bi: (bi, 0, col0 // qw)),
                  pl.BlockSpec((1, n, kvw), lambda bi: (bi, 0, (col0 + qw) // kvw)),
                  pl.BlockSpec((1, n, kvw), lambda bi: (bi, 0, (col0 + qw) // kvw + 1)),
                  pl.BlockSpec(memory_space=pltpu.SMEM)],
        out_specs=pl.BlockSpec((1, n, qw), lambda bi: (bi, 0, 0)),
        out_shape=jax.ShapeDtypeStruct((b, n, qw), F32),
        compiler_params=_cparams(("parallel",)),
    )(zc_mix, zc_mix, zc_mix, sink_arr)


def _merge_kernel(x_ref, s5_ref, hy_ref, na_ref, sw_ref, gt_ref, ga_ref, wglu_ref, wb_ref, wo_ref,
                  lg_ref, lb_ref, o_ref):
    g = jax.nn.gelu(s5_ref[0])
    s5 = g * jax.nn.sigmoid(jnp.dot(g.astype(BF16), wglu_ref[...], preferred_element_type=F32))
    branches = (s5, hy_ref[0], na_ref[0], sw_ref[0])
    acc = None
    for n in range(N_BRANCH):
        proj = jnp.dot(branches[n].astype(BF16), wb_ref[n], preferred_element_type=F32)
        t = gt_ref[0, :, n * D_MODEL:(n + 1) * D_MODEL] * proj
        acc = t if acc is None else acc + t
    mix = jnp.dot(acc.astype(BF16), wo_ref[...], preferred_element_type=F32)
    y = np.float32(DEEPNORM_ALPHA) * x_ref[0] + ga_ref[0] * mix
    o_ref[0] = _layernorm(y) * lg_ref[...] + lb_ref[...]


def _merge(x, s5y, hy, na, sw, gates, g_a, w_glu, w_branch, w_out, ln_g, ln_b, *, tm):
    b, l, d = x.shape
    br = pl.BlockSpec((1, tm, BRANCH_W), lambda bi, i: (bi, i, 0))
    full = lambda s: pl.BlockSpec(s, lambda bi, i: (0,) * len(s))
    return pl.pallas_call(
        _merge_kernel,
        grid=(b, l // tm),
        in_specs=[pl.BlockSpec((1, tm, d), lambda bi, i: (bi, i, 0)), br, br, br, br,
                  pl.BlockSpec((1, tm, GATE_W), lambda bi, i: (bi, i, 0)),
                  pl.BlockSpec((1, 1, d), lambda bi, i: (bi, 0, 0)),
                  full((BRANCH_W, BRANCH_W)), full((N_BRANCH, BRANCH_W, d)), full((d, d)),
                  full((1, d)), full((1, d))],
        out_specs=pl.BlockSpec((1, tm, d), lambda bi, i: (bi, i, 0)),
        out_shape=jax.ShapeDtypeStruct((b, l, d), F32),
        compiler_params=_cparams(("parallel", "parallel")),
    )(x, s5y, hy, na, sw, gates, g_a, w_glu, w_branch, w_out, ln_g, ln_b)


def _mlp_kernel(x_ref, sh_ref, sc_ref, gm_ref, w1_ref, w2_ref, lg_ref, lb_ref, o_ref, h_scr, acc_scr):
    j = pl.program_id(2)

    @pl.when(j == 0)
    def _():
        h = _layernorm(x_ref[0]) * (1.0 + sc_ref[0]) + sh_ref[0]
        h_scr[...] = h.astype(BF16)
        acc_scr[...] = jnp.zeros_like(acc_scr)

    a = jnp.dot(h_scr[...], w1_ref[...], preferred_element_type=F32)
    a = jnp.square(jnp.maximum(a, 0.0))
    acc_scr[...] += jnp.dot(a.astype(BF16), w2_ref[...], preferred_element_type=F32)

    @pl.when(j == pl.num_programs(2) - 1)
    def _():
        y = np.float32(DEEPNORM_ALPHA) * x_ref[0] + gm_ref[0] * acc_scr[...]
        o_ref[0] = _layernorm(y) * lg_ref[...] + lb_ref[...]


def _mlp(x, sh, sc, g_m, w1, w2, ln_g, ln_b, *, tm, th):
    b, l, d = x.shape
    hdim = w1.shape[1]
    mod = pl.BlockSpec((1, 1, d), lambda bi, i, j: (bi, 0, 0))
    vec = pl.BlockSpec((1, d), lambda bi, i, j: (0, 0))
    return pl.pallas_call(
        _mlp_kernel,
        grid=(b, l // tm, hdim // th),
        in_specs=[pl.BlockSpec((1, tm, d), lambda bi, i, j: (bi, i, 0)), mod, mod, mod,
                  pl.BlockSpec((d, th), lambda bi, i, j: (0, j)),
                  pl.BlockSpec((th, d), lambda bi, i, j: (j, 0)), vec, vec],
        out_specs=pl.BlockSpec((1, tm, d), lambda bi, i, j: (bi, i, 0)),
        out_shape=jax.ShapeDtypeStruct((b, l, d), F32),
        scratch_shapes=[pltpu.VMEM((tm, d), BF16), pltpu.VMEM((tm, d), F32)],
        compiler_params=_cparams(("parallel", "parallel", "arbitrary")),
    )(x, sh, sc, g_m, w1, w2, ln_g, ln_b)


def kernel(x, c, ctx, c_ctx, w_ada, b_ada, w_in, s5_lambda_re, s5_lambda_im, s5_log_dt, s5_b_re, s5_b_im, s5_c_re,
           s5_c_im, s5_d, s5_w_glu, hy_conv_w, hy_conv_b, hy_freq, hy_w1, hy_b1, hy_w2, hy_b2, hy_w3, hy_bias,
           na_rpb, sw_sink, w_branch, w_out, ln1_g, ln1_b, w_mlp1, w_mlp2, ln2_g, ln2_b):
    b, l, d = x.shape
    n_ctx = ctx.shape[1]
    depth = w_ada.shape[0]
    cc = jnp.zeros((8, d), F32).at[:b].set(c.astype(F32)).at[b].set(c_ctx.astype(F32))
    mod_all = _ada(cc, w_ada.astype(F32), b_ada.astype(F32))
    xc = ctx
    for layer in range(depth):
        need_ctx_out = layer < depth - 1
        mod = mod_all[layer, :b].reshape(b, 1, 6, d)
        mod_c = jnp.broadcast_to(mod_all[layer, b].reshape(1, 1, 6, d), (b, 1, 6, d))
        sh_a, sc_a, g_a, sh_m, sc_m, g_m = [mod[:, :, i] for i in range(6)]
        csh_a, csc_a, cg_a, csh_m, csc_m, cg_m = [mod_c[:, :, i] for i in range(6)]
        w_in_l = w_in[layer].astype(BF16)
        w_mix, w_gate = w_in_l[:, :MIX_W], w_in_l[:, MIX_W:]

        z_mix = _ln_mod_mm(x, sh_a, sc_a, w_mix, tm=1024, tn=MIX_W // 3)
        gates = _ln_mod_mm(x, sh_a, sc_a, w_gate, tm=1024, tn=1024, act="sigmoid")
        zc_mix = _ln_mod_mm(xc, csh_a, csc_a, w_mix, tm=n_ctx, tn=MIX_W // 3)

        ops = _s5_operators(s5_lambda_re[layer], s5_lambda_im[layer], s5_log_dt[layer], s5_b_re[layer],
                            s5_b_im[layer], s5_c_re[layer], s5_c_im[layer], s5_d[layer])
        yc_s5, e_ctx = _s5_mix(zc_mix[:, :, :BRANCH_W], ops, [jnp.zeros((b, 1, S5_STATE_W // 4), F32)] * 4)
        y_s5, _ = _s5_mix(z_mix[:, :, :BRANCH_W], ops, e_ctx)

        hy_args = (hy_freq[layer], hy_w1[layer], hy_b1[layer], hy_w2[layer], hy_b2[layer], hy_w3[layer])
        sp_re, sp_im = _hy_spectra(_hy_filters(l, *hy_args), l)
        hy_l = _hyena(z_mix, hy_conv_w[layer], hy_conv_b[layer], sp_re, sp_im, hy_bias[layer])

        na_l = _na_attention(z_mix, zc_mix, na_rpb[layer])
        sw_l = _sw_attention(z_mix, zc_mix, sw_sink[layer])

        w_glu = s5_w_glu[layer].astype(BF16)
        w_br = w_branch[layer].astype(BF16)
        w_o = w_out[layer].astype(BF16)
        lg1, lb1 = ln1_g[layer].astype(F32)[None], ln1_b[layer].astype(F32)[None]
        lg2, lb2 = ln2_g[layer].astype(F32)[None], ln2_b[layer].astype(F32)[None]
        w1 = w_mlp1[layer].astype(BF16)
        w2 = w_mlp2[layer].astype(BF16)

        x_new = _merge(x, y_s5, hy_l, na_l, sw_l, gates, g_a, w_glu, w_br, w_o, lg1, lb1, tm=512)
        x_new = _mlp(x_new, sh_m, sc_m, g_m, w1, w2, lg2, lb2, tm=1024, th=1024)

        if need_ctx_out:
            gates_c = _ln_mod_mm(xc, csh_a, csc_a, w_gate, tm=n_ctx, tn=1024, act="sigmoid")
            spc_re, spc_im = _hy_spectra(_hy_filters(n_ctx, *hy_args), n_ctx)
            hy_c = _hyena(zc_mix, hy_conv_w[layer], hy_conv_b[layer], spc_re, spc_im, hy_bias[layer])
            na_c = _ctx_attention(zc_mix, 1024, NA_HEADS, NA_HEADS, NA_HEAD_DIM, None)
            sw_c = _ctx_attention(zc_mix, 1792, SW_HEADS, SW_KV_HEADS, SW_HEAD_DIM, sw_sink[layer])
            xc_new = _merge(xc, yc_s5, hy_c, na_c, sw_c, gates_c, cg_a, w_glu, w_br, w_o, lg1, lb1, tm=n_ctx)
            xc = _mlp(xc_new, csh_m, csc_m, cg_m, w1, w2, lg2, lb2, tm=n_ctx, th=1024)
        x = x_new
    return x
```

```python
import functools
import math

import numpy as np
import jax
import jax.numpy as jnp
from jax import lax
from jax.experimental import pallas as pl
from jax.experimental.pallas import tpu as pltpu

F32 = jnp.float32
BF16 = jnp.bfloat16

D_MODEL = 1024
GRID_W = 64
BRANCH_W = 256
N_BRANCH = 4
S5_GROUP = 16
S5_GROUPS = 16
S5_STATE = 64
HY_WIDTH = 256
HY_ORDER = 2
HY_BANDS = 16
HY_EMB = 2 * HY_BANDS + 1
HY_FFN = 64
HY_MIN_DECAY = math.log(1e-2) / 1.5
HY_MAX_DECAY = math.log(1e-2) / 0.3
NA_HEADS = 4
NA_HEAD_DIM = 64
NA_WIN_H = 8
NA_WIN_W = 16
SW_HEADS = 4
SW_KV_HEADS = 2
SW_HEAD_DIM = 64
SW_WINDOW = 128
SW_BLOCK = 128
MLP_HIDDEN = 4 * D_MODEL
ROPE_BASE = 10000.0
LN_EPS = 1e-6
NEG_INF = -1e30
DEPTH = 2
DEEPNORM_ALPHA = (2 * DEPTH) ** 0.25

MIX_W = 2304
GATE_W = N_BRANCH * D_MODEL
S5_T = 8
S5_STATE_W = 4 * S5_GROUPS * S5_STATE
SUBLANES = 8
LANES = 128
VMEM_LIMIT = 56 * 1024 * 1024

HI = lax.Precision.HIGHEST


def _cparams(sem):
    return pltpu.CompilerParams(dimension_semantics=sem, vmem_limit_bytes=VMEM_LIMIT)


def _layernorm(x):
    mu = jnp.mean(x, axis=-1, keepdims=True)
    xc = x - mu
    var = jnp.mean(xc * xc, axis=-1, keepdims=True)
    return xc * lax.rsqrt(var + LN_EPS)


def _ada_kernel(c_ref, w_ref, b_ref, o_ref):
    c = c_ref[...]
    a = c * jax.nn.sigmoid(c)
    o_ref[0] = jnp.dot(a, w_ref[0], preferred_element_type=F32, precision=HI) + b_ref[0]


def _ada(cc, w_ada, b_ada):
    depth, d, n = w_ada.shape
    tn = 512
    return pl.pallas_call(
        _ada_kernel,
        grid=(depth, n // tn),
        in_specs=[pl.BlockSpec((8, d), lambda l, j: (0, 0)),
                  pl.BlockSpec((1, d, tn), lambda l, j: (l, 0, j)),
                  pl.BlockSpec((1, 1, tn), lambda l, j: (l, 0, j))],
        out_specs=pl.BlockSpec((1, 8, tn), lambda l, j: (l, 0, j)),
        out_shape=jax.ShapeDtypeStruct((depth, 8, n), F32),
        compiler_params=_cparams(("parallel", "parallel")), name="ada",
    )(cc, w_ada, b_ada.reshape(depth, 1, n))


def _ln_mod_mm_kernel(x_ref, sh_ref, sc_ref, w_ref, o_ref, h_scr, *, act):
    @pl.when(pl.program_id(2) == 0)
    def _():
        h = _layernorm(x_ref[0]) * (1.0 + sc_ref[0]) + sh_ref[0]
        h_scr[...] = h.astype(BF16)

    r = jnp.dot(h_scr[...], w_ref[...], preferred_element_type=F32)
    if act == "sigmoid":
        r = jax.nn.sigmoid(r)
    o_ref[0] = r


def _ln_mod_mm(x, sh, sc, w, *, tm, tn, act=None):
    b, l, d = x.shape
    n = w.shape[1]
    return pl.pallas_call(
        functools.partial(_ln_mod_mm_kernel, act=act),
        grid=(b, l // tm, n // tn),
        in_specs=[pl.BlockSpec((1, tm, d), lambda bi, i, j: (bi, i, 0)),
                  pl.BlockSpec((1, 1, d), lambda bi, i, j: (bi, 0, 0)),
                  pl.BlockSpec((1, 1, d), lambda bi, i, j: (bi, 0, 0)),
                  pl.BlockSpec((d, tn), lambda bi, i, j: (0, j))],
        out_specs=pl.BlockSpec((1, tm, tn), lambda bi, i, j: (bi, i, j)),
        out_shape=jax.ShapeDtypeStruct((b, l, n), F32),
        scratch_shapes=[pltpu.VMEM((tm, d), BF16)],
        compiler_params=_cparams(("parallel", "parallel", "arbitrary")), name="ln_mod_mm",
    )(x, sh, sc, w)


def _mm_sum_kernel(*refs, n_pairs):
    o_ref = refs[-1]
    acc = None
    for p in range(n_pairs):
        t = jnp.dot(refs[p][...].astype(BF16), refs[n_pairs + p][...], preferred_element_type=F32)
        acc = t if acc is None else acc + t
    o_ref[...] = acc


def _mm_sum(a_list, b_list, *, tm, tn):
    m = a_list[0].shape[0]
    n = b_list[0].shape[1]
    n_pairs = len(a_list)
    in_specs = [pl.BlockSpec((tm, a.shape[1]), lambda i, j: (i, 0)) for a in a_list]
    in_specs += [pl.BlockSpec((bm.shape[0], tn), lambda i, j: (0, j)) for bm in b_list]
    return pl.pallas_call(
        functools.partial(_mm_sum_kernel, n_pairs=n_pairs),
        grid=(m // tm, n // tn),
        in_specs=in_specs,
        out_specs=pl.BlockSpec((tm, tn), lambda i, j: (i, j)),
        out_shape=jax.ShapeDtypeStruct((m, n), F32),
        compiler_params=_cparams(("parallel", "parallel")), name="s5_mm",
    )(*a_list, *b_list)


def _s5_scan_kernel(gfr, gfi, gbr, gbi, afr, afi, abr, abi, h0fr, h0fi, h0br, h0bi,
                    hfr, hfi, hbr, hbi, efr, efi, ebr, ebi, *, n_chunks):
    a_fr = afr[...][None]
    a_fi = afi[...][None]
    a_br = abr[...][None]
    a_bi = abi[...][None]

    def body(k, carry):
        sfr, sfi, sbr, sbi = carry
        kb = n_chunks - 1 - k
        hfr[:, pl.ds(k, 1), :] = sfr
        hfi[:, pl.ds(k, 1), :] = sfi
        hbr[:, pl.ds(kb, 1), :] = sbr
        hbi[:, pl.ds(kb, 1), :] = sbi
        nfr = a_fr * sfr - a_fi * sfi + gfr[:, pl.ds(k, 1), :]
        nfi = a_fr * sfi + a_fi * sfr + gfi[:, pl.ds(k, 1), :]
        nbr = a_br * sbr - a_bi * sbi + gbr[:, pl.ds(kb, 1), :]
        nbi = a_br * sbi + a_bi * sbr + gbi[:, pl.ds(kb, 1), :]
        return nfr, nfi, nbr, nbi

    sfr, sfi, sbr, sbi = lax.fori_loop(0, n_chunks, body, (h0fr[...], h0fi[...], h0br[...], h0bi[...]))
    efr[...] = sfr
    efi[...] = sfi
    ebr[...] = sbr
    ebi[...] = sbi


def _s5_scan(g, a_t, h0):
    b, k, w4 = g.shape
    w = LANES
    q = w4 // 4
    nb = q // w

    def comp(c):
        return pl.BlockSpec((b, k, w), lambda j, c=c: (0, 0, c * nb + j))

    def comp_a(c):
        return pl.BlockSpec((1, w), lambda j, c=c: (0, c * nb + j))

    state = pl.BlockSpec((b, k, w), lambda j: (0, 0, j))
    edge = pl.BlockSpec((b, 1, w), lambda j: (0, 0, j))
    outs = pl.pallas_call(
        functools.partial(_s5_scan_kernel, n_chunks=k),
        grid=(nb,),
        in_specs=[comp(c) for c in range(4)] + [comp_a(c) for c in range(4)] + [edge] * 4,
        out_specs=[state] * 4 + [edge] * 4,
        out_shape=[jax.ShapeDtypeStruct((b, k, q), F32)] * 4 + [jax.ShapeDtypeStruct((b, 1, q), F32)] * 4,
        compiler_params=_cparams(("parallel",)), name="s5_scan",
    )(g, g, g, g, a_t, a_t, a_t, a_t, *h0)
    return outs[:4], outs[4:]


def _cmul(ar, ai, br, bi):
    return ar * br - ai * bi, ar * bi + ai * br


def _s5_operators(lam_re, lam_im, log_dt, b_re, b_im, c_re, c_im, d):
    t_ = S5_T
    g_, p_, w_ = S5_GROUPS, S5_STATE, S5_GROUP * S5_GROUPS
    eye = jnp.eye(g_, dtype=F32)
    ii = np.arange(t_)[None, :]
    jj = np.arange(t_)[:, None]
    m_tot = None
    q_parts, n_parts, a_parts = [], [], []
    for direction in range(2):
        lr = lam_re[direction].astype(F32)
        li = lam_im[direction].astype(F32)
        dt = jnp.exp(log_dt[direction].astype(F32))[:, None]
        ks = jnp.arange(t_ + 1, dtype=F32)[:, None, None]
        mag = jnp.exp(ks * lr * dt)
        pw_re = mag * jnp.cos(ks * li * dt)
        pw_im = mag * jnp.sin(ks * li * dt)
        a_re, a_im = pw_re[1], pw_im[1]
        den = lr ** 2 + li ** 2
        f_re = ((a_re - 1.0) * lr + a_im * li) / den
        f_im = (a_im * lr - (a_re - 1.0) * li) / den
        br = b_re[direction].astype(F32)
        bi = b_im[direction].astype(F32)
        bb_re = f_re[..., None] * br - f_im[..., None] * bi
        bb_im = f_re[..., None] * bi + f_im[..., None] * br
        cr = c_re[direction].astype(F32)
        ci = c_im[direction].astype(F32)
        wb_re, wb_im = _cmul(pw_re[..., None], pw_im[..., None], bb_re[None], bb_im[None])
        kern = jnp.einsum('gnp,kgpm->kgnm', cr, wb_re[:t_], precision=HI) \
            - jnp.einsum('gnp,kgpm->kgnm', ci, wb_im[:t_], precision=HI)
        kbd = jnp.einsum('kgnm,gh->kgmhn', kern, eye).reshape(t_, w_, w_)
        if direction == 0:
            sel = np.clip(ii - jj, 0, t_ - 1)
            ok = (ii >= jj)
        else:
            sel = np.clip(jj - ii, 0, t_ - 1)
            ok = (jj >= ii)
        m4 = kbd[sel] * jnp.asarray(ok, F32)[:, :, None, None]
        m4 = m4.transpose(0, 2, 1, 3).reshape(t_ * w_, t_ * w_)
        m_tot = m4 if m_tot is None else m_tot + m4
        e_q = (t_ - 1 - np.arange(t_)) if direction == 0 else np.arange(t_)
        qd_re = jnp.einsum('jgpm,gh->jhmgp', wb_re[e_q], eye).reshape(t_ * w_, g_ * p_)
        qd_im = jnp.einsum('jgpm,gh->jhmgp', wb_im[e_q], eye).reshape(t_ * w_, g_ * p_)
        q_parts += [qd_re, qd_im]
        e_n = (np.arange(t_) + 1) if direction == 0 else (t_ - np.arange(t_))
        ca_re, ca_im = _cmul(cr[None], ci[None], pw_re[e_n][:, :, None, :], pw_im[e_n][:, :, None, :])
        nd_re = jnp.einsum('ignp,gh->gpihn', ca_re, eye).reshape(g_ * p_, t_ * w_)
        nd_im = jnp.einsum('ignp,gh->gpihn', ca_im, eye).reshape(g_ * p_, t_ * w_)
        n_parts += [nd_re, -nd_im]
        a_parts += [pw_re[t_].reshape(1, g_ * p_), pw_im[t_].reshape(1, g_ * p_)]
    dd = jnp.tile(d.astype(F32), t_)
    m_tot = m_tot + jnp.diag(dd)
    return (m_tot.astype(BF16), jnp.concatenate(q_parts, axis=1).astype(BF16),
            [n.astype(BF16) for n in n_parts], jnp.concatenate(a_parts, axis=1))


def _s5_mix(u, ops, h0):
    m_tot, q_all, n_parts, a_t = ops
    b, n, w = u.shape
    k = n // S5_T
    rows = b * k
    uu = u.reshape(rows, S5_T * w)
    tm = min(rows, 256)
    g = _mm_sum([uu], [q_all], tm=tm, tn=1024)
    h, e = _s5_scan(g.reshape(b, k, S5_STATE_W), a_t, h0)
    y = _mm_sum([uu] + [hc.reshape(rows, -1) for hc in h], [m_tot] + n_parts, tm=tm, tn=512)
    return y.reshape(b, n, w), e


def _hy_filter_kernel(f_ref, w1_ref, b1_ref, w2_ref, b2_ref, w3_ref, fr_ref, dl_ref, o_ref, *, half_len, tile):
    feats = f_ref[...]
    h = jnp.dot(feats, w1_ref[...], preferred_element_type=F32, precision=HI) + b1_ref[...]
    h = jnp.sin(fr_ref[0:1, :] * h)
    h = jnp.dot(h, w2_ref[...], preferred_element_type=F32, precision=HI) + b2_ref[...]
    h = jnp.sin(fr_ref[1:2, :] * h)
    o = jnp.dot(h, w3_ref[...], preferred_element_type=F32, precision=HI)
    t = feats[:, 0:1]
    o = o * jnp.exp(-t * dl_ref[...])
    nw = HY_ORDER * HY_WIDTH
    n = pl.program_id(0) * tile + lax.broadcasted_iota(jnp.int32, (tile, 1), 0)
    o_ref[...] = jnp.where(n < half_len, o[:, :nw], jnp.where(n == half_len, 0.0, o[:, nw:]))


def _hy_filters(n_tokens, freq, w1, b1, w2, b2, w3):
    n2 = 2 * n_tokens
    idx = jnp.arange(n2, dtype=F32)
    pos = jnp.where(idx <= n_tokens, idx, n2 - idx)[:, None]
    t = pos / max(n_tokens - 1, 1)
    bands = jnp.linspace(1e-4, HY_BANDS - 1, HY_BANDS, dtype=F32)[None]
    ang = 2.0 * math.pi * bands * pos / n_tokens
    feats = jnp.concatenate([t, jnp.cos(ang), -jnp.sin(ang)], axis=-1)
    kpad = LANES - HY_EMB
    feats = jnp.pad(feats, ((0, 0), (0, kpad)))
    w1p = jnp.pad(w1.astype(F32), ((0, kpad), (0, 0)))
    deltas = jnp.abs(jnp.linspace(HY_MIN_DECAY, HY_MAX_DECAY, HY_WIDTH, dtype=F32))
    dl = jnp.tile(deltas, 2 * HY_ORDER)[None]
    tile = min(n2, 1024)
    nw = 2 * HY_ORDER * HY_WIDTH
    full = lambda s: pl.BlockSpec(s, lambda i: (0,) * len(s))
    return pl.pallas_call(
        functools.partial(_hy_filter_kernel, half_len=n_tokens, tile=tile),
        grid=(n2 // tile,),
        in_specs=[pl.BlockSpec((tile, LANES), lambda i: (i, 0)),
                  full((LANES, HY_FFN)), full((1, HY_FFN)), full((HY_FFN, HY_FFN)), full((1, HY_FFN)),
                  full((HY_FFN, nw)), full((2, HY_FFN)), full((1, nw))],
        out_specs=pl.BlockSpec((tile, HY_ORDER * HY_WIDTH), lambda i: (i, 0)),
        out_shape=jax.ShapeDtypeStruct((n2, HY_ORDER * HY_WIDTH), F32),
        compiler_params=_cparams(("parallel",)), name="hy_filter",
    )(feats, w1p, b1.astype(F32)[None], w2.astype(F32), b2.astype(F32)[None], w3.astype(F32),
      freq.astype(F32), dl)


FFT_G = 4


def _fft_tables(n):
    s = n // SUBLANES
    nst = int(round(math.log2(s)))
    m = np.arange(max(s // 2, 1))
    tw_slab = np.stack([np.cos(-2 * np.pi * m / s), np.sin(-2 * np.pi * m / s)]).astype(np.float32)
    pos = np.arange(s)
    rev = np.zeros(s, np.int64)
    for bit in range(nst):
        rev |= ((pos >> bit) & 1) << (nst - 1 - bit)
    ang = -2 * np.pi * (rev[:, None] * np.arange(SUBLANES)[None, :]) / n
    tw_mid = np.stack([np.cos(ang), np.sin(ang)]).astype(np.float32)
    tw_mid = np.broadcast_to(tw_mid[..., None], (2, s, SUBLANES, LANES)).copy()
    t = np.arange(n)
    ang2 = -2 * np.pi * t / (2 * n)
    mod = np.stack([np.cos(ang2), np.sin(ang2)]).astype(np.float32).reshape(2, s, SUBLANES)
    mod = np.broadcast_to(mod[..., None], (2, s, SUBLANES, LANES)).copy()
    return jnp.asarray(tw_slab), jnp.asarray(tw_mid), jnp.asarray(mod)


def _sub_patterns():
    sub = lax.broadcasted_iota(jnp.int32, (SUBLANES, LANES), 0)

    def table(vals):
        out = jnp.full((SUBLANES, LANES), vals[0], F32)
        for k in range(1, SUBLANES):
            out = jnp.where(sub == k, np.float32(vals[k]), out)
        return out

    pats = {}
    for dist in (4, 2, 1):
        lo = (sub & dist) == 0
        sgn = jnp.where(lo, 1.0, -1.0).astype(F32)
        wr = [1.0] * SUBLANES
        wi = [0.0] * SUBLANES
        for k in range(SUBLANES):
            if k & dist:
                e = (k % dist) * (SUBLANES // (2 * dist))
                wr[k] = math.cos(-2 * math.pi * e / SUBLANES)
                wi[k] = math.sin(-2 * math.pi * e / SUBLANES)
        pats[dist] = (lo, sgn, table(wr), table(wi))
    return pats


def _dft8_fwd(vr, vi, pats):
    for dist in (4, 2, 1):
        lo, sgn, wr, wi = pats[dist]
        up_r = pltpu.roll(vr, SUBLANES - dist, 0)
        up_i = pltpu.roll(vi, SUBLANES - dist, 0)
        if dist == 4:
            pr, pi_ = up_r, up_i
        else:
            pr = jnp.where(lo, up_r, pltpu.roll(vr, dist, 0))
            pi_ = jnp.where(lo, up_i, pltpu.roll(vi, dist, 0))
        tr = pr + sgn * vr
        ti = pi_ + sgn * vi
        if dist == 1:
            vr, vi = tr, ti
        else:
            vr, vi = _cmul(tr, ti, wr, wi)
    return vr, vi


def _dft8_inv(vr, vi, pats):
    for dist in (1, 2, 4):
        lo, sgn, wr, wi = pats[dist]
        if dist != 1:
            vr, vi = _cmul(vr, vi, wr, -wi)
        up_r = pltpu.roll(vr, SUBLANES - dist, 0)
        up_i = pltpu.roll(vi, SUBLANES - dist, 0)
        if dist == 4:
            pr, pi_ = up_r, up_i
        else:
            pr = jnp.where(lo, up_r, pltpu.roll(vr, dist, 0))
            pi_ = jnp.where(lo, up_i, pltpu.roll(vi, dist, 0))
        vr = pr + sgn * vr
        vi = pi_ + sgn * vi
    return vr, vi


def _slab_stage(re, im, tw, half, slabs, inverse):
    per_block = half // FFT_G
    step = slabs // (2 * half)

    def body(c, carry):
        blk = c // per_block
        jc = c % per_block
        i0 = blk * (2 * half) + jc * FFT_G
        i1 = i0 + half
        for g in range(FFT_G):
            wr = tw[0, (jc * FFT_G + g) * step]
            wi = tw[1, (jc * FFT_G + g) * step]
            ar, ai = re[i0 + g], im[i0 + g]
            br, bi = re[i1 + g], im[i1 + g]
            if inverse:
                br, bi = br * wr + bi * wi, bi * wr - br * wi
                re[i0 + g] = ar + br
                im[i0 + g] = ai + bi
                re[i1 + g] = ar - br
                im[i1 + g] = ai - bi
            else:
                re[i0 + g] = ar + br
                im[i0 + g] = ai + bi
                dr, di = ar - br, ai - bi
                re[i1 + g] = dr * wr - di * wi
                im[i1 + g] = dr * wi + di * wr
        return carry

    lax.fori_loop(0, slabs // (2 * FFT_G), body, 0)


def _fft_forward_big(re, im, tw, slabs):
    half = slabs // 2
    while half >= 4:
        _slab_stage(re, im, tw, half, slabs, inverse=False)
        half //= 2


def _fft_inverse_big(re, im, tw, slabs):
    half = 4
    while half <= slabs // 2:
        _slab_stage(re, im, tw, half, slabs, inverse=True)
        half *= 2


def _radix4_fwd(x):
    (x0r, x0i), (x1r, x1i), (x2r, x2i), (x3r, x3i) = x
    y0r, y0i = x0r + x2r, x0i + x2i
    y2r, y2i = x0r - x2r, x0i - x2i
    y1r, y1i = x1r + x3r, x1i + x3i
    dr, di = x1r - x3r, x1i - x3i
    y3r, y3i = di, -dr
    return [(y0r + y1r, y0i + y1i), (y0r - y1r, y0i - y1i), (y2r + y3r, y2i + y3i), (y2r - y3r, y2i - y3i)]


def _radix4_inv(z):
    (z0r, z0i), (z1r, z1i), (z2r, z2i), (z3r, z3i) = z
    y0r, y0i = z0r + z1r, z0i + z1i
    y1r, y1i = z0r - z1r, z0i - z1i
    y2r, y2i = z2r + z3r, z2i + z3i
    y3r, y3i = z2r - z3r, z2i - z3i
    qr, qi = -y3i, y3r
    return [(y0r + y2r, y0i + y2i), (y1r + qr, y1i + qi), (y0r - y2r, y0i - y2i), (y1r - qr, y1i - qi)]


def _fft_middle(re, im, twm_ref, slabs, pats, spec=None, out=None):
    def body(q, carry):
        p0 = q * FFT_G
        x = [(re[p0 + g], im[p0 + g]) for g in range(FFT_G)]
        z = _radix4_fwd(x)
        res = []
        for g in range(FFT_G):
            twr = twm_ref[0, p0 + g]
            twi = twm_ref[1, p0 + g]
            vr, vi = _cmul(z[g][0], z[g][1], twr, twi)
            vr, vi = _dft8_fwd(vr, vi, pats)
            if spec is None:
                out[0][0, 0, p0 + g] = vr * out[2]
                out[1][0, 0, p0 + g] = vi * out[2]
            else:
                vr, vi = _cmul(vr, vi, spec[0][0, 0, p0 + g], spec[1][0, 0, p0 + g])
                vr, vi = _dft8_inv(vr, vi, pats)
                res.append(_cmul(vr, vi, twr, -twi))
        if spec is not None:
            x = _radix4_inv(res)
            for g in range(FFT_G):
                re[p0 + g] = x[g][0]
                im[p0 + g] = x[g][1]
        return carry

    lax.fori_loop(0, slabs // FFT_G, body, 0)


def _hy_spec_kernel(tw_ref, k_ref, twm_ref, mod_ref, sr_ref, si_ref, re, im, *, slabs):
    h = pl.program_id(2)
    pats = _sub_patterns()
    a = k_ref[0:slabs]
    b = k_ref[slabs:2 * slabs]

    @pl.when(h == 0)
    def _():
        re[...] = a + b
        im[...] = jnp.zeros_like(a)

    @pl.when(h == 1)
    def _():
        dlt = a - b
        re[...] = dlt * mod_ref[0]
        im[...] = dlt * mod_ref[1]

    _fft_forward_big(re, im, tw_ref, slabs)
    scale = np.float32(1.0 / (2 * SUBLANES * slabs))
    _fft_middle(re, im, twm_ref, slabs, pats, out=(sr_ref, si_ref, scale))


def _hy_spectra(k, n_tokens):
    slabs = n_tokens // SUBLANES
    tw, twm, mod = _fft_tables(n_tokens)
    k4 = k.reshape(2 * slabs, SUBLANES, HY_ORDER * HY_WIDTH)
    nt = HY_WIDTH // LANES
    shape = jax.ShapeDtypeStruct((HY_ORDER, 2, slabs, SUBLANES, HY_WIDTH), F32)
    spec_out = pl.BlockSpec((1, 1, slabs, SUBLANES, LANES), lambda o, j, h: (o, h, 0, 0, j))
    tab = pl.BlockSpec((2, slabs, SUBLANES, LANES), lambda o, j, h: (0, 0, 0, 0))
    return pl.pallas_call(
        functools.partial(_hy_spec_kernel, slabs=slabs),
        grid=(HY_ORDER, nt, 2),
        in_specs=[pl.BlockSpec(memory_space=pltpu.SMEM),
                  pl.BlockSpec((2 * slabs, SUBLANES, LANES), lambda o, j, h: (0, 0, o * nt + j)),
                  tab, tab],
        out_specs=[spec_out, spec_out],
        out_shape=[shape, shape],
        scratch_shapes=[pltpu.VMEM((slabs, SUBLANES, LANES), F32)] * 2,
        compiler_params=_cparams(("parallel", "parallel", "arbitrary")), name="hy_spec",
    )(tw, k4, twm, mod)


def _hy_conv_kernel(tw_ref, u_ref, sr_ref, si_ref, twm_ref, mod_ref, o_ref, re, im, *, slabs):
    h = pl.program_id(2)
    pats = _sub_patterns()

    @pl.when(h == 0)
    def _():
        re[...] = u_ref[0]
        im[...] = u_ref[1]

    @pl.when(h == 1)
    def _():
        ur, ui = u_ref[0], u_ref[1]
        mr, mi = mod_ref[0], mod_ref[1]
        re[...] = ur * mr - ui * mi
        im[...] = ur * mi + ui * mr

    _fft_forward_big(re, im, tw_ref, slabs)
    _fft_middle(re, im, twm_ref, slabs, pats, spec=(sr_ref, si_ref))
    _fft_inverse_big(re, im, tw_ref, slabs)

    @pl.when(h == 0)
    def _():
        o_ref[0] = re[...]
        o_ref[1] = im[...]

    @pl.when(h == 1)
    def _():
        yr, yi = re[...], im[...]
        mr, mi = mod_ref[0], mod_ref[1]
        o_ref[0] = o_ref[0] + (yr * mr + yi * mi)
        o_ref[1] = o_ref[1] + (yi * mr - yr * mi)


def _hy_conv(u, spec_re, spec_im, order):
    b, n, w = u.shape
    slabs = n // SUBLANES
    tw, twm, mod = _fft_tables(n)
    u4 = u.reshape(b, slabs, SUBLANES, w)
    nt = w // LANES
    tab = pl.BlockSpec((2, slabs, SUBLANES, LANES), lambda j, p, h: (0, 0, 0, 0))
    spec_in = pl.BlockSpec((1, 1, slabs, SUBLANES, LANES), lambda j, p, h: (order, h, 0, 0, j))
    io = pl.BlockSpec((2, slabs, SUBLANES, LANES), lambda j, p, h: (p, 0, 0, j))
    y = pl.pallas_call(
        functools.partial(_hy_conv_kernel, slabs=slabs),
        grid=(nt, b // 2, 2),
        in_specs=[pl.BlockSpec(memory_space=pltpu.SMEM), io, spec_in, spec_in, tab, tab],
        out_specs=io,
        out_shape=jax.ShapeDtypeStruct((b, slabs, SUBLANES, w), F32),
        scratch_shapes=[pltpu.VMEM((slabs, SUBLANES, LANES), F32)] * 2,
        compiler_params=_cparams(("parallel", "parallel", "arbitrary")), name="hy_conv",
    )(tw, u4, spec_re, spec_im, twm, mod)
    return y.reshape(b, n, w)


def _shift_rows(x, n):
    row = lax.broadcasted_iota(jnp.int32, x.shape, 0)
    prev = jnp.where(row == 0, 0.0, pltpu.roll(x, 1, 0))
    nxt = jnp.where(row == n - 1, 0.0, pltpu.roll(x, n - 1, 0))
    return prev, nxt


def _hy_short_kernel(z_ref, w_ref, b_ref, o_ref, *, n):
    x = z_ref[0]
    prev, nxt = _shift_rows(x, n)
    o_ref[0] = prev * w_ref[0:1, :] + x * w_ref[1:2, :] + nxt * w_ref[2:3, :] + b_ref[...]


def _hy_short(z_mix, conv_w, conv_b):
    b, n, _ = z_mix.shape
    c3 = 3 * HY_WIDTH
    w = conv_w.reshape(3, c3).astype(F32)
    tc = 256
    col0 = BRANCH_W // tc
    return pl.pallas_call(
        functools.partial(_hy_short_kernel, n=n),
        grid=(b, c3 // tc),
        in_specs=[pl.BlockSpec((1, n, tc), lambda bi, j: (bi, 0, col0 + j)),
                  pl.BlockSpec((3, tc), lambda bi, j: (0, j)),
                  pl.BlockSpec((1, tc), lambda bi, j: (0, j))],
        out_specs=pl.BlockSpec((1, n, tc), lambda bi, j: (bi, 0, j)),
        out_shape=jax.ShapeDtypeStruct((b, n, c3), F32),
        compiler_params=_cparams(("parallel", "parallel")), name="hy_short",
    )(z_mix, w, conv_b.astype(F32)[None])


def _hy_gate_kernel(g_ref, y_ref, v_ref, b_ref, o_ref):
    v = v_ref[0]
    o_ref[0] = g_ref[0] * (y_ref[0] + b_ref[...] * v)


def _hy_gate(zc, gate_block, y, v, v_block, bias):
    b, n, w = y.shape
    tn = min(n, 1024)
    return pl.pallas_call(
        _hy_gate_kernel,
        grid=(b, n // tn),
        in_specs=[pl.BlockSpec((1, tn, w), lambda bi, i: (bi, i, gate_block)),
                  pl.BlockSpec((1, tn, w), lambda bi, i: (bi, i, 0)),
                  pl.BlockSpec((1, tn, w), lambda bi, i: (bi, i, v_block)),
                  pl.BlockSpec((1, w), lambda bi, i: (0, 0))],
        out_specs=pl.BlockSpec((1, tn, w), lambda bi, i: (bi, i, 0)),
        out_shape=jax.ShapeDtypeStruct((b, n, w), F32),
        compiler_params=_cparams(("parallel", "parallel")), name="hy_gate",
    )(zc, y, v, bias.astype(F32)[None])


def _hyena(z_mix, conv_w, conv_b, spec_re, spec_im, bias):
    zc = _hy_short(z_mix, conv_w, conv_b)
    y1 = _hy_conv(zc[:, :, :HY_WIDTH], spec_re, spec_im, 0)
    v1 = _hy_gate(zc, 1, y1, zc, 0, bias[0])
    y2 = _hy_conv(v1, spec_re, spec_im, 1)
    return _hy_gate(zc, 2, y2, v1, 0, bias[1])


NA_QROWS = 8
NA_KROWS = 16
NA_KPART = 256


def _softmax_parts(parts):
    m = None
    for s in parts:
        mm = jnp.max(s, axis=-1, keepdims=True)
        m = mm if m is None else jnp.maximum(m, mm)
    return m


def _na_kernel(q_ref, k0, k1, k2, k3, v0, v1, v2, v3, kc_ref, vc_ref, bias_ref, o_ref):
    q = q_ref[0] * np.float32(NA_HEAD_DIM ** -0.5)
    k = jnp.concatenate([k0[0], k1[0], k2[0], k3[0]], axis=0).astype(BF16)
    v = jnp.concatenate([v0[0], v1[0], v2[0], v3[0]], axis=0).astype(BF16)
    kc = kc_ref[0].astype(BF16)
    vc = vc_ref[0].astype(BF16)
    nt = (((1,), (1,)), ((), ()))
    outs = []
    for h in range(NA_HEADS):
        sl = slice(h * NA_HEAD_DIM, (h + 1) * NA_HEAD_DIM)
        qh = q[:, sl].astype(BF16)
        s_lat = lax.dot_general(qh, k[:, sl], nt, preferred_element_type=F32) + bias_ref[0, h]
        s_ctx = lax.dot_general(qh, kc[:, sl], nt, preferred_element_type=F32)
        m = _softmax_parts([s_lat, s_ctx])
        p_lat = jnp.exp(s_lat - m)
        p_ctx = jnp.exp(s_ctx - m)
        den = jnp.sum(p_lat, axis=-1, keepdims=True) + jnp.sum(p_ctx, axis=-1, keepdims=True)
        o = jnp.dot(p_lat.astype(BF16), v[:, sl], preferred_element_type=F32) \
            + jnp.dot(p_ctx.astype(BF16), vc[:, sl], preferred_element_type=F32)
        outs.append(o / den)
    o_ref[0] = jnp.concatenate(outs, axis=-1)


def _na_bias(rpb, rows):
    kh = min(NA_WIN_H, rows)
    col = np.arange(GRID_W)
    col_start = np.clip(col - NA_WIN_W // 2, 0, GRID_W - NA_WIN_W)
    col_ok = (col[None] >= col_start[:, None]) & (col[None] < col_start[:, None] + NA_WIN_W)
    off_c = np.clip(col[None] - col[:, None], -(NA_WIN_W - 1), NA_WIN_W - 1) + (NA_WIN_W - 1)
    nblk = rows // NA_QROWS
    out = []
    for j in (0, 1, nblk - 1):
        qr = j * NA_QROWS + np.arange(NA_QROWS)
        ws = int(np.clip(j * NA_QROWS - NA_WIN_H // 2, 0, rows - NA_KROWS))
        kr = ws + np.arange(NA_KROWS)
        start = np.clip(qr - kh // 2, 0, rows - kh)
        row_ok = (kr[None] >= start[:, None]) & (kr[None] < start[:, None] + kh)
        off_r = np.clip(kr[None] - qr[:, None] + (NA_WIN_H - 1), 0, 2 * NA_WIN_H - 2)
        sel_r = np.eye(2 * NA_WIN_H - 1, dtype=np.float32)[off_r]
        sel_c = np.eye(2 * NA_WIN_W - 1, dtype=np.float32)[off_c]
        bias = jnp.einsum('rka,hab,qcb->hrqkc', sel_r, rpb.astype(F32), sel_c, precision=HI)
        ok = row_ok[:, None, :, None] & col_ok[None, :, None, :]
        bias = jnp.where(jnp.asarray(ok)[None], bias, NEG_INF)
        out.append(bias.reshape(NA_HEADS, NA_QROWS * GRID_W, NA_KROWS * GRID_W))
    return jnp.stack(out)


def _na_attention(z_mix, zc_mix, rpb):
    b, n, _ = z_mix.shape
    n_ctx = zc_mix.shape[1]
    rows = n // GRID_W
    nblk = rows // NA_QROWS
    tq = NA_QROWS * GRID_W
    bias = _na_bias(rpb, rows)
    qb, kb, vb = 1024 // BRANCH_W, 1024 // BRANCH_W + 1, 1024 // BRANCH_W + 2
    kparts = NA_KROWS * GRID_W // NA_KPART
    max_k0 = (rows - NA_KROWS) * GRID_W // NA_KPART

    def k0_of(j):
        return jnp.clip(2 * j - 1, 0, max_k0)

    def kspec(part, blk):
        return pl.BlockSpec((1, NA_KPART, BRANCH_W), lambda bi, j: (bi, k0_of(j) + part, blk))

    def cls(j):
        return jnp.where(j == 0, 0, jnp.where(j == nblk - 1, 2, 1))

    return pl.pallas_call(
        _na_kernel,
        grid=(b, nblk),
        in_specs=[pl.BlockSpec((1, tq, BRANCH_W), lambda bi, j: (bi, j, qb))]
        + [kspec(p, kb) for p in range(kparts)] + [kspec(p, vb) for p in range(kparts)]
        + [pl.BlockSpec((1, n_ctx, BRANCH_W), lambda bi, j: (bi, 0, kb)),
           pl.BlockSpec((1, n_ctx, BRANCH_W), lambda bi, j: (bi, 0, vb)),
           pl.BlockSpec((1, NA_HEADS, tq, NA_KROWS * GRID_W), lambda bi, j: (cls(j), 0, 0, 0))],
        out_specs=pl.BlockSpec((1, tq, BRANCH_W), lambda bi, j: (bi, j, 0)),
        out_shape=jax.ShapeDtypeStruct((b, n, BRANCH_W), F32),
        compiler_params=_cparams(("parallel", "arbitrary")), name="na_attn",
    )(z_mix, *([z_mix] * (2 * kparts)), zc_mix, zc_mix, bias)


def _rope_tables(n_tokens, head_dim, heads):
    t = jnp.arange(n_tokens, dtype=jnp.int32)
    row = (t // GRID_W).astype(F32)
    col = (t % GRID_W).astype(F32)
    half = head_dim // 2
    inv = ROPE_BASE ** (-(jnp.arange(0, half, 2, dtype=F32) / half))
    ang = jnp.concatenate([row[:, None] * inv, col[:, None] * inv], axis=-1)
    cos, sin = jnp.cos(ang), jnp.sin(ang)
    cos_t = jnp.tile(jnp.concatenate([cos, cos], axis=-1), (1, heads))
    sin_t = jnp.tile(jnp.concatenate([-sin, sin], axis=-1), (1, heads))
    return cos_t, sin_t


def _rope(x, cos_t, sin_t):
    w = x.shape[-1]
    half = SW_HEAD_DIM // 2
    lane = lax.broadcasted_iota(jnp.int32, x.shape, 1)
    first = (lane % SW_HEAD_DIM) < half
    partner = jnp.where(first, pltpu.roll(x, w - half, 1), pltpu.roll(x, half, 1))
    return x * cos_t + partner * sin_t


def _sw_kernel(q_ref, kp_ref, kn_ref, kx_ref, vp_ref, vn_ref, vx_ref, kc_ref, vc_ref,
               cq_ref, sq_ref, cp_ref, sp_ref, cn_ref, sn_ref, cx_ref, sx_ref, sink_ref, o_ref, *, nblk):
    i = pl.program_id(1)
    kvw = SW_KV_HEADS * SW_HEAD_DIM
    q = _rope(q_ref[0], cq_ref[...], sq_ref[...]) * np.float32(SW_HEAD_DIM ** -0.5)
    k = jnp.concatenate([_rope(kp_ref[0], cp_ref[:, :kvw], sp_ref[:, :kvw]),
                         _rope(kn_ref[0], cn_ref[:, :kvw], sn_ref[:, :kvw]),
                         _rope(kx_ref[0], cx_ref[:, :kvw], sx_ref[:, :kvw])], axis=0).astype(BF16)
    v = jnp.concatenate([vp_ref[0], vn_ref[0], vx_ref[0]], axis=0).astype(BF16)
    kc = kc_ref[0].astype(BF16)
    vc = vc_ref[0].astype(BF16)
    g_ = SW_HEADS // SW_KV_HEADS
    rows = g_ * SW_BLOCK
    span = 3 * SW_BLOCK
    r = lax.broadcasted_iota(jnp.int32, (rows, span), 0) % SW_BLOCK
    c = lax.broadcasted_iota(jnp.int32, (rows, span), 1)
    diff = c - SW_BLOCK - r
    blk = i + c // SW_BLOCK - 1
    ok = (jnp.abs(diff) <= SW_WINDOW) & (blk >= 0) & (blk < nblk)
    nt = (((1,), (1,)), ((), ()))
    outs = []
    for kv in range(SW_KV_HEADS):
        ksl = slice(kv * SW_HEAD_DIM, (kv + 1) * SW_HEAD_DIM)
        qg = jnp.concatenate([q[:, (kv * g_ + g) * SW_HEAD_DIM:(kv * g_ + g + 1) * SW_HEAD_DIM] for g in range(g_)],
                             axis=0).astype(BF16)
        s_lat = jnp.where(ok, lax.dot_general(qg, k[:, ksl], nt, preferred_element_type=F32), NEG_INF)
        s_ctx = lax.dot_general(qg, kc[:, ksl], nt, preferred_element_type=F32)
        rr = lax.broadcasted_iota(jnp.int32, (rows, 1), 0)
        s_sink = jnp.zeros((rows, 1), F32)
        for g in range(g_):
            s_sink = jnp.where(rr // SW_BLOCK == g, sink_ref[kv * g_ + g], s_sink)
        m = jnp.maximum(_softmax_parts([s_lat, s_ctx]), s_sink)
        p_lat = jnp.exp(s_lat - m)
        p_ctx = jnp.exp(s_ctx - m)
        den = jnp.sum(p_lat, axis=-1, keepdims=True) + jnp.sum(p_ctx, axis=-1, keepdims=True) + jnp.exp(s_sink - m)
        o = jnp.dot(p_lat.astype(BF16), v[:, ksl], preferred_element_type=F32) \
            + jnp.dot(p_ctx.astype(BF16), vc[:, ksl], preferred_element_type=F32)
        o = o / den
        outs += [o[g * SW_BLOCK:(g + 1) * SW_BLOCK] for g in range(g_)]
    o_ref[0] = jnp.concatenate(outs, axis=-1)


def _sw_attention(z_mix, zc_mix, sink):
    b, n, _ = z_mix.shape
    n_ctx = zc_mix.shape[1]
    nblk = n // SW_BLOCK
    cos_t, sin_t = _rope_tables(n, SW_HEAD_DIM, SW_HEADS)
    qw = SW_HEADS * SW_HEAD_DIM
    kvw = SW_KV_HEADS * SW_HEAD_DIM
    q_blk = 1792 // qw
    k_blk = (1792 + qw) // kvw
    v_blk = k_blk + 1

    def prev(i):
        return jnp.maximum(i - 1, 0)

    def nxt(i):
        return jnp.minimum(i + 1, nblk - 1)

    def tok(fn, blk):
        return pl.BlockSpec((1, SW_BLOCK, kvw), lambda bi, i: (bi, fn(i), blk))

    def tab(fn):
        return pl.BlockSpec((SW_BLOCK, qw), lambda bi, i: (fn(i), 0))

    ident = lambda i: i
    return pl.pallas_call(
        functools.partial(_sw_kernel, nblk=nblk),
        grid=(b, nblk),
        in_specs=[pl.BlockSpec((1, SW_BLOCK, qw), lambda bi, i: (bi, i, q_blk)),
                  tok(prev, k_blk), tok(ident, k_blk), tok(nxt, k_blk),
                  tok(prev, v_blk), tok(ident, v_blk), tok(nxt, v_blk),
                  pl.BlockSpec((1, n_ctx, kvw), lambda bi, i: (bi, 0, k_blk)),
                  pl.BlockSpec((1, n_ctx, kvw), lambda bi, i: (bi, 0, v_blk)),
                  tab(ident), tab(ident), tab(prev), tab(prev), tab(ident), tab(ident), tab(nxt), tab(nxt),
                  pl.BlockSpec(memory_space=pltpu.SMEM)],
        out_specs=pl.BlockSpec((1, SW_BLOCK, qw), lambda bi, i: (bi, i, 0)),
        out_shape=jax.ShapeDtypeStruct((b, n, qw), F32),
        compiler_params=_cparams(("parallel", "arbitrary")), name="sw_attn",
    )(z_mix, *([z_mix] * 6), zc_mix, zc_mix, cos_t, sin_t, cos_t, sin_t, cos_t, sin_t, cos_t, sin_t,
      sink.astype(F32))


def _ctx_attn_kernel(q_ref, k_ref, v_ref, sink_ref, o_ref, *, heads, kv_heads, dh, use_sink):
    q = q_ref[0] * np.float32(dh ** -0.5)
    k = k_ref[0].astype(BF16)
    v = v_ref[0].astype(BF16)
    g_ = heads // kv_heads
    nt = (((1,), (1,)), ((), ()))
    outs = []
    for h in range(heads):
        kv = h // g_
        s = lax.dot_general(q[:, h * dh:(h + 1) * dh].astype(BF16), k[:, kv * dh:(kv + 1) * dh], nt,
                            preferred_element_type=F32)
        m = jnp.max(s, axis=-1, keepdims=True)
        if use_sink:
            m = jnp.maximum(m, sink_ref[h])
        p = jnp.exp(s - m)
        den = jnp.sum(p, axis=-1, keepdims=True)
        if use_sink:
            den = den + jnp.exp(sink_ref[h] - m)
        outs.append(jnp.dot(p.astype(BF16), v[:, kv * dh:(kv + 1) * dh], preferred_element_type=F32) / den)
    o_ref[0] = jnp.concatenate(outs, axis=-1)


def _ctx_attention(zc_mix, col0, heads, kv_heads, dh, sink):
    b, n, _ = zc_mix.shape
    qw, kvw = heads * dh, kv_heads * dh
    use_sink = sink is not None
    sink_arr = sink.astype(F32) if use_sink else jnp.zeros((heads,), F32)
    return pl.pallas_call(
        functools.partial(_ctx_attn_kernel, heads=heads, kv_heads=kv_heads, dh=dh, use_sink=use_sink),
        grid=(b,),
        in_specs=[pl.BlockSpec((1, n, qw), lambda bi: (bi, 0, col0 // qw)),
                  pl.BlockSpec((1, n, kvw), lambda bi: (bi, 0, (col0 + qw) // kvw)),
                  pl.BlockSpec((1, n, kvw), lambda bi: (bi, 0, (col0 + qw) // kvw + 1)),
                  pl.BlockSpec(memory_space=pltpu.SMEM)],
        out_specs=pl.BlockSpec((1, n, qw), lambda bi: (bi, 0, 0)),
        out_shape=jax.ShapeDtypeStruct((b, n, qw), F32),
        compiler_params=_cparams(("parallel",)), name="ctx_attn",
    )(zc_mix, zc_mix, zc_mix, sink_arr)


def _merge_kernel(x_ref, s5_ref, hy_ref, na_ref, sw_ref, gt_ref, ga_ref, wglu_ref, wb_ref, wo_ref,
                  lg_ref, lb_ref, o_ref):
    g = jax.nn.gelu(s5_ref[0])
    s5 = g * jax.nn.sigmoid(jnp.dot(g.astype(BF16), wglu_ref[...], preferred_element_type=F32))
    branches = (s5, hy_ref[0], na_ref[0], sw_ref[0])
    acc = None
    for n in range(N_BRANCH):
        proj = jnp.dot(branches[n].astype(BF16), wb_ref[n], preferred_element_type=F32)
        t = gt_ref[0, :, n * D_MODEL:(n + 1) * D_MODEL] * proj
        acc = t if acc is None else acc + t
    mix = jnp.dot(acc.astype(BF16), wo_ref[...], preferred_element_type=F32)
    y = np.float32(DEEPNORM_ALPHA) * x_ref[0] + ga_ref[0] * mix
    o_ref[0] = _layernorm(y) * lg_ref[...] + lb_ref[...]


def _merge(x, s5y, hy, na, sw, gates, g_a, w_glu, w_branch, w_out, ln_g, ln_b, *, tm):
    b, l, d = x.shape
    br = pl.BlockSpec((1, tm, BRANCH_W), lambda bi, i: (bi, i, 0))
    full = lambda s: pl.BlockSpec(s, lambda bi, i: (0,) * len(s))
    return pl.pallas_call(
        _merge_kernel,
        grid=(b, l // tm),
        in_specs=[pl.BlockSpec((1, tm, d), lambda bi, i: (bi, i, 0)), br, br, br, br,
                  pl.BlockSpec((1, tm, GATE_W), lambda bi, i: (bi, i, 0)),
                  pl.BlockSpec((1, 1, d), lambda bi, i: (bi, 0, 0)),
                  full((BRANCH_W, BRANCH_W)), full((N_BRANCH, BRANCH_W, d)), full((d, d)),
                  full((1, d)), full((1, d))],
        out_specs=pl.BlockSpec((1, tm, d), lambda bi, i: (bi, i, 0)),
        out_shape=jax.ShapeDtypeStruct((b, l, d), F32),
        compiler_params=_cparams(("parallel", "parallel")), name="merge",
    )(x, s5y, hy, na, sw, gates, g_a, w_glu, w_branch, w_out, ln_g, ln_b)


def _mlp_kernel(x_ref, sh_ref, sc_ref, gm_ref, w1_ref, w2_ref, lg_ref, lb_ref, o_ref, h_scr, acc_scr):
    j = pl.program_id(2)

    @pl.when(j == 0)
    def _():
        h = _layernorm(x_ref[0]) * (1.0 + sc_ref[0]) + sh_ref[0]
        h_scr[...] = h.astype(BF16)
        acc_scr[...] = jnp.zeros_like(acc_scr)

    a = jnp.dot(h_scr[...], w1_ref[...], preferred_element_type=F32)
    a = jnp.square(jnp.maximum(a, 0.0))
    acc_scr[...] += jnp.dot(a.astype(BF16), w2_ref[...], preferred_element_type=F32)

    @pl.when(j == pl.num_programs(2) - 1)
    def _():
        y = np.float32(DEEPNORM_ALPHA) * x_ref[0] + gm_ref[0] * acc_scr[...]
        o_ref[0] = _layernorm(y) * lg_ref[...] + lb_ref[...]


def _mlp(x, sh, sc, g_m, w1, w2, ln_g, ln_b, *, tm, th):
    b, l, d = x.shape
    hdim = w1.shape[1]
    mod = pl.BlockSpec((1, 1, d), lambda bi, i, j: (bi, 0, 0))
    vec = pl.BlockSpec((1, d), lambda bi, i, j: (0, 0))
    return pl.pallas_call(
        _mlp_kernel,
        grid=(b, l // tm, hdim // th),
        in_specs=[pl.BlockSpec((1, tm, d), lambda bi, i, j: (bi, i, 0)), mod, mod, mod,
                  pl.BlockSpec((d, th), lambda bi, i, j: (0, j)),
                  pl.BlockSpec((th, d), lambda bi, i, j: (j, 0)), vec, vec],
        out_specs=pl.BlockSpec((1, tm, d), lambda bi, i, j: (bi, i, 0)),
        out_shape=jax.ShapeDtypeStruct((b, l, d), F32),
        scratch_shapes=[pltpu.VMEM((tm, d), BF16), pltpu.VMEM((tm, d), F32)],
        compiler_params=_cparams(("parallel", "parallel", "arbitrary")), name="mlp",
    )(x, sh, sc, g_m, w1, w2, ln_g, ln_b)


def kernel(x, c, ctx, c_ctx, w_ada, b_ada, w_in, s5_lambda_re, s5_lambda_im, s5_log_dt, s5_b_re, s5_b_im, s5_c_re,
           s5_c_im, s5_d, s5_w_glu, hy_conv_w, hy_conv_b, hy_freq, hy_w1, hy_b1, hy_w2, hy_b2, hy_w3, hy_bias,
           na_rpb, sw_sink, w_branch, w_out, ln1_g, ln1_b, w_mlp1, w_mlp2, ln2_g, ln2_b):
    b, l, d = x.shape
    n_ctx = ctx.shape[1]
    depth = w_ada.shape[0]
    cc = jnp.zeros((8, d), F32).at[:b].set(c.astype(F32)).at[b].set(c_ctx.astype(F32))
    mod_all = _ada(cc, w_ada.astype(F32), b_ada.astype(F32))
    xc = ctx
    for layer in range(depth):
        need_ctx_out = layer < depth - 1
        mod = mod_all[layer, :b].reshape(b, 1, 6, d)
        mod_c = jnp.broadcast_to(mod_all[layer, b].reshape(1, 1, 6, d), (b, 1, 6, d))
        sh_a, sc_a, g_a, sh_m, sc_m, g_m = [mod[:, :, i] for i in range(6)]
        csh_a, csc_a, cg_a, csh_m, csc_m, cg_m = [mod_c[:, :, i] for i in range(6)]
        w_in_l = w_in[layer].astype(BF16)
        w_mix, w_gate = w_in_l[:, :MIX_W], w_in_l[:, MIX_W:]

        z_mix = _ln_mod_mm(x, sh_a, sc_a, w_mix, tm=1024, tn=MIX_W // 3)
        gates = _ln_mod_mm(x, sh_a, sc_a, w_gate, tm=1024, tn=1024, act="sigmoid")
        zc_mix = _ln_mod_mm(xc, csh_a, csc_a, w_mix, tm=n_ctx, tn=MIX_W // 3)

        ops = _s5_operators(s5_lambda_re[layer], s5_lambda_im[layer], s5_log_dt[layer], s5_b_re[layer],
                            s5_b_im[layer], s5_c_re[layer], s5_c_im[layer], s5_d[layer])
        yc_s5, e_ctx = _s5_mix(zc_mix[:, :, :BRANCH_W], ops, [jnp.zeros((b, 1, S5_STATE_W // 4), F32)] * 4)
        y_s5, _ = _s5_mix(z_mix[:, :, :BRANCH_W], ops, e_ctx)

        hy_args = (hy_freq[layer], hy_w1[layer], hy_b1[layer], hy_w2[layer], hy_b2[layer], hy_w3[layer])
        sp_re, sp_im = _hy_spectra(_hy_filters(l, *hy_args), l)
        hy_l = _hyena(z_mix, hy_conv_w[layer], hy_conv_b[layer], sp_re, sp_im, hy_bias[layer])

        na_l = _na_attention(z_mix, zc_mix, na_rpb[layer])
        sw_l = _sw_attention(z_mix, zc_mix, sw_sink[layer])

        w_glu = s5_w_glu[layer].astype(BF16)
        w_br = w_branch[layer].astype(BF16)
        w_o = w_out[layer].astype(BF16)
        lg1, lb1 = ln1_g[layer].astype(F32)[None], ln1_b[layer].astype(F32)[None]
        lg2, lb2 = ln2_g[layer].astype(F32)[None], ln2_b[layer].astype(F32)[None]
        w1 = w_mlp1[layer].astype(BF16)
        w2 = w_mlp2[layer].astype(BF16)

        x_new = _merge(x, y_s5, hy_l, na_l, sw_l, gates, g_a, w_glu, w_br, w_o, lg1, lb1, tm=512)
        x_new = _mlp(x_new, sh_m, sc_m, g_m, w1, w2, lg2, lb2, tm=1024, th=1024)

        if need_ctx_out:
            gates_c = _ln_mod_mm(xc, csh_a, csc_a, w_gate, tm=n_ctx, tn=1024, act="sigmoid")
            spc_re, spc_im = _hy_spectra(_hy_filters(n_ctx, *hy_args), n_ctx)
            hy_c = _hyena(zc_mix, hy_conv_w[layer], hy_conv_b[layer], spc_re, spc_im, hy_bias[layer])
            na_c = _ctx_attention(zc_mix, 1024, NA_HEADS, NA_HEADS, NA_HEAD_DIM, None)
            sw_c = _ctx_attention(zc_mix, 1792, SW_HEADS, SW_KV_HEADS, SW_HEAD_DIM, sw_sink[layer])
            xc_new = _merge(xc, yc_s5, hy_c, na_c, sw_c, gates_c, cg_a, w_glu, w_br, w_o, lg1, lb1, tm=n_ctx)
            xc = _mlp(xc_new, csh_m, csc_m, cg_m, w1, w2, lg2, lb2, tm=n_ctx, th=1024)
        x = x_new
    return x
```

```python
import functools
import math

import numpy as np
import jax
import jax.numpy as jnp
from jax import lax
from jax.experimental import pallas as pl
from jax.experimental.pallas import tpu as pltpu

F32 = jnp.float32
BF16 = jnp.bfloat16

D_MODEL = 1024
GRID_W = 64
BRANCH_W = 256
N_BRANCH = 4
S5_GROUP = 16
S5_GROUPS = 16
S5_STATE = 64
HY_WIDTH = 256
HY_ORDER = 2
HY_BANDS = 16
HY_EMB = 2 * HY_BANDS + 1
HY_FFN = 64
HY_MIN_DECAY = math.log(1e-2) / 1.5
HY_MAX_DECAY = math.log(1e-2) / 0.3
NA_HEADS = 4
NA_HEAD_DIM = 64
NA_WIN_H = 8
NA_WIN_W = 16
SW_HEADS = 4
SW_KV_HEADS = 2
SW_HEAD_DIM = 64
SW_WINDOW = 128
SW_BLOCK = 128
MLP_HIDDEN = 4 * D_MODEL
ROPE_BASE = 10000.0
LN_EPS = 1e-6
NEG_INF = -1e30
DEPTH = 2
DEEPNORM_ALPHA = (2 * DEPTH) ** 0.25

MIX_W = 2304
GATE_W = N_BRANCH * D_MODEL
S5_T = 8
S5_STATE_W = 4 * S5_GROUPS * S5_STATE
SUBLANES = 8
LANES = 128
VMEM_LIMIT = 56 * 1024 * 1024

HI = lax.Precision.HIGHEST


def _cparams(sem):
    return pltpu.CompilerParams(dimension_semantics=sem, vmem_limit_bytes=VMEM_LIMIT)


def _layernorm(x):
    mu = jnp.mean(x, axis=-1, keepdims=True)
    xc = x - mu
    var = jnp.mean(xc * xc, axis=-1, keepdims=True)
    return xc * lax.rsqrt(var + LN_EPS)


def _ada_kernel(c_ref, w_ref, b_ref, o_ref):
    c = c_ref[...]
    a = c * jax.nn.sigmoid(c)
    o_ref[0] = jnp.dot(a, w_ref[0], preferred_element_type=F32, precision=HI) + b_ref[0]


def _ada(cc, w_ada, b_ada):
    depth, d, n = w_ada.shape
    tn = 512
    return pl.pallas_call(
        _ada_kernel,
        grid=(depth, n // tn),
        in_specs=[pl.BlockSpec((8, d), lambda l, j: (0, 0)),
                  pl.BlockSpec((1, d, tn), lambda l, j: (l, 0, j)),
                  pl.BlockSpec((1, 1, tn), lambda l, j: (l, 0, j))],
        out_specs=pl.BlockSpec((1, 8, tn), lambda l, j: (l, 0, j)),
        out_shape=jax.ShapeDtypeStruct((depth, 8, n), F32),
        compiler_params=_cparams(("parallel", "parallel")), name="ada",
    )(cc, w_ada, b_ada.reshape(depth, 1, n))


def _ln_mod_mm_kernel(x_ref, sh_ref, sc_ref, w_ref, o_ref, h_scr, *, act):
    @pl.when(pl.program_id(2) == 0)
    def _():
        h = _layernorm(x_ref[0]) * (1.0 + sc_ref[0]) + sh_ref[0]
        h_scr[...] = h.astype(BF16)

    r = jnp.dot(h_scr[...], w_ref[...], preferred_element_type=F32)
    if act == "sigmoid":
        r = jax.nn.sigmoid(r)
    o_ref[0] = r.astype(o_ref.dtype)


def _ln_mod_mm(x, sh, sc, w, *, tm, tn, act=None, out_dtype=F32):
    b, l, d = x.shape
    n = w.shape[1]
    return pl.pallas_call(
        functools.partial(_ln_mod_mm_kernel, act=act),
        grid=(b, l // tm, n // tn),
        in_specs=[pl.BlockSpec((1, tm, d), lambda bi, i, j: (bi, i, 0)),
                  pl.BlockSpec((1, 1, d), lambda bi, i, j: (bi, 0, 0)),
                  pl.BlockSpec((1, 1, d), lambda bi, i, j: (bi, 0, 0)),
                  pl.BlockSpec((d, tn), lambda bi, i, j: (0, j))],
        out_specs=pl.BlockSpec((1, tm, tn), lambda bi, i, j: (bi, i, j)),
        out_shape=jax.ShapeDtypeStruct((b, l, n), out_dtype),
        scratch_shapes=[pltpu.VMEM((tm, d), BF16)],
        compiler_params=_cparams(("parallel", "parallel", "arbitrary")), name="ln_mod_mm",
    )(x, sh, sc, w)


def _s5_inc_kernel(u_ref, q_ref, o_ref):
    o_ref[...] = jnp.dot(u_ref[...].astype(BF16), q_ref[...], preferred_element_type=F32)


def _s5_increments(uu, q_all, *, tm, tn):
    m, k = uu.shape
    n = q_all.shape[1]
    return pl.pallas_call(
        _s5_inc_kernel,
        grid=(m // tm, n // tn),
        in_specs=[pl.BlockSpec((tm, k), lambda i, j: (i, 0)), pl.BlockSpec((k, tn), lambda i, j: (0, j))],
        out_specs=pl.BlockSpec((tm, tn), lambda i, j: (i, j)),
        out_shape=jax.ShapeDtypeStruct((m, n), F32),
        compiler_params=_cparams(("parallel", "parallel")), name="s5_inc",
    )(uu, q_all)


def _s5_out_kernel(u_ref, h0, h1, h2, h3, m_ref, n0, n1, n2, n3, o_ref):
    nt = (((1,), (1,)), ((), ()))
    acc = jnp.dot(u_ref[...].astype(BF16), m_ref[...], preferred_element_type=F32)
    for h_ref, n_ref in ((h0, n0), (h1, n1), (h2, n2), (h3, n3)):
        acc += lax.dot_general(h_ref[...].astype(BF16), n_ref[...], nt, preferred_element_type=F32)
    o_ref[...] = acc


def _s5_outputs(uu, h, m_tot, n_t, *, tm, tn):
    m, k = uu.shape
    gp = h[0].shape[1]
    n = m_tot.shape[1]
    return pl.pallas_call(
        _s5_out_kernel,
        grid=(m // tm, n // tn),
        in_specs=[pl.BlockSpec((tm, k), lambda i, j: (i, 0))]
        + [pl.BlockSpec((tm, gp), lambda i, j: (i, 0))] * 4
        + [pl.BlockSpec((k, tn), lambda i, j: (0, j))]
        + [pl.BlockSpec((tn, gp), lambda i, j, c=c: (j, c)) for c in range(4)],
        out_specs=pl.BlockSpec((tm, tn), lambda i, j: (i, j)),
        out_shape=jax.ShapeDtypeStruct((m, n), F32),
        compiler_params=_cparams(("parallel", "parallel")), name="s5_out",
    )(uu, *h, m_tot, n_t, n_t, n_t, n_t)


def _s5_scan_kernel(gfr, gfi, gbr, gbi, afr, afi, abr, abi, h0fr, h0fi, h0br, h0bi,
                    hfr, hfi, hbr, hbi, efr, efi, ebr, ebi, *, n_chunks):
    a_fr = afr[...][None]
    a_fi = afi[...][None]
    a_br = abr[...][None]
    a_bi = abi[...][None]

    def body(k, carry):
        sfr, sfi, sbr, sbi = carry
        kb = n_chunks - 1 - k
        hfr[:, pl.ds(k, 1), :] = sfr
        hfi[:, pl.ds(k, 1), :] = sfi
        hbr[:, pl.ds(kb, 1), :] = sbr
        hbi[:, pl.ds(kb, 1), :] = sbi
        nfr = a_fr * sfr - a_fi * sfi + gfr[:, pl.ds(k, 1), :]
        nfi = a_fr * sfi + a_fi * sfr + gfi[:, pl.ds(k, 1), :]
        nbr = a_br * sbr - a_bi * sbi + gbr[:, pl.ds(kb, 1), :]
        nbi = a_br * sbi + a_bi * sbr + gbi[:, pl.ds(kb, 1), :]
        return nfr, nfi, nbr, nbi

    sfr, sfi, sbr, sbi = lax.fori_loop(0, n_chunks, body, (h0fr[...], h0fi[...], h0br[...], h0bi[...]))
    efr[...] = sfr
    efi[...] = sfi
    ebr[...] = sbr
    ebi[...] = sbi


def _s5_scan(g, a_t, h0):
    b, k, w4 = g.shape
    w = LANES
    q = w4 // 4
    nb = q // w

    def comp(c):
        return pl.BlockSpec((b, k, w), lambda j, c=c: (0, 0, c * nb + j))

    def comp_a(c):
        return pl.BlockSpec((1, w), lambda j, c=c: (0, c * nb + j))

    state = pl.BlockSpec((b, k, w), lambda j: (0, 0, j))
    edge = pl.BlockSpec((b, 1, w), lambda j: (0, 0, j))
    outs = pl.pallas_call(
        functools.partial(_s5_scan_kernel, n_chunks=k),
        grid=(nb,),
        in_specs=[comp(c) for c in range(4)] + [comp_a(c) for c in range(4)] + [edge] * 4,
        out_specs=[state] * 4 + [edge] * 4,
        out_shape=[jax.ShapeDtypeStruct((b, k, q), F32)] * 4 + [jax.ShapeDtypeStruct((b, 1, q), F32)] * 4,
        compiler_params=_cparams(("parallel",)), name="s5_scan",
    )(g, g, g, g, a_t, a_t, a_t, a_t, *h0)
    return outs[:4], outs[4:]


def _cmul(ar, ai, br, bi):
    return ar * br - ai * bi, ar * bi + ai * br


def _s5_operators(lam_re, lam_im, log_dt, b_re, b_im, c_re, c_im, d):
    t_ = S5_T
    g_, p_, n_ = S5_GROUPS, S5_STATE, S5_GROUP
    gp, w_ = g_ * p_, g_ * n_
    lr, li = lam_re.astype(F32), lam_im.astype(F32)
    dt = jnp.exp(log_dt.astype(F32))[..., None]
    ks = jnp.arange(t_ + 1, dtype=F32)[:, None, None, None]
    mag = jnp.exp(ks * lr * dt)
    pw_re = mag * jnp.cos(ks * li * dt)
    pw_im = mag * jnp.sin(ks * li * dt)
    a_re, a_im = pw_re[1], pw_im[1]
    den = lr ** 2 + li ** 2
    f_re = ((a_re - 1.0) * lr + a_im * li) / den
    f_im = (a_im * lr - (a_re - 1.0) * li) / den
    br, bi = b_re.astype(F32), b_im.astype(F32)
    bb_re = f_re[..., None] * br - f_im[..., None] * bi
    bb_im = f_re[..., None] * bi + f_im[..., None] * br
    cr, ci = c_re.astype(F32), c_im.astype(F32)
    wb_re, wb_im = _cmul(pw_re[:t_, ..., None], pw_im[:t_, ..., None], bb_re[None], bb_im[None])
    kern = jnp.einsum('dgnp,kdgpm->dkgnm', cr, wb_re, precision=HI) \
        - jnp.einsum('dgnp,kdgpm->dkgnm', ci, wb_im, precision=HI)
    k_tile = jnp.tile(kern.transpose(0, 1, 2, 4, 3).reshape(2, t_, w_, n_), (1, 1, 1, g_))

    def rows_tiled(x):
        return jnp.tile(x.transpose(0, 3, 1, 2).reshape(2, n_, gp), (1, g_, 1))

    bt_re, bt_im = rows_tiled(bb_re), rows_tiled(bb_im)
    ct_re, ct_im = rows_tiled(cr.transpose(0, 1, 3, 2)), rows_tiled(ci.transpose(0, 1, 3, 2))
    pr = pw_re.transpose(1, 0, 2, 3).reshape(2, t_ + 1, gp)
    pi_ = pw_im.transpose(1, 0, 2, 3).reshape(2, t_ + 1, gp)
    full = lambda s: pl.BlockSpec(s, lambda t: (0,) * len(s))
    m_tot, q_all, n_t = pl.pallas_call(
        _s5_ops_kernel,
        grid=(t_,),
        in_specs=[full((2, t_ + 1, gp))] * 2 + [full((2, w_, gp))] * 4 + [full((2, t_, w_, w_)), full((1, w_))],
        out_specs=[pl.BlockSpec((w_, t_ * w_), lambda t: (t, 0)),
                   pl.BlockSpec((w_, 4 * gp), lambda t: (t, 0)),
                   pl.BlockSpec((w_, 4 * gp), lambda t: (t, 0))],
        out_shape=[jax.ShapeDtypeStruct((t_ * w_, t_ * w_), BF16),
                   jax.ShapeDtypeStruct((t_ * w_, 4 * gp), BF16),
                   jax.ShapeDtypeStruct((t_ * w_, 4 * gp), BF16)],
        compiler_params=_cparams(("parallel",)), name="s5_ops",
    )(pr, pi_, bt_re, bt_im, ct_re, ct_im, k_tile, d.astype(F32)[None])
    a_t = jnp.concatenate([pr[0, t_], pi_[0, t_], pr[1, t_], pi_[1, t_]])[None]
    return m_tot, q_all, n_t, a_t


def _s5_ops_kernel(pr_ref, pi_ref, btr_ref, bti_ref, ctr_ref, cti_ref, kt_ref, d_ref, m_ref, q_ref, nt_ref):
    t = pl.program_id(0)
    t_ = S5_T
    w_, gp = btr_ref.shape[1], btr_ref.shape[2]
    row_g = lax.broadcasted_iota(jnp.int32, (w_, gp), 0) // S5_GROUP
    col_g = lax.broadcasted_iota(jnp.int32, (w_, gp), 1) // S5_STATE
    same = row_g == col_g
    for dr in range(2):
        e_q = (t_ - 1 - t) if dr == 0 else t
        e_n = (t + 1) if dr == 0 else (t_ - t)
        b_re = jnp.where(same, btr_ref[dr], 0.0)
        b_im = jnp.where(same, bti_ref[dr], 0.0)
        q_re, q_im = _cmul(b_re, b_im, pr_ref[dr, pl.ds(e_q, 1), :], pi_ref[dr, pl.ds(e_q, 1), :])
        c_re = jnp.where(same, ctr_ref[dr], 0.0)
        c_im = jnp.where(same, cti_ref[dr], 0.0)
        n_re, n_im = _cmul(c_re, c_im, pr_ref[dr, pl.ds(e_n, 1), :], pi_ref[dr, pl.ds(e_n, 1), :])
        base = 2 * dr * gp
        q_ref[:, base:base + gp] = q_re.astype(BF16)
        q_ref[:, base + gp:base + 2 * gp] = q_im.astype(BF16)
        nt_ref[:, base:base + gp] = n_re.astype(BF16)
        nt_ref[:, base + gp:base + 2 * gp] = (-n_im).astype(BF16)
    r2 = lax.broadcasted_iota(jnp.int32, (w_, w_), 0)
    c2 = lax.broadcasted_iota(jnp.int32, (w_, w_), 1)
    same2 = (r2 // S5_GROUP) == (c2 // S5_GROUP)
    skip = jnp.where(r2 == c2, d_ref[...], 0.0)
    for i in range(t_):
        k_f = kt_ref[0, jnp.maximum(i - t, 0)] * jnp.where(i >= t, 1.0, 0.0)
        k_b = kt_ref[1, jnp.maximum(t - i, 0)] * jnp.where(t >= i, 1.0, 0.0)
        blk = jnp.where(same2, k_f + k_b, 0.0) + skip * jnp.where(t == i, 1.0, 0.0)
        m_ref[:, i * w_:(i + 1) * w_] = blk.astype(BF16)


def _s5_mix(u, ops, h0):
    m_tot, q_all, n_t, a_t = ops
    b, n, w = u.shape
    k = n // S5_T
    rows = b * k
    uu = u.reshape(rows, S5_T * w)
    tm = min(rows, 512)
    g = _s5_increments(uu, q_all, tm=tm, tn=1024)
    h, e = _s5_scan(g.reshape(b, k, S5_STATE_W), a_t, h0)
    y = _s5_outputs(uu, [hc.reshape(rows, -1) for hc in h], m_tot, n_t, tm=tm, tn=512)
    return y.reshape(b, n, w), e


def _hy_filter_kernel(f_ref, w1_ref, b1_ref, w2_ref, b2_ref, w3_ref, fr_ref, dl_ref, o_ref, *, half_len, tile):
    feats = f_ref[...]
    h = jnp.dot(feats, w1_ref[...], preferred_element_type=F32, precision=HI) + b1_ref[...]
    h = jnp.sin(fr_ref[0:1, :] * h)
    h = jnp.dot(h, w2_ref[...], preferred_element_type=F32, precision=HI) + b2_ref[...]
    h = jnp.sin(fr_ref[1:2, :] * h)
    o = jnp.dot(h, w3_ref[...], preferred_element_type=F32, precision=HI)
    t = feats[:, 0:1]
    o = o * jnp.exp(-t * dl_ref[...])
    nw = HY_ORDER * HY_WIDTH
    n = pl.program_id(0) * tile + lax.broadcasted_iota(jnp.int32, (tile, 1), 0)
    o_ref[...] = jnp.where(n < half_len, o[:, :nw], jnp.where(n == half_len, 0.0, o[:, nw:]))


def _hy_filters(n_tokens, freq, w1, b1, w2, b2, w3):
    n2 = 2 * n_tokens
    idx = jnp.arange(n2, dtype=F32)
    pos = jnp.where(idx <= n_tokens, idx, n2 - idx)[:, None]
    t = pos / max(n_tokens - 1, 1)
    bands = jnp.linspace(1e-4, HY_BANDS - 1, HY_BANDS, dtype=F32)[None]
    ang = 2.0 * math.pi * bands * pos / n_tokens
    feats = jnp.concatenate([t, jnp.cos(ang), -jnp.sin(ang)], axis=-1)
    kpad = LANES - HY_EMB
    feats = jnp.pad(feats, ((0, 0), (0, kpad)))
    w1p = jnp.pad(w1.astype(F32), ((0, kpad), (0, 0)))
    deltas = jnp.abs(jnp.linspace(HY_MIN_DECAY, HY_MAX_DECAY, HY_WIDTH, dtype=F32))
    dl = jnp.tile(deltas, 2 * HY_ORDER)[None]
    tile = min(n2, 1024)
    nw = 2 * HY_ORDER * HY_WIDTH
    full = lambda s: pl.BlockSpec(s, lambda i: (0,) * len(s))
    return pl.pallas_call(
        functools.partial(_hy_filter_kernel, half_len=n_tokens, tile=tile),
        grid=(n2 // tile,),
        in_specs=[pl.BlockSpec((tile, LANES), lambda i: (i, 0)),
                  full((LANES, HY_FFN)), full((1, HY_FFN)), full((HY_FFN, HY_FFN)), full((1, HY_FFN)),
                  full((HY_FFN, nw)), full((2, HY_FFN)), full((1, nw))],
        out_specs=pl.BlockSpec((tile, HY_ORDER * HY_WIDTH), lambda i: (i, 0)),
        out_shape=jax.ShapeDtypeStruct((n2, HY_ORDER * HY_WIDTH), F32),
        compiler_params=_cparams(("parallel",)), name="hy_filter",
    )(feats, w1p, b1.astype(F32)[None], w2.astype(F32), b2.astype(F32)[None], w3.astype(F32),
      freq.astype(F32), dl)


FFT_G = 4
FFT_UNROLL = 8


def _fft_tables(n):
    s = n // SUBLANES
    nst = int(round(math.log2(s)))
    runs = []
    half = s // 2
    while half >= FFT_G:
        runs.append(-2 * np.pi * np.arange(half) / (2 * half))
        half //= 2
    ang_s = np.concatenate(runs)
    tw_slab = np.stack([np.cos(ang_s), np.sin(ang_s)]).astype(np.float32)
    tw_slab = np.broadcast_to(tw_slab[..., None, None], (2, ang_s.size, SUBLANES, LANES)).copy()
    pos = np.arange(s)
    rev = np.zeros(s, np.int64)
    for bit in range(nst):
        rev |= ((pos >> bit) & 1) << (nst - 1 - bit)
    ang = -2 * np.pi * (rev[:, None] * np.arange(SUBLANES)[None, :]) / n
    tw_mid = np.stack([np.cos(ang), np.sin(ang)]).astype(np.float32)
    tw_mid = np.broadcast_to(tw_mid[..., None], (2, s, SUBLANES, LANES)).copy()
    t = np.arange(n)
    ang2 = -2 * np.pi * t / (2 * n)
    mod = np.stack([np.cos(ang2), np.sin(ang2)]).astype(np.float32).reshape(2, s, SUBLANES)
    mod = np.broadcast_to(mod[..., None], (2, s, SUBLANES, LANES)).copy()
    return jnp.asarray(tw_slab), jnp.asarray(tw_mid), jnp.asarray(mod)


def _sub_patterns():
    sub = lax.broadcasted_iota(jnp.int32, (SUBLANES, LANES), 0)

    def table(vals):
        out = jnp.full((SUBLANES, LANES), vals[0], F32)
        for k in range(1, SUBLANES):
            out = jnp.where(sub == k, np.float32(vals[k]), out)
        return out

    pats = {}
    for dist in (4, 2, 1):
        lo = (sub & dist) == 0
        sgn = jnp.where(lo, 1.0, -1.0).astype(F32)
        wr = [1.0] * SUBLANES
        wi = [0.0] * SUBLANES
        for k in range(SUBLANES):
            if k & dist:
                e = (k % dist) * (SUBLANES // (2 * dist))
                wr[k] = math.cos(-2 * math.pi * e / SUBLANES)
                wi[k] = math.sin(-2 * math.pi * e / SUBLANES)
        pats[dist] = (lo, sgn, table(wr), table(wi))
    pats["quarter"] = (sub & 3) == 3
    return pats


def _dft8_fwd(vr, vi, pats):
    for dist in (4, 2, 1):
        lo, sgn, wr, wi = pats[dist]
        up_r = pltpu.roll(vr, SUBLANES - dist, 0)
        up_i = pltpu.roll(vi, SUBLANES - dist, 0)
        if dist == 4:
            pr, pi_ = up_r, up_i
        else:
            pr = jnp.where(lo, up_r, pltpu.roll(vr, dist, 0))
            pi_ = jnp.where(lo, up_i, pltpu.roll(vi, dist, 0))
        tr = pr + sgn * vr
        ti = pi_ + sgn * vi
        if dist == 1:
            vr, vi = tr, ti
        elif dist == 2:
            qt = pats["quarter"]
            vr, vi = jnp.where(qt, ti, tr), jnp.where(qt, -tr, ti)
        else:
            vr, vi = _cmul(tr, ti, wr, wi)
    return vr, vi


def _dft8_inv(vr, vi, pats):
    for dist in (1, 2, 4):
        lo, sgn, wr, wi = pats[dist]
        if dist == 2:
            qt = pats["quarter"]
            vr, vi = jnp.where(qt, -vi, vr), jnp.where(qt, vr, vi)
        elif dist == 4:
            vr, vi = _cmul(vr, vi, wr, -wi)
        up_r = pltpu.roll(vr, SUBLANES - dist, 0)
        up_i = pltpu.roll(vi, SUBLANES - dist, 0)
        if dist == 4:
            pr, pi_ = up_r, up_i
        else:
            pr = jnp.where(lo, up_r, pltpu.roll(vr, dist, 0))
            pi_ = jnp.where(lo, up_i, pltpu.roll(vi, dist, 0))
        vr = pr + sgn * vr
        vi = pi_ + sgn * vi
    return vr, vi


def _slab_stage(re, im, tw, half, slabs, inverse):
    per_block = half // FFT_G
    nblk = slabs // (2 * half)
    unroll = min(FFT_UNROLL, nblk * per_block)
    tw_off = slabs - 2 * half
    if nblk >= unroll:
        blocks_per_it = unroll // per_block
        trips = nblk // blocks_per_it
        offsets = [(b * 2 * half + jc * FFT_G, jc * FFT_G) for b in range(blocks_per_it) for jc in range(per_block)]
        data_step, tw_step = blocks_per_it * 2 * half, 0
    else:
        chunks_per_it = unroll // nblk
        trips = per_block // chunks_per_it
        offsets = [(b * 2 * half + k * FFT_G, k * FFT_G) for b in range(nblk) for k in range(chunks_per_it)]
        data_step = tw_step = chunks_per_it * FFT_G

    def body(c, carry):
        d0 = pl.multiple_of(c * data_step, FFT_G)
        t0 = pl.multiple_of(tw_off + c * tw_step, FFT_G)
        twiddles = {}
        for d_off, t_off in offsets:
            if t_off not in twiddles:
                tws = pl.ds(t0 + t_off, FFT_G)
                twiddles[t_off] = (tw[0, tws], tw[1, tws])
            wr, wi = twiddles[t_off]
            lo = pl.ds(d0 + d_off, FFT_G)
            hi = pl.ds(d0 + d_off + half, FFT_G)
            ar, ai, br, bi = re[lo], im[lo], re[hi], im[hi]
            if inverse:
                br, bi = br * wr + bi * wi, bi * wr - br * wi
                re[lo] = ar + br
                im[lo] = ai + bi
                re[hi] = ar - br
                im[hi] = ai - bi
            else:
                re[lo] = ar + br
                im[lo] = ai + bi
                dr, di = ar - br, ai - bi
                re[hi] = dr * wr - di * wi
                im[hi] = dr * wi + di * wr
        return carry

    lax.fori_loop(0, trips, body, 0)


def _fft_forward_big(re, im, tw, slabs):
    half = slabs // 2
    while half >= 4:
        _slab_stage(re, im, tw, half, slabs, inverse=False)
        half //= 2


def _fft_inverse_big(re, im, tw, slabs):
    half = 4
    while half <= slabs // 2:
        _slab_stage(re, im, tw, half, slabs, inverse=True)
        half *= 2


def _radix4_fwd(x):
    (x0r, x0i), (x1r, x1i), (x2r, x2i), (x3r, x3i) = x
    y0r, y0i = x0r + x2r, x0i + x2i
    y2r, y2i = x0r - x2r, x0i - x2i
    y1r, y1i = x1r + x3r, x1i + x3i
    dr, di = x1r - x3r, x1i - x3i
    y3r, y3i = di, -dr
    return [(y0r + y1r, y0i + y1i), (y0r - y1r, y0i - y1i), (y2r + y3r, y2i + y3i), (y2r - y3r, y2i - y3i)]


def _radix4_inv(z):
    (z0r, z0i), (z1r, z1i), (z2r, z2i), (z3r, z3i) = z
    y0r, y0i = z0r + z1r, z0i + z1i
    y1r, y1i = z0r - z1r, z0i - z1i
    y2r, y2i = z2r + z3r, z2i + z3i
    y3r, y3i = z2r - z3r, z2i - z3i
    qr, qi = -y3i, y3r
    return [(y0r + y2r, y0i + y2i), (y1r + qr, y1i + qi), (y0r - y2r, y0i - y2i), (y1r - qr, y1i - qi)]


def _fft_middle(re, im, twm_ref, slabs, pats, spec=None, out=None):
    def body(q, carry):
        p0 = q * FFT_G
        x = [(re[p0 + g], im[p0 + g]) for g in range(FFT_G)]
        z = _radix4_fwd(x)
        res = []
        for g in range(FFT_G):
            twr = twm_ref[0, p0 + g]
            twi = twm_ref[1, p0 + g]
            vr, vi = _cmul(z[g][0], z[g][1], twr, twi)
            vr, vi = _dft8_fwd(vr, vi, pats)
            if spec is None:
                out[0][0, 0, p0 + g] = vr * out[2]
                out[1][0, 0, p0 + g] = vi * out[2]
            else:
                vr, vi = _cmul(vr, vi, spec[0][0, 0, p0 + g], spec[1][0, 0, p0 + g])
                vr, vi = _dft8_inv(vr, vi, pats)
                res.append(_cmul(vr, vi, twr, -twi))
        if spec is not None:
            x = _radix4_inv(res)
            for g in range(FFT_G):
                re[p0 + g] = x[g][0]
                im[p0 + g] = x[g][1]
        return carry

    lax.fori_loop(0, slabs // FFT_G, body, 0)


def _hy_spec_kernel(tw_ref, k_ref, twm_ref, mod_ref, sr_ref, si_ref, re, im, *, slabs):
    h = pl.program_id(2)
    pats = _sub_patterns()
    a = k_ref[0:slabs]
    b = k_ref[slabs:2 * slabs]

    @pl.when(h == 0)
    def _():
        re[...] = a + b
        im[...] = jnp.zeros_like(a)

    @pl.when(h == 1)
    def _():
        dlt = a - b
        re[...] = dlt * mod_ref[0]
        im[...] = dlt * mod_ref[1]

    _fft_forward_big(re, im, tw_ref, slabs)
    scale = np.float32(1.0 / (2 * SUBLANES * slabs))
    _fft_middle(re, im, twm_ref, slabs, pats, out=(sr_ref, si_ref, scale))


def _hy_spectra(k, n_tokens):
    slabs = n_tokens // SUBLANES
    tw, twm, mod = _fft_tables(n_tokens)
    k4 = k.reshape(2 * slabs, SUBLANES, HY_ORDER * HY_WIDTH)
    nt = HY_WIDTH // LANES
    shape = jax.ShapeDtypeStruct((HY_ORDER, 2, slabs, SUBLANES, HY_WIDTH), F32)
    spec_out = pl.BlockSpec((1, 1, slabs, SUBLANES, LANES), lambda o, j, h: (o, h, 0, 0, j))
    tab = pl.BlockSpec((2, slabs, SUBLANES, LANES), lambda o, j, h: (0, 0, 0, 0), pipeline_mode=pl.Buffered(1))
    return pl.pallas_call(
        functools.partial(_hy_spec_kernel, slabs=slabs),
        grid=(HY_ORDER, nt, 2),
        in_specs=[pl.BlockSpec(tw.shape, lambda o, j, h: (0, 0, 0, 0), pipeline_mode=pl.Buffered(1)),
                  pl.BlockSpec((2 * slabs, SUBLANES, LANES), lambda o, j, h: (0, 0, o * nt + j)),
                  tab, tab],
        out_specs=[spec_out, spec_out],
        out_shape=[shape, shape],
        scratch_shapes=[pltpu.VMEM((slabs, SUBLANES, LANES), F32)] * 2,
        compiler_params=_cparams(("parallel", "parallel", "arbitrary")), name="hy_spec",
    )(tw, k4, twm, mod)


def _hy_conv_kernel(tw_ref, u_ref, sr_ref, si_ref, twm_ref, mod_ref, o_ref, re, im, *, slabs):
    h = pl.program_id(2)
    pats = _sub_patterns()

    @pl.when(h == 0)
    def _():
        re[...] = u_ref[0]
        im[...] = u_ref[1]

    @pl.when(h == 1)
    def _():
        ur, ui = u_ref[0], u_ref[1]
        mr, mi = mod_ref[0], mod_ref[1]
        re[...] = ur * mr - ui * mi
        im[...] = ur * mi + ui * mr

    _fft_forward_big(re, im, tw_ref, slabs)
    _fft_middle(re, im, twm_ref, slabs, pats, spec=(sr_ref, si_ref))
    _fft_inverse_big(re, im, tw_ref, slabs)

    @pl.when(h == 0)
    def _():
        o_ref[0] = re[...]
        o_ref[1] = im[...]

    @pl.when(h == 1)
    def _():
        yr, yi = re[...], im[...]
        mr, mi = mod_ref[0], mod_ref[1]
        o_ref[0] = o_ref[0] + (yr * mr + yi * mi)
        o_ref[1] = o_ref[1] + (yi * mr - yr * mi)


def _hy_conv(u, spec_re, spec_im, order):
    b, n, w = u.shape
    slabs = n // SUBLANES
    tw, twm, mod = _fft_tables(n)
    u4 = u.reshape(b, slabs, SUBLANES, w)
    nt = w // LANES
    tab = pl.BlockSpec((2, slabs, SUBLANES, LANES), lambda j, p, h: (0, 0, 0, 0), pipeline_mode=pl.Buffered(1))
    spec_in = pl.BlockSpec((1, 1, slabs, SUBLANES, LANES), lambda j, p, h: (order, h, 0, 0, j))
    io = pl.BlockSpec((2, slabs, SUBLANES, LANES), lambda j, p, h: (p, 0, 0, j))
    y = pl.pallas_call(
        functools.partial(_hy_conv_kernel, slabs=slabs),
        grid=(nt, b // 2, 2),
        in_specs=[pl.BlockSpec(tw.shape, lambda j, p, h: (0, 0, 0, 0), pipeline_mode=pl.Buffered(1)),
                  io, spec_in, spec_in, tab, tab],
        out_specs=io,
        out_shape=jax.ShapeDtypeStruct((b, slabs, SUBLANES, w), F32),
        scratch_shapes=[pltpu.VMEM((slabs, SUBLANES, LANES), F32)] * 2,
        compiler_params=_cparams(("parallel", "parallel", "arbitrary")), name="hy_conv",
    )(tw, u4, spec_re, spec_im, twm, mod)
    return y.reshape(b, n, w)


def _shift_rows(x, n):
    row = lax.broadcasted_iota(jnp.int32, x.shape, 0)
    prev = jnp.where(row == 0, 0.0, pltpu.roll(x, 1, 0))
    nxt = jnp.where(row == n - 1, 0.0, pltpu.roll(x, n - 1, 0))
    return prev, nxt


def _hy_short_kernel(z_ref, w_ref, b_ref, o_ref, *, n):
    x = z_ref[0]
    prev, nxt = _shift_rows(x, n)
    o_ref[0] = prev * w_ref[0:1, :] + x * w_ref[1:2, :] + nxt * w_ref[2:3, :] + b_ref[...]


def _hy_short(z_mix, conv_w, conv_b):
    b, n, _ = z_mix.shape
    c3 = 3 * HY_WIDTH
    w = conv_w.reshape(3, c3).astype(F32)
    tc = 256
    col0 = BRANCH_W // tc
    return pl.pallas_call(
        functools.partial(_hy_short_kernel, n=n),
        grid=(b, c3 // tc),
        in_specs=[pl.BlockSpec((1, n, tc), lambda bi, j: (bi, 0, col0 + j)),
                  pl.BlockSpec((3, tc), lambda bi, j: (0, j)),
                  pl.BlockSpec((1, tc), lambda bi, j: (0, j))],
        out_specs=pl.BlockSpec((1, n, tc), lambda bi, j: (bi, 0, j)),
        out_shape=jax.ShapeDtypeStruct((b, n, c3), F32),
        compiler_params=_cparams(("parallel", "parallel")), name="hy_short",
    )(z_mix, w, conv_b.astype(F32)[None])


def _hy_gate_kernel(g_ref, y_ref, v_ref, b_ref, o_ref):
    v = v_ref[0]
    o_ref[0] = g_ref[0] * (y_ref[0] + b_ref[...] * v)


def _hy_gate(zc, gate_block, y, v, v_block, bias):
    b, n, w = y.shape
    tn = min(n, 1024)
    return pl.pallas_call(
        _hy_gate_kernel,
        grid=(b, n // tn),
        in_specs=[pl.BlockSpec((1, tn, w), lambda bi, i: (bi, i, gate_block)),
                  pl.BlockSpec((1, tn, w), lambda bi, i: (bi, i, 0)),
                  pl.BlockSpec((1, tn, w), lambda bi, i: (bi, i, v_block)),
                  pl.BlockSpec((1, w), lambda bi, i: (0, 0))],
        out_specs=pl.BlockSpec((1, tn, w), lambda bi, i: (bi, i, 0)),
        out_shape=jax.ShapeDtypeStruct((b, n, w), F32),
        compiler_params=_cparams(("parallel", "parallel")), name="hy_gate",
    )(zc, y, v, bias.astype(F32)[None])


def _hyena(z_mix, conv_w, conv_b, spec_re, spec_im, bias):
    zc = _hy_short(z_mix, conv_w, conv_b)
    y1 = _hy_conv(zc[:, :, :HY_WIDTH], spec_re, spec_im, 0)
    v1 = _hy_gate(zc, 1, y1, zc, 0, bias[0])
    y2 = _hy_conv(v1, spec_re, spec_im, 1)
    return _hy_gate(zc, 2, y2, v1, 0, bias[1])


NA_QROWS = 8
NA_KROWS = 16
NA_KPART = 256


def _softmax_parts(parts):
    m = None
    for s in parts:
        mm = jnp.max(s, axis=-1, keepdims=True)
        m = mm if m is None else jnp.maximum(m, mm)
    return m


def _na_kernel(q_ref, k0, k1, k2, k3, v0, v1, v2, v3, kc_ref, vc_ref, bias_ref, o_ref):
    q = q_ref[0] * np.float32(NA_HEAD_DIM ** -0.5)
    k = jnp.concatenate([k0[0], k1[0], k2[0], k3[0]], axis=0).astype(BF16)
    v = jnp.concatenate([v0[0], v1[0], v2[0], v3[0]], axis=0).astype(BF16)
    kc = kc_ref[0].astype(BF16)
    vc = vc_ref[0].astype(BF16)
    nt = (((1,), (1,)), ((), ()))
    outs = []
    for h in range(NA_HEADS):
        sl = slice(h * NA_HEAD_DIM, (h + 1) * NA_HEAD_DIM)
        qh = q[:, sl].astype(BF16)
        s_lat = lax.dot_general(qh, k[:, sl], nt, preferred_element_type=F32) + bias_ref[0, h]
        s_ctx = lax.dot_general(qh, kc[:, sl], nt, preferred_element_type=F32)
        m = _softmax_parts([s_lat, s_ctx])
        p_lat = jnp.exp(s_lat - m)
        p_ctx = jnp.exp(s_ctx - m)
        den = jnp.sum(p_lat, axis=-1, keepdims=True) + jnp.sum(p_ctx, axis=-1, keepdims=True)
        o = jnp.dot(p_lat.astype(BF16), v[:, sl], preferred_element_type=F32) \
            + jnp.dot(p_ctx.astype(BF16), vc[:, sl], preferred_element_type=F32)
        outs.append(o / den)
    o_ref[0] = jnp.concatenate(outs, axis=-1)


def _na_bias(rpb, rows):
    kh = min(NA_WIN_H, rows)
    col = np.arange(GRID_W)
    col_start = np.clip(col - NA_WIN_W // 2, 0, GRID_W - NA_WIN_W)
    col_ok = (col[None] >= col_start[:, None]) & (col[None] < col_start[:, None] + NA_WIN_W)
    off_c = np.clip(col[None] - col[:, None], -(NA_WIN_W - 1), NA_WIN_W - 1) + (NA_WIN_W - 1)
    nblk = rows // NA_QROWS
    out = []
    for j in (0, 1, nblk - 1):
        qr = j * NA_QROWS + np.arange(NA_QROWS)
        ws = int(np.clip(j * NA_QROWS - NA_WIN_H // 2, 0, rows - NA_KROWS))
        kr = ws + np.arange(NA_KROWS)
        start = np.clip(qr - kh // 2, 0, rows - kh)
        row_ok = (kr[None] >= start[:, None]) & (kr[None] < start[:, None] + kh)
        off_r = np.clip(kr[None] - qr[:, None] + (NA_WIN_H - 1), 0, 2 * NA_WIN_H - 2)
        sel_r = np.eye(2 * NA_WIN_H - 1, dtype=np.float32)[off_r]
        sel_c = np.eye(2 * NA_WIN_W - 1, dtype=np.float32)[off_c]
        bias = jnp.einsum('rka,hab,qcb->hrqkc', sel_r, rpb.astype(F32), sel_c, precision=HI)
        ok = row_ok[:, None, :, None] & col_ok[None, :, None, :]
        bias = jnp.where(jnp.asarray(ok)[None], bias, NEG_INF)
        out.append(bias.reshape(NA_HEADS, NA_QROWS * GRID_W, NA_KROWS * GRID_W))
    return jnp.stack(out)


def _na_attention(z_mix, zc_mix, rpb):
    b, n, _ = z_mix.shape
    n_ctx = zc_mix.shape[1]
    rows = n // GRID_W
    nblk = rows // NA_QROWS
    tq = NA_QROWS * GRID_W
    bias = _na_bias(rpb, rows)
    qb, kb, vb = 1024 // BRANCH_W, 1024 // BRANCH_W + 1, 1024 // BRANCH_W + 2
    kparts = NA_KROWS * GRID_W // NA_KPART
    max_k0 = (rows - NA_KROWS) * GRID_W // NA_KPART

    def k0_of(j):
        return jnp.clip(2 * j - 1, 0, max_k0)

    def kspec(part, blk):
        return pl.BlockSpec((1, NA_KPART, BRANCH_W), lambda bi, j: (bi, k0_of(j) + part, blk))

    def cls(j):
        return jnp.where(j == 0, 0, jnp.where(j == nblk - 1, 2, 1))

    return pl.pallas_call(
        _na_kernel,
        grid=(b, nblk),
        in_specs=[pl.BlockSpec((1, tq, BRANCH_W), lambda bi, j: (bi, j, qb))]
        + [kspec(p, kb) for p in range(kparts)] + [kspec(p, vb) for p in range(kparts)]
        + [pl.BlockSpec((1, n_ctx, BRANCH_W), lambda bi, j: (bi, 0, kb)),
           pl.BlockSpec((1, n_ctx, BRANCH_W), lambda bi, j: (bi, 0, vb)),
           pl.BlockSpec((1, NA_HEADS, tq, NA_KROWS * GRID_W), lambda bi, j: (cls(j), 0, 0, 0))],
        out_specs=pl.BlockSpec((1, tq, BRANCH_W), lambda bi, j: (bi, j, 0)),
        out_shape=jax.ShapeDtypeStruct((b, n, BRANCH_W), F32),
        compiler_params=_cparams(("parallel", "arbitrary")), name="na_attn",
    )(z_mix, *([z_mix] * (2 * kparts)), zc_mix, zc_mix, bias)


def _rope_tables(n_tokens, head_dim, heads):
    t = jnp.arange(n_tokens, dtype=jnp.int32)
    row = (t // GRID_W).astype(F32)
    col = (t % GRID_W).astype(F32)
    half = head_dim // 2
    inv = ROPE_BASE ** (-(jnp.arange(0, half, 2, dtype=F32) / half))
    ang = jnp.concatenate([row[:, None] * inv, col[:, None] * inv], axis=-1)
    cos, sin = jnp.cos(ang), jnp.sin(ang)
    cos_t = jnp.tile(jnp.concatenate([cos, cos], axis=-1), (1, heads))
    sin_t = jnp.tile(jnp.concatenate([-sin, sin], axis=-1), (1, heads))
    return cos_t, sin_t


def _rope(x, cos_t, sin_t):
    w = x.shape[-1]
    half = SW_HEAD_DIM // 2
    lane = lax.broadcasted_iota(jnp.int32, x.shape, 1)
    first = (lane % SW_HEAD_DIM) < half
    partner = jnp.where(first, pltpu.roll(x, w - half, 1), pltpu.roll(x, half, 1))
    return x * cos_t + partner * sin_t


def _sw_kernel(q_ref, kp_ref, kn_ref, kx_ref, vp_ref, vn_ref, vx_ref, kc_ref, vc_ref,
               cq_ref, sq_ref, cp_ref, sp_ref, cn_ref, sn_ref, cx_ref, sx_ref, sink_ref, o_ref, *, nblk):
    i = pl.program_id(1)
    kvw = SW_KV_HEADS * SW_HEAD_DIM
    q = _rope(q_ref[0], cq_ref[...], sq_ref[...]) * np.float32(SW_HEAD_DIM ** -0.5)
    k = jnp.concatenate([_rope(kp_ref[0], cp_ref[:, :kvw], sp_ref[:, :kvw]),
                         _rope(kn_ref[0], cn_ref[:, :kvw], sn_ref[:, :kvw]),
                         _rope(kx_ref[0], cx_ref[:, :kvw], sx_ref[:, :kvw])], axis=0).astype(BF16)
    v = jnp.concatenate([vp_ref[0], vn_ref[0], vx_ref[0]], axis=0).astype(BF16)
    kc = kc_ref[0].astype(BF16)
    vc = vc_ref[0].astype(BF16)
    g_ = SW_HEADS // SW_KV_HEADS
    rows = g_ * SW_BLOCK
    span = 3 * SW_BLOCK
    r = lax.broadcasted_iota(jnp.int32, (rows, span), 0) % SW_BLOCK
    c = lax.broadcasted_iota(jnp.int32, (rows, span), 1)
    diff = c - SW_BLOCK - r
    blk = i + c // SW_BLOCK - 1
    ok = (jnp.abs(diff) <= SW_WINDOW) & (blk >= 0) & (blk < nblk)
    nt = (((1,), (1,)), ((), ()))
    outs = []
    for kv in range(SW_KV_HEADS):
        ksl = slice(kv * SW_HEAD_DIM, (kv + 1) * SW_HEAD_DIM)
        qg = jnp.concatenate([q[:, (kv * g_ + g) * SW_HEAD_DIM:(kv * g_ + g + 1) * SW_HEAD_DIM] for g in range(g_)],
                             axis=0).astype(BF16)
        s_lat = jnp.where(ok, lax.dot_general(qg, k[:, ksl], nt, preferred_element_type=F32), NEG_INF)
        s_ctx = lax.dot_general(qg, kc[:, ksl], nt, preferred_element_type=F32)
        rr = lax.broadcasted_iota(jnp.int32, (rows, 1), 0)
        s_sink = jnp.zeros((rows, 1), F32)
        for g in range(g_):
            s_sink = jnp.where(rr // SW_BLOCK == g, sink_ref[kv * g_ + g], s_sink)
        m = jnp.maximum(_softmax_parts([s_lat, s_ctx]), s_sink)
        p_lat = jnp.exp(s_lat - m)
        p_ctx = jnp.exp(s_ctx - m)
        den = jnp.sum(p_lat, axis=-1, keepdims=True) + jnp.sum(p_ctx, axis=-1, keepdims=True) + jnp.exp(s_sink - m)
        o = jnp.dot(p_lat.astype(BF16), v[:, ksl], preferred_element_type=F32) \
            + jnp.dot(p_ctx.astype(BF16), vc[:, ksl], preferred_element_type=F32)
        o = o / den
        outs += [o[g * SW_BLOCK:(g + 1) * SW_BLOCK] for g in range(g_)]
    o_ref[0] = jnp.concatenate(outs, axis=-1)


def _sw_attention(z_mix, zc_mix, sink):
    b, n, _ = z_mix.shape
    n_ctx = zc_mix.shape[1]
    nblk = n // SW_BLOCK
    cos_t, sin_t = _rope_tables(n, SW_HEAD_DIM, SW_HEADS)
    qw = SW_HEADS * SW_HEAD_DIM
    kvw = SW_KV_HEADS * SW_HEAD_DIM
    q_blk = 1792 // qw
    k_blk = (1792 + qw) // kvw
    v_blk = k_blk + 1

    def prev(i):
        return jnp.maximum(i - 1, 0)

    def nxt(i):
        return jnp.minimum(i + 1, nblk - 1)

    def tok(fn, blk):
        return pl.BlockSpec((1, SW_BLOCK, kvw), lambda bi, i: (bi, fn(i), blk))

    def tab(fn):
        return pl.BlockSpec((SW_BLOCK, qw), lambda bi, i: (fn(i), 0))

    ident = lambda i: i
    return pl.pallas_call(
        functools.partial(_sw_kernel, nblk=nblk),
        grid=(b, nblk),
        in_specs=[pl.BlockSpec((1, SW_BLOCK, qw), lambda bi, i: (bi, i, q_blk)),
                  tok(prev, k_blk), tok(ident, k_blk), tok(nxt, k_blk),
                  tok(prev, v_blk), tok(ident, v_blk), tok(nxt, v_blk),
                  pl.BlockSpec((1, n_ctx, kvw), lambda bi, i: (bi, 0, k_blk)),
                  pl.BlockSpec((1, n_ctx, kvw), lambda bi, i: (bi, 0, v_blk)),
                  tab(ident), tab(ident), tab(prev), tab(prev), tab(ident), tab(ident), tab(nxt), tab(nxt),
                  pl.BlockSpec(memory_space=pltpu.SMEM)],
        out_specs=pl.BlockSpec((1, SW_BLOCK, qw), lambda bi, i: (bi, i, 0)),
        out_shape=jax.ShapeDtypeStruct((b, n, qw), F32),
        compiler_params=_cparams(("parallel", "arbitrary")), name="sw_attn",
    )(z_mix, *([z_mix] * 6), zc_mix, zc_mix, cos_t, sin_t, cos_t, sin_t, cos_t, sin_t, cos_t, sin_t,
      sink.astype(F32))


def _ctx_attn_kernel(q_ref, k_ref, v_ref, sink_ref, o_ref, *, heads, kv_heads, dh, use_sink):
    q = q_ref[0] * np.float32(dh ** -0.5)
    k = k_ref[0].astype(BF16)
    v = v_ref[0].astype(BF16)
    g_ = heads // kv_heads
    nt = (((1,), (1,)), ((), ()))
    outs = []
    for h in range(heads):
        kv = h // g_
        s = lax.dot_general(q[:, h * dh:(h + 1) * dh].astype(BF16), k[:, kv * dh:(kv + 1) * dh], nt,
                            preferred_element_type=F32)
        m = jnp.max(s, axis=-1, keepdims=True)
        if use_sink:
            m = jnp.maximum(m, sink_ref[h])
        p = jnp.exp(s - m)
        den = jnp.sum(p, axis=-1, keepdims=True)
        if use_sink:
            den = den + jnp.exp(sink_ref[h] - m)
        outs.append(jnp.dot(p.astype(BF16), v[:, kv * dh:(kv + 1) * dh], preferred_element_type=F32) / den)
    o_ref[0] = jnp.concatenate(outs, axis=-1)


def _ctx_attention(zc_mix, col0, heads, kv_heads, dh, sink):
    b, n, _ = zc_mix.shape
    qw, kvw = heads * dh, kv_heads * dh
    use_sink = sink is not None
    sink_arr = sink.astype(F32) if use_sink else jnp.zeros((heads,), F32)
    return pl.pallas_call(
        functools.partial(_ctx_attn_kernel, heads=heads, kv_heads=kv_heads, dh=dh, use_sink=use_sink),
        grid=(b,),
        in_specs=[pl.BlockSpec((1, n, qw), lambda bi: (bi, 0, col0 // qw)),
                  pl.BlockSpec((1, n, kvw), lambda bi: (bi, 0, (col0 + qw) // kvw)),
                  pl.BlockSpec((1, n, kvw), lambda bi: (bi, 0, (col0 + qw) // kvw + 1)),
                  pl.BlockSpec(memory_space=pltpu.SMEM)],
        out_specs=pl.BlockSpec((1, n, qw), lambda bi: (bi, 0, 0)),
        out_shape=jax.ShapeDtypeStruct((b, n, qw), F32),
        compiler_params=_cparams(("parallel",)), name="ctx_attn",
    )(zc_mix, zc_mix, zc_mix, sink_arr)


def _merge_kernel(x_ref, s5_ref, hy_ref, na_ref, sw_ref, gt_ref, ga_ref, wglu_ref, wb_ref, wo_ref,
                  lg_ref, lb_ref, o_ref):
    g = jax.nn.gelu(s5_ref[0])
    s5 = g * jax.nn.sigmoid(jnp.dot(g.astype(BF16), wglu_ref[...], preferred_element_type=F32))
    branches = (s5, hy_ref[0], na_ref[0], sw_ref[0])
    acc = None
    for n in range(N_BRANCH):
        proj = jnp.dot(branches[n].astype(BF16), wb_ref[n], preferred_element_type=F32)
        t = gt_ref[0, :, n * D_MODEL:(n + 1) * D_MODEL].astype(F32) * proj
        acc = t if acc is None else acc + t
    mix = jnp.dot(acc.astype(BF16), wo_ref[...], preferred_element_type=F32)
    y = np.float32(DEEPNORM_ALPHA) * x_ref[0] + ga_ref[0] * mix
    o_ref[0] = _layernorm(y) * lg_ref[...] + lb_ref[...]


def _merge(x, s5y, hy, na, sw, gates, g_a, w_glu, w_branch, w_out, ln_g, ln_b, *, tm):
    b, l, d = x.shape
    br = pl.BlockSpec((1, tm, BRANCH_W), lambda bi, i: (bi, i, 0))
    full = lambda s: pl.BlockSpec(s, lambda bi, i: (0,) * len(s))
    return pl.pallas_call(
        _merge_kernel,
        grid=(b, l // tm),
        in_specs=[pl.BlockSpec((1, tm, d), lambda bi, i: (bi, i, 0)), br, br, br, br,
                  pl.BlockSpec((1, tm, GATE_W), lambda bi, i: (bi, i, 0)),
                  pl.BlockSpec((1, 1, d), lambda bi, i: (bi, 0, 0)),
                  full((BRANCH_W, BRANCH_W)), full((N_BRANCH, BRANCH_W, d)), full((d, d)),
                  full((1, d)), full((1, d))],
        out_specs=pl.BlockSpec((1, tm, d), lambda bi, i: (bi, i, 0)),
        out_shape=jax.ShapeDtypeStruct((b, l, d), F32),
        compiler_params=_cparams(("parallel", "parallel")), name="merge",
    )(x, s5y, hy, na, sw, gates, g_a, w_glu, w_branch, w_out, ln_g, ln_b)


def _mlp_kernel(x_ref, sh_ref, sc_ref, gm_ref, w1_ref, w2_ref, lg_ref, lb_ref, o_ref, h_scr, acc_scr):
    j = pl.program_id(2)

    @pl.when(j == 0)
    def _():
        h = _layernorm(x_ref[0]) * (1.0 + sc_ref[0]) + sh_ref[0]
        h_scr[...] = h.astype(BF16)
        acc_scr[...] = jnp.zeros_like(acc_scr)

    a = jnp.dot(h_scr[...], w1_ref[...], preferred_element_type=F32)
    a = jnp.square(jnp.maximum(a, 0.0))
    acc_scr[...] += jnp.dot(a.astype(BF16), w2_ref[...], preferred_element_type=F32)

    @pl.when(j == pl.num_programs(2) - 1)
    def _():
        y = np.float32(DEEPNORM_ALPHA) * x_ref[0] + gm_ref[0] * acc_scr[...]
        o_ref[0] = _layernorm(y) * lg_ref[...] + lb_ref[...]


def _mlp(x, sh, sc, g_m, w1, w2, ln_g, ln_b, *, tm, th):
    b, l, d = x.shape
    hdim = w1.shape[1]
    mod = pl.BlockSpec((1, 1, d), lambda bi, i, j: (bi, 0, 0))
    vec = pl.BlockSpec((1, d), lambda bi, i, j: (0, 0))
    return pl.pallas_call(
        _mlp_kernel,
        grid=(b, l // tm, hdim // th),
        in_specs=[pl.BlockSpec((1, tm, d), lambda bi, i, j: (bi, i, 0)), mod, mod, mod,
                  pl.BlockSpec((d, th), lambda bi, i, j: (0, j)),
                  pl.BlockSpec((th, d), lambda bi, i, j: (j, 0)), vec, vec],
        out_specs=pl.BlockSpec((1, tm, d), lambda bi, i, j: (bi, i, 0)),
        out_shape=jax.ShapeDtypeStruct((b, l, d), F32),
        scratch_shapes=[pltpu.VMEM((tm, d), BF16), pltpu.VMEM((tm, d), F32)],
        compiler_params=_cparams(("parallel", "parallel", "arbitrary")), name="mlp",
    )(x, sh, sc, g_m, w1, w2, ln_g, ln_b)


def kernel(x, c, ctx, c_ctx, w_ada, b_ada, w_in, s5_lambda_re, s5_lambda_im, s5_log_dt, s5_b_re, s5_b_im, s5_c_re,
           s5_c_im, s5_d, s5_w_glu, hy_conv_w, hy_conv_b, hy_freq, hy_w1, hy_b1, hy_w2, hy_b2, hy_w3, hy_bias,
           na_rpb, sw_sink, w_branch, w_out, ln1_g, ln1_b, w_mlp1, w_mlp2, ln2_g, ln2_b):
    b, l, d = x.shape
    n_ctx = ctx.shape[1]
    depth = w_ada.shape[0]
    cc = jnp.zeros((8, d), F32).at[:b].set(c.astype(F32)).at[b].set(c_ctx.astype(F32))
    mod_all = _ada(cc, w_ada.astype(F32), b_ada.astype(F32))
    xc = ctx
    for layer in range(depth):
        need_ctx_out = layer < depth - 1
        mod = mod_all[layer, :b].reshape(b, 1, 6, d)
        mod_c = jnp.broadcast_to(mod_all[layer, b].reshape(1, 1, 6, d), (b, 1, 6, d))
        sh_a, sc_a, g_a, sh_m, sc_m, g_m = [mod[:, :, i] for i in range(6)]
        csh_a, csc_a, cg_a, csh_m, csc_m, cg_m = [mod_c[:, :, i] for i in range(6)]
        w_in_l = w_in[layer].astype(BF16)
        w_mix, w_gate = w_in_l[:, :MIX_W], w_in_l[:, MIX_W:]

        z_mix = _ln_mod_mm(x, sh_a, sc_a, w_mix, tm=1024, tn=MIX_W // 3)
        gates = _ln_mod_mm(x, sh_a, sc_a, w_gate, tm=1024, tn=1024, act="sigmoid", out_dtype=BF16)
        zc_mix = _ln_mod_mm(xc, csh_a, csc_a, w_mix, tm=n_ctx, tn=MIX_W // 3)

        ops = _s5_operators(s5_lambda_re[layer], s5_lambda_im[layer], s5_log_dt[layer], s5_b_re[layer],
                            s5_b_im[layer], s5_c_re[layer], s5_c_im[layer], s5_d[layer])
        yc_s5, e_ctx = _s5_mix(zc_mix[:, :, :BRANCH_W], ops, [jnp.zeros((b, 1, S5_STATE_W // 4), F32)] * 4)
        y_s5, _ = _s5_mix(z_mix[:, :, :BRANCH_W], ops, e_ctx)

        hy_args = (hy_freq[layer], hy_w1[layer], hy_b1[layer], hy_w2[layer], hy_b2[layer], hy_w3[layer])
        sp_re, sp_im = _hy_spectra(_hy_filters(l, *hy_args), l)
        hy_l = _hyena(z_mix, hy_conv_w[layer], hy_conv_b[layer], sp_re, sp_im, hy_bias[layer])

        na_l = _na_attention(z_mix, zc_mix, na_rpb[layer])
        sw_l = _sw_attention(z_mix, zc_mix, sw_sink[layer])

        w_glu = s5_w_glu[layer].astype(BF16)
        w_br = w_branch[layer].astype(BF16)
        w_o = w_out[layer].astype(BF16)
        lg1, lb1 = ln1_g[layer].astype(F32)[None], ln1_b[layer].astype(F32)[None]
        lg2, lb2 = ln2_g[layer].astype(F32)[None], ln2_b[layer].astype(F32)[None]
        w1 = w_mlp1[layer].astype(BF16)
        w2 = w_mlp2[layer].astype(BF16)

        x_new = _merge(x, y_s5, hy_l, na_l, sw_l, gates, g_a, w_glu, w_br, w_o, lg1, lb1, tm=512)
        x_new = _mlp(x_new, sh_m, sc_m, g_m, w1, w2, lg2, lb2, tm=1024, th=1024)

        if need_ctx_out:
            gates_c = _ln_mod_mm(xc, csh_a, csc_a, w_gate, tm=n_ctx, tn=1024, act="sigmoid", out_dtype=BF16)
            spc_re, spc_im = _hy_spectra(_hy_filters(n_ctx, *hy_args), n_ctx)
            hy_c = _hyena(zc_mix, hy_conv_w[layer], hy_conv_b[layer], spc_re, spc_im, hy_bias[layer])
            na_c = _ctx_attention(zc_mix, 1024, NA_HEADS, NA_HEADS, NA_HEAD_DIM, None)
            sw_c = _ctx_attention(zc_mix, 1792, SW_HEADS, SW_KV_HEADS, SW_HEAD_DIM, sw_sink[layer])
            xc_new = _merge(xc, yc_s5, hy_c, na_c, sw_c, gates_c, cg_a, w_glu, w_br, w_o, lg1, lb1, tm=n_ctx)
            xc = _mlp(xc_new, csh_m, csc_m, cg_m, w1, w2, lg2, lb2, tm=n_ctx, th=1024)
        x = x_new
    return x
```

```python
import functools
import math

import numpy as np
import jax
import jax.numpy as jnp
from jax import lax
from jax.experimental import pallas as pl
from jax.experimental.pallas import tpu as pltpu

F32 = jnp.float32
BF16 = jnp.bfloat16

D_MODEL = 1024
GRID_W = 64
BRANCH_W = 256
N_BRANCH = 4
S5_GROUP = 16
S5_GROUPS = 16
S5_STATE = 64
HY_WIDTH = 256
HY_ORDER = 2
HY_BANDS = 16
HY_EMB = 2 * HY_BANDS + 1
HY_FFN = 64
HY_MIN_DECAY = math.log(1e-2) / 1.5
HY_MAX_DECAY = math.log(1e-2) / 0.3
NA_HEADS = 4
NA_HEAD_DIM = 64
NA_WIN_H = 8
NA_WIN_W = 16
SW_HEADS = 4
SW_KV_HEADS = 2
SW_HEAD_DIM = 64
SW_WINDOW = 128
SW_BLOCK = 128
MLP_HIDDEN = 4 * D_MODEL
ROPE_BASE = 10000.0
LN_EPS = 1e-6
NEG_INF = -1e30
DEPTH = 2
DEEPNORM_ALPHA = (2 * DEPTH) ** 0.25

MIX_W = 2304
GATE_W = N_BRANCH * D_MODEL
S5_T = 8
S5_STATE_W = 4 * S5_GROUPS * S5_STATE
SUBLANES = 8
LANES = 128
VMEM_LIMIT = 56 * 1024 * 1024

HI = lax.Precision.HIGHEST


def _cparams(sem):
    return pltpu.CompilerParams(dimension_semantics=sem, vmem_limit_bytes=VMEM_LIMIT)


def _layernorm(x):
    mu = jnp.mean(x, axis=-1, keepdims=True)
    xc = x - mu
    var = jnp.mean(xc * xc, axis=-1, keepdims=True)
    return xc * lax.rsqrt(var + LN_EPS)


def _ada_kernel(c_ref, w_ref, b_ref, o_ref):
    c = c_ref[...]
    a = c * jax.nn.sigmoid(c)
    o_ref[0] = jnp.dot(a, w_ref[0], preferred_element_type=F32, precision=HI) + b_ref[0]


def _ada(cc, w_ada, b_ada):
    depth, d, n = w_ada.shape
    tn = 512
    return pl.pallas_call(
        _ada_kernel,
        grid=(depth, n // tn),
        in_specs=[pl.BlockSpec((8, d), lambda l, j: (0, 0)),
                  pl.BlockSpec((1, d, tn), lambda l, j: (l, 0, j)),
                  pl.BlockSpec((1, 1, tn), lambda l, j: (l, 0, j))],
        out_specs=pl.BlockSpec((1, 8, tn), lambda l, j: (l, 0, j)),
        out_shape=jax.ShapeDtypeStruct((depth, 8, n), F32),
        compiler_params=_cparams(("parallel", "parallel")), name="ada",
    )(cc, w_ada, b_ada.reshape(depth, 1, n))


def _ln_mod_mm_kernel(x_ref, sh_ref, sc_ref, w_ref, o_ref, h_scr, *, act):
    @pl.when(pl.program_id(2) == 0)
    def _():
        h = _layernorm(x_ref[0]) * (1.0 + sc_ref[0]) + sh_ref[0]
        h_scr[...] = h.astype(BF16)

    r = jnp.dot(h_scr[...], w_ref[...], preferred_element_type=F32)
    if act == "sigmoid":
        r = jax.nn.sigmoid(r)
    o_ref[0] = r.astype(o_ref.dtype)


def _ln_mod_mm(x, sh, sc, w, *, tm, tn, act=None, out_dtype=F32):
    b, l, d = x.shape
    n = w.shape[1]
    return pl.pallas_call(
        functools.partial(_ln_mod_mm_kernel, act=act),
        grid=(b, l // tm, n // tn),
        in_specs=[pl.BlockSpec((1, tm, d), lambda bi, i, j: (bi, i, 0)),
                  pl.BlockSpec((1, 1, d), lambda bi, i, j: (bi, 0, 0)),
                  pl.BlockSpec((1, 1, d), lambda bi, i, j: (bi, 0, 0)),
                  pl.BlockSpec((d, tn), lambda bi, i, j: (0, j))],
        out_specs=pl.BlockSpec((1, tm, tn), lambda bi, i, j: (bi, i, j)),
        out_shape=jax.ShapeDtypeStruct((b, l, n), out_dtype),
        scratch_shapes=[pltpu.VMEM((tm, d), BF16)],
        compiler_params=_cparams(("parallel", "parallel", "arbitrary")), name="ln_mod_mm",
    )(x, sh, sc, w)


def _s5_inc_kernel(u_ref, q_ref, o_ref):
    o_ref[...] = jnp.dot(u_ref[...].astype(BF16), q_ref[...], preferred_element_type=F32)


def _s5_increments(uu, q_all, *, tm, tn):
    m, k = uu.shape
    n = q_all.shape[1]
    return pl.pallas_call(
        _s5_inc_kernel,
        grid=(m // tm, n // tn),
        in_specs=[pl.BlockSpec((tm, k), lambda i, j: (i, 0)), pl.BlockSpec((k, tn), lambda i, j: (0, j))],
        out_specs=pl.BlockSpec((tm, tn), lambda i, j: (i, j)),
        out_shape=jax.ShapeDtypeStruct((m, n), F32),
        compiler_params=_cparams(("parallel", "parallel")), name="s5_inc",
    )(uu, q_all)


def _s5_out_kernel(u_ref, h0, h1, h2, h3, m_ref, n0, n1, n2, n3, o_ref):
    nt = (((1,), (1,)), ((), ()))
    acc = jnp.dot(u_ref[...].astype(BF16), m_ref[...], preferred_element_type=F32)
    for h_ref, n_ref in ((h0, n0), (h1, n1), (h2, n2), (h3, n3)):
        acc += lax.dot_general(h_ref[...].astype(BF16), n_ref[...], nt, preferred_element_type=F32)
    o_ref[...] = acc


def _s5_outputs(uu, h, m_tot, n_t, *, tm, tn):
    m, k = uu.shape
    gp = h[0].shape[1]
    n = m_tot.shape[1]
    return pl.pallas_call(
        _s5_out_kernel,
        grid=(m // tm, n // tn),
        in_specs=[pl.BlockSpec((tm, k), lambda i, j: (i, 0))]
        + [pl.BlockSpec((tm, gp), lambda i, j: (i, 0))] * 4
        + [pl.BlockSpec((k, tn), lambda i, j: (0, j))]
        + [pl.BlockSpec((tn, gp), lambda i, j, c=c: (j, c)) for c in range(4)],
        out_specs=pl.BlockSpec((tm, tn), lambda i, j: (i, j)),
        out_shape=jax.ShapeDtypeStruct((m, n), F32),
        compiler_params=_cparams(("parallel", "parallel")), name="s5_out",
    )(uu, *h, m_tot, n_t, n_t, n_t, n_t)


def _s5_scan_kernel(gfr, gfi, gbr, gbi, afr, afi, abr, abi, h0fr, h0fi, h0br, h0bi,
                    hfr, hfi, hbr, hbi, efr, efi, ebr, ebi, *, n_chunks):
    a_fr = afr[...][None]
    a_fi = afi[...][None]
    a_br = abr[...][None]
    a_bi = abi[...][None]

    def body(k, carry):
        sfr, sfi, sbr, sbi = carry
        kb = n_chunks - 1 - k
        hfr[:, pl.ds(k, 1), :] = sfr
        hfi[:, pl.ds(k, 1), :] = sfi
        hbr[:, pl.ds(kb, 1), :] = sbr
        hbi[:, pl.ds(kb, 1), :] = sbi
        nfr = a_fr * sfr - a_fi * sfi + gfr[:, pl.ds(k, 1), :]
        nfi = a_fr * sfi + a_fi * sfr + gfi[:, pl.ds(k, 1), :]
        nbr = a_br * sbr - a_bi * sbi + gbr[:, pl.ds(kb, 1), :]
        nbi = a_br * sbi + a_bi * sbr + gbi[:, pl.ds(kb, 1), :]
        return nfr, nfi, nbr, nbi

    sfr, sfi, sbr, sbi = lax.fori_loop(0, n_chunks, body, (h0fr[...], h0fi[...], h0br[...], h0bi[...]))
    efr[...] = sfr
    efi[...] = sfi
    ebr[...] = sbr
    ebi[...] = sbi


def _s5_scan(g, a_t, h0):
    b, k, w4 = g.shape
    w = 2 * LANES
    q = w4 // 4
    nb = q // w

    def comp(c):
        return pl.BlockSpec((b, k, w), lambda j, c=c: (0, 0, c * nb + j))

    def comp_a(c):
        return pl.BlockSpec((1, w), lambda j, c=c: (0, c * nb + j))

    state = pl.BlockSpec((b, k, w), lambda j: (0, 0, j))
    edge = pl.BlockSpec((b, 1, w), lambda j: (0, 0, j))
    outs = pl.pallas_call(
        functools.partial(_s5_scan_kernel, n_chunks=k),
        grid=(nb,),
        in_specs=[comp(c) for c in range(4)] + [comp_a(c) for c in range(4)] + [edge] * 4,
        out_specs=[state] * 4 + [edge] * 4,
        out_shape=[jax.ShapeDtypeStruct((b, k, q), F32)] * 4 + [jax.ShapeDtypeStruct((b, 1, q), F32)] * 4,
        compiler_params=_cparams(("parallel",)), name="s5_scan",
    )(g, g, g, g, a_t, a_t, a_t, a_t, *h0)
    return outs[:4], outs[4:]


def _cmul(ar, ai, br, bi):
    return ar * br - ai * bi, ar * bi + ai * br


def _s5_operators(lam_re, lam_im, log_dt, b_re, b_im, c_re, c_im, d):
    t_ = S5_T
    g_, p_, n_ = S5_GROUPS, S5_STATE, S5_GROUP
    gp, w_ = g_ * p_, g_ * n_
    lr, li = lam_re.astype(F32), lam_im.astype(F32)
    dt = jnp.exp(log_dt.astype(F32))[..., None]
    ks = jnp.arange(t_ + 1, dtype=F32)[:, None, None, None]
    mag = jnp.exp(ks * lr * dt)
    pw_re = mag * jnp.cos(ks * li * dt)
    pw_im = mag * jnp.sin(ks * li * dt)
    a_re, a_im = pw_re[1], pw_im[1]
    den = lr ** 2 + li ** 2
    f_re = ((a_re - 1.0) * lr + a_im * li) / den
    f_im = (a_im * lr - (a_re - 1.0) * li) / den
    br, bi = b_re.astype(F32), b_im.astype(F32)
    bb_re = f_re[..., None] * br - f_im[..., None] * bi
    bb_im = f_re[..., None] * bi + f_im[..., None] * br
    cr, ci = c_re.astype(F32), c_im.astype(F32)
    wb_re, wb_im = _cmul(pw_re[:t_, ..., None], pw_im[:t_, ..., None], bb_re[None], bb_im[None])
    kern = jnp.einsum('dgnp,kdgpm->dkgnm', cr, wb_re, precision=HI) \
        - jnp.einsum('dgnp,kdgpm->dkgnm', ci, wb_im, precision=HI)
    k_tile = jnp.tile(kern.transpose(0, 1, 2, 4, 3).reshape(2, t_, w_, n_), (1, 1, 1, g_))

    def rows_tiled(x):
        return jnp.tile(x.transpose(0, 3, 1, 2).reshape(2, n_, gp), (1, g_, 1))

    bt_re, bt_im = rows_tiled(bb_re), rows_tiled(bb_im)
    ct_re, ct_im = rows_tiled(cr.transpose(0, 1, 3, 2)), rows_tiled(ci.transpose(0, 1, 3, 2))
    pr = pw_re.transpose(1, 0, 2, 3).reshape(2, t_ + 1, gp)
    pi_ = pw_im.transpose(1, 0, 2, 3).reshape(2, t_ + 1, gp)
    full = lambda s: pl.BlockSpec(s, lambda t: (0,) * len(s))
    m_tot, q_all, n_t = pl.pallas_call(
        _s5_ops_kernel,
        grid=(t_,),
        in_specs=[full((2, t_ + 1, gp))] * 2 + [full((2, w_, gp))] * 4 + [full((2, t_, w_, w_)), full((1, w_))],
        out_specs=[pl.BlockSpec((w_, t_ * w_), lambda t: (t, 0)),
                   pl.BlockSpec((w_, 4 * gp), lambda t: (t, 0)),
                   pl.BlockSpec((w_, 4 * gp), lambda t: (t, 0))],
        out_shape=[jax.ShapeDtypeStruct((t_ * w_, t_ * w_), BF16),
                   jax.ShapeDtypeStruct((t_ * w_, 4 * gp), BF16),
                   jax.ShapeDtypeStruct((t_ * w_, 4 * gp), BF16)],
        compiler_params=_cparams(("parallel",)), name="s5_ops",
    )(pr, pi_, bt_re, bt_im, ct_re, ct_im, k_tile, d.astype(F32)[None])
    a_t = jnp.concatenate([pr[0, t_], pi_[0, t_], pr[1, t_], pi_[1, t_]])[None]
    return m_tot, q_all, n_t, a_t


def _s5_ops_kernel(pr_ref, pi_ref, btr_ref, bti_ref, ctr_ref, cti_ref, kt_ref, d_ref, m_ref, q_ref, nt_ref):
    t = pl.program_id(0)
    t_ = S5_T
    w_, gp = btr_ref.shape[1], btr_ref.shape[2]
    row_g = lax.broadcasted_iota(jnp.int32, (w_, gp), 0) // S5_GROUP
    col_g = lax.broadcasted_iota(jnp.int32, (w_, gp), 1) // S5_STATE
    same = row_g == col_g
    for dr in range(2):
        e_q = (t_ - 1 - t) if dr == 0 else t
        e_n = (t + 1) if dr == 0 else (t_ - t)
        b_re = jnp.where(same, btr_ref[dr], 0.0)
        b_im = jnp.where(same, bti_ref[dr], 0.0)
        q_re, q_im = _cmul(b_re, b_im, pr_ref[dr, pl.ds(e_q, 1), :], pi_ref[dr, pl.ds(e_q, 1), :])
        c_re = jnp.where(same, ctr_ref[dr], 0.0)
        c_im = jnp.where(same, cti_ref[dr], 0.0)
        n_re, n_im = _cmul(c_re, c_im, pr_ref[dr, pl.ds(e_n, 1), :], pi_ref[dr, pl.ds(e_n, 1), :])
        base = 2 * dr * gp
        q_ref[:, base:base + gp] = q_re.astype(BF16)
        q_ref[:, base + gp:base + 2 * gp] = q_im.astype(BF16)
        nt_ref[:, base:base + gp] = n_re.astype(BF16)
        nt_ref[:, base + gp:base + 2 * gp] = (-n_im).astype(BF16)
    r2 = lax.broadcasted_iota(jnp.int32, (w_, w_), 0)
    c2 = lax.broadcasted_iota(jnp.int32, (w_, w_), 1)
    same2 = (r2 // S5_GROUP) == (c2 // S5_GROUP)
    skip = jnp.where(r2 == c2, d_ref[...], 0.0)
    for i in range(t_):
        k_f = kt_ref[0, jnp.maximum(i - t, 0)] * jnp.where(i >= t, 1.0, 0.0)
        k_b = kt_ref[1, jnp.maximum(t - i, 0)] * jnp.where(t >= i, 1.0, 0.0)
        blk = jnp.where(same2, k_f + k_b, 0.0) + skip * jnp.where(t == i, 1.0, 0.0)
        m_ref[:, i * w_:(i + 1) * w_] = blk.astype(BF16)


def _s5_mix(u, ops, h0):
    m_tot, q_all, n_t, a_t = ops
    b, n, w = u.shape
    k = n // S5_T
    rows = b * k
    uu = u.reshape(rows, S5_T * w)
    tm = min(rows, 512)
    g = _s5_increments(uu, q_all, tm=tm, tn=1024)
    h, e = _s5_scan(g.reshape(b, k, S5_STATE_W), a_t, h0)
    y = _s5_outputs(uu, [hc.reshape(rows, -1) for hc in h], m_tot, n_t, tm=tm, tn=512)
    return y.reshape(b, n, w), e


def _hy_filter_kernel(f_ref, w1_ref, b1_ref, w2_ref, b2_ref, w3_ref, fr_ref, dl_ref, o_ref, *, half_len, tile):
    feats = f_ref[...]
    h = jnp.dot(feats, w1_ref[...], preferred_element_type=F32, precision=HI) + b1_ref[...]
    h = jnp.sin(fr_ref[0:1, :] * h)
    h = jnp.dot(h, w2_ref[...], preferred_element_type=F32, precision=HI) + b2_ref[...]
    h = jnp.sin(fr_ref[1:2, :] * h)
    o = jnp.dot(h, w3_ref[...], preferred_element_type=F32, precision=HI)
    t = feats[:, 0:1]
    o = o * jnp.exp(-t * dl_ref[...])
    nw = HY_ORDER * HY_WIDTH
    n = pl.program_id(0) * tile + lax.broadcasted_iota(jnp.int32, (tile, 1), 0)
    k = jnp.where(n < half_len, o[:, :nw], jnp.where(n == half_len, 0.0, o[:, nw:]))
    o_ref[...] = k.reshape(o_ref.shape)


def _hy_filters(n_tokens, freq, w1, b1, w2, b2, w3):
    n2 = 2 * n_tokens
    idx = np.arange(n2, dtype=np.float64)
    pos = np.where(idx <= n_tokens, idx, n2 - idx)[:, None]
    t = pos / max(n_tokens - 1, 1)
    bands = np.linspace(1e-4, HY_BANDS - 1, HY_BANDS)[None]
    ang = 2.0 * math.pi * bands * pos / n_tokens
    feats = np.concatenate([t, np.cos(ang), -np.sin(ang)], axis=-1)
    kpad = LANES - HY_EMB
    feats = jnp.asarray(np.pad(feats, ((0, 0), (0, kpad))), F32)
    w1p = jnp.pad(w1.astype(F32), ((0, kpad), (0, 0)))
    deltas = jnp.abs(jnp.linspace(HY_MIN_DECAY, HY_MAX_DECAY, HY_WIDTH, dtype=F32))
    dl = jnp.tile(deltas, 2 * HY_ORDER)[None]
    tile = min(n2, 1024)
    nw = 2 * HY_ORDER * HY_WIDTH
    full = lambda s: pl.BlockSpec(s, lambda i: (0,) * len(s))
    return pl.pallas_call(
        functools.partial(_hy_filter_kernel, half_len=n_tokens, tile=tile),
        grid=(n2 // tile,),
        in_specs=[pl.BlockSpec((tile, LANES), lambda i: (i, 0)),
                  full((LANES, HY_FFN)), full((1, HY_FFN)), full((HY_FFN, HY_FFN)), full((1, HY_FFN)),
                  full((HY_FFN, nw)), full((2, HY_FFN)), full((1, nw))],
        out_specs=pl.BlockSpec((tile // SUBLANES, SUBLANES, HY_ORDER * HY_WIDTH), lambda i: (i, 0, 0)),
        out_shape=jax.ShapeDtypeStruct((n2 // SUBLANES, SUBLANES, HY_ORDER * HY_WIDTH), F32),
        compiler_params=_cparams(("parallel",)), name="hy_filter",
    )(feats, w1p, b1.astype(F32)[None], w2.astype(F32), b2.astype(F32)[None], w3.astype(F32),
      freq.astype(F32), dl)


FFT_G = 4
FFT_UNROLL = 8


def _fft_tables(n):
    s = n // SUBLANES
    nst = int(round(math.log2(s)))
    runs = []
    half = s // 2
    while half >= FFT_G:
        runs.append(-2 * np.pi * np.arange(half) / (2 * half))
        half //= 2
    ang_s = np.concatenate(runs)
    tw_slab = np.stack([np.cos(ang_s), np.sin(ang_s)]).astype(np.float32)
    tw_slab = np.broadcast_to(tw_slab[..., None, None], (2, ang_s.size, SUBLANES, LANES)).copy()
    pos = np.arange(s)
    rev = np.zeros(s, np.int64)
    for bit in range(nst):
        rev |= ((pos >> bit) & 1) << (nst - 1 - bit)
    ang = -2 * np.pi * (rev[:, None] * np.arange(SUBLANES)[None, :]) / n
    tw_mid = np.stack([np.cos(ang), np.sin(ang)]).astype(np.float32)
    tw_mid = np.broadcast_to(tw_mid[..., None], (2, s, SUBLANES, LANES)).copy()
    t = np.arange(n)
    ang2 = -2 * np.pi * t / (2 * n)
    mod = np.stack([np.cos(ang2), np.sin(ang2)]).astype(np.float32).reshape(2, s, SUBLANES)
    mod = np.broadcast_to(mod[..., None], (2, s, SUBLANES, LANES)).copy()
    return jnp.asarray(tw_slab), jnp.asarray(tw_mid), jnp.asarray(mod)


def _sub_patterns():
    sub = lax.broadcasted_iota(jnp.int32, (SUBLANES, LANES), 0)

    def table(vals):
        out = jnp.full((SUBLANES, LANES), vals[0], F32)
        for k in range(1, SUBLANES):
            out = jnp.where(sub == k, np.float32(vals[k]), out)
        return out

    pats = {}
    for dist in (4, 2, 1):
        lo = (sub & dist) == 0
        sgn = jnp.where(lo, 1.0, -1.0).astype(F32)
        wr = [1.0] * SUBLANES
        wi = [0.0] * SUBLANES
        for k in range(SUBLANES):
            if k & dist:
                e = (k % dist) * (SUBLANES // (2 * dist))
                wr[k] = math.cos(-2 * math.pi * e / SUBLANES)
                wi[k] = math.sin(-2 * math.pi * e / SUBLANES)
        pats[dist] = (lo, sgn, table(wr), table(wi))
    pats["quarter"] = (sub & 3) == 3
    return pats


def _dft8_fwd(vr, vi, pats):
    for dist in (4, 2, 1):
        lo, sgn, wr, wi = pats[dist]
        up_r = pltpu.roll(vr, SUBLANES - dist, 0)
        up_i = pltpu.roll(vi, SUBLANES - dist, 0)
        if dist == 4:
            pr, pi_ = up_r, up_i
        else:
            pr = jnp.where(lo, up_r, pltpu.roll(vr, dist, 0))
            pi_ = jnp.where(lo, up_i, pltpu.roll(vi, dist, 0))
        tr = pr + sgn * vr
        ti = pi_ + sgn * vi
        if dist == 1:
            vr, vi = tr, ti
        elif dist == 2:
            qt = pats["quarter"]
            vr, vi = jnp.where(qt, ti, tr), jnp.where(qt, -tr, ti)
        else:
            vr, vi = _cmul(tr, ti, wr, wi)
    return vr, vi


def _dft8_inv(vr, vi, pats):
    for dist in (1, 2, 4):
        lo, sgn, wr, wi = pats[dist]
        if dist == 2:
            qt = pats["quarter"]
            vr, vi = jnp.where(qt, -vi, vr), jnp.where(qt, vr, vi)
        elif dist == 4:
            vr, vi = _cmul(vr, vi, wr, -wi)
        up_r = pltpu.roll(vr, SUBLANES - dist, 0)
        up_i = pltpu.roll(vi, SUBLANES - dist, 0)
        if dist == 4:
            pr, pi_ = up_r, up_i
        else:
            pr = jnp.where(lo, up_r, pltpu.roll(vr, dist, 0))
            pi_ = jnp.where(lo, up_i, pltpu.roll(vi, dist, 0))
        vr = pr + sgn * vr
        vi = pi_ + sgn * vi
    return vr, vi


def _slab_stage(re, im, tw, half, slabs, inverse):
    per_block = half // FFT_G
    nblk = slabs // (2 * half)
    unroll = min(FFT_UNROLL, nblk * per_block)
    tw_off = slabs - 2 * half
    if nblk >= unroll:
        blocks_per_it = unroll // per_block
        trips = nblk // blocks_per_it
        offsets = [(b * 2 * half + jc * FFT_G, jc * FFT_G) for b in range(blocks_per_it) for jc in range(per_block)]
        data_step, tw_step = blocks_per_it * 2 * half, 0
    else:
        chunks_per_it = unroll // nblk
        trips = per_block // chunks_per_it
        offsets = [(b * 2 * half + k * FFT_G, k * FFT_G) for b in range(nblk) for k in range(chunks_per_it)]
        data_step = tw_step = chunks_per_it * FFT_G

    def body(c, carry):
        d0 = pl.multiple_of(c * data_step, FFT_G)
        t0 = pl.multiple_of(tw_off + c * tw_step, FFT_G)
        twiddles = {}
        for d_off, t_off in offsets:
            if t_off not in twiddles:
                tws = pl.ds(t0 + t_off, FFT_G)
                twiddles[t_off] = (tw[0, tws], tw[1, tws])
            wr, wi = twiddles[t_off]
            lo = pl.ds(d0 + d_off, FFT_G)
            hi = pl.ds(d0 + d_off + half, FFT_G)
            ar, ai, br, bi = re[lo], im[lo], re[hi], im[hi]
            if inverse:
                br, bi = br * wr + bi * wi, bi * wr - br * wi
                re[lo] = ar + br
                im[lo] = ai + bi
                re[hi] = ar - br
                im[hi] = ai - bi
            else:
                re[lo] = ar + br
                im[lo] = ai + bi
                dr, di = ar - br, ai - bi
                re[hi] = dr * wr - di * wi
                im[hi] = dr * wi + di * wr
        return carry

    lax.fori_loop(0, trips, body, 0)


def _fft_forward_big(re, im, tw, slabs):
    half = slabs // 2
    while half >= 4:
        _slab_stage(re, im, tw, half, slabs, inverse=False)
        half //= 2


def _fft_inverse_big(re, im, tw, slabs):
    half = 4
    while half <= slabs // 2:
        _slab_stage(re, im, tw, half, slabs, inverse=True)
        half *= 2


def _radix4_fwd(x):
    (x0r, x0i), (x1r, x1i), (x2r, x2i), (x3r, x3i) = x
    y0r, y0i = x0r + x2r, x0i + x2i
    y2r, y2i = x0r - x2r, x0i - x2i
    y1r, y1i = x1r + x3r, x1i + x3i
    dr, di = x1r - x3r, x1i - x3i
    y3r, y3i = di, -dr
    return [(y0r + y1r, y0i + y1i), (y0r - y1r, y0i - y1i), (y2r + y3r, y2i + y3i), (y2r - y3r, y2i - y3i)]


def _radix4_inv(z):
    (z0r, z0i), (z1r, z1i), (z2r, z2i), (z3r, z3i) = z
    y0r, y0i = z0r + z1r, z0i + z1i
    y1r, y1i = z0r - z1r, z0i - z1i
    y2r, y2i = z2r + z3r, z2i + z3i
    y3r, y3i = z2r - z3r, z2i - z3i
    qr, qi = -y3i, y3r
    return [(y0r + y2r, y0i + y2i), (y1r + qr, y1i + qi), (y0r - y2r, y0i - y2i), (y1r - qr, y1i - qi)]


def _fft_middle(re, im, twm_ref, slabs, pats, spec=None, out=None):
    def body(q, carry):
        p0 = q * FFT_G
        x = [(re[p0 + g], im[p0 + g]) for g in range(FFT_G)]
        z = _radix4_fwd(x)
        res = []
        for g in range(FFT_G):
            twr = twm_ref[0, p0 + g]
            twi = twm_ref[1, p0 + g]
            vr, vi = _cmul(z[g][0], z[g][1], twr, twi)
            vr, vi = _dft8_fwd(vr, vi, pats)
            if spec is None:
                out[0][0, 0, p0 + g] = vr * out[2]
                out[1][0, 0, p0 + g] = vi * out[2]
            else:
                vr, vi = _cmul(vr, vi, spec[0][0, 0, p0 + g], spec[1][0, 0, p0 + g])
                vr, vi = _dft8_inv(vr, vi, pats)
                res.append(_cmul(vr, vi, twr, -twi))
        if spec is not None:
            x = _radix4_inv(res)
            for g in range(FFT_G):
                re[p0 + g] = x[g][0]
                im[p0 + g] = x[g][1]
        return carry

    lax.fori_loop(0, slabs // FFT_G, body, 0)


def _hy_spec_kernel(tw_ref, k_ref, twm_ref, mod_ref, sr_ref, si_ref, re, im, *, slabs):
    h = pl.program_id(2)
    pats = _sub_patterns()
    a = k_ref[0:slabs]
    b = k_ref[slabs:2 * slabs]

    @pl.when(h == 0)
    def _():
        re[...] = a + b
        im[...] = jnp.zeros_like(a)

    @pl.when(h == 1)
    def _():
        dlt = a - b
        re[...] = dlt * mod_ref[0]
        im[...] = dlt * mod_ref[1]

    _fft_forward_big(re, im, tw_ref, slabs)
    scale = np.float32(1.0 / (2 * SUBLANES * slabs))
    _fft_middle(re, im, twm_ref, slabs, pats, out=(sr_ref, si_ref, scale))


def _hy_spectra(k, n_tokens):
    slabs = n_tokens // SUBLANES
    tw, twm, mod = _fft_tables(n_tokens)
    k4 = k
    nt = HY_WIDTH // LANES
    shape = jax.ShapeDtypeStruct((HY_ORDER, 2, slabs, SUBLANES, HY_WIDTH), F32)
    spec_out = pl.BlockSpec((1, 1, slabs, SUBLANES, LANES), lambda o, j, h: (o, h, 0, 0, j))
    tab = pl.BlockSpec((2, slabs, SUBLANES, LANES), lambda o, j, h: (0, 0, 0, 0), pipeline_mode=pl.Buffered(1))
    return pl.pallas_call(
        functools.partial(_hy_spec_kernel, slabs=slabs),
        grid=(HY_ORDER, nt, 2),
        in_specs=[pl.BlockSpec(tw.shape, lambda o, j, h: (0, 0, 0, 0), pipeline_mode=pl.Buffered(1)),
                  pl.BlockSpec((2 * slabs, SUBLANES, LANES), lambda o, j, h: (0, 0, o * nt + j)),
                  tab, tab],
        out_specs=[spec_out, spec_out],
        out_shape=[shape, shape],
        scratch_shapes=[pltpu.VMEM((slabs, SUBLANES, LANES), F32)] * 2,
        compiler_params=_cparams(("parallel", "parallel", "arbitrary")), name="hy_spec",
    )(tw, k4, twm, mod)


def _hy_conv_kernel(tw_ref, u_ref, g_ref, sr_ref, si_ref, twm_ref, mod_ref, b_ref, o_ref, re, im, *, slabs):
    h = pl.program_id(2)
    pats = _sub_patterns()
    slab_shape = (slabs, SUBLANES, LANES)
    n = slabs * SUBLANES

    @pl.when(h == 0)
    def _():
        re[...] = u_ref[0].reshape(slab_shape)
        im[...] = u_ref[1].reshape(slab_shape)

    @pl.when(h == 1)
    def _():
        ur, ui = u_ref[0].reshape(slab_shape), u_ref[1].reshape(slab_shape)
        mr, mi = mod_ref[0], mod_ref[1]
        re[...] = ur * mr - ui * mi
        im[...] = ur * mi + ui * mr

    _fft_forward_big(re, im, tw_ref, slabs)
    _fft_middle(re, im, twm_ref, slabs, pats, spec=(sr_ref, si_ref))
    _fft_inverse_big(re, im, tw_ref, slabs)

    @pl.when(h == 0)
    def _():
        o_ref[0] = re[...].reshape(n, LANES)
        o_ref[1] = im[...].reshape(n, LANES)

    @pl.when(h == 1)
    def _():
        yr, yi = re[...], im[...]
        mr, mi = mod_ref[0], mod_ref[1]
        y0 = o_ref[0] + (yr * mr + yi * mi).reshape(n, LANES)
        y1 = o_ref[1] + (yi * mr - yr * mi).reshape(n, LANES)
        o_ref[0] = g_ref[0] * (y0 + b_ref[...] * u_ref[0])
        o_ref[1] = g_ref[1] * (y1 + b_ref[...] * u_ref[1])


def _hy_order(u, u_blk, gate, gate_blk, spec_re, spec_im, order, bias):
    b, n, _ = u.shape
    w = HY_WIDTH
    slabs = n // SUBLANES
    tw, twm, mod = _fft_tables(n)
    nt = w // LANES
    tab = pl.BlockSpec((2, slabs, SUBLANES, LANES), lambda j, p, h: (0, 0, 0, 0), pipeline_mode=pl.Buffered(1))
    spec_in = pl.BlockSpec((1, 1, slabs, SUBLANES, LANES), lambda j, p, h: (order, h, 0, 0, j))

    def io(blk):
        return pl.BlockSpec((2, n, LANES), lambda j, p, h: (p, 0, blk * nt + j))

    return pl.pallas_call(
        functools.partial(_hy_conv_kernel, slabs=slabs),
        grid=(nt, b // 2, 2),
        in_specs=[pl.BlockSpec(tw.shape, lambda j, p, h: (0, 0, 0, 0), pipeline_mode=pl.Buffered(1)),
                  io(u_blk), io(gate_blk), spec_in, spec_in, tab, tab,
                  pl.BlockSpec((1, LANES), lambda j, p, h: (0, j))],
        out_specs=io(0),
        out_shape=jax.ShapeDtypeStruct((b, n, w), F32),
        scratch_shapes=[pltpu.VMEM((slabs, SUBLANES, LANES), F32)] * 2,
        compiler_params=_cparams(("parallel", "parallel", "arbitrary")), name="hy_conv",
    )(tw, u, gate, spec_re, spec_im, twm, mod, bias.astype(F32)[None])


def _shift_rows(x, n):
    row = lax.broadcasted_iota(jnp.int32, x.shape, 0)
    prev = jnp.where(row == 0, 0.0, pltpu.roll(x, 1, 0))
    nxt = jnp.where(row == n - 1, 0.0, pltpu.roll(x, n - 1, 0))
    return prev, nxt


def _hy_short_kernel(z_ref, w_ref, b_ref, o_ref, *, n):
    x = z_ref[0]
    prev, nxt = _shift_rows(x, n)
    o_ref[0] = prev * w_ref[0:1, :] + x * w_ref[1:2, :] + nxt * w_ref[2:3, :] + b_ref[...]


def _hy_short(z_mix, conv_w, conv_b):
    b, n, _ = z_mix.shape
    c3 = 3 * HY_WIDTH
    w = conv_w.reshape(3, c3).astype(F32)
    tc = 256
    col0 = BRANCH_W // tc
    return pl.pallas_call(
        functools.partial(_hy_short_kernel, n=n),
        grid=(b, c3 // tc),
        in_specs=[pl.BlockSpec((1, n, tc), lambda bi, j: (bi, 0, col0 + j)),
                  pl.BlockSpec((3, tc), lambda bi, j: (0, j)),
                  pl.BlockSpec((1, tc), lambda bi, j: (0, j))],
        out_specs=pl.BlockSpec((1, n, tc), lambda bi, j: (bi, 0, j)),
        out_shape=jax.ShapeDtypeStruct((b, n, c3), F32),
        compiler_params=_cparams(("parallel", "parallel")), name="hy_short",
    )(z_mix, w, conv_b.astype(F32)[None])


def _hyena(z_mix, conv_w, conv_b, spec_re, spec_im, bias):
    zc = _hy_short(z_mix, conv_w, conv_b)
    v1 = _hy_order(zc, 0, zc, 1, spec_re, spec_im, 0, bias[0])
    return _hy_order(v1, 0, zc, 2, spec_re, spec_im, 1, bias[1])


NA_QROWS = 8
NA_KROWS = 16
NA_KPART = 256


def _softmax_parts(parts):
    m = None
    for s in parts:
        mm = jnp.max(s, axis=-1, keepdims=True)
        m = mm if m is None else jnp.maximum(m, mm)
    return m


def _na_kernel(q_ref, k0, k1, k2, k3, v0, v1, v2, v3, kc_ref, vc_ref, bias_ref, o_ref):
    q = q_ref[0] * np.float32(NA_HEAD_DIM ** -0.5)
    k = jnp.concatenate([k0[0], k1[0], k2[0], k3[0]], axis=0).astype(BF16)
    v = jnp.concatenate([v0[0], v1[0], v2[0], v3[0]], axis=0).astype(BF16)
    kc = kc_ref[0].astype(BF16)
    vc = vc_ref[0].astype(BF16)
    nt = (((1,), (1,)), ((), ()))
    outs = []
    for h in range(NA_HEADS):
        sl = slice(h * NA_HEAD_DIM, (h + 1) * NA_HEAD_DIM)
        qh = q[:, sl].astype(BF16)
        s_lat = lax.dot_general(qh, k[:, sl], nt, preferred_element_type=F32) + bias_ref[0, h]
        s_ctx = lax.dot_general(qh, kc[:, sl], nt, preferred_element_type=F32)
        m = _softmax_parts([s_lat, s_ctx])
        p_lat = jnp.exp(s_lat - m)
        p_ctx = jnp.exp(s_ctx - m)
        den = jnp.sum(p_lat, axis=-1, keepdims=True) + jnp.sum(p_ctx, axis=-1, keepdims=True)
        o = jnp.dot(p_lat.astype(BF16), v[:, sl], preferred_element_type=F32) \
            + jnp.dot(p_ctx.astype(BF16), vc[:, sl], preferred_element_type=F32)
        outs.append(o / den)
    o_ref[0] = jnp.concatenate(outs, axis=-1)


def _na_bias(rpb, rows):
    kh = min(NA_WIN_H, rows)
    col = np.arange(GRID_W)
    col_start = np.clip(col - NA_WIN_W // 2, 0, GRID_W - NA_WIN_W)
    col_ok = (col[None] >= col_start[:, None]) & (col[None] < col_start[:, None] + NA_WIN_W)
    off_c = np.clip(col[None] - col[:, None], -(NA_WIN_W - 1), NA_WIN_W - 1) + (NA_WIN_W - 1)
    nblk = rows // NA_QROWS
    n_r, n_c = 2 * NA_WIN_H - 1, 2 * NA_WIN_W - 1
    table = jnp.pad(rpb.astype(F32), ((0, 0), (0, 1), (0, 1)), constant_values=NEG_INF)
    sel_c = np.eye(n_c + 1, dtype=np.float32)[np.where(col_ok, off_c, n_c)]
    sel_r = []
    for j in (0, 1, nblk - 1):
        qr = j * NA_QROWS + np.arange(NA_QROWS)
        ws = int(np.clip(j * NA_QROWS - NA_WIN_H // 2, 0, rows - NA_KROWS))
        kr = ws + np.arange(NA_KROWS)
        start = np.clip(qr - kh // 2, 0, rows - kh)
        row_ok = (kr[None] >= start[:, None]) & (kr[None] < start[:, None] + kh)
        off_r = kr[None] - qr[:, None] + (NA_WIN_H - 1)
        sel_r.append(np.eye(n_r + 1, dtype=np.float32)[np.where(row_ok, off_r, n_r)])
    bias = jnp.einsum('trka,hab,qcb->thrqkc', np.stack(sel_r), table, sel_c, precision=HI)
    return bias.reshape(3, NA_HEADS, NA_QROWS * GRID_W, NA_KROWS * GRID_W)


def _na_attention(z_mix, zc_mix, rpb):
    b, n, _ = z_mix.shape
    n_ctx = zc_mix.shape[1]
    rows = n // GRID_W
    nblk = rows // NA_QROWS
    tq = NA_QROWS * GRID_W
    bias = _na_bias(rpb, rows)
    qb, kb, vb = 1024 // BRANCH_W, 1024 // BRANCH_W + 1, 1024 // BRANCH_W + 2
    kparts = NA_KROWS * GRID_W // NA_KPART
    max_k0 = (rows - NA_KROWS) * GRID_W // NA_KPART

    def k0_of(j):
        return jnp.clip(2 * j - 1, 0, max_k0)

    def kspec(part, blk):
        return pl.BlockSpec((1, NA_KPART, BRANCH_W), lambda bi, j: (bi, k0_of(j) + part, blk))

    def cls(j):
        return jnp.where(j == 0, 0, jnp.where(j == nblk - 1, 2, 1))

    return pl.pallas_call(
        _na_kernel,
        grid=(b, nblk),
        in_specs=[pl.BlockSpec((1, tq, BRANCH_W), lambda bi, j: (bi, j, qb))]
        + [kspec(p, kb) for p in range(kparts)] + [kspec(p, vb) for p in range(kparts)]
        + [pl.BlockSpec((1, n_ctx, BRANCH_W), lambda bi, j: (bi, 0, kb)),
           pl.BlockSpec((1, n_ctx, BRANCH_W), lambda bi, j: (bi, 0, vb)),
           pl.BlockSpec((1, NA_HEADS, tq, NA_KROWS * GRID_W), lambda bi, j: (cls(j), 0, 0, 0))],
        out_specs=pl.BlockSpec((1, tq, BRANCH_W), lambda bi, j: (bi, j, 0)),
        out_shape=jax.ShapeDtypeStruct((b, n, BRANCH_W), F32),
        compiler_params=_cparams(("parallel", "arbitrary")), name="na_attn",
    )(z_mix, *([z_mix] * (2 * kparts)), zc_mix, zc_mix, bias)


def _rope_tables(n_tokens, head_dim, heads):
    t = np.arange(n_tokens)
    row = (t // GRID_W).astype(np.float64)
    col = (t % GRID_W).astype(np.float64)
    half = head_dim // 2
    inv = ROPE_BASE ** (-(np.arange(0, half, 2, dtype=np.float64) / half))
    ang = np.concatenate([row[:, None] * inv, col[:, None] * inv], axis=-1)
    cos, sin = np.cos(ang), np.sin(ang)
    cos_t = np.tile(np.concatenate([cos, cos], axis=-1), (1, heads))
    sin_t = np.tile(np.concatenate([-sin, sin], axis=-1), (1, heads))
    return jnp.asarray(cos_t, F32), jnp.asarray(sin_t, F32)


def _rope(x, cos_t, sin_t):
    w = x.shape[-1]
    half = SW_HEAD_DIM // 2
    lane = lax.broadcasted_iota(jnp.int32, x.shape, 1)
    first = (lane % SW_HEAD_DIM) < half
    partner = jnp.where(first, pltpu.roll(x, w - half, 1), pltpu.roll(x, half, 1))
    return x * cos_t + partner * sin_t


SW_QBLK = 2 * SW_BLOCK
SW_KPARTS = SW_QBLK // SW_BLOCK + 2


def _sw_rope_kernel(q_ref, kv_ref, cos_ref, sin_ref, qo_ref, kvo_ref):
    cos_t, sin_t = cos_ref[...], sin_ref[...]
    q = _rope(q_ref[0], cos_t, sin_t) * np.float32(SW_HEAD_DIM ** -0.5)
    qo_ref[0] = q.astype(BF16)
    kv = kv_ref[0]
    lane = lax.broadcasted_iota(jnp.int32, kv.shape, 1)
    kvo_ref[0] = jnp.where(lane < SW_KV_HEADS * SW_HEAD_DIM, _rope(kv, cos_t, sin_t), kv).astype(BF16)


def _sw_rope(z_mix):
    b, n, _ = z_mix.shape
    qw = SW_HEADS * SW_HEAD_DIM
    cos_t, sin_t = _rope_tables(n, SW_HEAD_DIM, SW_HEADS)
    tm = min(n, 1024)
    tok = lambda blk: pl.BlockSpec((1, tm, qw), lambda bi, i: (bi, i, blk))
    tab = pl.BlockSpec((tm, qw), lambda bi, i: (i, 0))
    shape = jax.ShapeDtypeStruct((b, n, qw), BF16)
    return pl.pallas_call(
        _sw_rope_kernel,
        grid=(b, n // tm),
        in_specs=[tok(1792 // qw), tok(1792 // qw + 1), tab, tab],
        out_specs=[tok(0), tok(0)],
        out_shape=[shape, shape],
        compiler_params=_cparams(("parallel", "parallel")), name="sw_rope",
    )(z_mix, z_mix, cos_t, sin_t)


def _sw_kernel(q_ref, kv0, kv1, kv2, kv3, kc_ref, vc_ref, sink_ref, o_ref, *, nblk):
    i = pl.program_id(1)
    kvw = SW_KV_HEADS * SW_HEAD_DIM
    q = q_ref[0]
    kv = jnp.concatenate([kv0[0], kv1[0], kv2[0], kv3[0]], axis=0)
    kc = kc_ref[0].astype(BF16)
    vc = vc_ref[0].astype(BF16)
    g_ = SW_HEADS // SW_KV_HEADS
    rows = g_ * SW_QBLK
    span = SW_KPARTS * SW_BLOCK
    r = lax.broadcasted_iota(jnp.int32, (rows, span), 0) % SW_QBLK
    c = lax.broadcasted_iota(jnp.int32, (rows, span), 1)
    diff = c - SW_BLOCK - r
    blk = i * (SW_QBLK // SW_BLOCK) - 1 + c // SW_BLOCK
    ok = (jnp.abs(diff) <= SW_WINDOW) & (blk >= 0) & (blk < nblk)
    nt = (((1,), (1,)), ((), ()))
    outs = []
    for kvh in range(SW_KV_HEADS):
        ksl = slice(kvh * SW_HEAD_DIM, (kvh + 1) * SW_HEAD_DIM)
        vsl = slice(kvw + kvh * SW_HEAD_DIM, kvw + (kvh + 1) * SW_HEAD_DIM)
        qg = jnp.concatenate([q[:, (kvh * g_ + g) * SW_HEAD_DIM:(kvh * g_ + g + 1) * SW_HEAD_DIM] for g in range(g_)],
                             axis=0)
        s_lat = jnp.where(ok, lax.dot_general(qg, kv[:, ksl], nt, preferred_element_type=F32), NEG_INF)
        s_ctx = lax.dot_general(qg, kc[:, ksl], nt, preferred_element_type=F32)
        rr = lax.broadcasted_iota(jnp.int32, (rows, 1), 0)
        s_sink = jnp.zeros((rows, 1), F32)
        for g in range(g_):
            s_sink = jnp.where(rr // SW_QBLK == g, sink_ref[kvh * g_ + g], s_sink)
        m = jnp.maximum(_softmax_parts([s_lat, s_ctx]), s_sink)
        p_lat = jnp.exp(s_lat - m)
        p_ctx = jnp.exp(s_ctx - m)
        den = jnp.sum(p_lat, axis=-1, keepdims=True) + jnp.sum(p_ctx, axis=-1, keepdims=True) + jnp.exp(s_sink - m)
        o = jnp.dot(p_lat.astype(BF16), kv[:, vsl], preferred_element_type=F32) \
            + jnp.dot(p_ctx.astype(BF16), vc[:, ksl], preferred_element_type=F32)
        o = o / den
        outs += [o[g * SW_QBLK:(g + 1) * SW_QBLK] for g in range(g_)]
    o_ref[0] = jnp.concatenate(outs, axis=-1)


def _sw_attention(z_mix, zc_mix, sink):
    b, n, _ = z_mix.shape
    n_ctx = zc_mix.shape[1]
    nblk = n // SW_BLOCK
    qw = SW_HEADS * SW_HEAD_DIM
    kvw = SW_KV_HEADS * SW_HEAD_DIM
    k_blk = (1792 + qw) // kvw
    q_r, kv_r = _sw_rope(z_mix)
    per_q = SW_QBLK // SW_BLOCK

    def kpart(part):
        return pl.BlockSpec((1, SW_BLOCK, qw),
                            lambda bi, i: (bi, jnp.clip(i * per_q - 1 + part, 0, nblk - 1), 0))

    return pl.pallas_call(
        functools.partial(_sw_kernel, nblk=nblk),
        grid=(b, n // SW_QBLK),
        in_specs=[pl.BlockSpec((1, SW_QBLK, qw), lambda bi, i: (bi, i, 0))]
        + [kpart(p) for p in range(SW_KPARTS)]
        + [pl.BlockSpec((1, n_ctx, kvw), lambda bi, i: (bi, 0, k_blk)),
           pl.BlockSpec((1, n_ctx, kvw), lambda bi, i: (bi, 0, k_blk + 1)),
           pl.BlockSpec(memory_space=pltpu.SMEM)],
        out_specs=pl.BlockSpec((1, SW_QBLK, qw), lambda bi, i: (bi, i, 0)),
        out_shape=jax.ShapeDtypeStruct((b, n, qw), F32),
        compiler_params=_cparams(("parallel", "arbitrary")), name="sw_attn",
    )(q_r, *([kv_r] * SW_KPARTS), zc_mix, zc_mix, sink.astype(F32))


def _ctx_attn_kernel(q_ref, k_ref, v_ref, sink_ref, o_ref, *, heads, kv_heads, dh, use_sink):
    q = q_ref[0] * np.float32(dh ** -0.5)
    k = k_ref[0].astype(BF16)
    v = v_ref[0].astype(BF16)
    g_ = heads // kv_heads
    nt = (((1,), (1,)), ((), ()))
    outs = []
    for h in range(heads):
        kv = h // g_
        s = lax.dot_general(q[:, h * dh:(h + 1) * dh].astype(BF16), k[:, kv * dh:(kv + 1) * dh], nt,
                            preferred_element_type=F32)
        m = jnp.max(s, axis=-1, keepdims=True)
        if use_sink:
            m = jnp.maximum(m, sink_ref[h])
        p = jnp.exp(s - m)
        den = jnp.sum(p, axis=-1, keepdims=True)
        if use_sink:
            den = den + jnp.exp(sink_ref[h] - m)
        outs.append(jnp.dot(p.astype(BF16), v[:, kv * dh:(kv + 1) * dh], preferred_element_type=F32) / den)
    o_ref[0] = jnp.concatenate(outs, axis=-1)


def _ctx_attention(zc_mix, col0, heads, kv_heads, dh, sink):
    b, n, _ = zc_mix.shape
    qw, kvw = heads * dh, kv_heads * dh
    use_sink = sink is not None
    sink_arr = sink.astype(F32) if use_sink else jnp.zeros((heads,), F32)
    return pl.pallas_call(
        functools.partial(_ctx_attn_kernel, heads=heads, kv_heads=kv_heads, dh=dh, use_sink=use_sink),
        grid=(b,),
        in_specs=[pl.BlockSpec((1, n, qw), lambda bi: (bi, 0, col0 // qw)),
                  pl.BlockSpec((1, n, kvw), lambda bi: (bi, 0, (col0 + qw) // kvw)),
                  pl.BlockSpec((1, n, kvw), lambda bi: (bi, 0, (col0 + qw) // kvw + 1)),
                  pl.BlockSpec(memory_space=pltpu.SMEM)],
        out_specs=pl.BlockSpec((1, n, qw), lambda bi: (bi, 0, 0)),
        out_shape=jax.ShapeDtypeStruct((b, n, qw), F32),
        compiler_params=_cparams(("parallel",)), name="ctx_attn",
    )(zc_mix, zc_mix, zc_mix, sink_arr)


def _merge_kernel(x_ref, s5_ref, hy_ref, na_ref, sw_ref, gt_ref, ga_ref, wglu_ref, wb_ref, wo_ref,
                  lg_ref, lb_ref, o_ref):
    g = jax.nn.gelu(s5_ref[0])
    s5 = g * jax.nn.sigmoid(jnp.dot(g.astype(BF16), wglu_ref[...], preferred_element_type=F32))
    branches = (s5, hy_ref[0], na_ref[0], sw_ref[0])
    acc = None
    for n in range(N_BRANCH):
        proj = jnp.dot(branches[n].astype(BF16), wb_ref[n], preferred_element_type=F32)
        t = gt_ref[0, :, n * D_MODEL:(n + 1) * D_MODEL].astype(F32) * proj
        acc = t if acc is None else acc + t
    mix = jnp.dot(acc.astype(BF16), wo_ref[...], preferred_element_type=F32)
    y = np.float32(DEEPNORM_ALPHA) * x_ref[0] + ga_ref[0] * mix
    o_ref[0] = _layernorm(y) * lg_ref[...] + lb_ref[...]


def _merge(x, s5y, hy, na, sw, gates, g_a, w_glu, w_branch, w_out, ln_g, ln_b, *, tm):
    b, l, d = x.shape
    br = pl.BlockSpec((1, tm, BRANCH_W), lambda bi, i: (bi, i, 0))
    full = lambda s: pl.BlockSpec(s, lambda bi, i: (0,) * len(s))
    return pl.pallas_call(
        _merge_kernel,
        grid=(b, l // tm),
        in_specs=[pl.BlockSpec((1, tm, d), lambda bi, i: (bi, i, 0)), br, br, br, br,
                  pl.BlockSpec((1, tm, GATE_W), lambda bi, i: (bi, i, 0)),
                  pl.BlockSpec((1, 1, d), lambda bi, i: (bi, 0, 0)),
                  full((BRANCH_W, BRANCH_W)), full((N_BRANCH, BRANCH_W, d)), full((d, d)),
                  full((1, d)), full((1, d))],
        out_specs=pl.BlockSpec((1, tm, d), lambda bi, i: (bi, i, 0)),
        out_shape=jax.ShapeDtypeStruct((b, l, d), F32),
        compiler_params=_cparams(("parallel", "parallel")), name="merge",
    )(x, s5y, hy, na, sw, gates, g_a, w_glu, w_branch, w_out, ln_g, ln_b)


def _mlp_kernel(x_ref, sh_ref, sc_ref, gm_ref, w1_ref, w2_ref, lg_ref, lb_ref, o_ref, h_scr, acc_scr):
    j = pl.program_id(2)

    @pl.when(j == 0)
    def _():
        h = _layernorm(x_ref[0]) * (1.0 + sc_ref[0]) + sh_ref[0]
        h_scr[...] = h.astype(BF16)
        acc_scr[...] = jnp.zeros_like(acc_scr)

    a = jnp.dot(h_scr[...], w1_ref[...], preferred_element_type=F32)
    a = jnp.square(jnp.maximum(a, 0.0))
    acc_scr[...] += jnp.dot(a.astype(BF16), w2_ref[...], preferred_element_type=F32)

    @pl.when(j == pl.num_programs(2) - 1)
    def _():
        y = np.float32(DEEPNORM_ALPHA) * x_ref[0] + gm_ref[0] * acc_scr[...]
        o_ref[0] = _layernorm(y) * lg_ref[...] + lb_ref[...]


def _mlp(x, sh, sc, g_m, w1, w2, ln_g, ln_b, *, tm, th):
    b, l, d = x.shape
    hdim = w1.shape[1]
    mod = pl.BlockSpec((1, 1, d), lambda bi, i, j: (bi, 0, 0))
    vec = pl.BlockSpec((1, d), lambda bi, i, j: (0, 0))
    return pl.pallas_call(
        _mlp_kernel,
        grid=(b, l // tm, hdim // th),
        in_specs=[pl.BlockSpec((1, tm, d), lambda bi, i, j: (bi, i, 0)), mod, mod, mod,
                  pl.BlockSpec((d, th), lambda bi, i, j: (0, j)),
                  pl.BlockSpec((th, d), lambda bi, i, j: (j, 0)), vec, vec],
        out_specs=pl.BlockSpec((1, tm, d), lambda bi, i, j: (bi, i, 0)),
        out_shape=jax.ShapeDtypeStruct((b, l, d), F32),
        scratch_shapes=[pltpu.VMEM((tm, d), BF16), pltpu.VMEM((tm, d), F32)],
        compiler_params=_cparams(("parallel", "parallel", "arbitrary")), name="mlp",
    )(x, sh, sc, g_m, w1, w2, ln_g, ln_b)


def kernel(x, c, ctx, c_ctx, w_ada, b_ada, w_in, s5_lambda_re, s5_lambda_im, s5_log_dt, s5_b_re, s5_b_im, s5_c_re,
           s5_c_im, s5_d, s5_w_glu, hy_conv_w, hy_conv_b, hy_freq, hy_w1, hy_b1, hy_w2, hy_b2, hy_w3, hy_bias,
           na_rpb, sw_sink, w_branch, w_out, ln1_g, ln1_b, w_mlp1, w_mlp2, ln2_g, ln2_b):
    b, l, d = x.shape
    n_ctx = ctx.shape[1]
    depth = w_ada.shape[0]
    cc = jnp.zeros((8, d), F32).at[:b].set(c.astype(F32)).at[b].set(c_ctx.astype(F32))
    mod_all = _ada(cc, w_ada.astype(F32), b_ada.astype(F32))
    xc = ctx
    for layer in range(depth):
        need_ctx_out = layer < depth - 1
        mod = mod_all[layer, :b].reshape(b, 1, 6, d)
        mod_c = jnp.broadcast_to(mod_all[layer, b].reshape(1, 1, 6, d), (b, 1, 6, d))
        sh_a, sc_a, g_a, sh_m, sc_m, g_m = [mod[:, :, i] for i in range(6)]
        csh_a, csc_a, cg_a, csh_m, csc_m, cg_m = [mod_c[:, :, i] for i in range(6)]
        w_in_l = w_in[layer].astype(BF16)
        w_mix, w_gate = w_in_l[:, :MIX_W], w_in_l[:, MIX_W:]

        z_mix = _ln_mod_mm(x, sh_a, sc_a, w_mix, tm=1024, tn=MIX_W // 3)
        gates = _ln_mod_mm(x, sh_a, sc_a, w_gate, tm=1024, tn=1024, act="sigmoid", out_dtype=BF16)
        zc_mix = _ln_mod_mm(xc, csh_a, csc_a, w_mix, tm=n_ctx, tn=MIX_W // 3)

        ops = _s5_operators(s5_lambda_re[layer], s5_lambda_im[layer], s5_log_dt[layer], s5_b_re[layer],
                            s5_b_im[layer], s5_c_re[layer], s5_c_im[layer], s5_d[layer])
        yc_s5, e_ctx = _s5_mix(zc_mix[:, :, :BRANCH_W], ops, [jnp.zeros((b, 1, S5_STATE_W // 4), F32)] * 4)
        y_s5, _ = _s5_mix(z_mix[:, :, :BRANCH_W], ops, e_ctx)

        hy_args = (hy_freq[layer], hy_w1[layer], hy_b1[layer], hy_w2[layer], hy_b2[layer], hy_w3[layer])
        sp_re, sp_im = _hy_spectra(_hy_filters(l, *hy_args), l)
        hy_l = _hyena(z_mix, hy_conv_w[layer], hy_conv_b[layer], sp_re, sp_im, hy_bias[layer])

        na_l = _na_attention(z_mix, zc_mix, na_rpb[layer])
        sw_l = _sw_attention(z_mix, zc_mix, sw_sink[layer])

        w_glu = s5_w_glu[layer].astype(BF16)
        w_br = w_branch[layer].astype(BF16)
        w_o = w_out[layer].astype(BF16)
        lg1, lb1 = ln1_g[layer].astype(F32)[None], ln1_b[layer].astype(F32)[None]
        lg2, lb2 = ln2_g[layer].astype(F32)[None], ln2_b[layer].astype(F32)[None]
        w1 = w_mlp1[layer].astype(BF16)
        w2 = w_mlp2[layer].astype(BF16)

        x_new = _merge(x, y_s5, hy_l, na_l, sw_l, gates, g_a, w_glu, w_br, w_o, lg1, lb1, tm=512)
        x_new = _mlp(x_new, sh_m, sc_m, g_m, w1, w2, lg2, lb2, tm=1024, th=1024)

        if need_ctx_out:
            gates_c = _ln_mod_mm(xc, csh_a, csc_a, w_gate, tm=n_ctx, tn=1024, act="sigmoid", out_dtype=BF16)
            spc_re, spc_im = _hy_spectra(_hy_filters(n_ctx, *hy_args), n_ctx)
            hy_c = _hyena(zc_mix, hy_conv_w[layer], hy_conv_b[layer], spc_re, spc_im, hy_bias[layer])
            na_c = _ctx_attention(zc_mix, 1024, NA_HEADS, NA_HEADS, NA_HEAD_DIM, None)
            sw_c = _ctx_attention(zc_mix, 1792, SW_HEADS, SW_KV_HEADS, SW_HEAD_DIM, sw_sink[layer])
            xc_new = _merge(xc, yc_s5, hy_c, na_c, sw_c, gates_c, cg_a, w_glu, w_br, w_o, lg1, lb1, tm=n_ctx)
            xc = _mlp(xc_new, csh_m, csc_m, cg_m, w1, w2, lg2, lb2, tm=n_ctx, th=1024)
        x = x_new
    return x
```

```python
import functools
import math

import numpy as np
import jax
import jax.numpy as jnp
from jax import lax
from jax.experimental import pallas as pl
from jax.experimental.pallas import tpu as pltpu

F32 = jnp.float32
BF16 = jnp.bfloat16

D_MODEL = 1024
GRID_W = 64
BRANCH_W = 256
N_BRANCH = 4
S5_GROUP = 16
S5_GROUPS = 16
S5_STATE = 64
HY_WIDTH = 256
HY_ORDER = 2
HY_BANDS = 16
HY_EMB = 2 * HY_BANDS + 1
HY_FFN = 64
HY_MIN_DECAY = math.log(1e-2) / 1.5
HY_MAX_DECAY = math.log(1e-2) / 0.3
NA_HEADS = 4
NA_HEAD_DIM = 64
NA_WIN_H = 8
NA_WIN_W = 16
SW_HEADS = 4
SW_KV_HEADS = 2
SW_HEAD_DIM = 64
SW_WINDOW = 128
SW_BLOCK = 128
MLP_HIDDEN = 4 * D_MODEL
ROPE_BASE = 10000.0
LN_EPS = 1e-6
NEG_INF = -1e30
DEPTH = 2
DEEPNORM_ALPHA = (2 * DEPTH) ** 0.25

MIX_W = 2304
GATE_W = N_BRANCH * D_MODEL
S5_T = 8
S5_STATE_W = 4 * S5_GROUPS * S5_STATE
SUBLANES = 8
LANES = 128
VMEM_LIMIT = 56 * 1024 * 1024

HI = lax.Precision.HIGHEST


def _cparams(sem):
    return pltpu.CompilerParams(dimension_semantics=sem, vmem_limit_bytes=VMEM_LIMIT)


def _layernorm(x):
    mu = jnp.mean(x, axis=-1, keepdims=True)
    xc = x - mu
    var = jnp.mean(xc * xc, axis=-1, keepdims=True)
    return xc * lax.rsqrt(var + LN_EPS)


def _ada_kernel(c_ref, w_ref, b_ref, o_ref):
    c = c_ref[...]
    a = c * jax.nn.sigmoid(c)
    o_ref[0] = jnp.dot(a, w_ref[0], preferred_element_type=F32, precision=HI) + b_ref[0]


def _ada(cc, w_ada, b_ada):
    depth, d, n = w_ada.shape
    tn = 512
    return pl.pallas_call(
        _ada_kernel,
        grid=(depth, n // tn),
        in_specs=[pl.BlockSpec((8, d), lambda l, j: (0, 0)),
                  pl.BlockSpec((1, d, tn), lambda l, j: (l, 0, j)),
                  pl.BlockSpec((1, 1, tn), lambda l, j: (l, 0, j))],
        out_specs=pl.BlockSpec((1, 8, tn), lambda l, j: (l, 0, j)),
        out_shape=jax.ShapeDtypeStruct((depth, 8, n), F32),
        compiler_params=_cparams(("parallel", "parallel")), name="ada",
    )(cc, w_ada, b_ada.reshape(depth, 1, n))


def _ln_mod_mm_kernel(x_ref, sh_ref, sc_ref, w_ref, o_ref, *rest, act, chunked):
    h_scr = rest[-1]

    @pl.when(pl.program_id(2) == 0)
    def _():
        h = _layernorm(x_ref[0]) * (1.0 + sc_ref[0]) + sh_ref[0]
        h_scr[...] = h.astype(BF16)

    r = jnp.dot(h_scr[...], w_ref[...], preferred_element_type=F32)
    if act == "sigmoid":
        r = jax.nn.sigmoid(r)
    o_ref[0] = r.astype(o_ref.dtype)

    if chunked:
        u_ref, tok_scr = rest[0], rest[1]
        rows = tok_scr.shape[1] // S5_T

        @pl.when(pl.program_id(2) == 0)
        def _():
            for c in range(BRANCH_W // LANES):
                tok_scr[c] = r[:, c * LANES:(c + 1) * LANES]
                for t in range(S5_T):
                    col = t * BRANCH_W + c * LANES
                    u_ref[0, :, col:col + LANES] = tok_scr[c, pl.ds(t, rows, stride=S5_T), :]


def _ln_mod_mm(x, sh, sc, w, *, tm, tn, act=None, out_dtype=F32, chunked=False):
    b, l, d = x.shape
    n = w.shape[1]
    out_specs = [pl.BlockSpec((1, tm, tn), lambda bi, i, j: (bi, i, j))]
    out_shape = [jax.ShapeDtypeStruct((b, l, n), out_dtype)]
    scratch = [pltpu.VMEM((tm, d), BF16)]
    if chunked:
        out_specs.append(pl.BlockSpec((1, tm // S5_T, S5_T * BRANCH_W), lambda bi, i, j: (bi, i, 0)))
        out_shape.append(jax.ShapeDtypeStruct((b, l // S5_T, S5_T * BRANCH_W), F32))
        scratch.insert(0, pltpu.VMEM((BRANCH_W // LANES, tm, LANES), F32))
    outs = pl.pallas_call(
        functools.partial(_ln_mod_mm_kernel, act=act, chunked=chunked),
        grid=(b, l // tm, n // tn),
        in_specs=[pl.BlockSpec((1, tm, d), lambda bi, i, j: (bi, i, 0)),
                  pl.BlockSpec((1, 1, d), lambda bi, i, j: (bi, 0, 0)),
                  pl.BlockSpec((1, 1, d), lambda bi, i, j: (bi, 0, 0)),
                  pl.BlockSpec((d, tn), lambda bi, i, j: (0, j))],
        out_specs=out_specs,
        out_shape=out_shape,
        scratch_shapes=scratch,
        compiler_params=_cparams(("parallel", "parallel", "arbitrary")), name="ln_mod_mm",
    )(x, sh, sc, w)
    return outs if chunked else outs[0]


def _s5_inc_kernel(u_ref, q_ref, o_ref):
    o_ref[...] = jnp.dot(u_ref[...].astype(BF16), q_ref[...], preferred_element_type=F32)


def _s5_increments(uu, q_all, *, tm, tn):
    m, k = uu.shape
    n = q_all.shape[1]
    return pl.pallas_call(
        _s5_inc_kernel,
        grid=(m // tm, n // tn),
        in_specs=[pl.BlockSpec((tm, k), lambda i, j: (i, 0)), pl.BlockSpec((k, tn), lambda i, j: (0, j))],
        out_specs=pl.BlockSpec((tm, tn), lambda i, j: (i, j)),
        out_shape=jax.ShapeDtypeStruct((m, n), F32),
        compiler_params=_cparams(("parallel", "parallel")), name="s5_inc",
    )(uu, q_all)


def _s5_out_kernel(u_ref, h0, h1, h2, h3, m_ref, n0, n1, n2, n3, o_ref):
    nt = (((1,), (1,)), ((), ()))
    acc = jnp.dot(u_ref[...].astype(BF16), m_ref[...], preferred_element_type=F32)
    for h_ref, n_ref in ((h0, n0), (h1, n1), (h2, n2), (h3, n3)):
        acc += lax.dot_general(h_ref[...].astype(BF16), n_ref[...], nt, preferred_element_type=F32)
    o_ref[...] = acc


def _s5_outputs(uu, h, m_tot, n_t, *, tm, tn):
    m, k = uu.shape
    gp = h[0].shape[1]
    n = m_tot.shape[1]
    return pl.pallas_call(
        _s5_out_kernel,
        grid=(m // tm, n // tn),
        in_specs=[pl.BlockSpec((tm, k), lambda i, j: (i, 0))]
        + [pl.BlockSpec((tm, gp), lambda i, j: (i, 0))] * 4
        + [pl.BlockSpec((k, tn), lambda i, j: (0, j))]
        + [pl.BlockSpec((tn, gp), lambda i, j, c=c: (j, c)) for c in range(4)],
        out_specs=pl.BlockSpec((tm, tn), lambda i, j: (i, j)),
        out_shape=jax.ShapeDtypeStruct((m, n), F32),
        compiler_params=_cparams(("parallel", "parallel")), name="s5_out",
    )(uu, *h, m_tot, n_t, n_t, n_t, n_t)


def _s5_scan_kernel(gfr, gfi, gbr, gbi, afr, afi, abr, abi, h0fr, h0fi, h0br, h0bi,
                    hfr, hfi, hbr, hbi, efr, efi, ebr, ebi, *, n_chunks):
    a_fr = afr[...][None]
    a_fi = afi[...][None]
    a_br = abr[...][None]
    a_bi = abi[...][None]

    def body(k, carry):
        sfr, sfi, sbr, sbi = carry
        kb = n_chunks - 1 - k
        hfr[:, pl.ds(k, 1), :] = sfr
        hfi[:, pl.ds(k, 1), :] = sfi
        hbr[:, pl.ds(kb, 1), :] = sbr
        hbi[:, pl.ds(kb, 1), :] = sbi
        nfr = a_fr * sfr - a_fi * sfi + gfr[:, pl.ds(k, 1), :]
        nfi = a_fr * sfi + a_fi * sfr + gfi[:, pl.ds(k, 1), :]
        nbr = a_br * sbr - a_bi * sbi + gbr[:, pl.ds(kb, 1), :]
        nbi = a_br * sbi + a_bi * sbr + gbi[:, pl.ds(kb, 1), :]
        return nfr, nfi, nbr, nbi

    sfr, sfi, sbr, sbi = lax.fori_loop(0, n_chunks, body, (h0fr[...], h0fi[...], h0br[...], h0bi[...]))
    efr[...] = sfr
    efi[...] = sfi
    ebr[...] = sbr
    ebi[...] = sbi


def _s5_scan(g, a_t, h0):
    b, k, w4 = g.shape
    w = 2 * LANES
    q = w4 // 4
    nb = q // w

    def comp(c):
        return pl.BlockSpec((b, k, w), lambda j, c=c: (0, 0, c * nb + j))

    def comp_a(c):
        return pl.BlockSpec((1, w), lambda j, c=c: (0, c * nb + j))

    state = pl.BlockSpec((b, k, w), lambda j: (0, 0, j))
    edge = pl.BlockSpec((b, 1, w), lambda j: (0, 0, j))
    outs = pl.pallas_call(
        functools.partial(_s5_scan_kernel, n_chunks=k),
        grid=(nb,),
        in_specs=[comp(c) for c in range(4)] + [comp_a(c) for c in range(4)] + [edge] * 4,
        out_specs=[state] * 4 + [edge] * 4,
        out_shape=[jax.ShapeDtypeStruct((b, k, q), F32)] * 4 + [jax.ShapeDtypeStruct((b, 1, q), F32)] * 4,
        compiler_params=_cparams(("parallel",)), name="s5_scan",
    )(g, g, g, g, a_t, a_t, a_t, a_t, *h0)
    return outs[:4], outs[4:]


def _cmul(ar, ai, br, bi):
    return ar * br - ai * bi, ar * bi + ai * br


def _s5_operators(lam_re, lam_im, log_dt, b_re, b_im, c_re, c_im, d):
    t_ = S5_T
    g_, p_, n_ = S5_GROUPS, S5_STATE, S5_GROUP
    gp, w_ = g_ * p_, g_ * n_
    lr, li = lam_re.astype(F32), lam_im.astype(F32)
    dt = jnp.exp(log_dt.astype(F32))[..., None]
    ks = jnp.arange(t_ + 1, dtype=F32)[:, None, None, None]
    mag = jnp.exp(ks * lr * dt)
    pw_re = mag * jnp.cos(ks * li * dt)
    pw_im = mag * jnp.sin(ks * li * dt)
    a_re, a_im = pw_re[1], pw_im[1]
    den = lr ** 2 + li ** 2
    f_re = ((a_re - 1.0) * lr + a_im * li) / den
    f_im = (a_im * lr - (a_re - 1.0) * li) / den
    br, bi = b_re.astype(F32), b_im.astype(F32)
    bb_re = f_re[..., None] * br - f_im[..., None] * bi
    bb_im = f_re[..., None] * bi + f_im[..., None] * br
    cr, ci = c_re.astype(F32), c_im.astype(F32)
    wb_re, wb_im = _cmul(pw_re[:t_, ..., None], pw_im[:t_, ..., None], bb_re[None], bb_im[None])
    kern = jnp.einsum('dgnp,kdgpm->dkgnm', cr, wb_re, precision=HI) \
        - jnp.einsum('dgnp,kdgpm->dkgnm', ci, wb_im, precision=HI)
    k_tile = jnp.tile(kern.transpose(0, 1, 2, 4, 3).reshape(2, t_, w_, n_), (1, 1, 1, g_))

    def rows_tiled(x):
        return jnp.tile(x.transpose(0, 3, 1, 2).reshape(2, n_, gp), (1, g_, 1))

    bt_re, bt_im = rows_tiled(bb_re), rows_tiled(bb_im)
    ct_re, ct_im = rows_tiled(cr.transpose(0, 1, 3, 2)), rows_tiled(ci.transpose(0, 1, 3, 2))
    pr = pw_re.transpose(1, 0, 2, 3).reshape(2, t_ + 1, gp)
    pi_ = pw_im.transpose(1, 0, 2, 3).reshape(2, t_ + 1, gp)
    full = lambda s: pl.BlockSpec(s, lambda t: (0,) * len(s))
    m_tot, q_all, n_t = pl.pallas_call(
        _s5_ops_kernel,
        grid=(t_,),
        in_specs=[full((2, t_ + 1, gp))] * 2 + [full((2, w_, gp))] * 4 + [full((2, t_, w_, w_)), full((1, w_))],
        out_specs=[pl.BlockSpec((w_, t_ * w_), lambda t: (t, 0)),
                   pl.BlockSpec((w_, 4 * gp), lambda t: (t, 0)),
                   pl.BlockSpec((w_, 4 * gp), lambda t: (t, 0))],
        out_shape=[jax.ShapeDtypeStruct((t_ * w_, t_ * w_), BF16),
                   jax.ShapeDtypeStruct((t_ * w_, 4 * gp), BF16),
                   jax.ShapeDtypeStruct((t_ * w_, 4 * gp), BF16)],
        compiler_params=_cparams(("parallel",)), name="s5_ops",
    )(pr, pi_, bt_re, bt_im, ct_re, ct_im, k_tile, d.astype(F32)[None])
    a_t = jnp.concatenate([pr[0, t_], pi_[0, t_], pr[1, t_], pi_[1, t_]])[None]
    return m_tot, q_all, n_t, a_t


def _s5_ops_kernel(pr_ref, pi_ref, btr_ref, bti_ref, ctr_ref, cti_ref, kt_ref, d_ref, m_ref, q_ref, nt_ref):
    t = pl.program_id(0)
    t_ = S5_T
    w_, gp = btr_ref.shape[1], btr_ref.shape[2]
    row_g = lax.broadcasted_iota(jnp.int32, (w_, gp), 0) // S5_GROUP
    col_g = lax.broadcasted_iota(jnp.int32, (w_, gp), 1) // S5_STATE
    same = row_g == col_g
    for dr in range(2):
        e_q = (t_ - 1 - t) if dr == 0 else t
        e_n = (t + 1) if dr == 0 else (t_ - t)
        b_re = jnp.where(same, btr_ref[dr], 0.0)
        b_im = jnp.where(same, bti_ref[dr], 0.0)
        q_re, q_im = _cmul(b_re, b_im, pr_ref[dr, pl.ds(e_q, 1), :], pi_ref[dr, pl.ds(e_q, 1), :])
        c_re = jnp.where(same, ctr_ref[dr], 0.0)
        c_im = jnp.where(same, cti_ref[dr], 0.0)
        n_re, n_im = _cmul(c_re, c_im, pr_ref[dr, pl.ds(e_n, 1), :], pi_ref[dr, pl.ds(e_n, 1), :])
        base = 2 * dr * gp
        q_ref[:, base:base + gp] = q_re.astype(BF16)
        q_ref[:, base + gp:base + 2 * gp] = q_im.astype(BF16)
        nt_ref[:, base:base + gp] = n_re.astype(BF16)
        nt_ref[:, base + gp:base + 2 * gp] = (-n_im).astype(BF16)
    r2 = lax.broadcasted_iota(jnp.int32, (w_, w_), 0)
    c2 = lax.broadcasted_iota(jnp.int32, (w_, w_), 1)
    same2 = (r2 // S5_GROUP) == (c2 // S5_GROUP)
    skip = jnp.where(r2 == c2, d_ref[...], 0.0)
    for i in range(t_):
        k_f = kt_ref[0, jnp.maximum(i - t, 0)] * jnp.where(i >= t, 1.0, 0.0)
        k_b = kt_ref[1, jnp.maximum(t - i, 0)] * jnp.where(t >= i, 1.0, 0.0)
        blk = jnp.where(same2, k_f + k_b, 0.0) + skip * jnp.where(t == i, 1.0, 0.0)
        m_ref[:, i * w_:(i + 1) * w_] = blk.astype(BF16)


def _s5_mix(u, ops, h0):
    m_tot, q_all, n_t, a_t = ops
    b, k, tw = u.shape
    rows = b * k
    uu = u.reshape(rows, tw)
    tm = min(rows, 512)
    g = _s5_increments(uu, q_all, tm=tm, tn=1024)
    h, e = _s5_scan(g.reshape(b, k, S5_STATE_W), a_t, h0)
    y = _s5_outputs(uu, [hc.reshape(rows, -1) for hc in h], m_tot, n_t, tm=tm, tn=512)
    return y.reshape(b, k, tw), e


def _hy_filter_kernel(f_ref, w1_ref, b1_ref, w2_ref, b2_ref, w3_ref, fr_ref, dl_ref, o_ref, *, half_len, tile):
    feats = f_ref[...]
    h = jnp.dot(feats, w1_ref[...], preferred_element_type=F32, precision=HI) + b1_ref[...]
    h = jnp.sin(fr_ref[0:1, :] * h)
    h = jnp.dot(h, w2_ref[...], preferred_element_type=F32, precision=HI) + b2_ref[...]
    h = jnp.sin(fr_ref[1:2, :] * h)
    o = jnp.dot(h, w3_ref[...], preferred_element_type=F32, precision=HI)
    t = feats[:, 0:1]
    o = o * jnp.exp(-t * dl_ref[...])
    nw = HY_ORDER * HY_WIDTH
    n = pl.program_id(0) * tile + lax.broadcasted_iota(jnp.int32, (tile, 1), 0)
    k = jnp.where(n < half_len, o[:, :nw], jnp.where(n == half_len, 0.0, o[:, nw:]))
    o_ref[...] = k.reshape(o_ref.shape)


def _hy_filters(n_tokens, freq, w1, b1, w2, b2, w3):
    n2 = 2 * n_tokens
    idx = np.arange(n2, dtype=np.float64)
    pos = np.where(idx <= n_tokens, idx, n2 - idx)[:, None]
    t = pos / max(n_tokens - 1, 1)
    bands = np.linspace(1e-4, HY_BANDS - 1, HY_BANDS)[None]
    ang = 2.0 * math.pi * bands * pos / n_tokens
    feats = np.concatenate([t, np.cos(ang), -np.sin(ang)], axis=-1)
    kpad = LANES - HY_EMB
    feats = jnp.asarray(np.pad(feats, ((0, 0), (0, kpad))), F32)
    w1p = jnp.pad(w1.astype(F32), ((0, kpad), (0, 0)))
    deltas = jnp.abs(jnp.linspace(HY_MIN_DECAY, HY_MAX_DECAY, HY_WIDTH, dtype=F32))
    dl = jnp.tile(deltas, 2 * HY_ORDER)[None]
    tile = min(n2, 1024)
    nw = 2 * HY_ORDER * HY_WIDTH
    full = lambda s: pl.BlockSpec(s, lambda i: (0,) * len(s))
    return pl.pallas_call(
        functools.partial(_hy_filter_kernel, half_len=n_tokens, tile=tile),
        grid=(n2 // tile,),
        in_specs=[pl.BlockSpec((tile, LANES), lambda i: (i, 0)),
                  full((LANES, HY_FFN)), full((1, HY_FFN)), full((HY_FFN, HY_FFN)), full((1, HY_FFN)),
                  full((HY_FFN, nw)), full((2, HY_FFN)), full((1, nw))],
        out_specs=pl.BlockSpec((tile // SUBLANES, SUBLANES, HY_ORDER * HY_WIDTH), lambda i: (i, 0, 0)),
        out_shape=jax.ShapeDtypeStruct((n2 // SUBLANES, SUBLANES, HY_ORDER * HY_WIDTH), F32),
        compiler_params=_cparams(("parallel",)), name="hy_filter",
    )(feats, w1p, b1.astype(F32)[None], w2.astype(F32), b2.astype(F32)[None], w3.astype(F32),
      freq.astype(F32), dl)


FFT_G = 4
FFT_UNROLL = 8


def _fft_tables(n):
    s = n // SUBLANES
    nst = int(round(math.log2(s)))
    runs = []
    half = s // 2
    while half >= FFT_G:
        runs.append(-2 * np.pi * np.arange(half) / (2 * half))
        half //= 2
    ang_s = np.concatenate(runs)
    tw_slab = np.stack([np.cos(ang_s), np.sin(ang_s)]).astype(np.float32)
    tw_slab = np.broadcast_to(tw_slab[..., None, None], (2, ang_s.size, SUBLANES, LANES)).copy()
    pos = np.arange(s)
    rev = np.zeros(s, np.int64)
    for bit in range(nst):
        rev |= ((pos >> bit) & 1) << (nst - 1 - bit)
    ang = -2 * np.pi * (rev[:, None] * np.arange(SUBLANES)[None, :]) / n
    tw_mid = np.stack([np.cos(ang), np.sin(ang)]).astype(np.float32)
    tw_mid = np.broadcast_to(tw_mid[..., None], (2, s, SUBLANES, LANES)).copy()
    t = np.arange(n)
    ang2 = -2 * np.pi * t / (2 * n)
    mod = np.stack([np.cos(ang2), np.sin(ang2)]).astype(np.float32).reshape(2, s, SUBLANES)
    mod = np.broadcast_to(mod[..., None], (2, s, SUBLANES, LANES)).copy()
    return jnp.asarray(tw_slab), jnp.asarray(tw_mid), jnp.asarray(mod)


def _sub_patterns():
    sub = lax.broadcasted_iota(jnp.int32, (SUBLANES, LANES), 0)

    def table(vals):
        out = jnp.full((SUBLANES, LANES), vals[0], F32)
        for k in range(1, SUBLANES):
            out = jnp.where(sub == k, np.float32(vals[k]), out)
        return out

    pats = {}
    for dist in (4, 2, 1):
        lo = (sub & dist) == 0
        sgn = jnp.where(lo, 1.0, -1.0).astype(F32)
        wr = [1.0] * SUBLANES
        wi = [0.0] * SUBLANES
        for k in range(SUBLANES):
            if k & dist:
                e = (k % dist) * (SUBLANES // (2 * dist))
                wr[k] = math.cos(-2 * math.pi * e / SUBLANES)
                wi[k] = math.sin(-2 * math.pi * e / SUBLANES)
        pats[dist] = (lo, sgn, table(wr), table(wi))
    pats["quarter"] = (sub & 3) == 3
    return pats


def _dft8_fwd(vr, vi, pats):
    for dist in (4, 2, 1):
        lo, sgn, wr, wi = pats[dist]
        up_r = pltpu.roll(vr, SUBLANES - dist, 0)
        up_i = pltpu.roll(vi, SUBLANES - dist, 0)
        if dist == 4:
            pr, pi_ = up_r, up_i
        else:
            pr = jnp.where(lo, up_r, pltpu.roll(vr, dist, 0))
            pi_ = jnp.where(lo, up_i, pltpu.roll(vi, dist, 0))
        tr = pr + sgn * vr
        ti = pi_ + sgn * vi
        if dist == 1:
            vr, vi = tr, ti
        elif dist == 2:
            qt = pats["quarter"]
            vr, vi = jnp.where(qt, ti, tr), jnp.where(qt, -tr, ti)
        else:
            vr, vi = _cmul(tr, ti, wr, wi)
    return vr, vi


def _dft8_inv(vr, vi, pats):
    for dist in (1, 2, 4):
        lo, sgn, wr, wi = pats[dist]
        if dist == 2:
            qt = pats["quarter"]
            vr, vi = jnp.where(qt, -vi, vr), jnp.where(qt, vr, vi)
        elif dist == 4:
            vr, vi = _cmul(vr, vi, wr, -wi)
        up_r = pltpu.roll(vr, SUBLANES - dist, 0)
        up_i = pltpu.roll(vi, SUBLANES - dist, 0)
        if dist == 4:
            pr, pi_ = up_r, up_i
        else:
            pr = jnp.where(lo, up_r, pltpu.roll(vr, dist, 0))
            pi_ = jnp.where(lo, up_i, pltpu.roll(vi, dist, 0))
        vr = pr + sgn * vr
        vi = pi_ + sgn * vi
    return vr, vi


def _slab_stage(re, im, tw, half, slabs, inverse):
    per_block = half // FFT_G
    nblk = slabs // (2 * half)
    unroll = min(FFT_UNROLL, nblk * per_block)
    tw_off = slabs - 2 * half
    if nblk >= unroll:
        blocks_per_it = unroll // per_block
        trips = nblk // blocks_per_it
        offsets = [(b * 2 * half + jc * FFT_G, jc * FFT_G) for b in range(blocks_per_it) for jc in range(per_block)]
        data_step, tw_step = blocks_per_it * 2 * half, 0
    else:
        chunks_per_it = unroll // nblk
        trips = per_block // chunks_per_it
        offsets = [(b * 2 * half + k * FFT_G, k * FFT_G) for b in range(nblk) for k in range(chunks_per_it)]
        data_step = tw_step = chunks_per_it * FFT_G

    def body(c, carry):
        d0 = pl.multiple_of(c * data_step, FFT_G)
        t0 = pl.multiple_of(tw_off + c * tw_step, FFT_G)
        twiddles = {}
        for d_off, t_off in offsets:
            if t_off not in twiddles:
                tws = pl.ds(t0 + t_off, FFT_G)
                twiddles[t_off] = (tw[0, tws], tw[1, tws])
            wr, wi = twiddles[t_off]
            lo = pl.ds(d0 + d_off, FFT_G)
            hi = pl.ds(d0 + d_off + half, FFT_G)
            ar, ai, br, bi = re[lo], im[lo], re[hi], im[hi]
            if inverse:
                br, bi = br * wr + bi * wi, bi * wr - br * wi
                re[lo] = ar + br
                im[lo] = ai + bi
                re[hi] = ar - br
                im[hi] = ai - bi
            else:
                re[lo] = ar + br
                im[lo] = ai + bi
                dr, di = ar - br, ai - bi
                re[hi] = dr * wr - di * wi
                im[hi] = dr * wi + di * wr
        return carry

    lax.fori_loop(0, trips, body, 0)


def _fft_forward_big(re, im, tw, slabs):
    half = slabs // 2
    while half >= 4:
        _slab_stage(re, im, tw, half, slabs, inverse=False)
        half //= 2


def _fft_inverse_big(re, im, tw, slabs):
    half = 4
    while half <= slabs // 2:
        _slab_stage(re, im, tw, half, slabs, inverse=True)
        half *= 2


def _radix4_fwd(x):
    (x0r, x0i), (x1r, x1i), (x2r, x2i), (x3r, x3i) = x
    y0r, y0i = x0r + x2r, x0i + x2i
    y2r, y2i = x0r - x2r, x0i - x2i
    y1r, y1i = x1r + x3r, x1i + x3i
    dr, di = x1r - x3r, x1i - x3i
    y3r, y3i = di, -dr
    return [(y0r + y1r, y0i + y1i), (y0r - y1r, y0i - y1i), (y2r + y3r, y2i + y3i), (y2r - y3r, y2i - y3i)]


def _radix4_inv(z):
    (z0r, z0i), (z1r, z1i), (z2r, z2i), (z3r, z3i) = z
    y0r, y0i = z0r + z1r, z0i + z1i
    y1r, y1i = z0r - z1r, z0i - z1i
    y2r, y2i = z2r + z3r, z2i + z3i
    y3r, y3i = z2r - z3r, z2i - z3i
    qr, qi = -y3i, y3r
    return [(y0r + y2r, y0i + y2i), (y1r + qr, y1i + qi), (y0r - y2r, y0i - y2i), (y1r - qr, y1i - qi)]


def _fft_middle(re, im, twm_ref, slabs, pats, spec=None, out=None):
    def body(q, carry):
        p0 = q * FFT_G
        x = [(re[p0 + g], im[p0 + g]) for g in range(FFT_G)]
        z = _radix4_fwd(x)
        res = []
        for g in range(FFT_G):
            twr = twm_ref[0, p0 + g]
            twi = twm_ref[1, p0 + g]
            vr, vi = _cmul(z[g][0], z[g][1], twr, twi)
            vr, vi = _dft8_fwd(vr, vi, pats)
            if spec is None:
                out[0][0, 0, p0 + g] = vr * out[2]
                out[1][0, 0, p0 + g] = vi * out[2]
            else:
                vr, vi = _cmul(vr, vi, spec[0][0, 0, p0 + g], spec[1][0, 0, p0 + g])
                vr, vi = _dft8_inv(vr, vi, pats)
                res.append(_cmul(vr, vi, twr, -twi))
        if spec is not None:
            x = _radix4_inv(res)
            for g in range(FFT_G):
                re[p0 + g] = x[g][0]
                im[p0 + g] = x[g][1]
        return carry

    lax.fori_loop(0, slabs // FFT_G, body, 0)


def _hy_spec_kernel(tw_ref, k_ref, twm_ref, mod_ref, sr_ref, si_ref, re, im, *, slabs):
    h = pl.program_id(2)
    pats = _sub_patterns()
    a = k_ref[0:slabs]
    b = k_ref[slabs:2 * slabs]

    @pl.when(h == 0)
    def _():
        re[...] = a + b
        im[...] = jnp.zeros_like(a)

    @pl.when(h == 1)
    def _():
        dlt = a - b
        re[...] = dlt * mod_ref[0]
        im[...] = dlt * mod_ref[1]

    _fft_forward_big(re, im, tw_ref, slabs)
    scale = np.float32(1.0 / (2 * SUBLANES * slabs))
    _fft_middle(re, im, twm_ref, slabs, pats, out=(sr_ref, si_ref, scale))


def _hy_spectra(k, n_tokens):
    slabs = n_tokens // SUBLANES
    tw, twm, mod = _fft_tables(n_tokens)
    k4 = k
    nt = HY_WIDTH // LANES
    shape = jax.ShapeDtypeStruct((HY_ORDER, 2, slabs, SUBLANES, HY_WIDTH), F32)
    spec_out = pl.BlockSpec((1, 1, slabs, SUBLANES, LANES), lambda o, j, h: (o, h, 0, 0, j))
    tab = pl.BlockSpec((2, slabs, SUBLANES, LANES), lambda o, j, h: (0, 0, 0, 0), pipeline_mode=pl.Buffered(1))
    return pl.pallas_call(
        functools.partial(_hy_spec_kernel, slabs=slabs),
        grid=(HY_ORDER, nt, 2),
        in_specs=[pl.BlockSpec(tw.shape, lambda o, j, h: (0, 0, 0, 0), pipeline_mode=pl.Buffered(1)),
                  pl.BlockSpec((2 * slabs, SUBLANES, LANES), lambda o, j, h: (0, 0, o * nt + j)),
                  tab, tab],
        out_specs=[spec_out, spec_out],
        out_shape=[shape, shape],
        scratch_shapes=[pltpu.VMEM((slabs, SUBLANES, LANES), F32)] * 2,
        compiler_params=_cparams(("parallel", "parallel", "arbitrary")), name="hy_spec",
    )(tw, k4, twm, mod)


def _hy_conv_kernel(tw_ref, u_ref, g_ref, sr_ref, si_ref, twm_ref, mod_ref, b_ref, o_ref, re, im, *, slabs):
    h = pl.program_id(2)
    pats = _sub_patterns()
    slab_shape = (slabs, SUBLANES, LANES)
    n = slabs * SUBLANES

    @pl.when(h == 0)
    def _():
        re[...] = u_ref[0].reshape(slab_shape)
        im[...] = u_ref[1].reshape(slab_shape)

    @pl.when(h == 1)
    def _():
        ur, ui = u_ref[0].reshape(slab_shape), u_ref[1].reshape(slab_shape)
        mr, mi = mod_ref[0], mod_ref[1]
        re[...] = ur * mr - ui * mi
        im[...] = ur * mi + ui * mr

    _fft_forward_big(re, im, tw_ref, slabs)
    _fft_middle(re, im, twm_ref, slabs, pats, spec=(sr_ref, si_ref))
    _fft_inverse_big(re, im, tw_ref, slabs)

    @pl.when(h == 0)
    def _():
        o_ref[0] = re[...].reshape(n, LANES)
        o_ref[1] = im[...].reshape(n, LANES)

    @pl.when(h == 1)
    def _():
        yr, yi = re[...], im[...]
        mr, mi = mod_ref[0], mod_ref[1]
        y0 = o_ref[0] + (yr * mr + yi * mi).reshape(n, LANES)
        y1 = o_ref[1] + (yi * mr - yr * mi).reshape(n, LANES)
        o_ref[0] = g_ref[0] * (y0 + b_ref[...] * u_ref[0])
        o_ref[1] = g_ref[1] * (y1 + b_ref[...] * u_ref[1])


def _hy_order(u, u_blk, gate, gate_blk, spec_re, spec_im, order, bias):
    b, n, _ = u.shape
    w = HY_WIDTH
    slabs = n // SUBLANES
    tw, twm, mod = _fft_tables(n)
    nt = w // LANES
    tab = pl.BlockSpec((2, slabs, SUBLANES, LANES), lambda j, p, h: (0, 0, 0, 0), pipeline_mode=pl.Buffered(1))
    spec_in = pl.BlockSpec((1, 1, slabs, SUBLANES, LANES), lambda j, p, h: (order, h, 0, 0, j))

    def io(blk):
        return pl.BlockSpec((2, n, LANES), lambda j, p, h: (p, 0, blk * nt + j))

    return pl.pallas_call(
        functools.partial(_hy_conv_kernel, slabs=slabs),
        grid=(nt, b // 2, 2),
        in_specs=[pl.BlockSpec(tw.shape, lambda j, p, h: (0, 0, 0, 0), pipeline_mode=pl.Buffered(1)),
                  io(u_blk), io(gate_blk), spec_in, spec_in, tab, tab,
                  pl.BlockSpec((1, LANES), lambda j, p, h: (0, j))],
        out_specs=io(0),
        out_shape=jax.ShapeDtypeStruct((b, n, w), F32),
        scratch_shapes=[pltpu.VMEM((slabs, SUBLANES, LANES), F32)] * 2,
        compiler_params=_cparams(("parallel", "parallel", "arbitrary")), name="hy_conv",
    )(tw, u, gate, spec_re, spec_im, twm, mod, bias.astype(F32)[None])


def _shift_rows(x, n):
    row = lax.broadcasted_iota(jnp.int32, x.shape, 0)
    prev = jnp.where(row == 0, 0.0, pltpu.roll(x, 1, 0))
    nxt = jnp.where(row == n - 1, 0.0, pltpu.roll(x, n - 1, 0))
    return prev, nxt


def _hy_short_kernel(z_ref, w_ref, b_ref, o_ref, *, n):
    x = z_ref[0]
    prev, nxt = _shift_rows(x, n)
    o_ref[0] = prev * w_ref[0:1, :] + x * w_ref[1:2, :] + nxt * w_ref[2:3, :] + b_ref[...]


def _hy_short(z_mix, conv_w, conv_b):
    b, n, _ = z_mix.shape
    c3 = 3 * HY_WIDTH
    w = conv_w.reshape(3, c3).astype(F32)
    tc = 256
    col0 = BRANCH_W // tc
    return pl.pallas_call(
        functools.partial(_hy_short_kernel, n=n),
        grid=(b, c3 // tc),
        in_specs=[pl.BlockSpec((1, n, tc), lambda bi, j: (bi, 0, col0 + j)),
                  pl.BlockSpec((3, tc), lambda bi, j: (0, j)),
                  pl.BlockSpec((1, tc), lambda bi, j: (0, j))],
        out_specs=pl.BlockSpec((1, n, tc), lambda bi, j: (bi, 0, j)),
        out_shape=jax.ShapeDtypeStruct((b, n, c3), F32),
        compiler_params=_cparams(("parallel", "parallel")), name="hy_short",
    )(z_mix, w, conv_b.astype(F32)[None])


def _hyena(z_mix, conv_w, conv_b, spec_re, spec_im, bias):
    zc = _hy_short(z_mix, conv_w, conv_b)
    v1 = _hy_order(zc, 0, zc, 1, spec_re, spec_im, 0, bias[0])
    return _hy_order(v1, 0, zc, 2, spec_re, spec_im, 1, bias[1])


NA_QROWS = 8
NA_KROWS = 16
NA_KPART = 256


def _softmax_parts(parts):
    m = None
    for s in parts:
        mm = jnp.max(s, axis=-1, keepdims=True)
        m = mm if m is None else jnp.maximum(m, mm)
    return m


def _na_kernel(q_ref, k0, k1, k2, k3, v0, v1, v2, v3, kc_ref, vc_ref, bias_ref, o_ref):
    q = q_ref[0] * np.float32(NA_HEAD_DIM ** -0.5)
    k = jnp.concatenate([k0[0], k1[0], k2[0], k3[0]], axis=0).astype(BF16)
    v = jnp.concatenate([v0[0], v1[0], v2[0], v3[0]], axis=0).astype(BF16)
    kc = kc_ref[0].astype(BF16)
    vc = vc_ref[0].astype(BF16)
    nt = (((1,), (1,)), ((), ()))
    outs = []
    for h in range(NA_HEADS):
        sl = slice(h * NA_HEAD_DIM, (h + 1) * NA_HEAD_DIM)
        qh = q[:, sl].astype(BF16)
        s_lat = lax.dot_general(qh, k[:, sl], nt, preferred_element_type=F32) + bias_ref[0, h]
        s_ctx = lax.dot_general(qh, kc[:, sl], nt, preferred_element_type=F32)
        m = _softmax_parts([s_lat, s_ctx])
        p_lat = jnp.exp(s_lat - m)
        p_ctx = jnp.exp(s_ctx - m)
        den = jnp.sum(p_lat, axis=-1, keepdims=True) + jnp.sum(p_ctx, axis=-1, keepdims=True)
        o = jnp.dot(p_lat.astype(BF16), v[:, sl], preferred_element_type=F32) \
            + jnp.dot(p_ctx.astype(BF16), vc[:, sl], preferred_element_type=F32)
        outs.append(o / den)
    o_ref[0] = jnp.concatenate(outs, axis=-1)


def _na_bias(rpb, rows):
    kh = min(NA_WIN_H, rows)
    col = np.arange(GRID_W)
    col_start = np.clip(col - NA_WIN_W // 2, 0, GRID_W - NA_WIN_W)
    col_ok = (col[None] >= col_start[:, None]) & (col[None] < col_start[:, None] + NA_WIN_W)
    off_c = np.clip(col[None] - col[:, None], -(NA_WIN_W - 1), NA_WIN_W - 1) + (NA_WIN_W - 1)
    nblk = rows // NA_QROWS
    n_r, n_c = 2 * NA_WIN_H - 1, 2 * NA_WIN_W - 1
    table = jnp.pad(rpb.astype(F32), ((0, 0), (0, 1), (0, 1)), constant_values=NEG_INF)
    sel_c = np.eye(n_c + 1, dtype=np.float32)[np.where(col_ok, off_c, n_c)]
    blocks = jnp.einsum('hab,qcb->hqac', table, sel_c, precision=HI).reshape(NA_HEADS, GRID_W, (n_r + 1) * GRID_W)
    blocks_odd = jnp.roll(blocks, -GRID_W, axis=-1)
    row_off = []
    for j in (0, 1, nblk - 1):
        qr = j * NA_QROWS + np.arange(NA_QROWS)
        ws = int(np.clip(j * NA_QROWS - NA_WIN_H // 2, 0, rows - NA_KROWS))
        kr = ws + np.arange(NA_KROWS)
        start = np.clip(qr - kh // 2, 0, rows - kh)
        row_ok = (kr[None] >= start[:, None]) & (kr[None] < start[:, None] + kh)
        off_r = kr[None] - qr[:, None] + (NA_WIN_H - 1)
        row_off.append(np.where(row_ok, off_r, n_r))
    row_off = np.stack(row_off)
    wide = (n_r + 1) * GRID_W
    return pl.pallas_call(
        functools.partial(_na_bias_kernel, row_off=row_off),
        grid=(NA_HEADS,),
        in_specs=[pl.BlockSpec((1, GRID_W, wide), lambda h: (h, 0, 0))] * 2,
        out_specs=pl.BlockSpec((3, 1, NA_QROWS * GRID_W, NA_KROWS * GRID_W), lambda h: (0, h, 0, 0)),
        out_shape=jax.ShapeDtypeStruct((3, NA_HEADS, NA_QROWS * GRID_W, NA_KROWS * GRID_W), F32),
        compiler_params=_cparams(("parallel",)), name="na_bias",
    )(blocks, blocks_odd)


def _na_bias_kernel(even_ref, odd_ref, o_ref, *, row_off):
    n_cls, n_q, n_k = row_off.shape
    masked = 2 * NA_WIN_H - 1
    pair = 2 * GRID_W
    lane = lax.broadcasted_iota(jnp.int32, (GRID_W, pair), 1)

    def left(a):
        src, start = (even_ref, a) if a % 2 == 0 else (odd_ref, a - 1)
        return src[0, :, start * GRID_W:start * GRID_W + pair]

    def right(a):
        assert a % 2 == 1
        return even_ref[0, :, (a - 1) * GRID_W:(a - 1) * GRID_W + pair]

    for t in range(n_cls):
        for r in range(n_q):
            for kp in range(n_k // 2):
                a0, a1 = int(row_off[t, r, 2 * kp]), int(row_off[t, r, 2 * kp + 1])
                if a0 != masked and a1 == a0 + 1:
                    blk = left(a0)
                elif a0 == masked and a1 == masked:
                    blk = jnp.full((GRID_W, pair), NEG_INF, F32)
                elif a1 == masked:
                    blk = jnp.where(lane < GRID_W, left(a0), NEG_INF)
                else:
                    blk = jnp.where(lane >= GRID_W, right(a1), NEG_INF)
                o_ref[t, 0, r * GRID_W:(r + 1) * GRID_W, kp * pair:(kp + 1) * pair] = blk


def _na_attention(z_mix, zc_mix, rpb):
    b, n, _ = z_mix.shape
    n_ctx = zc_mix.shape[1]
    rows = n // GRID_W
    nblk = rows // NA_QROWS
    tq = NA_QROWS * GRID_W
    bias = _na_bias(rpb, rows)
    qb, kb, vb = 1024 // BRANCH_W, 1024 // BRANCH_W + 1, 1024 // BRANCH_W + 2
    kparts = NA_KROWS * GRID_W // NA_KPART
    max_k0 = (rows - NA_KROWS) * GRID_W // NA_KPART

    def k0_of(j):
        return jnp.clip(2 * j - 1, 0, max_k0)

    def kspec(part, blk):
        return pl.BlockSpec((1, NA_KPART, BRANCH_W), lambda bi, j: (bi, k0_of(j) + part, blk))

    def cls(j):
        return jnp.where(j == 0, 0, jnp.where(j == nblk - 1, 2, 1))

    return pl.pallas_call(
        _na_kernel,
        grid=(b, nblk),
        in_specs=[pl.BlockSpec((1, tq, BRANCH_W), lambda bi, j: (bi, j, qb))]
        + [kspec(p, kb) for p in range(kparts)] + [kspec(p, vb) for p in range(kparts)]
        + [pl.BlockSpec((1, n_ctx, BRANCH_W), lambda bi, j: (bi, 0, kb)),
           pl.BlockSpec((1, n_ctx, BRANCH_W), lambda bi, j: (bi, 0, vb)),
           pl.BlockSpec((1, NA_HEADS, tq, NA_KROWS * GRID_W), lambda bi, j: (cls(j), 0, 0, 0))],
        out_specs=pl.BlockSpec((1, tq, BRANCH_W), lambda bi, j: (bi, j, 0)),
        out_shape=jax.ShapeDtypeStruct((b, n, BRANCH_W), F32),
        compiler_params=_cparams(("parallel", "arbitrary")), name="na_attn",
    )(z_mix, *([z_mix] * (2 * kparts)), zc_mix, zc_mix, bias)


def _rope_tables(n_tokens, head_dim, heads):
    t = np.arange(n_tokens)
    row = (t // GRID_W).astype(np.float64)
    col = (t % GRID_W).astype(np.float64)
    half = head_dim // 2
    inv = ROPE_BASE ** (-(np.arange(0, half, 2, dtype=np.float64) / half))
    ang = np.concatenate([row[:, None] * inv, col[:, None] * inv], axis=-1)
    cos, sin = np.cos(ang), np.sin(ang)
    cos_t = np.tile(np.concatenate([cos, cos], axis=-1), (1, heads))
    sin_t = np.tile(np.concatenate([-sin, sin], axis=-1), (1, heads))
    return jnp.asarray(cos_t, F32), jnp.asarray(sin_t, F32)


def _rope(x, cos_t, sin_t):
    w = x.shape[-1]
    half = SW_HEAD_DIM // 2
    lane = lax.broadcasted_iota(jnp.int32, x.shape, 1)
    first = (lane % SW_HEAD_DIM) < half
    partner = jnp.where(first, pltpu.roll(x, w - half, 1), pltpu.roll(x, half, 1))
    return x * cos_t + partner * sin_t


SW_QBLK = 2 * SW_BLOCK
SW_KPARTS = SW_QBLK // SW_BLOCK + 2


def _sw_rope_kernel(q_ref, kv_ref, cos_ref, sin_ref, qo_ref, kvo_ref):
    cos_t, sin_t = cos_ref[...], sin_ref[...]
    q = _rope(q_ref[0], cos_t, sin_t) * np.float32(SW_HEAD_DIM ** -0.5)
    qo_ref[0] = q.astype(BF16)
    kv = kv_ref[0]
    lane = lax.broadcasted_iota(jnp.int32, kv.shape, 1)
    kvo_ref[0] = jnp.where(lane < SW_KV_HEADS * SW_HEAD_DIM, _rope(kv, cos_t, sin_t), kv).astype(BF16)


def _sw_rope(z_mix):
    b, n, _ = z_mix.shape
    qw = SW_HEADS * SW_HEAD_DIM
    cos_t, sin_t = _rope_tables(n, SW_HEAD_DIM, SW_HEADS)
    tm = min(n, 1024)
    tok = lambda blk: pl.BlockSpec((1, tm, qw), lambda bi, i: (bi, i, blk))
    tab = pl.BlockSpec((tm, qw), lambda bi, i: (i, 0))
    shape = jax.ShapeDtypeStruct((b, n, qw), BF16)
    return pl.pallas_call(
        _sw_rope_kernel,
        grid=(b, n // tm),
        in_specs=[tok(1792 // qw), tok(1792 // qw + 1), tab, tab],
        out_specs=[tok(0), tok(0)],
        out_shape=[shape, shape],
        compiler_params=_cparams(("parallel", "parallel")), name="sw_rope",
    )(z_mix, z_mix, cos_t, sin_t)


def _sw_kernel(q_ref, kv0, kv1, kv2, kv3, kc_ref, vc_ref, sink_ref, o_ref, *, nblk):
    i = pl.program_id(1)
    kvw = SW_KV_HEADS * SW_HEAD_DIM
    q = q_ref[0]
    kv = jnp.concatenate([kv0[0], kv1[0], kv2[0], kv3[0]], axis=0)
    kc = kc_ref[0].astype(BF16)
    vc = vc_ref[0].astype(BF16)
    g_ = SW_HEADS // SW_KV_HEADS
    rows = g_ * SW_QBLK
    span = SW_KPARTS * SW_BLOCK
    r = lax.broadcasted_iota(jnp.int32, (rows, span), 0) % SW_QBLK
    c = lax.broadcasted_iota(jnp.int32, (rows, span), 1)
    diff = c - SW_BLOCK - r
    blk = i * (SW_QBLK // SW_BLOCK) - 1 + c // SW_BLOCK
    ok = (jnp.abs(diff) <= SW_WINDOW) & (blk >= 0) & (blk < nblk)
    nt = (((1,), (1,)), ((), ()))
    outs = []
    for kvh in range(SW_KV_HEADS):
        ksl = slice(kvh * SW_HEAD_DIM, (kvh + 1) * SW_HEAD_DIM)
        vsl = slice(kvw + kvh * SW_HEAD_DIM, kvw + (kvh + 1) * SW_HEAD_DIM)
        qg = jnp.concatenate([q[:, (kvh * g_ + g) * SW_HEAD_DIM:(kvh * g_ + g + 1) * SW_HEAD_DIM] for g in range(g_)],
                             axis=0)
        s_lat = jnp.where(ok, lax.dot_general(qg, kv[:, ksl], nt, preferred_element_type=F32), NEG_INF)
        s_ctx = lax.dot_general(qg, kc[:, ksl], nt, preferred_element_type=F32)
        rr = lax.broadcasted_iota(jnp.int32, (rows, 1), 0)
        s_sink = jnp.zeros((rows, 1), F32)
        for g in range(g_):
            s_sink = jnp.where(rr // SW_QBLK == g, sink_ref[kvh * g_ + g], s_sink)
        m = jnp.maximum(_softmax_parts([s_lat, s_ctx]), s_sink)
        p_lat = jnp.exp(s_lat - m)
        p_ctx = jnp.exp(s_ctx - m)
        den = jnp.sum(p_lat, axis=-1, keepdims=True) + jnp.sum(p_ctx, axis=-1, keepdims=True) + jnp.exp(s_sink - m)
        o = jnp.dot(p_lat.astype(BF16), kv[:, vsl], preferred_element_type=F32) \
            + jnp.dot(p_ctx.astype(BF16), vc[:, ksl], preferred_element_type=F32)
        o = o / den
        outs += [o[g * SW_QBLK:(g + 1) * SW_QBLK] for g in range(g_)]
    o_ref[0] = jnp.concatenate(outs, axis=-1)


def _sw_attention(z_mix, zc_mix, sink):
    b, n, _ = z_mix.shape
    n_ctx = zc_mix.shape[1]
    nblk = n // SW_BLOCK
    qw = SW_HEADS * SW_HEAD_DIM
    kvw = SW_KV_HEADS * SW_HEAD_DIM
    k_blk = (1792 + qw) // kvw
    q_r, kv_r = _sw_rope(z_mix)
    per_q = SW_QBLK // SW_BLOCK

    def kpart(part):
        return pl.BlockSpec((1, SW_BLOCK, qw),
                            lambda bi, i: (bi, jnp.clip(i * per_q - 1 + part, 0, nblk - 1), 0))

    return pl.pallas_call(
        functools.partial(_sw_kernel, nblk=nblk),
        grid=(b, n // SW_QBLK),
        in_specs=[pl.BlockSpec((1, SW_QBLK, qw), lambda bi, i: (bi, i, 0))]
        + [kpart(p) for p in range(SW_KPARTS)]
        + [pl.BlockSpec((1, n_ctx, kvw), lambda bi, i: (bi, 0, k_blk)),
           pl.BlockSpec((1, n_ctx, kvw), lambda bi, i: (bi, 0, k_blk + 1)),
           pl.BlockSpec(memory_space=pltpu.SMEM)],
        out_specs=pl.BlockSpec((1, SW_QBLK, qw), lambda bi, i: (bi, i, 0)),
        out_shape=jax.ShapeDtypeStruct((b, n, qw), F32),
        compiler_params=_cparams(("parallel", "arbitrary")), name="sw_attn",
    )(q_r, *([kv_r] * SW_KPARTS), zc_mix, zc_mix, sink.astype(F32))


def _ctx_attn_kernel(q_ref, k_ref, v_ref, sink_ref, o_ref, *, heads, kv_heads, dh, use_sink):
    q = q_ref[0] * np.float32(dh ** -0.5)
    k = k_ref[0].astype(BF16)
    v = v_ref[0].astype(BF16)
    g_ = heads // kv_heads
    nt = (((1,), (1,)), ((), ()))
    outs = []
    for h in range(heads):
        kv = h // g_
        s = lax.dot_general(q[:, h * dh:(h + 1) * dh].astype(BF16), k[:, kv * dh:(kv + 1) * dh], nt,
                            preferred_element_type=F32)
        m = jnp.max(s, axis=-1, keepdims=True)
        if use_sink:
            m = jnp.maximum(m, sink_ref[h])
        p = jnp.exp(s - m)
        den = jnp.sum(p, axis=-1, keepdims=True)
        if use_sink:
            den = den + jnp.exp(sink_ref[h] - m)
        outs.append(jnp.dot(p.astype(BF16), v[:, kv * dh:(kv + 1) * dh], preferred_element_type=F32) / den)
    o_ref[0] = jnp.concatenate(outs, axis=-1)


def _ctx_attention(zc_mix, col0, heads, kv_heads, dh, sink):
    b, n, _ = zc_mix.shape
    qw, kvw = heads * dh, kv_heads * dh
    use_sink = sink is not None
    sink_arr = sink.astype(F32) if use_sink else jnp.zeros((heads,), F32)
    return pl.pallas_call(
        functools.partial(_ctx_attn_kernel, heads=heads, kv_heads=kv_heads, dh=dh, use_sink=use_sink),
        grid=(b,),
        in_specs=[pl.BlockSpec((1, n, qw), lambda bi: (bi, 0, col0 // qw)),
                  pl.BlockSpec((1, n, kvw), lambda bi: (bi, 0, (col0 + qw) // kvw)),
                  pl.BlockSpec((1, n, kvw), lambda bi: (bi, 0, (col0 + qw) // kvw + 1)),
                  pl.BlockSpec(memory_space=pltpu.SMEM)],
        out_specs=pl.BlockSpec((1, n, qw), lambda bi: (bi, 0, 0)),
        out_shape=jax.ShapeDtypeStruct((b, n, qw), F32),
        compiler_params=_cparams(("parallel",)), name="ctx_attn",
    )(zc_mix, zc_mix, zc_mix, sink_arr)


def _merge_kernel(x_ref, s5_ref, hy_ref, na_ref, sw_ref, gt_ref, ga_ref, wglu_ref, wb_ref, wo_ref,
                  lg_ref, lb_ref, o_ref, tok_scr):
    rows = tok_scr.shape[1] // S5_T
    for c in range(BRANCH_W // LANES):
        for t in range(S5_T):
            col = t * BRANCH_W + c * LANES
            tok_scr[c, pl.ds(t, rows, stride=S5_T), :] = s5_ref[0, :, col:col + LANES]
    g = jax.nn.gelu(jnp.concatenate([tok_scr[c] for c in range(BRANCH_W // LANES)], axis=-1))
    s5 = g * jax.nn.sigmoid(jnp.dot(g.astype(BF16), wglu_ref[...], preferred_element_type=F32))
    branches = (s5, hy_ref[0], na_ref[0], sw_ref[0])
    acc = None
    for n in range(N_BRANCH):
        proj = jnp.dot(branches[n].astype(BF16), wb_ref[n], preferred_element_type=F32)
        t = gt_ref[0, :, n * D_MODEL:(n + 1) * D_MODEL].astype(F32) * proj
        acc = t if acc is None else acc + t
    mix = jnp.dot(acc.astype(BF16), wo_ref[...], preferred_element_type=F32)
    y = np.float32(DEEPNORM_ALPHA) * x_ref[0] + ga_ref[0] * mix
    o_ref[0] = _layernorm(y) * lg_ref[...] + lb_ref[...]


def _merge(x, s5y, hy, na, sw, gates, g_a, w_glu, w_branch, w_out, ln_g, ln_b, *, tm):
    b, l, d = x.shape
    br = pl.BlockSpec((1, tm, BRANCH_W), lambda bi, i: (bi, i, 0))
    full = lambda s: pl.BlockSpec(s, lambda bi, i: (0,) * len(s))
    return pl.pallas_call(
        _merge_kernel,
        grid=(b, l // tm),
        in_specs=[pl.BlockSpec((1, tm, d), lambda bi, i: (bi, i, 0)),
                  pl.BlockSpec((1, tm // S5_T, S5_T * BRANCH_W), lambda bi, i: (bi, i, 0)), br, br, br,
                  pl.BlockSpec((1, tm, GATE_W), lambda bi, i: (bi, i, 0)),
                  pl.BlockSpec((1, 1, d), lambda bi, i: (bi, 0, 0)),
                  full((BRANCH_W, BRANCH_W)), full((N_BRANCH, BRANCH_W, d)), full((d, d)),
                  full((1, d)), full((1, d))],
        out_specs=pl.BlockSpec((1, tm, d), lambda bi, i: (bi, i, 0)),
        out_shape=jax.ShapeDtypeStruct((b, l, d), F32),
        scratch_shapes=[pltpu.VMEM((BRANCH_W // LANES, tm, LANES), F32)],
        compiler_params=_cparams(("parallel", "parallel")), name="merge",
    )(x, s5y, hy, na, sw, gates, g_a, w_glu, w_branch, w_out, ln_g, ln_b)


def _mlp_kernel(x_ref, sh_ref, sc_ref, gm_ref, w1_ref, w2_ref, lg_ref, lb_ref, o_ref, h_scr, acc_scr):
    j = pl.program_id(2)

    @pl.when(j == 0)
    def _():
        h = _layernorm(x_ref[0]) * (1.0 + sc_ref[0]) + sh_ref[0]
        h_scr[...] = h.astype(BF16)
        acc_scr[...] = jnp.zeros_like(acc_scr)

    a = jnp.dot(h_scr[...], w1_ref[...], preferred_element_type=F32)
    a = jnp.square(jnp.maximum(a, 0.0))
    acc_scr[...] += jnp.dot(a.astype(BF16), w2_ref[...], preferred_element_type=F32)

    @pl.when(j == pl.num_programs(2) - 1)
    def _():
        y = np.float32(DEEPNORM_ALPHA) * x_ref[0] + gm_ref[0] * acc_scr[...]
        o_ref[0] = _layernorm(y) * lg_ref[...] + lb_ref[...]


def _mlp(x, sh, sc, g_m, w1, w2, ln_g, ln_b, *, tm, th):
    b, l, d = x.shape
    hdim = w1.shape[1]
    mod = pl.BlockSpec((1, 1, d), lambda bi, i, j: (bi, 0, 0))
    vec = pl.BlockSpec((1, d), lambda bi, i, j: (0, 0))
    return pl.pallas_call(
        _mlp_kernel,
        grid=(b, l // tm, hdim // th),
        in_specs=[pl.BlockSpec((1, tm, d), lambda bi, i, j: (bi, i, 0)), mod, mod, mod,
                  pl.BlockSpec((d, th), lambda bi, i, j: (0, j)),
                  pl.BlockSpec((th, d), lambda bi, i, j: (j, 0)), vec, vec],
        out_specs=pl.BlockSpec((1, tm, d), lambda bi, i, j: (bi, i, 0)),
        out_shape=jax.ShapeDtypeStruct((b, l, d), F32),
        scratch_shapes=[pltpu.VMEM((tm, d), BF16), pltpu.VMEM((tm, d), F32)],
        compiler_params=_cparams(("parallel", "parallel", "arbitrary")), name="mlp",
    )(x, sh, sc, g_m, w1, w2, ln_g, ln_b)


def kernel(x, c, ctx, c_ctx, w_ada, b_ada, w_in, s5_lambda_re, s5_lambda_im, s5_log_dt, s5_b_re, s5_b_im, s5_c_re,
           s5_c_im, s5_d, s5_w_glu, hy_conv_w, hy_conv_b, hy_freq, hy_w1, hy_b1, hy_w2, hy_b2, hy_w3, hy_bias,
           na_rpb, sw_sink, w_branch, w_out, ln1_g, ln1_b, w_mlp1, w_mlp2, ln2_g, ln2_b):
    b, l, d = x.shape
    n_ctx = ctx.shape[1]
    depth = w_ada.shape[0]
    cc = jnp.zeros((8, d), F32).at[:b].set(c.astype(F32)).at[b].set(c_ctx.astype(F32))
    mod_all = _ada(cc, w_ada.astype(F32), b_ada.astype(F32))
    xc = ctx
    for layer in range(depth):
        need_ctx_out = layer < depth - 1
        mod = mod_all[layer, :b].reshape(b, 1, 6, d)
        mod_c = jnp.broadcast_to(mod_all[layer, b].reshape(1, 1, 6, d), (b, 1, 6, d))
        sh_a, sc_a, g_a, sh_m, sc_m, g_m = [mod[:, :, i] for i in range(6)]
        csh_a, csc_a, cg_a, csh_m, csc_m, cg_m = [mod_c[:, :, i] for i in range(6)]
        w_in_l = w_in[layer].astype(BF16)
        w_mix, w_gate = w_in_l[:, :MIX_W], w_in_l[:, MIX_W:]

        z_mix, u_s5 = _ln_mod_mm(x, sh_a, sc_a, w_mix, tm=1024, tn=MIX_W // 3, chunked=True)
        gates = _ln_mod_mm(x, sh_a, sc_a, w_gate, tm=1024, tn=1024, act="sigmoid", out_dtype=BF16)
        zc_mix, uc_s5 = _ln_mod_mm(xc, csh_a, csc_a, w_mix, tm=n_ctx, tn=MIX_W // 3, chunked=True)

        ops = _s5_operators(s5_lambda_re[layer], s5_lambda_im[layer], s5_log_dt[layer], s5_b_re[layer],
                            s5_b_im[layer], s5_c_re[layer], s5_c_im[layer], s5_d[layer])
        yc_s5, e_ctx = _s5_mix(uc_s5, ops, [jnp.zeros((b, 1, S5_STATE_W // 4), F32)] * 4)
        y_s5, _ = _s5_mix(u_s5, ops, e_ctx)

        hy_args = (hy_freq[layer], hy_w1[layer], hy_b1[layer], hy_w2[layer], hy_b2[layer], hy_w3[layer])
        sp_re, sp_im = _hy_spectra(_hy_filters(l, *hy_args), l)
        hy_l = _hyena(z_mix, hy_conv_w[layer], hy_conv_b[layer], sp_re, sp_im, hy_bias[layer])

        na_l = _na_attention(z_mix, zc_mix, na_rpb[layer])
        sw_l = _sw_attention(z_mix, zc_mix, sw_sink[layer])

        w_glu = s5_w_glu[layer].astype(BF16)
        w_br = w_branch[layer].astype(BF16)
        w_o = w_out[layer].astype(BF16)
        lg1, lb1 = ln1_g[layer].astype(F32)[None], ln1_b[layer].astype(F32)[None]
        lg2, lb2 = ln2_g[layer].astype(F32)[None], ln2_b[layer].astype(F32)[None]
        w1 = w_mlp1[layer].astype(BF16)
        w2 = w_mlp2[layer].astype(BF16)

        x_new = _merge(x, y_s5, hy_l, na_l, sw_l, gates, g_a, w_glu, w_br, w_o, lg1, lb1, tm=512)
        x_new = _mlp(x_new, sh_m, sc_m, g_m, w1, w2, lg2, lb2, tm=1024, th=1024)

        if need_ctx_out:
            gates_c = _ln_mod_mm(xc, csh_a, csc_a, w_gate, tm=n_ctx, tn=1024, act="sigmoid", out_dtype=BF16)
            spc_re, spc_im = _hy_spectra(_hy_filters(n_ctx, *hy_args), n_ctx)
            hy_c = _hyena(zc_mix, hy_conv_w[layer], hy_conv_b[layer], spc_re, spc_im, hy_bias[layer])
            na_c = _ctx_attention(zc_mix, 1024, NA_HEADS, NA_HEADS, NA_HEAD_DIM, None)
            sw_c = _ctx_attention(zc_mix, 1792, SW_HEADS, SW_KV_HEADS, SW_HEAD_DIM, sw_sink[layer])
            xc_new = _merge(xc, yc_s5, hy_c, na_c, sw_c, gates_c, cg_a, w_glu, w_br, w_o, lg1, lb1, tm=n_ctx)
            xc = _mlp(xc_new, csh_m, csc_m, cg_m, w1, w2, lg2, lb2, tm=n_ctx, th=1024)
        x = x_new
    return x
```

```python
import functools
import math

import numpy as np
import jax
import jax.numpy as jnp
from jax import lax
from jax.experimental import pallas as pl
from jax.experimental.pallas import tpu as pltpu

F32 = jnp.float32
BF16 = jnp.bfloat16

D_MODEL = 1024
GRID_W = 64
BRANCH_W = 256
N_BRANCH = 4
S5_GROUP = 16
S5_GROUPS = 16
S5_STATE = 64
HY_WIDTH = 256
HY_ORDER = 2
HY_BANDS = 16
HY_EMB = 2 * HY_BANDS + 1
HY_FFN = 64
HY_MIN_DECAY = math.log(1e-2) / 1.5
HY_MAX_DECAY = math.log(1e-2) / 0.3
NA_HEADS = 4
NA_HEAD_DIM = 64
NA_WIN_H = 8
NA_WIN_W = 16
SW_HEADS = 4
SW_KV_HEADS = 2
SW_HEAD_DIM = 64
SW_WINDOW = 128
SW_BLOCK = 128
MLP_HIDDEN = 4 * D_MODEL
ROPE_BASE = 10000.0
LN_EPS = 1e-6
NEG_INF = -1e30
DEPTH = 2
DEEPNORM_ALPHA = (2 * DEPTH) ** 0.25

MIX_W = 2304
GATE_W = N_BRANCH * D_MODEL
S5_T = 8
S5_STATE_W = 4 * S5_GROUPS * S5_STATE
SUBLANES = 8
LANES = 128
VMEM_LIMIT = 56 * 1024 * 1024

HI = lax.Precision.HIGHEST


def _cparams(sem):
    return pltpu.CompilerParams(dimension_semantics=sem, vmem_limit_bytes=VMEM_LIMIT)


def _layernorm(x):
    mu = jnp.mean(x, axis=-1, keepdims=True)
    xc = x - mu
    var = jnp.mean(xc * xc, axis=-1, keepdims=True)
    return xc * lax.rsqrt(var + LN_EPS)


def _ada_kernel(c_ref, w_ref, b_ref, o_ref):
    c = c_ref[...]
    a = c * jax.nn.sigmoid(c)
    o_ref[0] = jnp.dot(a, w_ref[0], preferred_element_type=F32, precision=HI) + b_ref[0]


def _ada(cc, w_ada, b_ada):
    depth, d, n = w_ada.shape
    tn = 512
    return pl.pallas_call(
        _ada_kernel,
        grid=(depth, n // tn),
        in_specs=[pl.BlockSpec((8, d), lambda l, j: (0, 0)),
                  pl.BlockSpec((1, d, tn), lambda l, j: (l, 0, j)),
                  pl.BlockSpec((1, 1, tn), lambda l, j: (l, 0, j))],
        out_specs=pl.BlockSpec((1, 8, tn), lambda l, j: (l, 0, j)),
        out_shape=jax.ShapeDtypeStruct((depth, 8, n), F32),
        compiler_params=_cparams(("parallel", "parallel")), name="ada",
    )(cc, w_ada, b_ada.reshape(depth, 1, n))


IN_MIX_TN = MIX_W // 3
IN_GATE_TN = GATE_W // 4


def _in_proj_kernel(x_ref, sh_ref, sc_ref, wm_ref, *rest, n_mix, with_gates):
    if with_gates:
        wg_ref, z_ref, u_ref, g_ref, tok_scr, h_scr = rest
    else:
        z_ref, u_ref, tok_scr, h_scr = rest
    j = pl.program_id(2)

    @pl.when(j == 0)
    def _():
        h = _layernorm(x_ref[0]) * (1.0 + sc_ref[0]) + sh_ref[0]
        h_scr[...] = h.astype(BF16)

    @pl.when(j < n_mix)
    def _():
        r = jnp.dot(h_scr[...], wm_ref[...], preferred_element_type=F32)
        z_ref[0] = r

        @pl.when(j == 0)
        def _():
            rows = tok_scr.shape[1] // S5_T
            for c in range(BRANCH_W // LANES):
                tok_scr[c] = r[:, c * LANES:(c + 1) * LANES]
                for t in range(S5_T):
                    col = t * BRANCH_W + c * LANES
                    u_ref[0, :, col:col + LANES] = tok_scr[c, pl.ds(t, rows, stride=S5_T), :]

    if with_gates:
        @pl.when(j >= n_mix)
        def _():
            r = jnp.dot(h_scr[...], wg_ref[...], preferred_element_type=F32)
            g_ref[0] = jax.nn.sigmoid(r).astype(g_ref.dtype)


def _in_proj(x, sh, sc, w_mix, w_gate, *, tm):
    b, l, d = x.shape
    with_gates = w_gate is not None
    n_mix = MIX_W // IN_MIX_TN
    n_gate = GATE_W // IN_GATE_TN if with_gates else 0
    mix_j = lambda j: jnp.minimum(j, n_mix - 1)
    gate_j = lambda j: jnp.maximum(j - n_mix, 0)
    in_specs = [pl.BlockSpec((1, tm, d), lambda bi, i, j: (bi, i, 0)),
                pl.BlockSpec((1, 1, d), lambda bi, i, j: (bi, 0, 0)),
                pl.BlockSpec((1, 1, d), lambda bi, i, j: (bi, 0, 0)),
                pl.BlockSpec((d, IN_MIX_TN), lambda bi, i, j: (0, mix_j(j)))]
    out_specs = [pl.BlockSpec((1, tm, IN_MIX_TN), lambda bi, i, j: (bi, i, mix_j(j))),
                 pl.BlockSpec((1, tm // S5_T, S5_T * BRANCH_W), lambda bi, i, j: (bi, i, 0))]
    out_shape = [jax.ShapeDtypeStruct((b, l, MIX_W), F32),
                 jax.ShapeDtypeStruct((b, l // S5_T, S5_T * BRANCH_W), F32)]
    args = [x, sh, sc, w_mix]
    if with_gates:
        in_specs.append(pl.BlockSpec((d, IN_GATE_TN), lambda bi, i, j: (0, gate_j(j))))
        out_specs.append(pl.BlockSpec((1, tm, IN_GATE_TN), lambda bi, i, j: (bi, i, gate_j(j))))
        out_shape.append(jax.ShapeDtypeStruct((b, l, GATE_W), BF16))
        args.append(w_gate)
    outs = pl.pallas_call(
        functools.partial(_in_proj_kernel, n_mix=n_mix, with_gates=with_gates),
        grid=(b, l // tm, n_mix + n_gate),
        in_specs=in_specs,
        out_specs=out_specs,
        out_shape=out_shape,
        scratch_shapes=[pltpu.VMEM((BRANCH_W // LANES, tm, LANES), F32), pltpu.VMEM((tm, d), BF16)],
        compiler_params=_cparams(("parallel", "parallel", "arbitrary")), name="in_proj",
    )(*args)
    return outs if with_gates else (outs[0], outs[1], None)


def _s5_inc_kernel(u_ref, q_ref, o_ref):
    o_ref[...] = jnp.dot(u_ref[...].astype(BF16), q_ref[...], preferred_element_type=F32)


def _s5_increments(uu, q_all, *, tm, tn):
    m, k = uu.shape
    n = q_all.shape[1]
    return pl.pallas_call(
        _s5_inc_kernel,
        grid=(m // tm, n // tn),
        in_specs=[pl.BlockSpec((tm, k), lambda i, j: (i, 0)), pl.BlockSpec((k, tn), lambda i, j: (0, j))],
        out_specs=pl.BlockSpec((tm, tn), lambda i, j: (i, j)),
        out_shape=jax.ShapeDtypeStruct((m, n), F32),
        compiler_params=_cparams(("parallel", "parallel")), name="s5_inc",
    )(uu, q_all)


def _s5_out_kernel(u_ref, h0, h1, h2, h3, m_ref, n0, n1, n2, n3, o_ref):
    nt = (((1,), (1,)), ((), ()))
    acc = jnp.dot(u_ref[...].astype(BF16), m_ref[...], preferred_element_type=F32)
    for h_ref, n_ref in ((h0, n0), (h1, n1), (h2, n2), (h3, n3)):
        acc += lax.dot_general(h_ref[...].astype(BF16), n_ref[...], nt, preferred_element_type=F32)
    o_ref[...] = acc


def _s5_outputs(uu, h, m_tot, n_t, *, tm, tn):
    m, k = uu.shape
    gp = h[0].shape[1]
    n = m_tot.shape[1]
    return pl.pallas_call(
        _s5_out_kernel,
        grid=(m // tm, n // tn),
        in_specs=[pl.BlockSpec((tm, k), lambda i, j: (i, 0))]
        + [pl.BlockSpec((tm, gp), lambda i, j: (i, 0))] * 4
        + [pl.BlockSpec((k, tn), lambda i, j: (0, j))]
        + [pl.BlockSpec((tn, gp), lambda i, j, c=c: (j, c)) for c in range(4)],
        out_specs=pl.BlockSpec((tm, tn), lambda i, j: (i, j)),
        out_shape=jax.ShapeDtypeStruct((m, n), F32),
        compiler_params=_cparams(("parallel", "parallel")), name="s5_out",
    )(uu, *h, m_tot, n_t, n_t, n_t, n_t)


def _s5_scan_kernel(gfr, gfi, gbr, gbi, afr, afi, abr, abi, h0fr, h0fi, h0br, h0bi,
                    hfr, hfi, hbr, hbi, efr, efi, ebr, ebi, *, n_chunks):
    a_fr = afr[...][None]
    a_fi = afi[...][None]
    a_br = abr[...][None]
    a_bi = abi[...][None]

    def body(k, carry):
        sfr, sfi, sbr, sbi = carry
        kb = n_chunks - 1 - k
        hfr[:, pl.ds(k, 1), :] = sfr
        hfi[:, pl.ds(k, 1), :] = sfi
        hbr[:, pl.ds(kb, 1), :] = sbr
        hbi[:, pl.ds(kb, 1), :] = sbi
        nfr = a_fr * sfr - a_fi * sfi + gfr[:, pl.ds(k, 1), :]
        nfi = a_fr * sfi + a_fi * sfr + gfi[:, pl.ds(k, 1), :]
        nbr = a_br * sbr - a_bi * sbi + gbr[:, pl.ds(kb, 1), :]
        nbi = a_br * sbi + a_bi * sbr + gbi[:, pl.ds(kb, 1), :]
        return nfr, nfi, nbr, nbi

    sfr, sfi, sbr, sbi = lax.fori_loop(0, n_chunks, body, (h0fr[...], h0fi[...], h0br[...], h0bi[...]))
    efr[...] = sfr
    efi[...] = sfi
    ebr[...] = sbr
    ebi[...] = sbi


def _s5_scan(g, a_t, h0):
    b, k, w4 = g.shape
    w = 2 * LANES
    q = w4 // 4
    nb = q // w

    def comp(c):
        return pl.BlockSpec((b, k, w), lambda j, c=c: (0, 0, c * nb + j))

    def comp_a(c):
        return pl.BlockSpec((1, w), lambda j, c=c: (0, c * nb + j))

    state = pl.BlockSpec((b, k, w), lambda j: (0, 0, j))
    edge = pl.BlockSpec((b, 1, w), lambda j: (0, 0, j))
    outs = pl.pallas_call(
        functools.partial(_s5_scan_kernel, n_chunks=k),
        grid=(nb,),
        in_specs=[comp(c) for c in range(4)] + [comp_a(c) for c in range(4)] + [edge] * 4,
        out_specs=[state] * 4 + [edge] * 4,
        out_shape=[jax.ShapeDtypeStruct((b, k, q), F32)] * 4 + [jax.ShapeDtypeStruct((b, 1, q), F32)] * 4,
        compiler_params=_cparams(("parallel",)), name="s5_scan",
    )(g, g, g, g, a_t, a_t, a_t, a_t, *h0)
    return outs[:4], outs[4:]


def _cmul(ar, ai, br, bi):
    return ar * br - ai * bi, ar * bi + ai * br


def _s5_operators(lam_re, lam_im, log_dt, b_re, b_im, c_re, c_im, d):
    t_ = S5_T
    g_, p_, n_ = S5_GROUPS, S5_STATE, S5_GROUP
    gp, w_ = g_ * p_, g_ * n_
    lr, li = lam_re.astype(F32), lam_im.astype(F32)
    dt = jnp.exp(log_dt.astype(F32))[..., None]
    ks = jnp.arange(t_ + 1, dtype=F32)[:, None, None, None]
    mag = jnp.exp(ks * lr * dt)
    pw_re = mag * jnp.cos(ks * li * dt)
    pw_im = mag * jnp.sin(ks * li * dt)
    a_re, a_im = pw_re[1], pw_im[1]
    den = lr ** 2 + li ** 2
    f_re = ((a_re - 1.0) * lr + a_im * li) / den
    f_im = (a_im * lr - (a_re - 1.0) * li) / den
    br, bi = b_re.astype(F32), b_im.astype(F32)
    bb_re = f_re[..., None] * br - f_im[..., None] * bi
    bb_im = f_re[..., None] * bi + f_im[..., None] * br
    cr, ci = c_re.astype(F32), c_im.astype(F32)
    wb_re, wb_im = _cmul(pw_re[:t_, ..., None], pw_im[:t_, ..., None], bb_re[None], bb_im[None])
    kern = jnp.einsum('dgnp,kdgpm->dkgnm', cr, wb_re, precision=HI) \
        - jnp.einsum('dgnp,kdgpm->dkgnm', ci, wb_im, precision=HI)
    k_tile = jnp.tile(kern.transpose(0, 1, 2, 4, 3).reshape(2, t_, w_, n_), (1, 1, 1, g_))

    def rows_tiled(x):
        return jnp.tile(x.transpose(0, 3, 1, 2).reshape(2, n_, gp), (1, g_, 1))

    bt_re, bt_im = rows_tiled(bb_re), rows_tiled(bb_im)
    ct_re, ct_im = rows_tiled(cr.transpose(0, 1, 3, 2)), rows_tiled(ci.transpose(0, 1, 3, 2))
    pr = pw_re.transpose(1, 0, 2, 3).reshape(2, t_ + 1, gp)
    pi_ = pw_im.transpose(1, 0, 2, 3).reshape(2, t_ + 1, gp)
    full = lambda s: pl.BlockSpec(s, lambda t: (0,) * len(s))
    m_tot, q_all, n_t = pl.pallas_call(
        _s5_ops_kernel,
        grid=(t_,),
        in_specs=[full((2, t_ + 1, gp))] * 2 + [full((2, w_, gp))] * 4 + [full((2, t_, w_, w_)), full((1, w_))],
        out_specs=[pl.BlockSpec((w_, t_ * w_), lambda t: (t, 0)),
                   pl.BlockSpec((w_, 4 * gp), lambda t: (t, 0)),
                   pl.BlockSpec((w_, 4 * gp), lambda t: (t, 0))],
        out_shape=[jax.ShapeDtypeStruct((t_ * w_, t_ * w_), BF16),
                   jax.ShapeDtypeStruct((t_ * w_, 4 * gp), BF16),
                   jax.ShapeDtypeStruct((t_ * w_, 4 * gp), BF16)],
        compiler_params=_cparams(("parallel",)), name="s5_ops",
    )(pr, pi_, bt_re, bt_im, ct_re, ct_im, k_tile, d.astype(F32)[None])
    a_t = jnp.concatenate([pr[0, t_], pi_[0, t_], pr[1, t_], pi_[1, t_]])[None]
    return m_tot, q_all, n_t, a_t


def _s5_ops_kernel(pr_ref, pi_ref, btr_ref, bti_ref, ctr_ref, cti_ref, kt_ref, d_ref, m_ref, q_ref, nt_ref):
    t = pl.program_id(0)
    t_ = S5_T
    w_, gp = btr_ref.shape[1], btr_ref.shape[2]
    row_g = lax.broadcasted_iota(jnp.int32, (w_, gp), 0) // S5_GROUP
    col_g = lax.broadcasted_iota(jnp.int32, (w_, gp), 1) // S5_STATE
    same = row_g == col_g
    for dr in range(2):
        e_q = (t_ - 1 - t) if dr == 0 else t
        e_n = (t + 1) if dr == 0 else (t_ - t)
        b_re = jnp.where(same, btr_ref[dr], 0.0)
        b_im = jnp.where(same, bti_ref[dr], 0.0)
        q_re, q_im = _cmul(b_re, b_im, pr_ref[dr, pl.ds(e_q, 1), :], pi_ref[dr, pl.ds(e_q, 1), :])
        c_re = jnp.where(same, ctr_ref[dr], 0.0)
        c_im = jnp.where(same, cti_ref[dr], 0.0)
        n_re, n_im = _cmul(c_re, c_im, pr_ref[dr, pl.ds(e_n, 1), :], pi_ref[dr, pl.ds(e_n, 1), :])
        base = 2 * dr * gp
        q_ref[:, base:base + gp] = q_re.astype(BF16)
        q_ref[:, base + gp:base + 2 * gp] = q_im.astype(BF16)
        nt_ref[:, base:base + gp] = n_re.astype(BF16)
        nt_ref[:, base + gp:base + 2 * gp] = (-n_im).astype(BF16)
    r2 = lax.broadcasted_iota(jnp.int32, (w_, w_), 0)
    c2 = lax.broadcasted_iota(jnp.int32, (w_, w_), 1)
    same2 = (r2 // S5_GROUP) == (c2 // S5_GROUP)
    skip = jnp.where(r2 == c2, d_ref[...], 0.0)
    for i in range(t_):
        k_f = kt_ref[0, jnp.maximum(i - t, 0)] * jnp.where(i >= t, 1.0, 0.0)
        k_b = kt_ref[1, jnp.maximum(t - i, 0)] * jnp.where(t >= i, 1.0, 0.0)
        blk = jnp.where(same2, k_f + k_b, 0.0) + skip * jnp.where(t == i, 1.0, 0.0)
        m_ref[:, i * w_:(i + 1) * w_] = blk.astype(BF16)


def _s5_mix(u, ops, h0):
    m_tot, q_all, n_t, a_t = ops
    b, k, tw = u.shape
    rows = b * k
    uu = u.reshape(rows, tw)
    tm = min(rows, 512)
    g = _s5_increments(uu, q_all, tm=tm, tn=1024)
    h, e = _s5_scan(g.reshape(b, k, S5_STATE_W), a_t, h0)
    y = _s5_outputs(uu, [hc.reshape(rows, -1) for hc in h], m_tot, n_t, tm=tm, tn=512)
    return y.reshape(b, k, tw), e


def _hy_filter_kernel(f_ref, w1_ref, b1_ref, w2_ref, b2_ref, w3_ref, fr_ref, dl_ref, o_ref, *, half_len, tile):
    feats = f_ref[...]
    h = jnp.dot(feats, w1_ref[...], preferred_element_type=F32, precision=HI) + b1_ref[...]
    h = jnp.sin(fr_ref[0:1, :] * h)
    h = jnp.dot(h, w2_ref[...], preferred_element_type=F32, precision=HI) + b2_ref[...]
    h = jnp.sin(fr_ref[1:2, :] * h)
    o = jnp.dot(h.astype(BF16), w3_ref[...].astype(BF16), preferred_element_type=F32)
    t = feats[:, 0:1]
    o = o * jnp.exp(-t * dl_ref[...])
    nw = HY_ORDER * HY_WIDTH
    n = pl.program_id(0) * tile + lax.broadcasted_iota(jnp.int32, (tile, 1), 0)
    k = jnp.where(n < half_len, o[:, :nw], jnp.where(n == half_len, 0.0, o[:, nw:]))
    o_ref[...] = k.reshape(o_ref.shape)


def _hy_filters(n_tokens, freq, w1, b1, w2, b2, w3):
    n2 = 2 * n_tokens
    idx = np.arange(n2, dtype=np.float64)
    pos = np.where(idx <= n_tokens, idx, n2 - idx)[:, None]
    t = pos / max(n_tokens - 1, 1)
    bands = np.linspace(1e-4, HY_BANDS - 1, HY_BANDS)[None]
    ang = 2.0 * math.pi * bands * pos / n_tokens
    feats = np.concatenate([t, np.cos(ang), -np.sin(ang)], axis=-1)
    kpad = LANES - HY_EMB
    feats = jnp.asarray(np.pad(feats, ((0, 0), (0, kpad))), F32)
    w1p = jnp.pad(w1.astype(F32), ((0, kpad), (0, 0)))
    deltas = jnp.abs(jnp.linspace(HY_MIN_DECAY, HY_MAX_DECAY, HY_WIDTH, dtype=F32))
    dl = jnp.tile(deltas, 2 * HY_ORDER)[None]
    tile = min(n2, 1024)
    nw = 2 * HY_ORDER * HY_WIDTH
    full = lambda s: pl.BlockSpec(s, lambda i: (0,) * len(s))
    return pl.pallas_call(
        functools.partial(_hy_filter_kernel, half_len=n_tokens, tile=tile),
        grid=(n2 // tile,),
        in_specs=[pl.BlockSpec((tile, LANES), lambda i: (i, 0)),
                  full((LANES, HY_FFN)), full((1, HY_FFN)), full((HY_FFN, HY_FFN)), full((1, HY_FFN)),
                  full((HY_FFN, nw)), full((2, HY_FFN)), full((1, nw))],
        out_specs=pl.BlockSpec((tile // SUBLANES, SUBLANES, HY_ORDER * HY_WIDTH), lambda i: (i, 0, 0)),
        out_shape=jax.ShapeDtypeStruct((n2 // SUBLANES, SUBLANES, HY_ORDER * HY_WIDTH), F32),
        compiler_params=_cparams(("parallel",)), name="hy_filter",
    )(feats, w1p, b1.astype(F32)[None], w2.astype(F32), b2.astype(F32)[None], w3.astype(F32),
      freq.astype(F32), dl)


FFT_G = 4
FFT_UNROLL = 8


def _fft_tables(n):
    s = n // SUBLANES
    nst = int(round(math.log2(s)))
    runs = []
    half = s // 2
    while half >= FFT_G:
        runs.append(-2 * np.pi * np.arange(half) / (2 * half))
        half //= 2
    ang_s = np.concatenate(runs)
    tw_slab = np.stack([np.cos(ang_s), np.sin(ang_s)]).astype(np.float32)
    tw_slab = np.broadcast_to(tw_slab[..., None, None], (2, ang_s.size, SUBLANES, LANES)).copy()
    pos = np.arange(s)
    rev = np.zeros(s, np.int64)
    for bit in range(nst):
        rev |= ((pos >> bit) & 1) << (nst - 1 - bit)
    ang = -2 * np.pi * (rev[:, None] * np.arange(SUBLANES)[None, :]) / n
    tw_mid = np.stack([np.cos(ang), np.sin(ang)]).astype(np.float32)
    tw_mid = np.broadcast_to(tw_mid[..., None], (2, s, SUBLANES, LANES)).copy()
    t = np.arange(n)
    ang2 = -2 * np.pi * t / (2 * n)
    mod = np.stack([np.cos(ang2), np.sin(ang2)]).astype(np.float32).reshape(2, s, SUBLANES)
    mod = np.broadcast_to(mod[..., None], (2, s, SUBLANES, LANES)).copy()
    return jnp.asarray(tw_slab), jnp.asarray(tw_mid), jnp.asarray(mod)


def _sub_patterns():
    sub = lax.broadcasted_iota(jnp.int32, (SUBLANES, LANES), 0)

    def table(vals):
        out = jnp.full((SUBLANES, LANES), vals[0], F32)
        for k in range(1, SUBLANES):
            out = jnp.where(sub == k, np.float32(vals[k]), out)
        return out

    pats = {}
    for dist in (4, 2, 1):
        lo = (sub & dist) == 0
        sgn = jnp.where(lo, 1.0, -1.0).astype(F32)
        wr = [1.0] * SUBLANES
        wi = [0.0] * SUBLANES
        for k in range(SUBLANES):
            if k & dist:
                e = (k % dist) * (SUBLANES // (2 * dist))
                wr[k] = math.cos(-2 * math.pi * e / SUBLANES)
                wi[k] = math.sin(-2 * math.pi * e / SUBLANES)
        pats[dist] = (lo, sgn, table(wr), table(wi))
    pats["quarter"] = (sub & 3) == 3
    return pats


def _dft8_fwd(vr, vi, pats):
    for dist in (4, 2, 1):
        lo, sgn, wr, wi = pats[dist]
        up_r = pltpu.roll(vr, SUBLANES - dist, 0)
        up_i = pltpu.roll(vi, SUBLANES - dist, 0)
        if dist == 4:
            pr, pi_ = up_r, up_i
        else:
            pr = jnp.where(lo, up_r, pltpu.roll(vr, dist, 0))
            pi_ = jnp.where(lo, up_i, pltpu.roll(vi, dist, 0))
        tr = pr + sgn * vr
        ti = pi_ + sgn * vi
        if dist == 1:
            vr, vi = tr, ti
        elif dist == 2:
            qt = pats["quarter"]
            vr, vi = jnp.where(qt, ti, tr), jnp.where(qt, -tr, ti)
        else:
            vr, vi = _cmul(tr, ti, wr, wi)
    return vr, vi


def _dft8_inv(vr, vi, pats):
    for dist in (1, 2, 4):
        lo, sgn, wr, wi = pats[dist]
        if dist == 2:
            qt = pats["quarter"]
            vr, vi = jnp.where(qt, -vi, vr), jnp.where(qt, vr, vi)
        elif dist == 4:
            vr, vi = _cmul(vr, vi, wr, -wi)
        up_r = pltpu.roll(vr, SUBLANES - dist, 0)
        up_i = pltpu.roll(vi, SUBLANES - dist, 0)
        if dist == 4:
            pr, pi_ = up_r, up_i
        else:
            pr = jnp.where(lo, up_r, pltpu.roll(vr, dist, 0))
            pi_ = jnp.where(lo, up_i, pltpu.roll(vi, dist, 0))
        vr = pr + sgn * vr
        vi = pi_ + sgn * vi
    return vr, vi


def _slab_stage(re, im, tw, half, slabs, inverse):
    per_block = half // FFT_G
    nblk = slabs // (2 * half)
    unroll = min(FFT_UNROLL, nblk * per_block)
    tw_off = slabs - 2 * half
    if nblk >= unroll:
        blocks_per_it = unroll // per_block
        trips = nblk // blocks_per_it
        offsets = [(b * 2 * half + jc * FFT_G, jc * FFT_G) for b in range(blocks_per_it) for jc in range(per_block)]
        data_step, tw_step = blocks_per_it * 2 * half, 0
    else:
        chunks_per_it = unroll // nblk
        trips = per_block // chunks_per_it
        offsets = [(b * 2 * half + k * FFT_G, k * FFT_G) for b in range(nblk) for k in range(chunks_per_it)]
        data_step = tw_step = chunks_per_it * FFT_G

    def body(c, carry):
        d0 = pl.multiple_of(c * data_step, FFT_G)
        t0 = pl.multiple_of(tw_off + c * tw_step, FFT_G)
        twiddles = {}
        for d_off, t_off in offsets:
            if t_off not in twiddles:
                tws = pl.ds(t0 + t_off, FFT_G)
                twiddles[t_off] = (tw[0, tws], tw[1, tws])
            wr, wi = twiddles[t_off]
            lo = pl.ds(d0 + d_off, FFT_G)
            hi = pl.ds(d0 + d_off + half, FFT_G)
            ar, ai, br, bi = re[lo], im[lo], re[hi], im[hi]
            if inverse:
                br, bi = br * wr + bi * wi, bi * wr - br * wi
                re[lo] = ar + br
                im[lo] = ai + bi
                re[hi] = ar - br
                im[hi] = ai - bi
            else:
                re[lo] = ar + br
                im[lo] = ai + bi
                dr, di = ar - br, ai - bi
                re[hi] = dr * wr - di * wi
                im[hi] = dr * wi + di * wr
        return carry

    lax.fori_loop(0, trips, body, 0)


def _fft_forward_big(re, im, tw, slabs):
    half = slabs // 2
    while half >= 4:
        _slab_stage(re, im, tw, half, slabs, inverse=False)
        half //= 2


def _fft_inverse_big(re, im, tw, slabs):
    half = 4
    while half <= slabs // 2:
        _slab_stage(re, im, tw, half, slabs, inverse=True)
        half *= 2


def _radix4_fwd(x):
    (x0r, x0i), (x1r, x1i), (x2r, x2i), (x3r, x3i) = x
    y0r, y0i = x0r + x2r, x0i + x2i
    y2r, y2i = x0r - x2r, x0i - x2i
    y1r, y1i = x1r + x3r, x1i + x3i
    dr, di = x1r - x3r, x1i - x3i
    y3r, y3i = di, -dr
    return [(y0r + y1r, y0i + y1i), (y0r - y1r, y0i - y1i), (y2r + y3r, y2i + y3i), (y2r - y3r, y2i - y3i)]


def _radix4_inv(z):
    (z0r, z0i), (z1r, z1i), (z2r, z2i), (z3r, z3i) = z
    y0r, y0i = z0r + z1r, z0i + z1i
    y1r, y1i = z0r - z1r, z0i - z1i
    y2r, y2i = z2r + z3r, z2i + z3i
    y3r, y3i = z2r - z3r, z2i - z3i
    qr, qi = -y3i, y3r
    return [(y0r + y2r, y0i + y2i), (y1r + qr, y1i + qi), (y0r - y2r, y0i - y2i), (y1r - qr, y1i - qi)]


def _fft_middle(re, im, twm_ref, slabs, pats, spec=None, out=None):
    def body(q, carry):
        p0 = q * FFT_G
        x = [(re[p0 + g], im[p0 + g]) for g in range(FFT_G)]
        z = _radix4_fwd(x)
        res = []
        for g in range(FFT_G):
            twr = twm_ref[0, p0 + g]
            twi = twm_ref[1, p0 + g]
            vr, vi = _cmul(z[g][0], z[g][1], twr, twi)
            vr, vi = _dft8_fwd(vr, vi, pats)
            if spec is None:
                out[0][0, 0, p0 + g] = vr * out[2]
                out[1][0, 0, p0 + g] = vi * out[2]
            else:
                vr, vi = _cmul(vr, vi, spec[0][0, 0, p0 + g], spec[1][0, 0, p0 + g])
                vr, vi = _dft8_inv(vr, vi, pats)
                res.append(_cmul(vr, vi, twr, -twi))
        if spec is not None:
            x = _radix4_inv(res)
            for g in range(FFT_G):
                re[p0 + g] = x[g][0]
                im[p0 + g] = x[g][1]
        return carry

    lax.fori_loop(0, slabs // FFT_G, body, 0)


def _hy_spec_kernel(tw_ref, k_ref, twm_ref, mod_ref, sr_ref, si_ref, re, im, *, slabs):
    h = pl.program_id(2)
    pats = _sub_patterns()
    a = k_ref[0:slabs]
    b = k_ref[slabs:2 * slabs]

    @pl.when(h == 0)
    def _():
        re[...] = a + b
        im[...] = jnp.zeros_like(a)

    @pl.when(h == 1)
    def _():
        dlt = a - b
        re[...] = dlt * mod_ref[0]
        im[...] = dlt * mod_ref[1]

    _fft_forward_big(re, im, tw_ref, slabs)
    scale = np.float32(1.0 / (2 * SUBLANES * slabs))
    _fft_middle(re, im, twm_ref, slabs, pats, out=(sr_ref, si_ref, scale))


def _hy_spectra(k, n_tokens):
    slabs = n_tokens // SUBLANES
    tw, twm, mod = _fft_tables(n_tokens)
    k4 = k
    nt = HY_WIDTH // LANES
    shape = jax.ShapeDtypeStruct((HY_ORDER, 2, slabs, SUBLANES, HY_WIDTH), F32)
    spec_out = pl.BlockSpec((1, 1, slabs, SUBLANES, LANES), lambda o, j, h: (o, h, 0, 0, j))
    tab = pl.BlockSpec((2, slabs, SUBLANES, LANES), lambda o, j, h: (0, 0, 0, 0), pipeline_mode=pl.Buffered(1))
    return pl.pallas_call(
        functools.partial(_hy_spec_kernel, slabs=slabs),
        grid=(HY_ORDER, nt, 2),
        in_specs=[pl.BlockSpec(tw.shape, lambda o, j, h: (0, 0, 0, 0), pipeline_mode=pl.Buffered(1)),
                  pl.BlockSpec((2 * slabs, SUBLANES, LANES), lambda o, j, h: (0, 0, o * nt + j)),
                  tab, tab],
        out_specs=[spec_out, spec_out],
        out_shape=[shape, shape],
        scratch_shapes=[pltpu.VMEM((slabs, SUBLANES, LANES), F32)] * 2,
        compiler_params=_cparams(("parallel", "parallel", "arbitrary")), name="hy_spec",
    )(tw, k4, twm, mod)


def _hy_conv_kernel(tw_ref, u_ref, g_ref, sr_ref, si_ref, twm_ref, mod_ref, b_ref, o_ref, re, im, *, slabs):
    h = pl.program_id(2)
    pats = _sub_patterns()
    slab_shape = (slabs, SUBLANES, LANES)
    n = slabs * SUBLANES

    @pl.when(h == 0)
    def _():
        re[...] = u_ref[0].reshape(slab_shape)
        im[...] = u_ref[1].reshape(slab_shape)

    @pl.when(h == 1)
    def _():
        ur, ui = u_ref[0].reshape(slab_shape), u_ref[1].reshape(slab_shape)
        mr, mi = mod_ref[0], mod_ref[1]
        re[...] = ur * mr - ui * mi
        im[...] = ur * mi + ui * mr

    _fft_forward_big(re, im, tw_ref, slabs)
    _fft_middle(re, im, twm_ref, slabs, pats, spec=(sr_ref, si_ref))
    _fft_inverse_big(re, im, tw_ref, slabs)

    @pl.when(h == 0)
    def _():
        o_ref[0] = re[...].reshape(n, LANES)
        o_ref[1] = im[...].reshape(n, LANES)

    @pl.when(h == 1)
    def _():
        yr, yi = re[...], im[...]
        mr, mi = mod_ref[0], mod_ref[1]
        y0 = o_ref[0] + (yr * mr + yi * mi).reshape(n, LANES)
        y1 = o_ref[1] + (yi * mr - yr * mi).reshape(n, LANES)
        o_ref[0] = g_ref[0] * (y0 + b_ref[...] * u_ref[0])
        o_ref[1] = g_ref[1] * (y1 + b_ref[...] * u_ref[1])


def _hy_order(u, u_blk, gate, gate_blk, spec_re, spec_im, order, bias):
    b, n, _ = u.shape
    w = HY_WIDTH
    slabs = n // SUBLANES
    tw, twm, mod = _fft_tables(n)
    nt = w // LANES
    tab = pl.BlockSpec((2, slabs, SUBLANES, LANES), lambda j, p, h: (0, 0, 0, 0), pipeline_mode=pl.Buffered(1))
    spec_in = pl.BlockSpec((1, 1, slabs, SUBLANES, LANES), lambda j, p, h: (order, h, 0, 0, j))

    def io(blk):
        return pl.BlockSpec((2, n, LANES), lambda j, p, h: (p, 0, blk * nt + j))

    return pl.pallas_call(
        functools.partial(_hy_conv_kernel, slabs=slabs),
        grid=(nt, b // 2, 2),
        in_specs=[pl.BlockSpec(tw.shape, lambda j, p, h: (0, 0, 0, 0), pipeline_mode=pl.Buffered(1)),
                  io(u_blk), io(gate_blk), spec_in, spec_in, tab, tab,
                  pl.BlockSpec((1, LANES), lambda j, p, h: (0, j))],
        out_specs=io(0),
        out_shape=jax.ShapeDtypeStruct((b, n, w), F32),
        scratch_shapes=[pltpu.VMEM((slabs, SUBLANES, LANES), F32)] * 2,
        compiler_params=_cparams(("parallel", "parallel", "arbitrary")), name="hy_conv",
    )(tw, u, gate, spec_re, spec_im, twm, mod, bias.astype(F32)[None])


def _shift_rows(x, n):
    row = lax.broadcasted_iota(jnp.int32, x.shape, 0)
    prev = jnp.where(row == 0, 0.0, pltpu.roll(x, 1, 0))
    nxt = jnp.where(row == n - 1, 0.0, pltpu.roll(x, n - 1, 0))
    return prev, nxt


def _hy_short_kernel(z_ref, w_ref, b_ref, o_ref, *, n):
    x = z_ref[0]
    prev, nxt = _shift_rows(x, n)
    o_ref[0] = prev * w_ref[0:1, :] + x * w_ref[1:2, :] + nxt * w_ref[2:3, :] + b_ref[...]


def _hy_short(z_mix, conv_w, conv_b):
    b, n, _ = z_mix.shape
    c3 = 3 * HY_WIDTH
    w = conv_w.reshape(3, c3).astype(F32)
    tc = 256
    col0 = BRANCH_W // tc
    return pl.pallas_call(
        functools.partial(_hy_short_kernel, n=n),
        grid=(b, c3 // tc),
        in_specs=[pl.BlockSpec((1, n, tc), lambda bi, j: (bi, 0, col0 + j)),
                  pl.BlockSpec((3, tc), lambda bi, j: (0, j)),
                  pl.BlockSpec((1, tc), lambda bi, j: (0, j))],
        out_specs=pl.BlockSpec((1, n, tc), lambda bi, j: (bi, 0, j)),
        out_shape=jax.ShapeDtypeStruct((b, n, c3), F32),
        compiler_params=_cparams(("parallel", "parallel")), name="hy_short",
    )(z_mix, w, conv_b.astype(F32)[None])


def _hyena(z_mix, conv_w, conv_b, spec_re, spec_im, bias):
    zc = _hy_short(z_mix, conv_w, conv_b)
    v1 = _hy_order(zc, 0, zc, 1, spec_re, spec_im, 0, bias[0])
    return _hy_order(v1, 0, zc, 2, spec_re, spec_im, 1, bias[1])


NA_QROWS = 4
NA_KROWS = NA_QROWS + NA_WIN_H
NA_KPART = 256


def _softmax_parts(parts):
    m = None
    for s in parts:
        mm = jnp.max(s, axis=-1, keepdims=True)
        m = mm if m is None else jnp.maximum(m, mm)
    return m


def _na_kernel(*refs, kparts):
    q_ref, k_refs, v_refs = refs[0], refs[1:1 + kparts], refs[1 + kparts:1 + 2 * kparts]
    kc_ref, vc_ref, bias_ref, o_ref = refs[1 + 2 * kparts:]
    q = q_ref[0] * np.float32(NA_HEAD_DIM ** -0.5)
    k = jnp.concatenate([r[0] for r in k_refs], axis=0).astype(BF16)
    v = jnp.concatenate([r[0] for r in v_refs], axis=0).astype(BF16)
    kc = kc_ref[0].astype(BF16)
    vc = vc_ref[0].astype(BF16)
    nt = (((1,), (1,)), ((), ()))
    outs = []
    for h in range(NA_HEADS):
        sl = slice(h * NA_HEAD_DIM, (h + 1) * NA_HEAD_DIM)
        qh = q[:, sl].astype(BF16)
        s_lat = lax.dot_general(qh, k[:, sl], nt, preferred_element_type=F32) + bias_ref[0, h]
        s_ctx = lax.dot_general(qh, kc[:, sl], nt, preferred_element_type=F32)
        m = _softmax_parts([s_lat, s_ctx])
        p_lat = jnp.exp(s_lat - m)
        p_ctx = jnp.exp(s_ctx - m)
        den = jnp.sum(p_lat, axis=-1, keepdims=True) + jnp.sum(p_ctx, axis=-1, keepdims=True)
        o = jnp.dot(p_lat.astype(BF16), v[:, sl], preferred_element_type=F32) \
            + jnp.dot(p_ctx.astype(BF16), vc[:, sl], preferred_element_type=F32)
        outs.append(o / den)
    o_ref[0] = jnp.concatenate(outs, axis=-1)


def _na_bias(rpb, rows):
    kh = min(NA_WIN_H, rows)
    col = np.arange(GRID_W)
    col_start = np.clip(col - NA_WIN_W // 2, 0, GRID_W - NA_WIN_W)
    col_ok = (col[None] >= col_start[:, None]) & (col[None] < col_start[:, None] + NA_WIN_W)
    off_c = np.clip(col[None] - col[:, None], -(NA_WIN_W - 1), NA_WIN_W - 1) + (NA_WIN_W - 1)
    nblk = rows // NA_QROWS
    n_r, n_c = 2 * NA_WIN_H - 1, 2 * NA_WIN_W - 1
    table = jnp.pad(rpb.astype(F32), ((0, 0), (0, 1), (0, 1)), constant_values=NEG_INF)
    sel_c = np.eye(n_c + 1, dtype=np.float32)[np.where(col_ok, off_c, n_c)]
    blocks = jnp.einsum('hab,qcb->hqac', table, sel_c, precision=HI).reshape(NA_HEADS, GRID_W, (n_r + 1) * GRID_W)
    blocks_odd = jnp.roll(blocks, -GRID_W, axis=-1)
    row_off = []
    for j in (0, 1, nblk - 1):
        qr = j * NA_QROWS + np.arange(NA_QROWS)
        ws = int(np.clip(j * NA_QROWS - NA_WIN_H // 2, 0, rows - NA_KROWS))
        kr = ws + np.arange(NA_KROWS)
        start = np.clip(qr - kh // 2, 0, rows - kh)
        row_ok = (kr[None] >= start[:, None]) & (kr[None] < start[:, None] + kh)
        off_r = kr[None] - qr[:, None] + (NA_WIN_H - 1)
        row_off.append(np.where(row_ok, off_r, n_r))
    row_off = np.stack(row_off)
    wide = (n_r + 1) * GRID_W
    return pl.pallas_call(
        functools.partial(_na_bias_kernel, row_off=row_off),
        grid=(NA_HEADS,),
        in_specs=[pl.BlockSpec((1, GRID_W, wide), lambda h: (h, 0, 0))] * 2,
        out_specs=pl.BlockSpec((3, 1, NA_QROWS * GRID_W, NA_KROWS * GRID_W), lambda h: (0, h, 0, 0)),
        out_shape=jax.ShapeDtypeStruct((3, NA_HEADS, NA_QROWS * GRID_W, NA_KROWS * GRID_W), F32),
        compiler_params=_cparams(("parallel",)), name="na_bias",
    )(blocks, blocks_odd)


def _na_bias_kernel(even_ref, odd_ref, o_ref, *, row_off):
    n_cls, n_q, n_k = row_off.shape
    masked = 2 * NA_WIN_H - 1
    pair = 2 * GRID_W
    lane = lax.broadcasted_iota(jnp.int32, (GRID_W, pair), 1)

    def left(a):
        src, start = (even_ref, a) if a % 2 == 0 else (odd_ref, a - 1)
        return src[0, :, start * GRID_W:start * GRID_W + pair]

    def right(a):
        assert a % 2 == 1
        return even_ref[0, :, (a - 1) * GRID_W:(a - 1) * GRID_W + pair]

    for t in range(n_cls):
        for r in range(n_q):
            for kp in range(n_k // 2):
                a0, a1 = int(row_off[t, r, 2 * kp]), int(row_off[t, r, 2 * kp + 1])
                if a0 != masked and a1 == a0 + 1:
                    blk = left(a0)
                elif a0 == masked and a1 == masked:
                    blk = jnp.full((GRID_W, pair), NEG_INF, F32)
                elif a1 == masked:
                    blk = jnp.where(lane < GRID_W, left(a0), NEG_INF)
                else:
                    blk = jnp.where(lane >= GRID_W, right(a1), NEG_INF)
                o_ref[t, 0, r * GRID_W:(r + 1) * GRID_W, kp * pair:(kp + 1) * pair] = blk


def _na_attention(z_mix, zc_mix, rpb):
    b, n, _ = z_mix.shape
    n_ctx = zc_mix.shape[1]
    rows = n // GRID_W
    nblk = rows // NA_QROWS
    tq = NA_QROWS * GRID_W
    bias = _na_bias(rpb, rows)
    qb, kb, vb = 1024 // BRANCH_W, 1024 // BRANCH_W + 1, 1024 // BRANCH_W + 2
    kparts = NA_KROWS * GRID_W // NA_KPART
    max_k0 = (rows - NA_KROWS) * GRID_W // NA_KPART

    def k0_of(j):
        per_step = NA_QROWS * GRID_W // NA_KPART
        lead = (NA_WIN_H // 2) * GRID_W // NA_KPART
        return jnp.clip(j * per_step - lead, 0, max_k0)

    def kspec(part, blk):
        return pl.BlockSpec((1, NA_KPART, BRANCH_W), lambda bi, j: (bi, k0_of(j) + part, blk))

    def cls(j):
        return jnp.where(j == 0, 0, jnp.where(j == nblk - 1, 2, 1))

    return pl.pallas_call(
        functools.partial(_na_kernel, kparts=kparts),
        grid=(b, nblk),
        in_specs=[pl.BlockSpec((1, tq, BRANCH_W), lambda bi, j: (bi, j, qb))]
        + [kspec(p, kb) for p in range(kparts)] + [kspec(p, vb) for p in range(kparts)]
        + [pl.BlockSpec((1, n_ctx, BRANCH_W), lambda bi, j: (bi, 0, kb)),
           pl.BlockSpec((1, n_ctx, BRANCH_W), lambda bi, j: (bi, 0, vb)),
           pl.BlockSpec((1, NA_HEADS, tq, NA_KROWS * GRID_W), lambda bi, j: (cls(j), 0, 0, 0))],
        out_specs=pl.BlockSpec((1, tq, BRANCH_W), lambda bi, j: (bi, j, 0)),
        out_shape=jax.ShapeDtypeStruct((b, n, BRANCH_W), F32),
        compiler_params=_cparams(("parallel", "arbitrary")), name="na_attn",
    )(z_mix, *([z_mix] * (2 * kparts)), zc_mix, zc_mix, bias)


def _rope_tables(n_tokens, head_dim, heads):
    t = np.arange(n_tokens)
    row = (t // GRID_W).astype(np.float64)
    col = (t % GRID_W).astype(np.float64)
    half = head_dim // 2
    inv = ROPE_BASE ** (-(np.arange(0, half, 2, dtype=np.float64) / half))
    ang = np.concatenate([row[:, None] * inv, col[:, None] * inv], axis=-1)
    cos, sin = np.cos(ang), np.sin(ang)
    cos_t = np.tile(np.concatenate([cos, cos], axis=-1), (1, heads))
    sin_t = np.tile(np.concatenate([-sin, sin], axis=-1), (1, heads))
    return jnp.asarray(cos_t, F32), jnp.asarray(sin_t, F32)


def _rope(x, cos_t, sin_t):
    w = x.shape[-1]
    half = SW_HEAD_DIM // 2
    lane = lax.broadcasted_iota(jnp.int32, x.shape, 1)
    first = (lane % SW_HEAD_DIM) < half
    partner = jnp.where(first, pltpu.roll(x, w - half, 1), pltpu.roll(x, half, 1))
    return x * cos_t + partner * sin_t


SW_QBLK = 2 * SW_BLOCK
SW_KPARTS = SW_QBLK // SW_BLOCK + 2


def _sw_rope_kernel(q_ref, kv_ref, cos_ref, sin_ref, qo_ref, kvo_ref):
    cos_t, sin_t = cos_ref[...], sin_ref[...]
    q = _rope(q_ref[0], cos_t, sin_t) * np.float32(SW_HEAD_DIM ** -0.5)
    qo_ref[0] = q.astype(BF16)
    kv = kv_ref[0]
    lane = lax.broadcasted_iota(jnp.int32, kv.shape, 1)
    kvo_ref[0] = jnp.where(lane < SW_KV_HEADS * SW_HEAD_DIM, _rope(kv, cos_t, sin_t), kv).astype(BF16)


def _sw_rope(z_mix):
    b, n, _ = z_mix.shape
    qw = SW_HEADS * SW_HEAD_DIM
    cos_t, sin_t = _rope_tables(n, SW_HEAD_DIM, SW_HEADS)
    tm = min(n, 1024)
    tok = lambda blk: pl.BlockSpec((1, tm, qw), lambda bi, i: (bi, i, blk))
    tab = pl.BlockSpec((tm, qw), lambda bi, i: (i, 0))
    shape = jax.ShapeDtypeStruct((b, n, qw), BF16)
    return pl.pallas_call(
        _sw_rope_kernel,
        grid=(b, n // tm),
        in_specs=[tok(1792 // qw), tok(1792 // qw + 1), tab, tab],
        out_specs=[tok(0), tok(0)],
        out_shape=[shape, shape],
        compiler_params=_cparams(("parallel", "parallel")), name="sw_rope",
    )(z_mix, z_mix, cos_t, sin_t)


def _sw_kernel(q_ref, kv0, kv1, kv2, kv3, kc_ref, vc_ref, sink_ref, o_ref, *, nblk):
    i = pl.program_id(1)
    kvw = SW_KV_HEADS * SW_HEAD_DIM
    q = q_ref[0]
    kv = jnp.concatenate([kv0[0], kv1[0], kv2[0], kv3[0]], axis=0)
    kc = kc_ref[0].astype(BF16)
    vc = vc_ref[0].astype(BF16)
    g_ = SW_HEADS // SW_KV_HEADS
    rows = g_ * SW_QBLK
    span = SW_KPARTS * SW_BLOCK
    r = lax.broadcasted_iota(jnp.int32, (rows, span), 0) % SW_QBLK
    c = lax.broadcasted_iota(jnp.int32, (rows, span), 1)
    diff = c - SW_BLOCK - r
    blk = i * (SW_QBLK // SW_BLOCK) - 1 + c // SW_BLOCK
    ok = (jnp.abs(diff) <= SW_WINDOW) & (blk >= 0) & (blk < nblk)
    nt = (((1,), (1,)), ((), ()))
    outs = []
    for kvh in range(SW_KV_HEADS):
        ksl = slice(kvh * SW_HEAD_DIM, (kvh + 1) * SW_HEAD_DIM)
        vsl = slice(kvw + kvh * SW_HEAD_DIM, kvw + (kvh + 1) * SW_HEAD_DIM)
        qg = jnp.concatenate([q[:, (kvh * g_ + g) * SW_HEAD_DIM:(kvh * g_ + g + 1) * SW_HEAD_DIM] for g in range(g_)],
                             axis=0)
        s_lat = jnp.where(ok, lax.dot_general(qg, kv[:, ksl], nt, preferred_element_type=F32), NEG_INF)
        s_ctx = lax.dot_general(qg, kc[:, ksl], nt, preferred_element_type=F32)
        rr = lax.broadcasted_iota(jnp.int32, (rows, 1), 0)
        s_sink = jnp.zeros((rows, 1), F32)
        for g in range(g_):
            s_sink = jnp.where(rr // SW_QBLK == g, sink_ref[kvh * g_ + g], s_sink)
        m = jnp.maximum(_softmax_parts([s_lat, s_ctx]), s_sink)
        p_lat = jnp.exp(s_lat - m)
        p_ctx = jnp.exp(s_ctx - m)
        den = jnp.sum(p_lat, axis=-1, keepdims=True) + jnp.sum(p_ctx, axis=-1, keepdims=True) + jnp.exp(s_sink - m)
        o = jnp.dot(p_lat.astype(BF16), kv[:, vsl], preferred_element_type=F32) \
            + jnp.dot(p_ctx.astype(BF16), vc[:, ksl], preferred_element_type=F32)
        o = o / den
        outs += [o[g * SW_QBLK:(g + 1) * SW_QBLK] for g in range(g_)]
    o_ref[0] = jnp.concatenate(outs, axis=-1)


def _sw_attention(z_mix, zc_mix, sink):
    b, n, _ = z_mix.shape
    n_ctx = zc_mix.shape[1]
    nblk = n // SW_BLOCK
    qw = SW_HEADS * SW_HEAD_DIM
    kvw = SW_KV_HEADS * SW_HEAD_DIM
    k_blk = (1792 + qw) // kvw
    q_r, kv_r = _sw_rope(z_mix)
    per_q = SW_QBLK // SW_BLOCK

    def kpart(part):
        return pl.BlockSpec((1, SW_BLOCK, qw),
                            lambda bi, i: (bi, jnp.clip(i * per_q - 1 + part, 0, nblk - 1), 0))

    return pl.pallas_call(
        functools.partial(_sw_kernel, nblk=nblk),
        grid=(b, n // SW_QBLK),
        in_specs=[pl.BlockSpec((1, SW_QBLK, qw), lambda bi, i: (bi, i, 0))]
        + [kpart(p) for p in range(SW_KPARTS)]
        + [pl.BlockSpec((1, n_ctx, kvw), lambda bi, i: (bi, 0, k_blk)),
           pl.BlockSpec((1, n_ctx, kvw), lambda bi, i: (bi, 0, k_blk + 1)),
           pl.BlockSpec(memory_space=pltpu.SMEM)],
        out_specs=pl.BlockSpec((1, SW_QBLK, qw), lambda bi, i: (bi, i, 0)),
        out_shape=jax.ShapeDtypeStruct((b, n, qw), F32),
        compiler_params=_cparams(("parallel", "arbitrary")), name="sw_attn",
    )(q_r, *([kv_r] * SW_KPARTS), zc_mix, zc_mix, sink.astype(F32))


def _ctx_attn_kernel(q_ref, k_ref, v_ref, sink_ref, o_ref, *, heads, kv_heads, dh, use_sink):
    q = q_ref[0] * np.float32(dh ** -0.5)
    k = k_ref[0].astype(BF16)
    v = v_ref[0].astype(BF16)
    g_ = heads // kv_heads
    nt = (((1,), (1,)), ((), ()))
    outs = []
    for h in range(heads):
        kv = h // g_
        s = lax.dot_general(q[:, h * dh:(h + 1) * dh].astype(BF16), k[:, kv * dh:(kv + 1) * dh], nt,
                            preferred_element_type=F32)
        m = jnp.max(s, axis=-1, keepdims=True)
        if use_sink:
            m = jnp.maximum(m, sink_ref[h])
        p = jnp.exp(s - m)
        den = jnp.sum(p, axis=-1, keepdims=True)
        if use_sink:
            den = den + jnp.exp(sink_ref[h] - m)
        outs.append(jnp.dot(p.astype(BF16), v[:, kv * dh:(kv + 1) * dh], preferred_element_type=F32) / den)
    o_ref[0] = jnp.concatenate(outs, axis=-1)


def _ctx_attention(zc_mix, col0, heads, kv_heads, dh, sink):
    b, n, _ = zc_mix.shape
    qw, kvw = heads * dh, kv_heads * dh
    use_sink = sink is not None
    sink_arr = sink.astype(F32) if use_sink else jnp.zeros((heads,), F32)
    return pl.pallas_call(
        functools.partial(_ctx_attn_kernel, heads=heads, kv_heads=kv_heads, dh=dh, use_sink=use_sink),
        grid=(b,),
        in_specs=[pl.BlockSpec((1, n, qw), lambda bi: (bi, 0, col0 // qw)),
                  pl.BlockSpec((1, n, kvw), lambda bi: (bi, 0, (col0 + qw) // kvw)),
                  pl.BlockSpec((1, n, kvw), lambda bi: (bi, 0, (col0 + qw) // kvw + 1)),
                  pl.BlockSpec(memory_space=pltpu.SMEM)],
        out_specs=pl.BlockSpec((1, n, qw), lambda bi: (bi, 0, 0)),
        out_shape=jax.ShapeDtypeStruct((b, n, qw), F32),
        compiler_params=_cparams(("parallel",)), name="ctx_attn",
    )(zc_mix, zc_mix, zc_mix, sink_arr)


def _merge_kernel(x_ref, s5_ref, hy_ref, na_ref, sw_ref, gt_ref, ga_ref, wglu_ref, wb_ref, wo_ref,
                  lg_ref, lb_ref, o_ref, tok_scr):
    rows = tok_scr.shape[1] // S5_T
    for c in range(BRANCH_W // LANES):
        for t in range(S5_T):
            col = t * BRANCH_W + c * LANES
            tok_scr[c, pl.ds(t, rows, stride=S5_T), :] = s5_ref[0, :, col:col + LANES]
    g = jax.nn.gelu(jnp.concatenate([tok_scr[c] for c in range(BRANCH_W // LANES)], axis=-1))
    s5 = g * jax.nn.sigmoid(jnp.dot(g.astype(BF16), wglu_ref[...], preferred_element_type=F32))
    branches = (s5, hy_ref[0], na_ref[0], sw_ref[0])
    acc = None
    for n in range(N_BRANCH):
        proj = jnp.dot(branches[n].astype(BF16), wb_ref[n], preferred_element_type=F32)
        t = gt_ref[0, :, n * D_MODEL:(n + 1) * D_MODEL].astype(F32) * proj
        acc = t if acc is None else acc + t
    mix = jnp.dot(acc.astype(BF16), wo_ref[...], preferred_element_type=F32)
    y = np.float32(DEEPNORM_ALPHA) * x_ref[0] + ga_ref[0] * mix
    o_ref[0] = _layernorm(y) * lg_ref[...] + lb_ref[...]


def _merge(x, s5y, hy, na, sw, gates, g_a, w_glu, w_branch, w_out, ln_g, ln_b, *, tm):
    b, l, d = x.shape
    br = pl.BlockSpec((1, tm, BRANCH_W), lambda bi, i: (bi, i, 0))
    full = lambda s: pl.BlockSpec(s, lambda bi, i: (0,) * len(s))
    return pl.pallas_call(
        _merge_kernel,
        grid=(b, l // tm),
        in_specs=[pl.BlockSpec((1, tm, d), lambda bi, i: (bi, i, 0)),
                  pl.BlockSpec((1, tm // S5_T, S5_T * BRANCH_W), lambda bi, i: (bi, i, 0)), br, br, br,
                  pl.BlockSpec((1, tm, GATE_W), lambda bi, i: (bi, i, 0)),
                  pl.BlockSpec((1, 1, d), lambda bi, i: (bi, 0, 0)),
                  full((BRANCH_W, BRANCH_W)), full((N_BRANCH, BRANCH_W, d)), full((d, d)),
                  full((1, d)), full((1, d))],
        out_specs=pl.BlockSpec((1, tm, d), lambda bi, i: (bi, i, 0)),
        out_shape=jax.ShapeDtypeStruct((b, l, d), F32),
        scratch_shapes=[pltpu.VMEM((BRANCH_W // LANES, tm, LANES), F32)],
        compiler_params=_cparams(("parallel", "parallel")), name="merge",
    )(x, s5y, hy, na, sw, gates, g_a, w_glu, w_branch, w_out, ln_g, ln_b)


def _mlp_kernel(x_ref, sh_ref, sc_ref, gm_ref, w1_ref, w2_ref, lg_ref, lb_ref, o_ref, h_scr, acc_scr):
    j = pl.program_id(2)

    @pl.when(j == 0)
    def _():
        h = _layernorm(x_ref[0]) * (1.0 + sc_ref[0]) + sh_ref[0]
        h_scr[...] = h.astype(BF16)
        acc_scr[...] = jnp.zeros_like(acc_scr)

    a = jnp.dot(h_scr[...], w1_ref[...], preferred_element_type=F32)
    a = jnp.square(jnp.maximum(a, 0.0))
    acc_scr[...] += jnp.dot(a.astype(BF16), w2_ref[...], preferred_element_type=F32)

    @pl.when(j == pl.num_programs(2) - 1)
    def _():
        y = np.float32(DEEPNORM_ALPHA) * x_ref[0] + gm_ref[0] * acc_scr[...]
        o_ref[0] = _layernorm(y) * lg_ref[...] + lb_ref[...]


def _mlp(x, sh, sc, g_m, w1, w2, ln_g, ln_b, *, tm, th):
    b, l, d = x.shape
    hdim = w1.shape[1]
    mod = pl.BlockSpec((1, 1, d), lambda bi, i, j: (bi, 0, 0))
    vec = pl.BlockSpec((1, d), lambda bi, i, j: (0, 0))
    return pl.pallas_call(
        _mlp_kernel,
        grid=(b, l // tm, hdim // th),
        in_specs=[pl.BlockSpec((1, tm, d), lambda bi, i, j: (bi, i, 0)), mod, mod, mod,
                  pl.BlockSpec((d, th), lambda bi, i, j: (0, j)),
                  pl.BlockSpec((th, d), lambda bi, i, j: (j, 0)), vec, vec],
        out_specs=pl.BlockSpec((1, tm, d), lambda bi, i, j: (bi, i, 0)),
        out_shape=jax.ShapeDtypeStruct((b, l, d), F32),
        scratch_shapes=[pltpu.VMEM((tm, d), BF16), pltpu.VMEM((tm, d), F32)],
        compiler_params=_cparams(("parallel", "parallel", "arbitrary")), name="mlp",
    )(x, sh, sc, g_m, w1, w2, ln_g, ln_b)


def kernel(x, c, ctx, c_ctx, w_ada, b_ada, w_in, s5_lambda_re, s5_lambda_im, s5_log_dt, s5_b_re, s5_b_im, s5_c_re,
           s5_c_im, s5_d, s5_w_glu, hy_conv_w, hy_conv_b, hy_freq, hy_w1, hy_b1, hy_w2, hy_b2, hy_w3, hy_bias,
           na_rpb, sw_sink, w_branch, w_out, ln1_g, ln1_b, w_mlp1, w_mlp2, ln2_g, ln2_b):
    b, l, d = x.shape
    n_ctx = ctx.shape[1]
    depth = w_ada.shape[0]
    cc = jnp.zeros((8, d), F32).at[:b].set(c.astype(F32)).at[b].set(c_ctx.astype(F32))
    mod_all = _ada(cc, w_ada.astype(F32), b_ada.astype(F32))
    xc = ctx
    for layer in range(depth):
        need_ctx_out = layer < depth - 1
        mod = mod_all[layer, :b].reshape(b, 1, 6, d)
        mod_c = jnp.broadcast_to(mod_all[layer, b].reshape(1, 1, 6, d), (b, 1, 6, d))
        sh_a, sc_a, g_a, sh_m, sc_m, g_m = [mod[:, :, i] for i in range(6)]
        csh_a, csc_a, cg_a, csh_m, csc_m, cg_m = [mod_c[:, :, i] for i in range(6)]
        w_in_l = w_in[layer].astype(BF16)
        w_mix, w_gate = w_in_l[:, :MIX_W], w_in_l[:, MIX_W:]

        z_mix, u_s5, gates = _in_proj(x, sh_a, sc_a, w_mix, w_gate, tm=1024)
        zc_mix, uc_s5, gates_c = _in_proj(xc, csh_a, csc_a, w_mix, w_gate if need_ctx_out else None, tm=n_ctx)

        ops = _s5_operators(s5_lambda_re[layer], s5_lambda_im[layer], s5_log_dt[layer], s5_b_re[layer],
                            s5_b_im[layer], s5_c_re[layer], s5_c_im[layer], s5_d[layer])
        yc_s5, e_ctx = _s5_mix(uc_s5, ops, [jnp.zeros((b, 1, S5_STATE_W // 4), F32)] * 4)
        y_s5, _ = _s5_mix(u_s5, ops, e_ctx)

        hy_args = (hy_freq[layer], hy_w1[layer], hy_b1[layer], hy_w2[layer], hy_b2[layer], hy_w3[layer])
        sp_re, sp_im = _hy_spectra(_hy_filters(l, *hy_args), l)
        hy_l = _hyena(z_mix, hy_conv_w[layer], hy_conv_b[layer], sp_re, sp_im, hy_bias[layer])

        na_l = _na_attention(z_mix, zc_mix, na_rpb[layer])
        sw_l = _sw_attention(z_mix, zc_mix, sw_sink[layer])

        w_glu = s5_w_glu[layer].astype(BF16)
        w_br = w_branch[layer].astype(BF16)
        w_o = w_out[layer].astype(BF16)
        lg1, lb1 = ln1_g[layer].astype(F32)[None], ln1_b[layer].astype(F32)[None]
        lg2, lb2 = ln2_g[layer].astype(F32)[None], ln2_b[layer].astype(F32)[None]
        w1 = w_mlp1[layer].astype(BF16)
        w2 = w_mlp2[layer].astype(BF16)

        x_new = _merge(x, y_s5, hy_l, na_l, sw_l, gates, g_a, w_glu, w_br, w_o, lg1, lb1, tm=512)
        x_new = _mlp(x_new, sh_m, sc_m, g_m, w1, w2, lg2, lb2, tm=1024, th=1024)

        if need_ctx_out:
            spc_re, spc_im = _hy_spectra(_hy_filters(n_ctx, *hy_args), n_ctx)
            hy_c = _hyena(zc_mix, hy_conv_w[layer], hy_conv_b[layer], spc_re, spc_im, hy_bias[layer])
            na_c = _ctx_attention(zc_mix, 1024, NA_HEADS, NA_HEADS, NA_HEAD_DIM, None)
            sw_c = _ctx_attention(zc_mix, 1792, SW_HEADS, SW_KV_HEADS, SW_HEAD_DIM, sw_sink[layer])
            xc_new = _merge(xc, yc_s5, hy_c, na_c, sw_c, gates_c, cg_a, w_glu, w_br, w_o, lg1, lb1, tm=n_ctx)
            xc = _mlp(xc_new, csh_m, csc_m, cg_m, w1, w2, lg2, lb2, tm=n_ctx, th=1024)
        x = x_new
    return x
```

```python
import functools
import math

import numpy as np
import jax
import jax.numpy as jnp
from jax import lax
from jax.experimental import pallas as pl
from jax.experimental.pallas import tpu as pltpu

F32 = jnp.float32
BF16 = jnp.bfloat16

D_MODEL = 1024
GRID_W = 64
BRANCH_W = 256
N_BRANCH = 4
S5_GROUP = 16
S5_GROUPS = 16
S5_STATE = 64
HY_WIDTH = 256
HY_ORDER = 2
HY_BANDS = 16
HY_EMB = 2 * HY_BANDS + 1
HY_FFN = 64
HY_MIN_DECAY = math.log(1e-2) / 1.5
HY_MAX_DECAY = math.log(1e-2) / 0.3
NA_HEADS = 4
NA_HEAD_DIM = 64
NA_WIN_H = 8
NA_WIN_W = 16
SW_HEADS = 4
SW_KV_HEADS = 2
SW_HEAD_DIM = 64
SW_WINDOW = 128
SW_BLOCK = 128
MLP_HIDDEN = 4 * D_MODEL
ROPE_BASE = 10000.0
LN_EPS = 1e-6
NEG_INF = -1e30
DEPTH = 2
DEEPNORM_ALPHA = (2 * DEPTH) ** 0.25

MIX_W = 2304
NA_COL0 = BRANCH_W + 3 * HY_WIDTH
SW_COL0 = NA_COL0 + 3 * BRANCH_W
GATE_W = N_BRANCH * D_MODEL
S5_T = 8
S5_STATE_W = 4 * S5_GROUPS * S5_STATE
SUBLANES = 8
LANES = 128
VMEM_LIMIT = 56 * 1024 * 1024

HI = lax.Precision.HIGHEST


def _cparams(sem):
    return pltpu.CompilerParams(dimension_semantics=sem, vmem_limit_bytes=VMEM_LIMIT)


def _layernorm(x):
    mu = jnp.mean(x, axis=-1, keepdims=True)
    xc = x - mu
    var = jnp.mean(xc * xc, axis=-1, keepdims=True)
    return xc * lax.rsqrt(var + LN_EPS)


def _ada_kernel(c_ref, w_ref, b_ref, o_ref):
    c = c_ref[...]
    a = c * jax.nn.sigmoid(c)
    o_ref[0] = jnp.dot(a, w_ref[0], preferred_element_type=F32, precision=HI) + b_ref[0]


def _ada(cc, w_ada, b_ada):
    depth, d, n = w_ada.shape
    tn = 512
    return pl.pallas_call(
        _ada_kernel,
        grid=(depth, n // tn),
        in_specs=[pl.BlockSpec((8, d), lambda l, j: (0, 0)),
                  pl.BlockSpec((1, d, tn), lambda l, j: (l, 0, j)),
                  pl.BlockSpec((1, 1, tn), lambda l, j: (l, 0, j))],
        out_specs=pl.BlockSpec((1, 8, tn), lambda l, j: (l, 0, j)),
        out_shape=jax.ShapeDtypeStruct((depth, 8, n), F32),
        compiler_params=_cparams(("parallel", "parallel")), name="ada",
    )(cc, w_ada, b_ada.reshape(depth, 1, n))


IN_MIX_TN = MIX_W // 3
IN_GATE_TN = GATE_W // 4


def _in_proj_kernel(x_ref, sh_ref, sc_ref, wm_ref, *rest, n_mix, with_gates, with_rope):
    rest = list(rest)
    wg_ref = rest.pop(0) if with_gates else None
    cos_ref, sin_ref = (rest.pop(0), rest.pop(0)) if with_rope else (None, None)
    z_ref, u_ref = rest.pop(0), rest.pop(0)
    g_ref = rest.pop(0) if with_gates else None
    qr_ref, kvr_ref = (rest.pop(0), rest.pop(0)) if with_rope else (None, None)
    tok_scr, h_scr = rest
    j = pl.program_id(2)

    @pl.when(j == 0)
    def _():
        h = _layernorm(x_ref[0]) * (1.0 + sc_ref[0]) + sh_ref[0]
        h_scr[...] = h.astype(BF16)

    @pl.when(j < n_mix)
    def _():
        r = jnp.dot(h_scr[...], wm_ref[...], preferred_element_type=F32)
        z_ref[0] = r

        @pl.when(j == 0)
        def _():
            rows = tok_scr.shape[1] // S5_T
            for c in range(BRANCH_W // LANES):
                tok_scr[c] = r[:, c * LANES:(c + 1) * LANES]
                for t in range(S5_T):
                    col = t * BRANCH_W + c * LANES
                    u_ref[0, :, col:col + LANES] = tok_scr[c, pl.ds(t, rows, stride=S5_T), :]

        if with_rope:
            @pl.when(j == n_mix - 1)
            def _():
                qw = SW_HEADS * SW_HEAD_DIM
                q0 = SW_COL0 - (n_mix - 1) * IN_MIX_TN
                cos_t, sin_t = cos_ref[...], sin_ref[...]
                q = _rope(r[:, q0:q0 + qw], cos_t, sin_t) * np.float32(SW_HEAD_DIM ** -0.5)
                qr_ref[0] = q.astype(BF16)
                kv = r[:, q0 + qw:q0 + 2 * qw]
                lane = lax.broadcasted_iota(jnp.int32, kv.shape, 1)
                kvr_ref[0] = jnp.where(lane < SW_KV_HEADS * SW_HEAD_DIM, _rope(kv, cos_t, sin_t), kv).astype(BF16)

    if with_gates:
        @pl.when(j >= n_mix)
        def _():
            r = jnp.dot(h_scr[...], wg_ref[...], preferred_element_type=F32)
            g_ref[0] = jax.nn.sigmoid(r).astype(g_ref.dtype)


def _in_proj(x, sh, sc, w_mix, w_gate, *, tm, with_rope=False):
    b, l, d = x.shape
    with_gates = w_gate is not None
    n_mix = MIX_W // IN_MIX_TN
    n_gate = GATE_W // IN_GATE_TN if with_gates else 0
    mix_j = lambda j: jnp.minimum(j, n_mix - 1)
    gate_j = lambda j: jnp.maximum(j - n_mix, 0)
    in_specs = [pl.BlockSpec((1, tm, d), lambda bi, i, j: (bi, i, 0)),
                pl.BlockSpec((1, 1, d), lambda bi, i, j: (bi, 0, 0)),
                pl.BlockSpec((1, 1, d), lambda bi, i, j: (bi, 0, 0)),
                pl.BlockSpec((d, IN_MIX_TN), lambda bi, i, j: (0, mix_j(j)))]
    out_specs = [pl.BlockSpec((1, tm, IN_MIX_TN), lambda bi, i, j: (bi, i, mix_j(j))),
                 pl.BlockSpec((1, tm // S5_T, S5_T * BRANCH_W), lambda bi, i, j: (bi, i, 0))]
    out_shape = [jax.ShapeDtypeStruct((b, l, MIX_W), F32),
                 jax.ShapeDtypeStruct((b, l // S5_T, S5_T * BRANCH_W), F32)]
    args = [x, sh, sc, w_mix]
    names = ["z", "u"]
    if with_gates:
        in_specs.append(pl.BlockSpec((d, IN_GATE_TN), lambda bi, i, j: (0, gate_j(j))))
        out_specs.append(pl.BlockSpec((1, tm, IN_GATE_TN), lambda bi, i, j: (bi, i, gate_j(j))))
        out_shape.append(jax.ShapeDtypeStruct((b, l, GATE_W), BF16))
        args.append(w_gate)
        names.append("gates")
    if with_rope:
        qw = SW_HEADS * SW_HEAD_DIM
        in_specs += [pl.BlockSpec((tm, qw), lambda bi, i, j: (i, 0))] * 2
        out_specs += [pl.BlockSpec((1, tm, qw), lambda bi, i, j: (bi, i, 0))] * 2
        out_shape += [jax.ShapeDtypeStruct((b, l, qw), BF16)] * 2
        args += list(_rope_tables(l, SW_HEAD_DIM, SW_HEADS))
        names += ["q_rope", "kv_rope"]
    outs = pl.pallas_call(
        functools.partial(_in_proj_kernel, n_mix=n_mix, with_gates=with_gates, with_rope=with_rope),
        grid=(b, l // tm, n_mix + n_gate),
        in_specs=in_specs,
        out_specs=out_specs,
        out_shape=out_shape,
        scratch_shapes=[pltpu.VMEM((BRANCH_W // LANES, tm, LANES), F32), pltpu.VMEM((tm, d), BF16)],
        compiler_params=_cparams(("parallel", "parallel", "arbitrary")), name="in_proj",
    )(*args)
    return dict(zip(names, outs))


def _s5_inc_kernel(u_ref, q_ref, o_ref):
    o_ref[...] = jnp.dot(u_ref[...].astype(BF16), q_ref[...], preferred_element_type=F32)


def _s5_increments(uu, q_all, *, tm, tn):
    m, k = uu.shape
    n = q_all.shape[1]
    return pl.pallas_call(
        _s5_inc_kernel,
        grid=(m // tm, n // tn),
        in_specs=[pl.BlockSpec((tm, k), lambda i, j: (i, 0)), pl.BlockSpec((k, tn), lambda i, j: (0, j))],
        out_specs=pl.BlockSpec((tm, tn), lambda i, j: (i, j)),
        out_shape=jax.ShapeDtypeStruct((m, n), F32),
        compiler_params=_cparams(("parallel", "parallel")), name="s5_inc",
    )(uu, q_all)


def _s5_out_kernel(u_ref, h0, h1, h2, h3, m_ref, n0, n1, n2, n3, o_ref):
    nt = (((1,), (1,)), ((), ()))
    acc = jnp.dot(u_ref[...].astype(BF16), m_ref[...], preferred_element_type=F32)
    for h_ref, n_ref in ((h0, n0), (h1, n1), (h2, n2), (h3, n3)):
        acc += lax.dot_general(h_ref[...].astype(BF16), n_ref[...], nt, preferred_element_type=F32)
    o_ref[...] = acc


def _s5_outputs(uu, h, m_tot, n_t, *, tm, tn):
    m, k = uu.shape
    gp = h[0].shape[1]
    n = m_tot.shape[1]
    return pl.pallas_call(
        _s5_out_kernel,
        grid=(m // tm, n // tn),
        in_specs=[pl.BlockSpec((tm, k), lambda i, j: (i, 0))]
        + [pl.BlockSpec((tm, gp), lambda i, j: (i, 0))] * 4
        + [pl.BlockSpec((k, tn), lambda i, j: (0, j))]
        + [pl.BlockSpec((tn, gp), lambda i, j, c=c: (j, c)) for c in range(4)],
        out_specs=pl.BlockSpec((tm, tn), lambda i, j: (i, j)),
        out_shape=jax.ShapeDtypeStruct((m, n), F32),
        compiler_params=_cparams(("parallel", "parallel")), name="s5_out",
    )(uu, *h, m_tot, n_t, n_t, n_t, n_t)


def _s5_scan_kernel(gfr, gfi, gbr, gbi, afr, afi, abr, abi, h0fr, h0fi, h0br, h0bi,
                    hfr, hfi, hbr, hbi, efr, efi, ebr, ebi, *, n_chunks):
    a_fr = afr[...][None]
    a_fi = afi[...][None]
    a_br = abr[...][None]
    a_bi = abi[...][None]

    def body(k, carry):
        sfr, sfi, sbr, sbi = carry
        kb = n_chunks - 1 - k
        hfr[:, pl.ds(k, 1), :] = sfr
        hfi[:, pl.ds(k, 1), :] = sfi
        hbr[:, pl.ds(kb, 1), :] = sbr
        hbi[:, pl.ds(kb, 1), :] = sbi
        nfr = a_fr * sfr - a_fi * sfi + gfr[:, pl.ds(k, 1), :]
        nfi = a_fr * sfi + a_fi * sfr + gfi[:, pl.ds(k, 1), :]
        nbr = a_br * sbr - a_bi * sbi + gbr[:, pl.ds(kb, 1), :]
        nbi = a_br * sbi + a_bi * sbr + gbi[:, pl.ds(kb, 1), :]
        return nfr, nfi, nbr, nbi

    sfr, sfi, sbr, sbi = lax.fori_loop(0, n_chunks, body, (h0fr[...], h0fi[...], h0br[...], h0bi[...]))
    efr[...] = sfr
    efi[...] = sfi
    ebr[...] = sbr
    ebi[...] = sbi


def _s5_scan(g, a_t, h0):
    b, k, w4 = g.shape
    w = 2 * LANES
    q = w4 // 4
    nb = q // w

    def comp(c):
        return pl.BlockSpec((b, k, w), lambda j, c=c: (0, 0, c * nb + j))

    def comp_a(c):
        return pl.BlockSpec((1, w), lambda j, c=c: (0, c * nb + j))

    state = pl.BlockSpec((b, k, w), lambda j: (0, 0, j))
    edge = pl.BlockSpec((b, 1, w), lambda j: (0, 0, j))
    outs = pl.pallas_call(
        functools.partial(_s5_scan_kernel, n_chunks=k),
        grid=(nb,),
        in_specs=[comp(c) for c in range(4)] + [comp_a(c) for c in range(4)] + [edge] * 4,
        out_specs=[state] * 4 + [edge] * 4,
        out_shape=[jax.ShapeDtypeStruct((b, k, q), F32)] * 4 + [jax.ShapeDtypeStruct((b, 1, q), F32)] * 4,
        compiler_params=_cparams(("parallel",)), name="s5_scan",
    )(g, g, g, g, a_t, a_t, a_t, a_t, *h0)
    return outs[:4], outs[4:]


def _cmul(ar, ai, br, bi):
    return ar * br - ai * bi, ar * bi + ai * br


def _s5_operators(lam_re, lam_im, log_dt, b_re, b_im, c_re, c_im, d):
    t_ = S5_T
    g_, p_, n_ = S5_GROUPS, S5_STATE, S5_GROUP
    gp, w_ = g_ * p_, g_ * n_
    lr, li = lam_re.astype(F32), lam_im.astype(F32)
    dt = jnp.exp(log_dt.astype(F32))[..., None]
    ks = jnp.arange(t_ + 1, dtype=F32)[:, None, None, None]
    mag = jnp.exp(ks * lr * dt)
    pw_re = mag * jnp.cos(ks * li * dt)
    pw_im = mag * jnp.sin(ks * li * dt)
    a_re, a_im = pw_re[1], pw_im[1]
    den = lr ** 2 + li ** 2
    f_re = ((a_re - 1.0) * lr + a_im * li) / den
    f_im = (a_im * lr - (a_re - 1.0) * li) / den
    br, bi = b_re.astype(F32), b_im.astype(F32)
    bb_re = f_re[..., None] * br - f_im[..., None] * bi
    bb_im = f_re[..., None] * bi + f_im[..., None] * br
    cr, ci = c_re.astype(F32), c_im.astype(F32)
    wb_re, wb_im = _cmul(pw_re[:t_, ..., None], pw_im[:t_, ..., None], bb_re[None], bb_im[None])
    kern = jnp.einsum('dgnp,kdgpm->dkgnm', cr, wb_re, precision=HI) \
        - jnp.einsum('dgnp,kdgpm->dkgnm', ci, wb_im, precision=HI)
    k_tile = jnp.tile(kern.transpose(0, 1, 2, 4, 3).reshape(2, t_, w_, n_), (1, 1, 1, g_))

    def rows_tiled(x):
        return jnp.tile(x.transpose(0, 3, 1, 2).reshape(2, n_, gp), (1, g_, 1))

    bt_re, bt_im = rows_tiled(bb_re), rows_tiled(bb_im)
    ct_re, ct_im = rows_tiled(cr.transpose(0, 1, 3, 2)), rows_tiled(ci.transpose(0, 1, 3, 2))
    pr = pw_re.transpose(1, 0, 2, 3).reshape(2, t_ + 1, gp)
    pi_ = pw_im.transpose(1, 0, 2, 3).reshape(2, t_ + 1, gp)
    full = lambda s: pl.BlockSpec(s, lambda t: (0,) * len(s))
    m_tot, q_all, n_t = pl.pallas_call(
        _s5_ops_kernel,
        grid=(t_,),
        in_specs=[full((2, t_ + 1, gp))] * 2 + [full((2, w_, gp))] * 4 + [full((2, t_, w_, w_)), full((1, w_))],
        out_specs=[pl.BlockSpec((w_, t_ * w_), lambda t: (t, 0)),
                   pl.BlockSpec((w_, 4 * gp), lambda t: (t, 0)),
                   pl.BlockSpec((w_, 4 * gp), lambda t: (t, 0))],
        out_shape=[jax.ShapeDtypeStruct((t_ * w_, t_ * w_), BF16),
                   jax.ShapeDtypeStruct((t_ * w_, 4 * gp), BF16),
                   jax.ShapeDtypeStruct((t_ * w_, 4 * gp), BF16)],
        compiler_params=_cparams(("parallel",)), name="s5_ops",
    )(pr, pi_, bt_re, bt_im, ct_re, ct_im, k_tile, d.astype(F32)[None])
    a_t = jnp.concatenate([pr[0, t_], pi_[0, t_], pr[1, t_], pi_[1, t_]])[None]
    return m_tot, q_all, n_t, a_t


def _s5_ops_kernel(pr_ref, pi_ref, btr_ref, bti_ref, ctr_ref, cti_ref, kt_ref, d_ref, m_ref, q_ref, nt_ref):
    t = pl.program_id(0)
    t_ = S5_T
    w_, gp = btr_ref.shape[1], btr_ref.shape[2]
    row_g = lax.broadcasted_iota(jnp.int32, (w_, gp), 0) // S5_GROUP
    col_g = lax.broadcasted_iota(jnp.int32, (w_, gp), 1) // S5_STATE
    same = row_g == col_g
    for dr in range(2):
        e_q = (t_ - 1 - t) if dr == 0 else t
        e_n = (t + 1) if dr == 0 else (t_ - t)
        b_re = jnp.where(same, btr_ref[dr], 0.0)
        b_im = jnp.where(same, bti_ref[dr], 0.0)
        q_re, q_im = _cmul(b_re, b_im, pr_ref[dr, pl.ds(e_q, 1), :], pi_ref[dr, pl.ds(e_q, 1), :])
        c_re = jnp.where(same, ctr_ref[dr], 0.0)
        c_im = jnp.where(same, cti_ref[dr], 0.0)
        n_re, n_im = _cmul(c_re, c_im, pr_ref[dr, pl.ds(e_n, 1), :], pi_ref[dr, pl.ds(e_n, 1), :])
        base = 2 * dr * gp
        q_ref[:, base:base + gp] = q_re.astype(BF16)
        q_ref[:, base + gp:base + 2 * gp] = q_im.astype(BF16)
        nt_ref[:, base:base + gp] = n_re.astype(BF16)
        nt_ref[:, base + gp:base + 2 * gp] = (-n_im).astype(BF16)
    r2 = lax.broadcasted_iota(jnp.int32, (w_, w_), 0)
    c2 = lax.broadcasted_iota(jnp.int32, (w_, w_), 1)
    same2 = (r2 // S5_GROUP) == (c2 // S5_GROUP)
    skip = jnp.where(r2 == c2, d_ref[...], 0.0)
    for i in range(t_):
        k_f = kt_ref[0, jnp.maximum(i - t, 0)] * jnp.where(i >= t, 1.0, 0.0)
        k_b = kt_ref[1, jnp.maximum(t - i, 0)] * jnp.where(t >= i, 1.0, 0.0)
        blk = jnp.where(same2, k_f + k_b, 0.0) + skip * jnp.where(t == i, 1.0, 0.0)
        m_ref[:, i * w_:(i + 1) * w_] = blk.astype(BF16)


def _s5_mix(u, ops, h0):
    m_tot, q_all, n_t, a_t = ops
    b, k, tw = u.shape
    rows = b * k
    uu = u.reshape(rows, tw)
    tm = min(rows, 512)
    g = _s5_increments(uu, q_all, tm=tm, tn=1024)
    h, e = _s5_scan(g.reshape(b, k, S5_STATE_W), a_t, h0)
    y = _s5_outputs(uu, [hc.reshape(rows, -1) for hc in h], m_tot, n_t, tm=tm, tn=512)
    return y.reshape(b, k, tw), e


def _hy_filter_kernel(f_ref, w1_ref, b1_ref, w2_ref, b2_ref, w3_ref, fr_ref, dl_ref, o_ref, *, half_len, tile):
    feats = f_ref[...]
    h = jnp.dot(feats, w1_ref[...], preferred_element_type=F32, precision=HI) + b1_ref[...]
    h = jnp.sin(fr_ref[0:1, :] * h)
    h = jnp.dot(h, w2_ref[...], preferred_element_type=F32, precision=HI) + b2_ref[...]
    h = jnp.sin(fr_ref[1:2, :] * h)
    o = jnp.dot(h.astype(BF16), w3_ref[...].astype(BF16), preferred_element_type=F32)
    t = feats[:, 0:1]
    o = o * jnp.exp(-t * dl_ref[...])
    nw = HY_ORDER * HY_WIDTH
    n = pl.program_id(0) * tile + lax.broadcasted_iota(jnp.int32, (tile, 1), 0)
    k = jnp.where(n < half_len, o[:, :nw], jnp.where(n == half_len, 0.0, o[:, nw:]))
    o_ref[...] = k.reshape(o_ref.shape)


def _hy_filters(n_tokens, freq, w1, b1, w2, b2, w3):
    n2 = 2 * n_tokens
    idx = np.arange(n2, dtype=np.float64)
    pos = np.where(idx <= n_tokens, idx, n2 - idx)[:, None]
    t = pos / max(n_tokens - 1, 1)
    bands = np.linspace(1e-4, HY_BANDS - 1, HY_BANDS)[None]
    ang = 2.0 * math.pi * bands * pos / n_tokens
    feats = np.concatenate([t, np.cos(ang), -np.sin(ang)], axis=-1)
    kpad = LANES - HY_EMB
    feats = jnp.asarray(np.pad(feats, ((0, 0), (0, kpad))), F32)
    w1p = jnp.pad(w1.astype(F32), ((0, kpad), (0, 0)))
    deltas = jnp.abs(jnp.linspace(HY_MIN_DECAY, HY_MAX_DECAY, HY_WIDTH, dtype=F32))
    dl = jnp.tile(deltas, 2 * HY_ORDER)[None]
    tile = min(n2, 1024)
    nw = 2 * HY_ORDER * HY_WIDTH
    full = lambda s: pl.BlockSpec(s, lambda i: (0,) * len(s))
    return pl.pallas_call(
        functools.partial(_hy_filter_kernel, half_len=n_tokens, tile=tile),
        grid=(n2 // tile,),
        in_specs=[pl.BlockSpec((tile, LANES), lambda i: (i, 0)),
                  full((LANES, HY_FFN)), full((1, HY_FFN)), full((HY_FFN, HY_FFN)), full((1, HY_FFN)),
                  full((HY_FFN, nw)), full((2, HY_FFN)), full((1, nw))],
        out_specs=pl.BlockSpec((tile // SUBLANES, SUBLANES, HY_ORDER * HY_WIDTH), lambda i: (i, 0, 0)),
        out_shape=jax.ShapeDtypeStruct((n2 // SUBLANES, SUBLANES, HY_ORDER * HY_WIDTH), F32),
        compiler_params=_cparams(("parallel",)), name="hy_filter",
    )(feats, w1p, b1.astype(F32)[None], w2.astype(F32), b2.astype(F32)[None], w3.astype(F32),
      freq.astype(F32), dl)


FFT_G = 4
FFT_UNROLL = 8


def _fft_tables(n):
    s = n // SUBLANES
    nst = int(round(math.log2(s)))
    runs = []
    half = s // 2
    while half >= FFT_G:
        runs.append(-2 * np.pi * np.arange(half) / (2 * half))
        half //= 2
    ang_s = np.concatenate(runs)
    tw_slab = np.stack([np.cos(ang_s), np.sin(ang_s)]).astype(np.float32)
    tw_slab = np.broadcast_to(tw_slab[..., None, None], (2, ang_s.size, SUBLANES, LANES)).copy()
    pos = np.arange(s)
    rev = np.zeros(s, np.int64)
    for bit in range(nst):
        rev |= ((pos >> bit) & 1) << (nst - 1 - bit)
    ang = -2 * np.pi * (rev[:, None] * np.arange(SUBLANES)[None, :]) / n
    tw_mid = np.stack([np.cos(ang), np.sin(ang)]).astype(np.float32)
    tw_mid = np.broadcast_to(tw_mid[..., None], (2, s, SUBLANES, LANES)).copy()
    t = np.arange(n)
    ang2 = -2 * np.pi * t / (2 * n)
    mod = np.stack([np.cos(ang2), np.sin(ang2)]).astype(np.float32).reshape(2, s, SUBLANES)
    mod = np.broadcast_to(mod[..., None], (2, s, SUBLANES, LANES)).copy()
    return jnp.asarray(tw_slab), jnp.asarray(tw_mid), jnp.asarray(mod)


def _sub_patterns():
    sub = lax.broadcasted_iota(jnp.int32, (SUBLANES, LANES), 0)

    def table(vals):
        out = jnp.full((SUBLANES, LANES), vals[0], F32)
        for k in range(1, SUBLANES):
            out = jnp.where(sub == k, np.float32(vals[k]), out)
        return out

    pats = {}
    for dist in (4, 2, 1):
        lo = (sub & dist) == 0
        sgn = jnp.where(lo, 1.0, -1.0).astype(F32)
        wr = [1.0] * SUBLANES
        wi = [0.0] * SUBLANES
        for k in range(SUBLANES):
            if k & dist:
                e = (k % dist) * (SUBLANES // (2 * dist))
                wr[k] = math.cos(-2 * math.pi * e / SUBLANES)
                wi[k] = math.sin(-2 * math.pi * e / SUBLANES)
        pats[dist] = (lo, sgn, table(wr), table(wi))
    pats["quarter"] = (sub & 3) == 3
    return pats


def _dft8_fwd(vr, vi, pats):
    for dist in (4, 2, 1):
        lo, sgn, wr, wi = pats[dist]
        up_r = pltpu.roll(vr, SUBLANES - dist, 0)
        up_i = pltpu.roll(vi, SUBLANES - dist, 0)
        if dist == 4:
            pr, pi_ = up_r, up_i
        else:
            pr = jnp.where(lo, up_r, pltpu.roll(vr, dist, 0))
            pi_ = jnp.where(lo, up_i, pltpu.roll(vi, dist, 0))
        tr = pr + sgn * vr
        ti = pi_ + sgn * vi
        if dist == 1:
            vr, vi = tr, ti
        elif dist == 2:
            qt = pats["quarter"]
            vr, vi = jnp.where(qt, ti, tr), jnp.where(qt, -tr, ti)
        else:
            vr, vi = _cmul(tr, ti, wr, wi)
    return vr, vi


def _dft8_inv(vr, vi, pats):
    for dist in (1, 2, 4):
        lo, sgn, wr, wi = pats[dist]
        if dist == 2:
            qt = pats["quarter"]
            vr, vi = jnp.where(qt, -vi, vr), jnp.where(qt, vr, vi)
        elif dist == 4:
            vr, vi = _cmul(vr, vi, wr, -wi)
        up_r = pltpu.roll(vr, SUBLANES - dist, 0)
        up_i = pltpu.roll(vi, SUBLANES - dist, 0)
        if dist == 4:
            pr, pi_ = up_r, up_i
        else:
            pr = jnp.where(lo, up_r, pltpu.roll(vr, dist, 0))
            pi_ = jnp.where(lo, up_i, pltpu.roll(vi, dist, 0))
        vr = pr + sgn * vr
        vi = pi_ + sgn * vi
    return vr, vi


def _slab_stage(re, im, tw, half, slabs, inverse):
    per_block = half // FFT_G
    nblk = slabs // (2 * half)
    unroll = min(FFT_UNROLL, nblk * per_block)
    tw_off = slabs - 2 * half
    if nblk >= unroll:
        blocks_per_it = unroll // per_block
        trips = nblk // blocks_per_it
        offsets = [(b * 2 * half + jc * FFT_G, jc * FFT_G) for b in range(blocks_per_it) for jc in range(per_block)]
        data_step, tw_step = blocks_per_it * 2 * half, 0
    else:
        chunks_per_it = unroll // nblk
        trips = per_block // chunks_per_it
        offsets = [(b * 2 * half + k * FFT_G, k * FFT_G) for b in range(nblk) for k in range(chunks_per_it)]
        data_step = tw_step = chunks_per_it * FFT_G

    def body(c, carry):
        d0 = pl.multiple_of(c * data_step, FFT_G)
        t0 = pl.multiple_of(tw_off + c * tw_step, FFT_G)
        twiddles = {}
        for d_off, t_off in offsets:
            if t_off not in twiddles:
                tws = pl.ds(t0 + t_off, FFT_G)
                twiddles[t_off] = (tw[0, tws], tw[1, tws])
            wr, wi = twiddles[t_off]
            lo = pl.ds(d0 + d_off, FFT_G)
            hi = pl.ds(d0 + d_off + half, FFT_G)
            ar, ai, br, bi = re[lo], im[lo], re[hi], im[hi]
            if inverse:
                br, bi = br * wr + bi * wi, bi * wr - br * wi
                re[lo] = ar + br
                im[lo] = ai + bi
                re[hi] = ar - br
                im[hi] = ai - bi
            else:
                re[lo] = ar + br
                im[lo] = ai + bi
                dr, di = ar - br, ai - bi
                re[hi] = dr * wr - di * wi
                im[hi] = dr * wi + di * wr
        return carry

    lax.fori_loop(0, trips, body, 0)


def _fft_forward_big(re, im, tw, slabs):
    half = slabs // 2
    while half >= 4:
        _slab_stage(re, im, tw, half, slabs, inverse=False)
        half //= 2


def _fft_inverse_big(re, im, tw, slabs):
    half = 4
    while half <= slabs // 2:
        _slab_stage(re, im, tw, half, slabs, inverse=True)
        half *= 2


def _radix4_fwd(x):
    (x0r, x0i), (x1r, x1i), (x2r, x2i), (x3r, x3i) = x
    y0r, y0i = x0r + x2r, x0i + x2i
    y2r, y2i = x0r - x2r, x0i - x2i
    y1r, y1i = x1r + x3r, x1i + x3i
    dr, di = x1r - x3r, x1i - x3i
    y3r, y3i = di, -dr
    return [(y0r + y1r, y0i + y1i), (y0r - y1r, y0i - y1i), (y2r + y3r, y2i + y3i), (y2r - y3r, y2i - y3i)]


def _radix4_inv(z):
    (z0r, z0i), (z1r, z1i), (z2r, z2i), (z3r, z3i) = z
    y0r, y0i = z0r + z1r, z0i + z1i
    y1r, y1i = z0r - z1r, z0i - z1i
    y2r, y2i = z2r + z3r, z2i + z3i
    y3r, y3i = z2r - z3r, z2i - z3i
    qr, qi = -y3i, y3r
    return [(y0r + y2r, y0i + y2i), (y1r + qr, y1i + qi), (y0r - y2r, y0i - y2i), (y1r - qr, y1i - qi)]


def _fft_middle(re, im, twm_ref, slabs, pats, spec=None, out=None):
    def body(q, carry):
        p0 = q * FFT_G
        x = [(re[p0 + g], im[p0 + g]) for g in range(FFT_G)]
        z = _radix4_fwd(x)
        res = []
        for g in range(FFT_G):
            twr = twm_ref[0, p0 + g]
            twi = twm_ref[1, p0 + g]
            vr, vi = _cmul(z[g][0], z[g][1], twr, twi)
            vr, vi = _dft8_fwd(vr, vi, pats)
            if spec is None:
                out[0][0, 0, p0 + g] = vr * out[2]
                out[1][0, 0, p0 + g] = vi * out[2]
            else:
                vr, vi = _cmul(vr, vi, spec[0][0, 0, p0 + g], spec[1][0, 0, p0 + g])
                vr, vi = _dft8_inv(vr, vi, pats)
                res.append(_cmul(vr, vi, twr, -twi))
        if spec is not None:
            x = _radix4_inv(res)
            for g in range(FFT_G):
                re[p0 + g] = x[g][0]
                im[p0 + g] = x[g][1]
        return carry

    lax.fori_loop(0, slabs // FFT_G, body, 0)


def _hy_spec_kernel(tw_ref, k_ref, twm_ref, mod_ref, sr_ref, si_ref, re, im, *, slabs):
    h = pl.program_id(2)
    pats = _sub_patterns()
    a = k_ref[0:slabs]
    b = k_ref[slabs:2 * slabs]

    @pl.when(h == 0)
    def _():
        re[...] = a + b
        im[...] = jnp.zeros_like(a)

    @pl.when(h == 1)
    def _():
        dlt = a - b
        re[...] = dlt * mod_ref[0]
        im[...] = dlt * mod_ref[1]

    _fft_forward_big(re, im, tw_ref, slabs)
    scale = np.float32(1.0 / (2 * SUBLANES * slabs))
    _fft_middle(re, im, twm_ref, slabs, pats, out=(sr_ref, si_ref, scale))


def _hy_spectra(k, n_tokens):
    slabs = n_tokens // SUBLANES
    tw, twm, mod = _fft_tables(n_tokens)
    nt = HY_WIDTH // LANES
    shape = jax.ShapeDtypeStruct((HY_ORDER, 2, slabs, SUBLANES, HY_WIDTH), F32)
    spec_out = pl.BlockSpec((1, 1, slabs, SUBLANES, LANES), lambda o, j, h: (o, h, 0, 0, j))
    tab = pl.BlockSpec((2, slabs, SUBLANES, LANES), lambda o, j, h: (0, 0, 0, 0), pipeline_mode=pl.Buffered(1))
    return pl.pallas_call(
        functools.partial(_hy_spec_kernel, slabs=slabs),
        grid=(HY_ORDER, nt, 2),
        in_specs=[pl.BlockSpec(tw.shape, lambda o, j, h: (0, 0, 0, 0), pipeline_mode=pl.Buffered(1)),
                  pl.BlockSpec((2 * slabs, SUBLANES, LANES), lambda o, j, h: (0, 0, o * nt + j)),
                  tab, tab],
        out_specs=[spec_out, spec_out],
        out_shape=[shape, shape],
        scratch_shapes=[pltpu.VMEM((slabs, SUBLANES, LANES), F32)] * 2,
        compiler_params=_cparams(("parallel", "parallel", "arbitrary")), name="hy_spec",
    )(tw, k, twm, mod)


def _hy_conv_kernel(tw_ref, u_ref, g_ref, sr_ref, si_ref, twm_ref, mod_ref, b_ref, o_ref, re, im, *, slabs):
    h = pl.program_id(2)
    pats = _sub_patterns()
    slab_shape = (slabs, SUBLANES, LANES)
    n = slabs * SUBLANES

    @pl.when(h == 0)
    def _():
        re[...] = u_ref[0].reshape(slab_shape)
        im[...] = u_ref[1].reshape(slab_shape)

    @pl.when(h == 1)
    def _():
        ur, ui = u_ref[0].reshape(slab_shape), u_ref[1].reshape(slab_shape)
        mr, mi = mod_ref[0], mod_ref[1]
        re[...] = ur * mr - ui * mi
        im[...] = ur * mi + ui * mr

    _fft_forward_big(re, im, tw_ref, slabs)
    _fft_middle(re, im, twm_ref, slabs, pats, spec=(sr_ref, si_ref))
    _fft_inverse_big(re, im, tw_ref, slabs)

    @pl.when(h == 0)
    def _():
        o_ref[0] = re[...].reshape(n, LANES)
        o_ref[1] = im[...].reshape(n, LANES)

    @pl.when(h == 1)
    def _():
        yr, yi = re[...], im[...]
        mr, mi = mod_ref[0], mod_ref[1]
        y0 = o_ref[0] + (yr * mr + yi * mi).reshape(n, LANES)
        y1 = o_ref[1] + (yi * mr - yr * mi).reshape(n, LANES)
        o_ref[0] = g_ref[0] * (y0 + b_ref[...] * u_ref[0])
        o_ref[1] = g_ref[1] * (y1 + b_ref[...] * u_ref[1])


def _hy_order(u, u_blk, gate, gate_blk, spec_re, spec_im, order, bias):
    b, n, _ = u.shape
    w = HY_WIDTH
    slabs = n // SUBLANES
    tw, twm, mod = _fft_tables(n)
    nt = w // LANES
    tab = pl.BlockSpec((2, slabs, SUBLANES, LANES), lambda j, p, h: (0, 0, 0, 0), pipeline_mode=pl.Buffered(1))
    spec_in = pl.BlockSpec((1, 1, slabs, SUBLANES, LANES), lambda j, p, h: (order, h, 0, 0, j))

    def io(blk):
        return pl.BlockSpec((2, n, LANES), lambda j, p, h: (p, 0, blk * nt + j))

    return pl.pallas_call(
        functools.partial(_hy_conv_kernel, slabs=slabs),
        grid=(nt, b // 2, 2),
        in_specs=[pl.BlockSpec(tw.shape, lambda j, p, h: (0, 0, 0, 0), pipeline_mode=pl.Buffered(1)),
                  io(u_blk), io(gate_blk), spec_in, spec_in, tab, tab,
                  pl.BlockSpec((1, LANES), lambda j, p, h: (0, j))],
        out_specs=io(0),
        out_shape=jax.ShapeDtypeStruct((b, n, w), F32),
        scratch_shapes=[pltpu.VMEM((slabs, SUBLANES, LANES), F32)] * 2,
        compiler_params=_cparams(("parallel", "parallel", "arbitrary")), name="hy_conv",
    )(tw, u, gate, spec_re, spec_im, twm, mod, bias.astype(F32)[None])


def _shift_rows(x, n):
    row = lax.broadcasted_iota(jnp.int32, x.shape, 0)
    prev = jnp.where(row == 0, 0.0, pltpu.roll(x, 1, 0))
    nxt = jnp.where(row == n - 1, 0.0, pltpu.roll(x, n - 1, 0))
    return prev, nxt


def _hy_short_kernel(z_ref, w_ref, b_ref, o_ref, *, n):
    x = z_ref[0]
    prev, nxt = _shift_rows(x, n)
    o_ref[0] = prev * w_ref[0:1, :] + x * w_ref[1:2, :] + nxt * w_ref[2:3, :] + b_ref[...]


def _hy_short(z_mix, conv_w, conv_b):
    b, n, _ = z_mix.shape
    c3 = 3 * HY_WIDTH
    w = conv_w.reshape(3, c3).astype(F32)
    tc = 256
    col0 = BRANCH_W // tc
    return pl.pallas_call(
        functools.partial(_hy_short_kernel, n=n),
        grid=(b, c3 // tc),
        in_specs=[pl.BlockSpec((1, n, tc), lambda bi, j: (bi, 0, col0 + j)),
                  pl.BlockSpec((3, tc), lambda bi, j: (0, j)),
                  pl.BlockSpec((1, tc), lambda bi, j: (0, j))],
        out_specs=pl.BlockSpec((1, n, tc), lambda bi, j: (bi, 0, j)),
        out_shape=jax.ShapeDtypeStruct((b, n, c3), F32),
        compiler_params=_cparams(("parallel", "parallel")), name="hy_short",
    )(z_mix, w, conv_b.astype(F32)[None])


def _hyena(z_mix, conv_w, conv_b, spec_re, spec_im, bias):
    zc = _hy_short(z_mix, conv_w, conv_b)
    v1 = _hy_order(zc, 0, zc, 1, spec_re, spec_im, 0, bias[0])
    return _hy_order(v1, 0, zc, 2, spec_re, spec_im, 1, bias[1])


NA_QROWS = 4
NA_KROWS = NA_QROWS + NA_WIN_H
NA_KPART = 256


def _softmax_parts(parts):
    m = None
    for s in parts:
        mm = jnp.max(s, axis=-1, keepdims=True)
        m = mm if m is None else jnp.maximum(m, mm)
    return m


def _na_kernel(*refs, kparts):
    q_ref, k_refs, v_refs = refs[0], refs[1:1 + kparts], refs[1 + kparts:1 + 2 * kparts]
    kc_ref, vc_ref, bias_ref, o_ref = refs[1 + 2 * kparts:]
    q = q_ref[0] * np.float32(NA_HEAD_DIM ** -0.5)
    k = jnp.concatenate([r[0] for r in k_refs], axis=0).astype(BF16)
    v = jnp.concatenate([r[0] for r in v_refs], axis=0).astype(BF16)
    kc = kc_ref[0].astype(BF16)
    vc = vc_ref[0].astype(BF16)
    nt = (((1,), (1,)), ((), ()))
    outs = []
    for h in range(NA_HEADS):
        sl = slice(h * NA_HEAD_DIM, (h + 1) * NA_HEAD_DIM)
        qh = q[:, sl].astype(BF16)
        s_lat = lax.dot_general(qh, k[:, sl], nt, preferred_element_type=F32) + bias_ref[0, h]
        s_ctx = lax.dot_general(qh, kc[:, sl], nt, preferred_element_type=F32)
        m = _softmax_parts([s_lat, s_ctx])
        p_lat = jnp.exp(s_lat - m)
        p_ctx = jnp.exp(s_ctx - m)
        den = jnp.sum(p_lat, axis=-1, keepdims=True) + jnp.sum(p_ctx, axis=-1, keepdims=True)
        o = jnp.dot(p_lat.astype(BF16), v[:, sl], preferred_element_type=F32) \
            + jnp.dot(p_ctx.astype(BF16), vc[:, sl], preferred_element_type=F32)
        outs.append(o / den)
    o_ref[0] = jnp.concatenate(outs, axis=-1)


def _na_bias(rpb, rows):
    kh = min(NA_WIN_H, rows)
    col = np.arange(GRID_W)
    col_start = np.clip(col - NA_WIN_W // 2, 0, GRID_W - NA_WIN_W)
    col_ok = (col[None] >= col_start[:, None]) & (col[None] < col_start[:, None] + NA_WIN_W)
    off_c = np.clip(col[None] - col[:, None], -(NA_WIN_W - 1), NA_WIN_W - 1) + (NA_WIN_W - 1)
    nblk = rows // NA_QROWS
    n_r, n_c = 2 * NA_WIN_H - 1, 2 * NA_WIN_W - 1
    table = jnp.pad(rpb.astype(F32), ((0, 0), (0, 1), (0, 1)), constant_values=NEG_INF)
    sel_c = np.eye(n_c + 1, dtype=np.float32)[np.where(col_ok, off_c, n_c)]
    blocks = jnp.einsum('hab,qcb->hqac', table, sel_c, precision=HI).reshape(NA_HEADS, GRID_W, (n_r + 1) * GRID_W)
    blocks_odd = jnp.roll(blocks, -GRID_W, axis=-1)
    row_off = []
    for j in (0, 1, nblk - 1):
        qr = j * NA_QROWS + np.arange(NA_QROWS)
        ws = int(np.clip(j * NA_QROWS - NA_WIN_H // 2, 0, rows - NA_KROWS))
        kr = ws + np.arange(NA_KROWS)
        start = np.clip(qr - kh // 2, 0, rows - kh)
        row_ok = (kr[None] >= start[:, None]) & (kr[None] < start[:, None] + kh)
        off_r = kr[None] - qr[:, None] + (NA_WIN_H - 1)
        row_off.append(np.where(row_ok, off_r, n_r))
    row_off = np.stack(row_off)
    wide = (n_r + 1) * GRID_W
    return pl.pallas_call(
        functools.partial(_na_bias_kernel, row_off=row_off),
        grid=(NA_HEADS,),
        in_specs=[pl.BlockSpec((1, GRID_W, wide), lambda h: (h, 0, 0))] * 2,
        out_specs=pl.BlockSpec((3, 1, NA_QROWS * GRID_W, NA_KROWS * GRID_W), lambda h: (0, h, 0, 0)),
        out_shape=jax.ShapeDtypeStruct((3, NA_HEADS, NA_QROWS * GRID_W, NA_KROWS * GRID_W), F32),
        compiler_params=_cparams(("parallel",)), name="na_bias",
    )(blocks, blocks_odd)


def _na_bias_kernel(even_ref, odd_ref, o_ref, *, row_off):
    n_cls, n_q, n_k = row_off.shape
    masked = 2 * NA_WIN_H - 1
    pair = 2 * GRID_W
    lane = lax.broadcasted_iota(jnp.int32, (GRID_W, pair), 1)

    def left(a):
        src, start = (even_ref, a) if a % 2 == 0 else (odd_ref, a - 1)
        return src[0, :, start * GRID_W:start * GRID_W + pair]

    def right(a):
        assert a % 2 == 1
        return even_ref[0, :, (a - 1) * GRID_W:(a - 1) * GRID_W + pair]

    for t in range(n_cls):
        for r in range(n_q):
            for kp in range(n_k // 2):
                a0, a1 = int(row_off[t, r, 2 * kp]), int(row_off[t, r, 2 * kp + 1])
                if a0 != masked and a1 == a0 + 1:
                    blk = left(a0)
                elif a0 == masked and a1 == masked:
                    blk = jnp.full((GRID_W, pair), NEG_INF, F32)
                elif a1 == masked:
                    blk = jnp.where(lane < GRID_W, left(a0), NEG_INF)
                else:
                    blk = jnp.where(lane >= GRID_W, right(a1), NEG_INF)
                o_ref[t, 0, r * GRID_W:(r + 1) * GRID_W, kp * pair:(kp + 1) * pair] = blk


def _na_attention(z_mix, zc_mix, rpb):
    b, n, _ = z_mix.shape
    n_ctx = zc_mix.shape[1]
    rows = n // GRID_W
    nblk = rows // NA_QROWS
    tq = NA_QROWS * GRID_W
    bias = _na_bias(rpb, rows)
    qb, kb, vb = NA_COL0 // BRANCH_W, NA_COL0 // BRANCH_W + 1, NA_COL0 // BRANCH_W + 2
    kparts = NA_KROWS * GRID_W // NA_KPART
    max_k0 = (rows - NA_KROWS) * GRID_W // NA_KPART

    def k0_of(j):
        per_step = NA_QROWS * GRID_W // NA_KPART
        lead = (NA_WIN_H // 2) * GRID_W // NA_KPART
        return jnp.clip(j * per_step - lead, 0, max_k0)

    def kspec(part, blk):
        return pl.BlockSpec((1, NA_KPART, BRANCH_W), lambda bi, j: (bi, k0_of(j) + part, blk))

    def cls(j):
        return jnp.where(j == 0, 0, jnp.where(j == nblk - 1, 2, 1))

    return pl.pallas_call(
        functools.partial(_na_kernel, kparts=kparts),
        grid=(b, nblk),
        in_specs=[pl.BlockSpec((1, tq, BRANCH_W), lambda bi, j: (bi, j, qb))]
        + [kspec(p, kb) for p in range(kparts)] + [kspec(p, vb) for p in range(kparts)]
        + [pl.BlockSpec((1, n_ctx, BRANCH_W), lambda bi, j: (bi, 0, kb)),
           pl.BlockSpec((1, n_ctx, BRANCH_W), lambda bi, j: (bi, 0, vb)),
           pl.BlockSpec((1, NA_HEADS, tq, NA_KROWS * GRID_W), lambda bi, j: (cls(j), 0, 0, 0))],
        out_specs=pl.BlockSpec((1, tq, BRANCH_W), lambda bi, j: (bi, j, 0)),
        out_shape=jax.ShapeDtypeStruct((b, n, BRANCH_W), F32),
        compiler_params=_cparams(("parallel", "arbitrary")), name="na_attn",
    )(z_mix, *([z_mix] * (2 * kparts)), zc_mix, zc_mix, bias)


def _rope_tables(n_tokens, head_dim, heads):
    t = np.arange(n_tokens)
    row = (t // GRID_W).astype(np.float64)
    col = (t % GRID_W).astype(np.float64)
    half = head_dim // 2
    inv = ROPE_BASE ** (-(np.arange(0, half, 2, dtype=np.float64) / half))
    ang = np.concatenate([row[:, None] * inv, col[:, None] * inv], axis=-1)
    cos, sin = np.cos(ang), np.sin(ang)
    cos_t = np.tile(np.concatenate([cos, cos], axis=-1), (1, heads))
    sin_t = np.tile(np.concatenate([-sin, sin], axis=-1), (1, heads))
    return jnp.asarray(cos_t, F32), jnp.asarray(sin_t, F32)


def _rope(x, cos_t, sin_t):
    w = x.shape[-1]
    half = SW_HEAD_DIM // 2
    lane = lax.broadcasted_iota(jnp.int32, x.shape, 1)
    first = (lane % SW_HEAD_DIM) < half
    partner = jnp.where(first, pltpu.roll(x, w - half, 1), pltpu.roll(x, half, 1))
    return x * cos_t + partner * sin_t


SW_QBLK = 2 * SW_BLOCK
SW_KPARTS = SW_QBLK // SW_BLOCK + 2


def _sw_kernel(q_ref, kv0, kv1, kv2, kv3, kc_ref, vc_ref, sink_ref, o_ref, *, nblk):
    i = pl.program_id(1)
    kvw = SW_KV_HEADS * SW_HEAD_DIM
    q = q_ref[0]
    kv = jnp.concatenate([kv0[0], kv1[0], kv2[0], kv3[0]], axis=0)
    kc = kc_ref[0].astype(BF16)
    vc = vc_ref[0].astype(BF16)
    g_ = SW_HEADS // SW_KV_HEADS
    rows = g_ * SW_QBLK
    span = SW_KPARTS * SW_BLOCK
    r = lax.broadcasted_iota(jnp.int32, (rows, span), 0) % SW_QBLK
    c = lax.broadcasted_iota(jnp.int32, (rows, span), 1)
    diff = c - SW_BLOCK - r
    blk = i * (SW_QBLK // SW_BLOCK) - 1 + c // SW_BLOCK
    ok = (jnp.abs(diff) <= SW_WINDOW) & (blk >= 0) & (blk < nblk)
    nt = (((1,), (1,)), ((), ()))
    outs = []
    for kvh in range(SW_KV_HEADS):
        ksl = slice(kvh * SW_HEAD_DIM, (kvh + 1) * SW_HEAD_DIM)
        vsl = slice(kvw + kvh * SW_HEAD_DIM, kvw + (kvh + 1) * SW_HEAD_DIM)
        qg = jnp.concatenate([q[:, (kvh * g_ + g) * SW_HEAD_DIM:(kvh * g_ + g + 1) * SW_HEAD_DIM] for g in range(g_)],
                             axis=0)
        s_lat = jnp.where(ok, lax.dot_general(qg, kv[:, ksl], nt, preferred_element_type=F32), NEG_INF)
        s_ctx = lax.dot_general(qg, kc[:, ksl], nt, preferred_element_type=F32)
        rr = lax.broadcasted_iota(jnp.int32, (rows, 1), 0)
        s_sink = jnp.zeros((rows, 1), F32)
        for g in range(g_):
            s_sink = jnp.where(rr // SW_QBLK == g, sink_ref[kvh * g_ + g], s_sink)
        m = jnp.maximum(_softmax_parts([s_lat, s_ctx]), s_sink)
        p_lat = jnp.exp(s_lat - m)
        p_ctx = jnp.exp(s_ctx - m)
        den = jnp.sum(p_lat, axis=-1, keepdims=True) + jnp.sum(p_ctx, axis=-1, keepdims=True) + jnp.exp(s_sink - m)
        o = jnp.dot(p_lat.astype(BF16), kv[:, vsl], preferred_element_type=F32) \
            + jnp.dot(p_ctx.astype(BF16), vc[:, ksl], preferred_element_type=F32)
        o = o / den
        outs += [o[g * SW_QBLK:(g + 1) * SW_QBLK] for g in range(g_)]
    o_ref[0] = jnp.concatenate(outs, axis=-1)


def _sw_attention(q_r, kv_r, zc_mix, sink):
    b, n, _ = q_r.shape
    n_ctx = zc_mix.shape[1]
    nblk = n // SW_BLOCK
    qw = SW_HEADS * SW_HEAD_DIM
    kvw = SW_KV_HEADS * SW_HEAD_DIM
    k_blk = (SW_COL0 + qw) // kvw
    per_q = SW_QBLK // SW_BLOCK

    def kpart(part):
        return pl.BlockSpec((1, SW_BLOCK, qw),
                            lambda bi, i: (bi, jnp.clip(i * per_q - 1 + part, 0, nblk - 1), 0))

    return pl.pallas_call(
        functools.partial(_sw_kernel, nblk=nblk),
        grid=(b, n // SW_QBLK),
        in_specs=[pl.BlockSpec((1, SW_QBLK, qw), lambda bi, i: (bi, i, 0))]
        + [kpart(p) for p in range(SW_KPARTS)]
        + [pl.BlockSpec((1, n_ctx, kvw), lambda bi, i: (bi, 0, k_blk)),
           pl.BlockSpec((1, n_ctx, kvw), lambda bi, i: (bi, 0, k_blk + 1)),
           pl.BlockSpec(memory_space=pltpu.SMEM)],
        out_specs=pl.BlockSpec((1, SW_QBLK, qw), lambda bi, i: (bi, i, 0)),
        out_shape=jax.ShapeDtypeStruct((b, n, qw), F32),
        compiler_params=_cparams(("parallel", "arbitrary")), name="sw_attn",
    )(q_r, *([kv_r] * SW_KPARTS), zc_mix, zc_mix, sink.astype(F32))


def _ctx_attn_kernel(q_ref, k_ref, v_ref, sink_ref, o_ref, *, heads, kv_heads, dh, use_sink):
    q = q_ref[0] * np.float32(dh ** -0.5)
    k = k_ref[0].astype(BF16)
    v = v_ref[0].astype(BF16)
    g_ = heads // kv_heads
    nt = (((1,), (1,)), ((), ()))
    outs = []
    for h in range(heads):
        kv = h // g_
        s = lax.dot_general(q[:, h * dh:(h + 1) * dh].astype(BF16), k[:, kv * dh:(kv + 1) * dh], nt,
                            preferred_element_type=F32)
        m = jnp.max(s, axis=-1, keepdims=True)
        if use_sink:
            m = jnp.maximum(m, sink_ref[h])
        p = jnp.exp(s - m)
        den = jnp.sum(p, axis=-1, keepdims=True)
        if use_sink:
            den = den + jnp.exp(sink_ref[h] - m)
        outs.append(jnp.dot(p.astype(BF16), v[:, kv * dh:(kv + 1) * dh], preferred_element_type=F32) / den)
    o_ref[0] = jnp.concatenate(outs, axis=-1)


def _ctx_attention(zc_mix, col0, heads, kv_heads, dh, sink):
    b, n, _ = zc_mix.shape
    qw, kvw = heads * dh, kv_heads * dh
    use_sink = sink is not None
    sink_arr = sink.astype(F32) if use_sink else jnp.zeros((heads,), F32)
    return pl.pallas_call(
        functools.partial(_ctx_attn_kernel, heads=heads, kv_heads=kv_heads, dh=dh, use_sink=use_sink),
        grid=(b,),
        in_specs=[pl.BlockSpec((1, n, qw), lambda bi: (bi, 0, col0 // qw)),
                  pl.BlockSpec((1, n, kvw), lambda bi: (bi, 0, (col0 + qw) // kvw)),
                  pl.BlockSpec((1, n, kvw), lambda bi: (bi, 0, (col0 + qw) // kvw + 1)),
                  pl.BlockSpec(memory_space=pltpu.SMEM)],
        out_specs=pl.BlockSpec((1, n, qw), lambda bi: (bi, 0, 0)),
        out_shape=jax.ShapeDtypeStruct((b, n, qw), F32),
        compiler_params=_cparams(("parallel",)), name="ctx_attn",
    )(zc_mix, zc_mix, zc_mix, sink_arr)


def _merge_kernel(x_ref, s5_ref, hy_ref, na_ref, sw_ref, gt_ref, ga_ref, wglu_ref, wb_ref, wo_ref,
                  lg_ref, lb_ref, o_ref, tok_scr):
    rows = tok_scr.shape[1] // S5_T
    for c in range(BRANCH_W // LANES):
        for t in range(S5_T):
            col = t * BRANCH_W + c * LANES
            tok_scr[c, pl.ds(t, rows, stride=S5_T), :] = s5_ref[0, :, col:col + LANES]
    g = jax.nn.gelu(jnp.concatenate([tok_scr[c] for c in range(BRANCH_W // LANES)], axis=-1))
    s5 = g * jax.nn.sigmoid(jnp.dot(g.astype(BF16), wglu_ref[...], preferred_element_type=F32))
    branches = (s5, hy_ref[0], na_ref[0], sw_ref[0])
    acc = None
    for n in range(N_BRANCH):
        proj = jnp.dot(branches[n].astype(BF16), wb_ref[n], preferred_element_type=F32)
        t = gt_ref[0, :, n * D_MODEL:(n + 1) * D_MODEL].astype(F32) * proj
        acc = t if acc is None else acc + t
    mix = jnp.dot(acc.astype(BF16), wo_ref[...], preferred_element_type=F32)
    y = np.float32(DEEPNORM_ALPHA) * x_ref[0] + ga_ref[0] * mix
    o_ref[0] = _layernorm(y) * lg_ref[...] + lb_ref[...]


def _merge(x, s5y, hy, na, sw, gates, g_a, w_glu, w_branch, w_out, ln_g, ln_b, *, tm):
    b, l, d = x.shape
    br = pl.BlockSpec((1, tm, BRANCH_W), lambda bi, i: (bi, i, 0))
    full = lambda s: pl.BlockSpec(s, lambda bi, i: (0,) * len(s))
    return pl.pallas_call(
        _merge_kernel,
        grid=(b, l // tm),
        in_specs=[pl.BlockSpec((1, tm, d), lambda bi, i: (bi, i, 0)),
                  pl.BlockSpec((1, tm // S5_T, S5_T * BRANCH_W), lambda bi, i: (bi, i, 0)), br, br, br,
                  pl.BlockSpec((1, tm, GATE_W), lambda bi, i: (bi, i, 0)),
                  pl.BlockSpec((1, 1, d), lambda bi, i: (bi, 0, 0)),
                  full((BRANCH_W, BRANCH_W)), full((N_BRANCH, BRANCH_W, d)), full((d, d)),
                  full((1, d)), full((1, d))],
        out_specs=pl.BlockSpec((1, tm, d), lambda bi, i: (bi, i, 0)),
        out_shape=jax.ShapeDtypeStruct((b, l, d), F32),
        scratch_shapes=[pltpu.VMEM((BRANCH_W // LANES, tm, LANES), F32)],
        compiler_params=_cparams(("parallel", "parallel")), name="merge",
    )(x, s5y, hy, na, sw, gates, g_a, w_glu, w_branch, w_out, ln_g, ln_b)


def _mlp_kernel(x_ref, sh_ref, sc_ref, gm_ref, w1_ref, w2_ref, lg_ref, lb_ref, o_ref, h_scr, acc_scr):
    j = pl.program_id(2)

    @pl.when(j == 0)
    def _():
        h = _layernorm(x_ref[0]) * (1.0 + sc_ref[0]) + sh_ref[0]
        h_scr[...] = h.astype(BF16)
        acc_scr[...] = jnp.zeros_like(acc_scr)

    a = jnp.dot(h_scr[...], w1_ref[...], preferred_element_type=F32)
    a = jnp.square(jnp.maximum(a, 0.0))
    acc_scr[...] += jnp.dot(a.astype(BF16), w2_ref[...], preferred_element_type=F32)

    @pl.when(j == pl.num_programs(2) - 1)
    def _():
        y = np.float32(DEEPNORM_ALPHA) * x_ref[0] + gm_ref[0] * acc_scr[...]
        o_ref[0] = _layernorm(y) * lg_ref[...] + lb_ref[...]


def _mlp(x, sh, sc, g_m, w1, w2, ln_g, ln_b, *, tm, th):
    b, l, d = x.shape
    hdim = w1.shape[1]
    mod = pl.BlockSpec((1, 1, d), lambda bi, i, j: (bi, 0, 0))
    vec = pl.BlockSpec((1, d), lambda bi, i, j: (0, 0))
    return pl.pallas_call(
        _mlp_kernel,
        grid=(b, l // tm, hdim // th),
        in_specs=[pl.BlockSpec((1, tm, d), lambda bi, i, j: (bi, i, 0)), mod, mod, mod,
                  pl.BlockSpec((d, th), lambda bi, i, j: (0, j)),
                  pl.BlockSpec((th, d), lambda bi, i, j: (j, 0)), vec, vec],
        out_specs=pl.BlockSpec((1, tm, d), lambda bi, i, j: (bi, i, 0)),
        out_shape=jax.ShapeDtypeStruct((b, l, d), F32),
        scratch_shapes=[pltpu.VMEM((tm, d), BF16), pltpu.VMEM((tm, d), F32)],
        compiler_params=_cparams(("parallel", "parallel", "arbitrary")), name="mlp",
    )(x, sh, sc, g_m, w1, w2, ln_g, ln_b)


def kernel(x, c, ctx, c_ctx, w_ada, b_ada, w_in, s5_lambda_re, s5_lambda_im, s5_log_dt, s5_b_re, s5_b_im, s5_c_re,
           s5_c_im, s5_d, s5_w_glu, hy_conv_w, hy_conv_b, hy_freq, hy_w1, hy_b1, hy_w2, hy_b2, hy_w3, hy_bias,
           na_rpb, sw_sink, w_branch, w_out, ln1_g, ln1_b, w_mlp1, w_mlp2, ln2_g, ln2_b):
    b, l, d = x.shape
    n_ctx = ctx.shape[1]
    depth = w_ada.shape[0]
    cc = jnp.zeros((8, d), F32).at[:b].set(c.astype(F32)).at[b].set(c_ctx.astype(F32))
    mod_all = _ada(cc, w_ada.astype(F32), b_ada.astype(F32))
    xc = ctx
    for layer in range(depth):
        need_ctx_out = layer < depth - 1
        mod = mod_all[layer, :b].reshape(b, 1, 6, d)
        mod_c = mod_all[layer, b].reshape(1, 1, 6, d)
        sh_a, sc_a, g_a, sh_m, sc_m, g_m = [mod[:, :, i] for i in range(6)]
        csh_a, csc_a, cg_a, csh_m, csc_m, cg_m = [mod_c[:, :, i] for i in range(6)]
        w_in_l = w_in[layer].astype(BF16)
        w_mix, w_gate = w_in_l[:, :MIX_W], w_in_l[:, MIX_W:]

        lat = _in_proj(x, sh_a, sc_a, w_mix, w_gate, tm=1024, with_rope=True)
        z_mix, gates = lat["z"], lat["gates"]
        flat = lambda a: a.reshape(1, b * a.shape[1], a.shape[2])
        unflat = lambda a: a.reshape(b, a.shape[1] // b, a.shape[2])
        con = _in_proj(flat(xc), csh_a, csc_a, w_mix, w_gate if need_ctx_out else None, tm=b * n_ctx)
        zc_mix = unflat(con["z"])

        ops = _s5_operators(s5_lambda_re[layer], s5_lambda_im[layer], s5_log_dt[layer], s5_b_re[layer],
                            s5_b_im[layer], s5_c_re[layer], s5_c_im[layer], s5_d[layer])
        yc_s5, e_ctx = _s5_mix(unflat(con["u"]), ops, [jnp.zeros((b, 1, S5_STATE_W // 4), F32)] * 4)
        y_s5, _ = _s5_mix(lat["u"], ops, e_ctx)

        hy_args = (hy_freq[layer], hy_w1[layer], hy_b1[layer], hy_w2[layer], hy_b2[layer], hy_w3[layer])
        sp_re, sp_im = _hy_spectra(_hy_filters(l, *hy_args), l)
        hy_l = _hyena(z_mix, hy_conv_w[layer], hy_conv_b[layer], sp_re, sp_im, hy_bias[layer])

        na_l = _na_attention(z_mix, zc_mix, na_rpb[layer])
        sw_l = _sw_attention(lat["q_rope"], lat["kv_rope"], zc_mix, sw_sink[layer])

        w_glu = s5_w_glu[layer].astype(BF16)
        w_br = w_branch[layer].astype(BF16)
        w_o = w_out[layer].astype(BF16)
        lg1, lb1 = ln1_g[layer].astype(F32)[None], ln1_b[layer].astype(F32)[None]
        lg2, lb2 = ln2_g[layer].astype(F32)[None], ln2_b[layer].astype(F32)[None]
        w1 = w_mlp1[layer].astype(BF16)
        w2 = w_mlp2[layer].astype(BF16)

        x_new = _merge(x, y_s5, hy_l, na_l, sw_l, gates, g_a, w_glu, w_br, w_o, lg1, lb1, tm=512)
        x_new = _mlp(x_new, sh_m, sc_m, g_m, w1, w2, lg2, lb2, tm=1024, th=2048)

        if need_ctx_out:
            spc_re, spc_im = _hy_spectra(_hy_filters(n_ctx, *hy_args), n_ctx)
            hy_c = _hyena(zc_mix, hy_conv_w[layer], hy_conv_b[layer], spc_re, spc_im, hy_bias[layer])
            na_c = _ctx_attention(zc_mix, NA_COL0, NA_HEADS, NA_HEADS, NA_HEAD_DIM, None)
            sw_c = _ctx_attention(zc_mix, SW_COL0, SW_HEADS, SW_KV_HEADS, SW_HEAD_DIM, sw_sink[layer])
            xc_new = _merge(flat(xc), flat(yc_s5), flat(hy_c), flat(na_c), flat(sw_c), con["gates"], cg_a,
                            w_glu, w_br, w_o, lg1, lb1, tm=512)
            xc = unflat(_mlp(xc_new, csh_m, csc_m, cg_m, w1, w2, lg2, lb2, tm=b * n_ctx, th=1024))
        x = x_new
    return x
```

```python
import functools
import math

import numpy as np
import jax
import jax.numpy as jnp
from jax import lax
from jax.experimental import pallas as pl
from jax.experimental.pallas import tpu as pltpu

F32 = jnp.float32
BF16 = jnp.bfloat16

D_MODEL = 1024
GRID_W = 64
BRANCH_W = 256
N_BRANCH = 4
S5_GROUP = 16
S5_GROUPS = 16
S5_STATE = 64
HY_WIDTH = 256
HY_ORDER = 2
HY_BANDS = 16
HY_EMB = 2 * HY_BANDS + 1
HY_FFN = 64
HY_MIN_DECAY = math.log(1e-2) / 1.5
HY_MAX_DECAY = math.log(1e-2) / 0.3
NA_HEADS = 4
NA_HEAD_DIM = 64
NA_WIN_H = 8
NA_WIN_W = 16
SW_HEADS = 4
SW_KV_HEADS = 2
SW_HEAD_DIM = 64
SW_WINDOW = 128
SW_BLOCK = 128
MLP_HIDDEN = 4 * D_MODEL
ROPE_BASE = 10000.0
LN_EPS = 1e-6
NEG_INF = -1e30
DEPTH = 2
DEEPNORM_ALPHA = (2 * DEPTH) ** 0.25

MIX_W = 2304
NA_COL0 = BRANCH_W + 3 * HY_WIDTH
SW_COL0 = NA_COL0 + 3 * BRANCH_W
GATE_W = N_BRANCH * D_MODEL
S5_T = 8
S5_STATE_W = 4 * S5_GROUPS * S5_STATE
SUBLANES = 8
LANES = 128
VMEM_LIMIT = 56 * 1024 * 1024

HI = lax.Precision.HIGHEST


def _cparams(sem):
    return pltpu.CompilerParams(dimension_semantics=sem, vmem_limit_bytes=VMEM_LIMIT)


def _layernorm(x):
    mu = jnp.mean(x, axis=-1, keepdims=True)
    xc = x - mu
    var = jnp.mean(xc * xc, axis=-1, keepdims=True)
    return xc * lax.rsqrt(var + LN_EPS)


def _ada_kernel(c_ref, w_ref, b_ref, o_ref):
    c = c_ref[...]
    a = c * jax.nn.sigmoid(c)
    o_ref[0] = jnp.dot(a, w_ref[0], preferred_element_type=F32, precision=HI) + b_ref[0]


def _ada(cc, w_ada, b_ada):
    depth, d, n = w_ada.shape
    tn = 512
    return pl.pallas_call(
        _ada_kernel,
        grid=(depth, n // tn),
        in_specs=[pl.BlockSpec((8, d), lambda l, j: (0, 0)),
                  pl.BlockSpec((1, d, tn), lambda l, j: (l, 0, j)),
                  pl.BlockSpec((1, 1, tn), lambda l, j: (l, 0, j))],
        out_specs=pl.BlockSpec((1, 8, tn), lambda l, j: (l, 0, j)),
        out_shape=jax.ShapeDtypeStruct((depth, 8, n), F32),
        compiler_params=_cparams(("parallel", "parallel")), name="ada",
    )(cc, w_ada, b_ada.reshape(depth, 1, n))


IN_MIX_TN = MIX_W // 2
IN_GATE_TN = GATE_W // 2


def _in_proj_kernel(x_ref, sh_ref, sc_ref, wm_ref, *rest, n_mix, with_gates, with_rope):
    rest = list(rest)
    wg_ref = rest.pop(0) if with_gates else None
    cos_ref, sin_ref = (rest.pop(0), rest.pop(0)) if with_rope else (None, None)
    z_ref, u_ref = rest.pop(0), rest.pop(0)
    g_ref = rest.pop(0) if with_gates else None
    qr_ref, kvr_ref = (rest.pop(0), rest.pop(0)) if with_rope else (None, None)
    tok_scr, h_scr = rest
    j = pl.program_id(2)

    @pl.when(j == 0)
    def _():
        h = _layernorm(x_ref[0]) * (1.0 + sc_ref[0]) + sh_ref[0]
        h_scr[...] = h.astype(BF16)

    @pl.when(j < n_mix)
    def _():
        r = jnp.dot(h_scr[...], wm_ref[...], preferred_element_type=F32)
        z_ref[0] = r

        @pl.when(j == 0)
        def _():
            rows = tok_scr.shape[1] // S5_T
            for c in range(BRANCH_W // LANES):
                tok_scr[c] = r[:, c * LANES:(c + 1) * LANES]
                for t in range(S5_T):
                    col = t * BRANCH_W + c * LANES
                    u_ref[0, :, col:col + LANES] = tok_scr[c, pl.ds(t, rows, stride=S5_T), :]

        if with_rope:
            @pl.when(j == n_mix - 1)
            def _():
                qw = SW_HEADS * SW_HEAD_DIM
                q0 = SW_COL0 - (n_mix - 1) * IN_MIX_TN
                cos_t, sin_t = cos_ref[...], sin_ref[...]
                q = _rope(r[:, q0:q0 + qw], cos_t, sin_t) * np.float32(SW_HEAD_DIM ** -0.5)
                qr_ref[0] = q.astype(BF16)
                kv = r[:, q0 + qw:q0 + 2 * qw]
                lane = lax.broadcasted_iota(jnp.int32, kv.shape, 1)
                kvr_ref[0] = jnp.where(lane < SW_KV_HEADS * SW_HEAD_DIM, _rope(kv, cos_t, sin_t), kv).astype(BF16)

    if with_gates:
        @pl.when(j >= n_mix)
        def _():
            r = jnp.dot(h_scr[...], wg_ref[...], preferred_element_type=F32)
            g_ref[0] = jax.nn.sigmoid(r).astype(g_ref.dtype)


def _in_proj(x, sh, sc, w_mix, w_gate, *, tm, with_rope=False):
    b, l, d = x.shape
    with_gates = w_gate is not None
    n_mix = MIX_W // IN_MIX_TN
    n_gate = GATE_W // IN_GATE_TN if with_gates else 0
    mix_j = lambda j: jnp.minimum(j, n_mix - 1)
    gate_j = lambda j: jnp.maximum(j - n_mix, 0)
    in_specs = [pl.BlockSpec((1, tm, d), lambda bi, i, j: (bi, i, 0)),
                pl.BlockSpec((1, 1, d), lambda bi, i, j: (bi, 0, 0)),
                pl.BlockSpec((1, 1, d), lambda bi, i, j: (bi, 0, 0)),
                pl.BlockSpec((d, IN_MIX_TN), lambda bi, i, j: (0, mix_j(j)))]
    out_specs = [pl.BlockSpec((1, tm, IN_MIX_TN), lambda bi, i, j: (bi, i, mix_j(j))),
                 pl.BlockSpec((1, tm // S5_T, S5_T * BRANCH_W), lambda bi, i, j: (bi, i, 0))]
    out_shape = [jax.ShapeDtypeStruct((b, l, MIX_W), F32),
                 jax.ShapeDtypeStruct((b, l // S5_T, S5_T * BRANCH_W), F32)]
    args = [x, sh, sc, w_mix]
    names = ["z", "u"]
    if with_gates:
        in_specs.append(pl.BlockSpec((d, IN_GATE_TN), lambda bi, i, j: (0, gate_j(j))))
        out_specs.append(pl.BlockSpec((1, tm, IN_GATE_TN), lambda bi, i, j: (bi, i, gate_j(j))))
        out_shape.append(jax.ShapeDtypeStruct((b, l, GATE_W), BF16))
        args.append(w_gate)
        names.append("gates")
    if with_rope:
        qw = SW_HEADS * SW_HEAD_DIM
        in_specs += [pl.BlockSpec((tm, qw), lambda bi, i, j: (i, 0))] * 2
        out_specs += [pl.BlockSpec((1, tm, qw), lambda bi, i, j: (bi, i, 0))] * 2
        out_shape += [jax.ShapeDtypeStruct((b, l, qw), BF16)] * 2
        args += list(_rope_tables(l, SW_HEAD_DIM, SW_HEADS))
        names += ["q_rope", "kv_rope"]
    outs = pl.pallas_call(
        functools.partial(_in_proj_kernel, n_mix=n_mix, with_gates=with_gates, with_rope=with_rope),
        grid=(b, l // tm, n_mix + n_gate),
        in_specs=in_specs,
        out_specs=out_specs,
        out_shape=out_shape,
        scratch_shapes=[pltpu.VMEM((BRANCH_W // LANES, tm, LANES), F32), pltpu.VMEM((tm, d), BF16)],
        compiler_params=_cparams(("parallel", "parallel", "arbitrary")), name="in_proj",
    )(*args)
    return dict(zip(names, outs))


def _s5_inc_kernel(u_ref, q_ref, o_ref):
    o_ref[...] = jnp.dot(u_ref[...].astype(BF16), q_ref[...], preferred_element_type=F32)


def _s5_increments(uu, q_all, *, tm, tn):
    m, k = uu.shape
    n = q_all.shape[1]
    return pl.pallas_call(
        _s5_inc_kernel,
        grid=(m // tm, n // tn),
        in_specs=[pl.BlockSpec((tm, k), lambda i, j: (i, 0)), pl.BlockSpec((k, tn), lambda i, j: (0, j))],
        out_specs=pl.BlockSpec((tm, tn), lambda i, j: (i, j)),
        out_shape=jax.ShapeDtypeStruct((m, n), F32),
        compiler_params=_cparams(("parallel", "parallel")), name="s5_inc",
    )(uu, q_all)


def _s5_out_kernel(u_ref, h0, h1, h2, h3, m_ref, n0, n1, n2, n3, o_ref):
    nt = (((1,), (1,)), ((), ()))
    acc = jnp.dot(u_ref[...].astype(BF16), m_ref[...], preferred_element_type=F32)
    for h_ref, n_ref in ((h0, n0), (h1, n1), (h2, n2), (h3, n3)):
        acc += lax.dot_general(h_ref[...].astype(BF16), n_ref[...], nt, preferred_element_type=F32)
    o_ref[...] = acc


def _s5_outputs(uu, h, m_tot, n_t, *, tm, tn):
    m, k = uu.shape
    gp = h[0].shape[1]
    n = m_tot.shape[1]
    return pl.pallas_call(
        _s5_out_kernel,
        grid=(m // tm, n // tn),
        in_specs=[pl.BlockSpec((tm, k), lambda i, j: (i, 0))]
        + [pl.BlockSpec((tm, gp), lambda i, j: (i, 0))] * 4
        + [pl.BlockSpec((k, tn), lambda i, j: (0, j))]
        + [pl.BlockSpec((tn, gp), lambda i, j, c=c: (j, c)) for c in range(4)],
        out_specs=pl.BlockSpec((tm, tn), lambda i, j: (i, j)),
        out_shape=jax.ShapeDtypeStruct((m, n), F32),
        compiler_params=_cparams(("parallel", "parallel")), name="s5_out",
    )(uu, *h, m_tot, n_t, n_t, n_t, n_t)


def _s5_scan_kernel(gfr, gfi, gbr, gbi, afr, afi, abr, abi, h0fr, h0fi, h0br, h0bi,
                    hfr, hfi, hbr, hbi, efr, efi, ebr, ebi, *, n_chunks):
    a_fr = afr[...][None]
    a_fi = afi[...][None]
    a_br = abr[...][None]
    a_bi = abi[...][None]

    def body(k, carry):
        sfr, sfi, sbr, sbi = carry
        kb = n_chunks - 1 - k
        hfr[:, pl.ds(k, 1), :] = sfr
        hfi[:, pl.ds(k, 1), :] = sfi
        hbr[:, pl.ds(kb, 1), :] = sbr
        hbi[:, pl.ds(kb, 1), :] = sbi
        nfr = a_fr * sfr - a_fi * sfi + gfr[:, pl.ds(k, 1), :]
        nfi = a_fr * sfi + a_fi * sfr + gfi[:, pl.ds(k, 1), :]
        nbr = a_br * sbr - a_bi * sbi + gbr[:, pl.ds(kb, 1), :]
        nbi = a_br * sbi + a_bi * sbr + gbi[:, pl.ds(kb, 1), :]
        return nfr, nfi, nbr, nbi

    sfr, sfi, sbr, sbi = lax.fori_loop(0, n_chunks, body, (h0fr[...], h0fi[...], h0br[...], h0bi[...]), unroll=4)
    efr[...] = sfr
    efi[...] = sfi
    ebr[...] = sbr
    ebi[...] = sbi


def _s5_scan(g, a_t, h0):
    b, k, w4 = g.shape
    w = 2 * LANES
    q = w4 // 4
    nb = q // w

    def comp(c):
        return pl.BlockSpec((b, k, w), lambda j, c=c: (0, 0, c * nb + j))

    def comp_a(c):
        return pl.BlockSpec((1, w), lambda j, c=c: (0, c * nb + j))

    state = pl.BlockSpec((b, k, w), lambda j: (0, 0, j))
    edge = pl.BlockSpec((b, 1, w), lambda j: (0, 0, j))
    outs = pl.pallas_call(
        functools.partial(_s5_scan_kernel, n_chunks=k),
        grid=(nb,),
        in_specs=[comp(c) for c in range(4)] + [comp_a(c) for c in range(4)] + [edge] * 4,
        out_specs=[state] * 4 + [edge] * 4,
        out_shape=[jax.ShapeDtypeStruct((b, k, q), F32)] * 4 + [jax.ShapeDtypeStruct((b, 1, q), F32)] * 4,
        compiler_params=_cparams(("parallel",)), name="s5_scan",
    )(g, g, g, g, a_t, a_t, a_t, a_t, *h0)
    return outs[:4], outs[4:]


def _cmul(ar, ai, br, bi):
    return ar * br - ai * bi, ar * bi + ai * br


def _s5_operators(lam_re, lam_im, log_dt, b_re, b_im, c_re, c_im, d):
    t_ = S5_T
    g_, p_, n_ = S5_GROUPS, S5_STATE, S5_GROUP
    gp, w_ = g_ * p_, g_ * n_
    lr, li = lam_re.astype(F32), lam_im.astype(F32)
    dt = jnp.exp(log_dt.astype(F32))[..., None]
    ks = jnp.arange(t_ + 1, dtype=F32)[:, None, None, None]
    mag = jnp.exp(ks * lr * dt)
    pw_re = mag * jnp.cos(ks * li * dt)
    pw_im = mag * jnp.sin(ks * li * dt)
    a_re, a_im = pw_re[1], pw_im[1]
    den = lr ** 2 + li ** 2
    f_re = ((a_re - 1.0) * lr + a_im * li) / den
    f_im = (a_im * lr - (a_re - 1.0) * li) / den
    br, bi = b_re.astype(F32), b_im.astype(F32)
    bb_re = f_re[..., None] * br - f_im[..., None] * bi
    bb_im = f_re[..., None] * bi + f_im[..., None] * br
    cr, ci = c_re.astype(F32), c_im.astype(F32)
    wb_re, wb_im = _cmul(pw_re[:t_, ..., None], pw_im[:t_, ..., None], bb_re[None], bb_im[None])
    kern = jnp.einsum('dgnp,kdgpm->dkgnm', cr, wb_re, precision=HI) \
        - jnp.einsum('dgnp,kdgpm->dkgnm', ci, wb_im, precision=HI)
    k_tile = jnp.tile(kern.transpose(0, 1, 2, 4, 3).reshape(2, t_, w_, n_), (1, 1, 1, g_))

    def rows_tiled(x):
        return jnp.tile(x.transpose(0, 3, 1, 2).reshape(2, n_, gp), (1, g_, 1))

    bt_re, bt_im = rows_tiled(bb_re), rows_tiled(bb_im)
    ct_re, ct_im = rows_tiled(cr.transpose(0, 1, 3, 2)), rows_tiled(ci.transpose(0, 1, 3, 2))
    pr = pw_re.transpose(1, 0, 2, 3).reshape(2, t_ + 1, gp)
    pi_ = pw_im.transpose(1, 0, 2, 3).reshape(2, t_ + 1, gp)
    full = lambda s: pl.BlockSpec(s, lambda t: (0,) * len(s))
    m_tot, q_all, n_t = pl.pallas_call(
        _s5_ops_kernel,
        grid=(t_,),
        in_specs=[full((2, t_ + 1, gp))] * 2 + [full((2, w_, gp))] * 4 + [full((2, t_, w_, w_)), full((1, w_))],
        out_specs=[pl.BlockSpec((w_, t_ * w_), lambda t: (t, 0)),
                   pl.BlockSpec((w_, 4 * gp), lambda t: (t, 0)),
                   pl.BlockSpec((w_, 4 * gp), lambda t: (t, 0))],
        out_shape=[jax.ShapeDtypeStruct((t_ * w_, t_ * w_), BF16),
                   jax.ShapeDtypeStruct((t_ * w_, 4 * gp), BF16),
                   jax.ShapeDtypeStruct((t_ * w_, 4 * gp), BF16)],
        compiler_params=_cparams(("parallel",)), name="s5_ops",
    )(pr, pi_, bt_re, bt_im, ct_re, ct_im, k_tile, d.astype(F32)[None])
    a_t = jnp.concatenate([pr[0, t_], pi_[0, t_], pr[1, t_], pi_[1, t_]])[None]
    return m_tot, q_all, n_t, a_t


def _s5_ops_kernel(pr_ref, pi_ref, btr_ref, bti_ref, ctr_ref, cti_ref, kt_ref, d_ref, m_ref, q_ref, nt_ref):
    t = pl.program_id(0)
    t_ = S5_T
    w_, gp = btr_ref.shape[1], btr_ref.shape[2]
    row_g = lax.broadcasted_iota(jnp.int32, (w_, gp), 0) // S5_GROUP
    col_g = lax.broadcasted_iota(jnp.int32, (w_, gp), 1) // S5_STATE
    same = row_g == col_g
    for dr in range(2):
        e_q = (t_ - 1 - t) if dr == 0 else t
        e_n = (t + 1) if dr == 0 else (t_ - t)
        b_re = jnp.where(same, btr_ref[dr], 0.0)
        b_im = jnp.where(same, bti_ref[dr], 0.0)
        q_re, q_im = _cmul(b_re, b_im, pr_ref[dr, pl.ds(e_q, 1), :], pi_ref[dr, pl.ds(e_q, 1), :])
        c_re = jnp.where(same, ctr_ref[dr], 0.0)
        c_im = jnp.where(same, cti_ref[dr], 0.0)
        n_re, n_im = _cmul(c_re, c_im, pr_ref[dr, pl.ds(e_n, 1), :], pi_ref[dr, pl.ds(e_n, 1), :])
        base = 2 * dr * gp
        q_ref[:, base:base + gp] = q_re.astype(BF16)
        q_ref[:, base + gp:base + 2 * gp] = q_im.astype(BF16)
        nt_ref[:, base:base + gp] = n_re.astype(BF16)
        nt_ref[:, base + gp:base + 2 * gp] = (-n_im).astype(BF16)
    r2 = lax.broadcasted_iota(jnp.int32, (w_, w_), 0)
    c2 = lax.broadcasted_iota(jnp.int32, (w_, w_), 1)
    same2 = (r2 // S5_GROUP) == (c2 // S5_GROUP)
    skip = jnp.where(r2 == c2, d_ref[...], 0.0)
    for i in range(t_):
        k_f = kt_ref[0, jnp.maximum(i - t, 0)] * jnp.where(i >= t, 1.0, 0.0)
        k_b = kt_ref[1, jnp.maximum(t - i, 0)] * jnp.where(t >= i, 1.0, 0.0)
        blk = jnp.where(same2, k_f + k_b, 0.0) + skip * jnp.where(t == i, 1.0, 0.0)
        m_ref[:, i * w_:(i + 1) * w_] = blk.astype(BF16)


def _s5_mix(u, ops, h0):
    m_tot, q_all, n_t, a_t = ops
    b, k, tw = u.shape
    rows = b * k
    uu = u.reshape(rows, tw)
    tm = min(rows, 512)
    g = _s5_increments(uu, q_all, tm=tm, tn=2048)
    h, e = _s5_scan(g.reshape(b, k, S5_STATE_W), a_t, h0)
    y = _s5_outputs(uu, [hc.reshape(rows, -1) for hc in h], m_tot, n_t, tm=tm, tn=512)
    return y.reshape(b, k, tw), e


def _hy_filter_kernel(f_ref, w1_ref, b1_ref, w2_ref, b2_ref, w3_ref, fr_ref, dl_ref, o_ref, *, half_len, tile):
    feats = f_ref[...]
    h = jnp.dot(feats, w1_ref[...], preferred_element_type=F32, precision=HI) + b1_ref[...]
    h = jnp.sin(fr_ref[0:1, :] * h)
    h = jnp.dot(h, w2_ref[...], preferred_element_type=F32, precision=HI) + b2_ref[...]
    h = jnp.sin(fr_ref[1:2, :] * h)
    o = jnp.dot(h.astype(BF16), w3_ref[...].astype(BF16), preferred_element_type=F32)
    t = feats[:, 0:1]
    o = o * jnp.exp(-t * dl_ref[...])
    nw = HY_ORDER * HY_WIDTH
    n = pl.program_id(0) * tile + lax.broadcasted_iota(jnp.int32, (tile, 1), 0)
    k = jnp.where(n < half_len, o[:, :nw], jnp.where(n == half_len, 0.0, o[:, nw:]))
    o_ref[...] = k.reshape(o_ref.shape)


def _hy_filters(n_tokens, freq, w1, b1, w2, b2, w3):
    n2 = 2 * n_tokens
    idx = np.arange(n2, dtype=np.float64)
    pos = np.where(idx <= n_tokens, idx, n2 - idx)[:, None]
    t = pos / max(n_tokens - 1, 1)
    bands = np.linspace(1e-4, HY_BANDS - 1, HY_BANDS)[None]
    ang = 2.0 * math.pi * bands * pos / n_tokens
    feats = np.concatenate([t, np.cos(ang), -np.sin(ang)], axis=-1)
    kpad = LANES - HY_EMB
    feats = jnp.asarray(np.pad(feats, ((0, 0), (0, kpad))), F32)
    w1p = jnp.pad(w1.astype(F32), ((0, kpad), (0, 0)))
    deltas = jnp.abs(jnp.linspace(HY_MIN_DECAY, HY_MAX_DECAY, HY_WIDTH, dtype=F32))
    dl = jnp.tile(deltas, 2 * HY_ORDER)[None]
    tile = min(n2, 1024)
    nw = 2 * HY_ORDER * HY_WIDTH
    full = lambda s: pl.BlockSpec(s, lambda i: (0,) * len(s))
    return pl.pallas_call(
        functools.partial(_hy_filter_kernel, half_len=n_tokens, tile=tile),
        grid=(n2 // tile,),
        in_specs=[pl.BlockSpec((tile, LANES), lambda i: (i, 0)),
                  full((LANES, HY_FFN)), full((1, HY_FFN)), full((HY_FFN, HY_FFN)), full((1, HY_FFN)),
                  full((HY_FFN, nw)), full((2, HY_FFN)), full((1, nw))],
        out_specs=pl.BlockSpec((tile // SUBLANES, SUBLANES, HY_ORDER * HY_WIDTH), lambda i: (i, 0, 0)),
        out_shape=jax.ShapeDtypeStruct((n2 // SUBLANES, SUBLANES, HY_ORDER * HY_WIDTH), F32),
        compiler_params=_cparams(("parallel",)), name="hy_filter",
    )(feats, w1p, b1.astype(F32)[None], w2.astype(F32), b2.astype(F32)[None], w3.astype(F32),
      freq.astype(F32), dl)


FFT_G = 4
FFT_UNROLL = 8


def _fft_tables(n):
    s = n // SUBLANES
    nst = int(round(math.log2(s)))
    runs = []
    half = s // 2
    while half >= FFT_G:
        runs.append(-2 * np.pi * np.arange(half) / (2 * half))
        half //= 2
    ang_s = np.concatenate(runs)
    tw_slab = np.stack([np.cos(ang_s), np.sin(ang_s)]).astype(np.float32)
    tw_slab = np.broadcast_to(tw_slab[..., None, None], (2, ang_s.size, SUBLANES, LANES)).copy()
    pos = np.arange(s)
    rev = np.zeros(s, np.int64)
    for bit in range(nst):
        rev |= ((pos >> bit) & 1) << (nst - 1 - bit)
    ang = -2 * np.pi * (rev[:, None] * np.arange(SUBLANES)[None, :]) / n
    tw_mid = np.stack([np.cos(ang), np.sin(ang)]).astype(np.float32)
    tw_mid = np.broadcast_to(tw_mid[..., None], (2, s, SUBLANES, LANES)).copy()
    t = np.arange(n)
    ang2 = -2 * np.pi * t / (2 * n)
    mod = np.stack([np.cos(ang2), np.sin(ang2)]).astype(np.float32).reshape(2, s, SUBLANES)
    mod = np.broadcast_to(mod[..., None], (2, s, SUBLANES, LANES)).copy()
    return jnp.asarray(tw_slab), jnp.asarray(tw_mid), jnp.asarray(mod)


def _sub_patterns():
    sub = lax.broadcasted_iota(jnp.int32, (SUBLANES, LANES), 0)

    def table(vals):
        out = jnp.full((SUBLANES, LANES), vals[0], F32)
        for k in range(1, SUBLANES):
            out = jnp.where(sub == k, np.float32(vals[k]), out)
        return out

    pats = {}
    for dist in (4, 2, 1):
        lo = (sub & dist) == 0
        sgn = jnp.where(lo, 1.0, -1.0).astype(F32)
        wr = [1.0] * SUBLANES
        wi = [0.0] * SUBLANES
        for k in range(SUBLANES):
            if k & dist:
                e = (k % dist) * (SUBLANES // (2 * dist))
                wr[k] = math.cos(-2 * math.pi * e / SUBLANES)
                wi[k] = math.sin(-2 * math.pi * e / SUBLANES)
        pats[dist] = (lo, sgn, table(wr), table(wi))
    pats["quarter"] = (sub & 3) == 3
    return pats


def _dft8_fwd(vr, vi, pats):
    for dist in (4, 2, 1):
        lo, sgn, wr, wi = pats[dist]
        up_r = pltpu.roll(vr, SUBLANES - dist, 0)
        up_i = pltpu.roll(vi, SUBLANES - dist, 0)
        if dist == 4:
            pr, pi_ = up_r, up_i
        else:
            pr = jnp.where(lo, up_r, pltpu.roll(vr, dist, 0))
            pi_ = jnp.where(lo, up_i, pltpu.roll(vi, dist, 0))
        tr = pr + sgn * vr
        ti = pi_ + sgn * vi
        if dist == 1:
            vr, vi = tr, ti
        elif dist == 2:
            qt = pats["quarter"]
            vr, vi = jnp.where(qt, ti, tr), jnp.where(qt, -tr, ti)
        else:
            vr, vi = _cmul(tr, ti, wr, wi)
    return vr, vi


def _dft8_inv(vr, vi, pats):
    for dist in (1, 2, 4):
        lo, sgn, wr, wi = pats[dist]
        if dist == 2:
            qt = pats["quarter"]
            vr, vi = jnp.where(qt, -vi, vr), jnp.where(qt, vr, vi)
        elif dist == 4:
            vr, vi = _cmul(vr, vi, wr, -wi)
        up_r = pltpu.roll(vr, SUBLANES - dist, 0)
        up_i = pltpu.roll(vi, SUBLANES - dist, 0)
        if dist == 4:
            pr, pi_ = up_r, up_i
        else:
            pr = jnp.where(lo, up_r, pltpu.roll(vr, dist, 0))
            pi_ = jnp.where(lo, up_i, pltpu.roll(vi, dist, 0))
        vr = pr + sgn * vr
        vi = pi_ + sgn * vi
    return vr, vi


def _slab_stage(re, im, tw, half, slabs, inverse):
    per_block = half // FFT_G
    nblk = slabs // (2 * half)
    unroll = min(FFT_UNROLL, nblk * per_block)
    tw_off = slabs - 2 * half
    if nblk >= unroll:
        blocks_per_it = unroll // per_block
        trips = nblk // blocks_per_it
        offsets = [(b * 2 * half + jc * FFT_G, jc * FFT_G) for b in range(blocks_per_it) for jc in range(per_block)]
        data_step, tw_step = blocks_per_it * 2 * half, 0
    else:
        chunks_per_it = unroll // nblk
        trips = per_block // chunks_per_it
        offsets = [(b * 2 * half + k * FFT_G, k * FFT_G) for b in range(nblk) for k in range(chunks_per_it)]
        data_step = tw_step = chunks_per_it * FFT_G

    def body(c, carry):
        d0 = pl.multiple_of(c * data_step, FFT_G)
        t0 = pl.multiple_of(tw_off + c * tw_step, FFT_G)
        twiddles = {}
        for d_off, t_off in offsets:
            if t_off not in twiddles:
                tws = pl.ds(t0 + t_off, FFT_G)
                twiddles[t_off] = (tw[0, tws], tw[1, tws])
            wr, wi = twiddles[t_off]
            lo = pl.ds(d0 + d_off, FFT_G)
            hi = pl.ds(d0 + d_off + half, FFT_G)
            ar, ai, br, bi = re[lo], im[lo], re[hi], im[hi]
            if inverse:
                br, bi = br * wr + bi * wi, bi * wr - br * wi
                re[lo] = ar + br
                im[lo] = ai + bi
                re[hi] = ar - br
                im[hi] = ai - bi
            else:
                re[lo] = ar + br
                im[lo] = ai + bi
                dr, di = ar - br, ai - bi
                re[hi] = dr * wr - di * wi
                im[hi] = dr * wi + di * wr
        return carry

    lax.fori_loop(0, trips, body, 0)


def _fft_forward_big(re, im, tw, slabs):
    half = slabs // 2
    while half >= 4:
        _slab_stage(re, im, tw, half, slabs, inverse=False)
        half //= 2


def _fft_inverse_big(re, im, tw, slabs):
    half = 4
    while half <= slabs // 2:
        _slab_stage(re, im, tw, half, slabs, inverse=True)
        half *= 2


def _radix4_fwd(x):
    (x0r, x0i), (x1r, x1i), (x2r, x2i), (x3r, x3i) = x
    y0r, y0i = x0r + x2r, x0i + x2i
    y2r, y2i = x0r - x2r, x0i - x2i
    y1r, y1i = x1r + x3r, x1i + x3i
    dr, di = x1r - x3r, x1i - x3i
    y3r, y3i = di, -dr
    return [(y0r + y1r, y0i + y1i), (y0r - y1r, y0i - y1i), (y2r + y3r, y2i + y3i), (y2r - y3r, y2i - y3i)]


def _radix4_inv(z):
    (z0r, z0i), (z1r, z1i), (z2r, z2i), (z3r, z3i) = z
    y0r, y0i = z0r + z1r, z0i + z1i
    y1r, y1i = z0r - z1r, z0i - z1i
    y2r, y2i = z2r + z3r, z2i + z3i
    y3r, y3i = z2r - z3r, z2i - z3i
    qr, qi = -y3i, y3r
    return [(y0r + y2r, y0i + y2i), (y1r + qr, y1i + qi), (y0r - y2r, y0i - y2i), (y1r - qr, y1i - qi)]


def _fft_middle(re, im, twm_ref, slabs, pats, spec=None, out=None):
    def body(q, carry):
        p0 = q * FFT_G
        x = [(re[p0 + g], im[p0 + g]) for g in range(FFT_G)]
        z = _radix4_fwd(x)
        res = []
        for g in range(FFT_G):
            twr = twm_ref[0, p0 + g]
            twi = twm_ref[1, p0 + g]
            vr, vi = _cmul(z[g][0], z[g][1], twr, twi)
            vr, vi = _dft8_fwd(vr, vi, pats)
            if spec is None:
                out[0][0, 0, p0 + g] = vr * out[2]
                out[1][0, 0, p0 + g] = vi * out[2]
            else:
                vr, vi = _cmul(vr, vi, spec[0][0, 0, p0 + g], spec[1][0, 0, p0 + g])
                vr, vi = _dft8_inv(vr, vi, pats)
                res.append(_cmul(vr, vi, twr, -twi))
        if spec is not None:
            x = _radix4_inv(res)
            for g in range(FFT_G):
                re[p0 + g] = x[g][0]
                im[p0 + g] = x[g][1]
        return carry

    lax.fori_loop(0, slabs // FFT_G, body, 0)


def _hy_spec_kernel(tw_ref, k_ref, twm_ref, mod_ref, sr_ref, si_ref, re, im, *, slabs):
    h = pl.program_id(2)
    pats = _sub_patterns()
    a = k_ref[0:slabs]
    b = k_ref[slabs:2 * slabs]

    @pl.when(h == 0)
    def _():
        re[...] = a + b
        im[...] = jnp.zeros_like(a)

    @pl.when(h == 1)
    def _():
        dlt = a - b
        re[...] = dlt * mod_ref[0]
        im[...] = dlt * mod_ref[1]

    _fft_forward_big(re, im, tw_ref, slabs)
    scale = np.float32(1.0 / (2 * SUBLANES * slabs))
    _fft_middle(re, im, twm_ref, slabs, pats, out=(sr_ref, si_ref, scale))


def _hy_spectra(k, n_tokens):
    slabs = n_tokens // SUBLANES
    tw, twm, mod = _fft_tables(n_tokens)
    nt = HY_WIDTH // LANES
    shape = jax.ShapeDtypeStruct((HY_ORDER, 2, slabs, SUBLANES, HY_WIDTH), F32)
    spec_out = pl.BlockSpec((1, 1, slabs, SUBLANES, LANES), lambda o, j, h: (o, h, 0, 0, j))
    tab = pl.BlockSpec((2, slabs, SUBLANES, LANES), lambda o, j, h: (0, 0, 0, 0), pipeline_mode=pl.Buffered(1))
    return pl.pallas_call(
        functools.partial(_hy_spec_kernel, slabs=slabs),
        grid=(HY_ORDER, nt, 2),
        in_specs=[pl.BlockSpec(tw.shape, lambda o, j, h: (0, 0, 0, 0), pipeline_mode=pl.Buffered(1)),
                  pl.BlockSpec((2 * slabs, SUBLANES, LANES), lambda o, j, h: (0, 0, o * nt + j)),
                  tab, tab],
        out_specs=[spec_out, spec_out],
        out_shape=[shape, shape],
        scratch_shapes=[pltpu.VMEM((slabs, SUBLANES, LANES), F32)] * 2,
        compiler_params=_cparams(("parallel", "parallel", "arbitrary")), name="hy_spec",
    )(tw, k, twm, mod)


def _hy_conv_kernel(tw_ref, u_ref, g_ref, sr_ref, si_ref, twm_ref, mod_ref, b_ref, o_ref, re, im, *, slabs):
    h = pl.program_id(2)
    pats = _sub_patterns()
    slab_shape = (slabs, SUBLANES, LANES)
    n = slabs * SUBLANES

    @pl.when(h == 0)
    def _():
        re[...] = u_ref[0].reshape(slab_shape)
        im[...] = u_ref[1].reshape(slab_shape)

    @pl.when(h == 1)
    def _():
        ur, ui = u_ref[0].reshape(slab_shape), u_ref[1].reshape(slab_shape)
        mr, mi = mod_ref[0], mod_ref[1]
        re[...] = ur * mr - ui * mi
        im[...] = ur * mi + ui * mr

    _fft_forward_big(re, im, tw_ref, slabs)
    _fft_middle(re, im, twm_ref, slabs, pats, spec=(sr_ref, si_ref))
    _fft_inverse_big(re, im, tw_ref, slabs)

    @pl.when(h == 0)
    def _():
        o_ref[0] = re[...].reshape(n, LANES)
        o_ref[1] = im[...].reshape(n, LANES)

    @pl.when(h == 1)
    def _():
        yr, yi = re[...], im[...]
        mr, mi = mod_ref[0], mod_ref[1]
        y0 = o_ref[0] + (yr * mr + yi * mi).reshape(n, LANES)
        y1 = o_ref[1] + (yi * mr - yr * mi).reshape(n, LANES)
        o_ref[0] = g_ref[0].astype(F32) * (y0 + b_ref[...] * u_ref[0])
        o_ref[1] = g_ref[1].astype(F32) * (y1 + b_ref[...] * u_ref[1])


def _hy_order(u, u_blk, gate, gate_blk, spec_re, spec_im, order, bias):
    b, n, _ = u.shape
    w = HY_WIDTH
    slabs = n // SUBLANES
    tw, twm, mod = _fft_tables(n)
    nt = w // LANES
    tab = pl.BlockSpec((2, slabs, SUBLANES, LANES), lambda j, p, h: (0, 0, 0, 0), pipeline_mode=pl.Buffered(1))
    spec_in = pl.BlockSpec((1, 1, slabs, SUBLANES, LANES), lambda j, p, h: (order, h, 0, 0, j))

    def io(blk):
        return pl.BlockSpec((2, n, LANES), lambda j, p, h: (p, 0, blk * nt + j))

    return pl.pallas_call(
        functools.partial(_hy_conv_kernel, slabs=slabs),
        grid=(nt, b // 2, 2),
        in_specs=[pl.BlockSpec(tw.shape, lambda j, p, h: (0, 0, 0, 0), pipeline_mode=pl.Buffered(1)),
                  io(u_blk), io(gate_blk), spec_in, spec_in, tab, tab,
                  pl.BlockSpec((1, LANES), lambda j, p, h: (0, j))],
        out_specs=io(0),
        out_shape=jax.ShapeDtypeStruct((b, n, w), F32),
        scratch_shapes=[pltpu.VMEM((slabs, SUBLANES, LANES), F32)] * 2,
        compiler_params=_cparams(("parallel", "parallel", "arbitrary")), name="hy_conv",
    )(tw, u, gate, spec_re, spec_im, twm, mod, bias.astype(F32)[None])


HY_SHORT_ROWS = 64


def _hy_short_kernel(z_ref, w_ref, b_ref, v_ref, g_ref, *, n):
    j = pl.program_id(1)
    rb = min(HY_SHORT_ROWS, n)
    row = lax.broadcasted_iota(jnp.int32, (rb, z_ref.shape[2]), 0)
    zero_row = jnp.zeros((1, z_ref.shape[2]), F32)
    w0, w1, w2, bias = w_ref[0:1, :], w_ref[1:2, :], w_ref[2:3, :], b_ref[...]
    for c in range(n // rb):
        r0 = c * rb
        x = z_ref[0, r0:r0 + rb]
        before = z_ref[0, r0 - 1:r0] if c > 0 else zero_row
        after = z_ref[0, r0 + rb:r0 + rb + 1] if r0 + rb < n else zero_row
        prev = jnp.where(row == 0, before, pltpu.roll(x, 1, 0))
        nxt = jnp.where(row == rb - 1, after, pltpu.roll(x, rb - 1, 0))
        y = prev * w0 + x * w1 + nxt * w2 + bias

        @pl.when(j == 0)
        def _():
            v_ref[0, r0:r0 + rb] = y

        @pl.when(j > 0)
        def _():
            g_ref[0, r0:r0 + rb] = y.astype(g_ref.dtype)


def _hy_short(z_mix, conv_w, conv_b):
    b, n, _ = z_mix.shape
    w_ = HY_WIDTH
    w = conv_w.reshape(3, 3 * w_).astype(F32)
    col0 = BRANCH_W // w_
    return pl.pallas_call(
        functools.partial(_hy_short_kernel, n=n),
        grid=(b, 3),
        in_specs=[pl.BlockSpec((1, n, w_), lambda bi, j: (bi, 0, col0 + j)),
                  pl.BlockSpec((3, w_), lambda bi, j: (0, j)),
                  pl.BlockSpec((1, w_), lambda bi, j: (0, j))],
        out_specs=[pl.BlockSpec((1, n, w_), lambda bi, j: (bi, 0, 0)),
                   pl.BlockSpec((1, n, w_), lambda bi, j: (bi, 0, jnp.maximum(j - 1, 0)))],
        out_shape=[jax.ShapeDtypeStruct((b, n, w_), F32), jax.ShapeDtypeStruct((b, n, 2 * w_), BF16)],
        compiler_params=_cparams(("parallel", "arbitrary")), name="hy_short",
    )(z_mix, w, conv_b.astype(F32)[None])


def _hyena(z_mix, conv_w, conv_b, spec_re, spec_im, bias):
    v, gates = _hy_short(z_mix, conv_w, conv_b)
    v1 = _hy_order(v, 0, gates, 0, spec_re, spec_im, 0, bias[0])
    return _hy_order(v1, 0, gates, 1, spec_re, spec_im, 1, bias[1])


NA_QROWS = 4
NA_KROWS = NA_QROWS + NA_WIN_H
NA_KPART = 256


def _softmax_parts(parts):
    m = None
    for s in parts:
        mm = jnp.max(s, axis=-1, keepdims=True)
        m = mm if m is None else jnp.maximum(m, mm)
    return m


def _na_kernel(*refs, kparts):
    q_ref, k_refs, v_refs = refs[0], refs[1:1 + kparts], refs[1 + kparts:1 + 2 * kparts]
    kc_ref, vc_ref, bias_ref, o_ref = refs[1 + 2 * kparts:]
    q = q_ref[0] * np.float32(NA_HEAD_DIM ** -0.5)
    k = jnp.concatenate([r[0] for r in k_refs], axis=0).astype(BF16)
    v = jnp.concatenate([r[0] for r in v_refs], axis=0).astype(BF16)
    kc = kc_ref[0].astype(BF16)
    vc = vc_ref[0].astype(BF16)
    nt = (((1,), (1,)), ((), ()))
    outs = []
    for h in range(NA_HEADS):
        sl = slice(h * NA_HEAD_DIM, (h + 1) * NA_HEAD_DIM)
        qh = q[:, sl].astype(BF16)
        s_lat = lax.dot_general(qh, k[:, sl], nt, preferred_element_type=F32) + bias_ref[0, h]
        s_ctx = lax.dot_general(qh, kc[:, sl], nt, preferred_element_type=F32)
        m = _softmax_parts([s_lat, s_ctx])
        p_lat = jnp.exp(s_lat - m)
        p_ctx = jnp.exp(s_ctx - m)
        den = jnp.sum(p_lat, axis=-1, keepdims=True) + jnp.sum(p_ctx, axis=-1, keepdims=True)
        o = jnp.dot(p_lat.astype(BF16), v[:, sl], preferred_element_type=F32) \
            + jnp.dot(p_ctx.astype(BF16), vc[:, sl], preferred_element_type=F32)
        outs.append(o / den)
    o_ref[0] = jnp.concatenate(outs, axis=-1)


def _na_bias(rpb, rows):
    kh = min(NA_WIN_H, rows)
    col = np.arange(GRID_W)
    col_start = np.clip(col - NA_WIN_W // 2, 0, GRID_W - NA_WIN_W)
    col_ok = (col[None] >= col_start[:, None]) & (col[None] < col_start[:, None] + NA_WIN_W)
    off_c = np.clip(col[None] - col[:, None], -(NA_WIN_W - 1), NA_WIN_W - 1) + (NA_WIN_W - 1)
    nblk = rows // NA_QROWS
    n_r, n_c = 2 * NA_WIN_H - 1, 2 * NA_WIN_W - 1
    table = jnp.pad(rpb.astype(F32), ((0, 0), (0, 1), (0, 1)), constant_values=NEG_INF)
    sel_c = np.eye(n_c + 1, dtype=np.float32)[np.where(col_ok, off_c, n_c)]
    blocks = jnp.einsum('hab,qcb->hqac', table, sel_c, precision=HI).reshape(NA_HEADS, GRID_W, (n_r + 1) * GRID_W)
    blocks_odd = jnp.roll(blocks, -GRID_W, axis=-1)
    row_off = []
    for j in (0, 1, nblk - 1):
        qr = j * NA_QROWS + np.arange(NA_QROWS)
        ws = int(np.clip(j * NA_QROWS - NA_WIN_H // 2, 0, rows - NA_KROWS))
        kr = ws + np.arange(NA_KROWS)
        start = np.clip(qr - kh // 2, 0, rows - kh)
        row_ok = (kr[None] >= start[:, None]) & (kr[None] < start[:, None] + kh)
        off_r = kr[None] - qr[:, None] + (NA_WIN_H - 1)
        row_off.append(np.where(row_ok, off_r, n_r))
    row_off = np.stack(row_off)
    wide = (n_r + 1) * GRID_W
    return pl.pallas_call(
        functools.partial(_na_bias_kernel, row_off=row_off),
        grid=(NA_HEADS,),
        in_specs=[pl.BlockSpec((1, GRID_W, wide), lambda h: (h, 0, 0))] * 2,
        out_specs=pl.BlockSpec((3, 1, NA_QROWS * GRID_W, NA_KROWS * GRID_W), lambda h: (0, h, 0, 0)),
        out_shape=jax.ShapeDtypeStruct((3, NA_HEADS, NA_QROWS * GRID_W, NA_KROWS * GRID_W), F32),
        compiler_params=_cparams(("parallel",)), name="na_bias",
    )(blocks, blocks_odd)


def _na_bias_kernel(even_ref, odd_ref, o_ref, *, row_off):
    n_cls, n_q, n_k = row_off.shape
    masked = 2 * NA_WIN_H - 1
    pair = 2 * GRID_W
    lane = lax.broadcasted_iota(jnp.int32, (GRID_W, pair), 1)

    def left(a):
        src, start = (even_ref, a) if a % 2 == 0 else (odd_ref, a - 1)
        return src[0, :, start * GRID_W:start * GRID_W + pair]

    def right(a):
        assert a % 2 == 1
        return even_ref[0, :, (a - 1) * GRID_W:(a - 1) * GRID_W + pair]

    for t in range(n_cls):
        for r in range(n_q):
            for kp in range(n_k // 2):
                a0, a1 = int(row_off[t, r, 2 * kp]), int(row_off[t, r, 2 * kp + 1])
                if a0 != masked and a1 == a0 + 1:
                    blk = left(a0)
                elif a0 == masked and a1 == masked:
                    blk = jnp.full((GRID_W, pair), NEG_INF, F32)
                elif a1 == masked:
                    blk = jnp.where(lane < GRID_W, left(a0), NEG_INF)
                else:
                    blk = jnp.where(lane >= GRID_W, right(a1), NEG_INF)
                o_ref[t, 0, r * GRID_W:(r + 1) * GRID_W, kp * pair:(kp + 1) * pair] = blk


def _na_attention(z_mix, zc_mix, rpb):
    b, n, _ = z_mix.shape
    n_ctx = zc_mix.shape[1]
    rows = n // GRID_W
    nblk = rows // NA_QROWS
    tq = NA_QROWS * GRID_W
    bias = _na_bias(rpb, rows)
    qb, kb, vb = NA_COL0 // BRANCH_W, NA_COL0 // BRANCH_W + 1, NA_COL0 // BRANCH_W + 2
    kparts = NA_KROWS * GRID_W // NA_KPART
    max_k0 = (rows - NA_KROWS) * GRID_W // NA_KPART

    def k0_of(j):
        per_step = NA_QROWS * GRID_W // NA_KPART
        lead = (NA_WIN_H // 2) * GRID_W // NA_KPART
        return jnp.clip(j * per_step - lead, 0, max_k0)

    def kspec(part, blk):
        return pl.BlockSpec((1, NA_KPART, BRANCH_W), lambda bi, j: (bi, k0_of(j) + part, blk))

    def cls(j):
        return jnp.where(j == 0, 0, jnp.where(j == nblk - 1, 2, 1))

    return pl.pallas_call(
        functools.partial(_na_kernel, kparts=kparts),
        grid=(b, nblk),
        in_specs=[pl.BlockSpec((1, tq, BRANCH_W), lambda bi, j: (bi, j, qb))]
        + [kspec(p, kb) for p in range(kparts)] + [kspec(p, vb) for p in range(kparts)]
        + [pl.BlockSpec((1, n_ctx, BRANCH_W), lambda bi, j: (bi, 0, kb)),
           pl.BlockSpec((1, n_ctx, BRANCH_W), lambda bi, j: (bi, 0, vb)),
           pl.BlockSpec((1, NA_HEADS, tq, NA_KROWS * GRID_W), lambda bi, j: (cls(j), 0, 0, 0))],
        out_specs=pl.BlockSpec((1, tq, BRANCH_W), lambda bi, j: (bi, j, 0)),
        out_shape=jax.ShapeDtypeStruct((b, n, BRANCH_W), F32),
        compiler_params=_cparams(("parallel", "arbitrary")), name="na_attn",
    )(z_mix, *([z_mix] * (2 * kparts)), zc_mix, zc_mix, bias)


def _rope_tables(n_tokens, head_dim, heads):
    t = np.arange(n_tokens)
    row = (t // GRID_W).astype(np.float64)
    col = (t % GRID_W).astype(np.float64)
    half = head_dim // 2
    inv = ROPE_BASE ** (-(np.arange(0, half, 2, dtype=np.float64) / half))
    ang = np.concatenate([row[:, None] * inv, col[:, None] * inv], axis=-1)
    cos, sin = np.cos(ang), np.sin(ang)
    cos_t = np.tile(np.concatenate([cos, cos], axis=-1), (1, heads))
    sin_t = np.tile(np.concatenate([-sin, sin], axis=-1), (1, heads))
    return jnp.asarray(cos_t, F32), jnp.asarray(sin_t, F32)


def _rope(x, cos_t, sin_t):
    w = x.shape[-1]
    half = SW_HEAD_DIM // 2
    lane = lax.broadcasted_iota(jnp.int32, x.shape, 1)
    first = (lane % SW_HEAD_DIM) < half
    partner = jnp.where(first, pltpu.roll(x, w - half, 1), pltpu.roll(x, half, 1))
    return x * cos_t + partner * sin_t


SW_QBLK = 2 * SW_BLOCK
SW_KPARTS = SW_QBLK // SW_BLOCK + 2


def _sw_kernel(q_ref, kv0, kv1, kv2, kv3, kc_ref, vc_ref, sink_ref, o_ref, *, nblk):
    i = pl.program_id(1)
    kvw = SW_KV_HEADS * SW_HEAD_DIM
    q = q_ref[0]
    kv = jnp.concatenate([kv0[0], kv1[0], kv2[0], kv3[0]], axis=0)
    kc = kc_ref[0].astype(BF16)
    vc = vc_ref[0].astype(BF16)
    g_ = SW_HEADS // SW_KV_HEADS
    rows = g_ * SW_QBLK
    span = SW_KPARTS * SW_BLOCK
    r = lax.broadcasted_iota(jnp.int32, (rows, span), 0) % SW_QBLK
    c = lax.broadcasted_iota(jnp.int32, (rows, span), 1)
    diff = c - SW_BLOCK - r
    blk = i * (SW_QBLK // SW_BLOCK) - 1 + c // SW_BLOCK
    ok = (jnp.abs(diff) <= SW_WINDOW) & (blk >= 0) & (blk < nblk)
    nt = (((1,), (1,)), ((), ()))
    outs = []
    for kvh in range(SW_KV_HEADS):
        ksl = slice(kvh * SW_HEAD_DIM, (kvh + 1) * SW_HEAD_DIM)
        vsl = slice(kvw + kvh * SW_HEAD_DIM, kvw + (kvh + 1) * SW_HEAD_DIM)
        qg = jnp.concatenate([q[:, (kvh * g_ + g) * SW_HEAD_DIM:(kvh * g_ + g + 1) * SW_HEAD_DIM] for g in range(g_)],
                             axis=0)
        s_lat = jnp.where(ok, lax.dot_general(qg, kv[:, ksl], nt, preferred_element_type=F32), NEG_INF)
        s_ctx = lax.dot_general(qg, kc[:, ksl], nt, preferred_element_type=F32)
        rr = lax.broadcasted_iota(jnp.int32, (rows, 1), 0)
        s_sink = jnp.zeros((rows, 1), F32)
        for g in range(g_):
            s_sink = jnp.where(rr // SW_QBLK == g, sink_ref[kvh * g_ + g], s_sink)
        m = jnp.maximum(_softmax_parts([s_lat, s_ctx]), s_sink)
        p_lat = jnp.exp(s_lat - m)
        p_ctx = jnp.exp(s_ctx - m)
        den = jnp.sum(p_lat, axis=-1, keepdims=True) + jnp.sum(p_ctx, axis=-1, keepdims=True) + jnp.exp(s_sink - m)
        o = jnp.dot(p_lat.astype(BF16), kv[:, vsl], preferred_element_type=F32) \
            + jnp.dot(p_ctx.astype(BF16), vc[:, ksl], preferred_element_type=F32)
        o = o / den
        outs += [o[g * SW_QBLK:(g + 1) * SW_QBLK] for g in range(g_)]
    o_ref[0] = jnp.concatenate(outs, axis=-1)


def _sw_attention(q_r, kv_r, zc_mix, sink):
    b, n, _ = q_r.shape
    n_ctx = zc_mix.shape[1]
    nblk = n // SW_BLOCK
    qw = SW_HEADS * SW_HEAD_DIM
    kvw = SW_KV_HEADS * SW_HEAD_DIM
    k_blk = (SW_COL0 + qw) // kvw
    per_q = SW_QBLK // SW_BLOCK

    def kpart(part):
        return pl.BlockSpec((1, SW_BLOCK, qw),
                            lambda bi, i: (bi, jnp.clip(i * per_q - 1 + part, 0, nblk - 1), 0))

    return pl.pallas_call(
        functools.partial(_sw_kernel, nblk=nblk),
        grid=(b, n // SW_QBLK),
        in_specs=[pl.BlockSpec((1, SW_QBLK, qw), lambda bi, i: (bi, i, 0))]
        + [kpart(p) for p in range(SW_KPARTS)]
        + [pl.BlockSpec((1, n_ctx, kvw), lambda bi, i: (bi, 0, k_blk)),
           pl.BlockSpec((1, n_ctx, kvw), lambda bi, i: (bi, 0, k_blk + 1)),
           pl.BlockSpec(memory_space=pltpu.SMEM)],
        out_specs=pl.BlockSpec((1, SW_QBLK, qw), lambda bi, i: (bi, i, 0)),
        out_shape=jax.ShapeDtypeStruct((b, n, qw), F32),
        compiler_params=_cparams(("parallel", "arbitrary")), name="sw_attn",
    )(q_r, *([kv_r] * SW_KPARTS), zc_mix, zc_mix, sink.astype(F32))


def _ctx_attn_kernel(q_ref, k_ref, v_ref, sink_ref, o_ref, *, heads, kv_heads, dh, use_sink):
    q = q_ref[0] * np.float32(dh ** -0.5)
    k = k_ref[0].astype(BF16)
    v = v_ref[0].astype(BF16)
    g_ = heads // kv_heads
    nt = (((1,), (1,)), ((), ()))
    outs = []
    for h in range(heads):
        kv = h // g_
        s = lax.dot_general(q[:, h * dh:(h + 1) * dh].astype(BF16), k[:, kv * dh:(kv + 1) * dh], nt,
                            preferred_element_type=F32)
        m = jnp.max(s, axis=-1, keepdims=True)
        if use_sink:
            m = jnp.maximum(m, sink_ref[h])
        p = jnp.exp(s - m)
        den = jnp.sum(p, axis=-1, keepdims=True)
        if use_sink:
            den = den + jnp.exp(sink_ref[h] - m)
        outs.append(jnp.dot(p.astype(BF16), v[:, kv * dh:(kv + 1) * dh], preferred_element_type=F32) / den)
    o_ref[0] = jnp.concatenate(outs, axis=-1)


def _ctx_attention(zc_mix, col0, heads, kv_heads, dh, sink):
    b, n, _ = zc_mix.shape
    qw, kvw = heads * dh, kv_heads * dh
    use_sink = sink is not None
    sink_arr = sink.astype(F32) if use_sink else jnp.zeros((heads,), F32)
    return pl.pallas_call(
        functools.partial(_ctx_attn_kernel, heads=heads, kv_heads=kv_heads, dh=dh, use_sink=use_sink),
        grid=(b,),
        in_specs=[pl.BlockSpec((1, n, qw), lambda bi: (bi, 0, col0 // qw)),
                  pl.BlockSpec((1, n, kvw), lambda bi: (bi, 0, (col0 + qw) // kvw)),
                  pl.BlockSpec((1, n, kvw), lambda bi: (bi, 0, (col0 + qw) // kvw + 1)),
                  pl.BlockSpec(memory_space=pltpu.SMEM)],
        out_specs=pl.BlockSpec((1, n, qw), lambda bi: (bi, 0, 0)),
        out_shape=jax.ShapeDtypeStruct((b, n, qw), F32),
        compiler_params=_cparams(("parallel",)), name="ctx_attn",
    )(zc_mix, zc_mix, zc_mix, sink_arr)


def _merge_kernel(x_ref, s5_ref, hy_ref, na_ref, sw_ref, gt_ref, ga_ref, wglu_ref, wb_ref, wo_ref,
                  lg_ref, lb_ref, o_ref, tok_scr):
    rows = tok_scr.shape[1] // S5_T
    for c in range(BRANCH_W // LANES):
        for t in range(S5_T):
            col = t * BRANCH_W + c * LANES
            tok_scr[c, pl.ds(t, rows, stride=S5_T), :] = s5_ref[0, :, col:col + LANES]
    g = jax.nn.gelu(jnp.concatenate([tok_scr[c] for c in range(BRANCH_W // LANES)], axis=-1))
    s5 = g * jax.nn.sigmoid(jnp.dot(g.astype(BF16), wglu_ref[...], preferred_element_type=F32))
    branches = (s5, hy_ref[0], na_ref[0], sw_ref[0])
    acc = None
    for n in range(N_BRANCH):
        proj = jnp.dot(branches[n].astype(BF16), wb_ref[n], preferred_element_type=F32)
        t = gt_ref[0, :, n * D_MODEL:(n + 1) * D_MODEL].astype(F32) * proj
        acc = t if acc is None else acc + t
    mix = jnp.dot(acc.astype(BF16), wo_ref[...], preferred_element_type=F32)
    y = np.float32(DEEPNORM_ALPHA) * x_ref[0] + ga_ref[0] * mix
    o_ref[0] = _layernorm(y) * lg_ref[...] + lb_ref[...]


def _merge(x, s5y, hy, na, sw, gates, g_a, w_glu, w_branch, w_out, ln_g, ln_b, *, tm):
    b, l, d = x.shape
    br = pl.BlockSpec((1, tm, BRANCH_W), lambda bi, i: (bi, i, 0))
    full = lambda s: pl.BlockSpec(s, lambda bi, i: (0,) * len(s), pipeline_mode=pl.Buffered(1))
    return pl.pallas_call(
        _merge_kernel,
        grid=(b, l // tm),
        in_specs=[pl.BlockSpec((1, tm, d), lambda bi, i: (bi, i, 0)),
                  pl.BlockSpec((1, tm // S5_T, S5_T * BRANCH_W), lambda bi, i: (bi, i, 0)), br, br, br,
                  pl.BlockSpec((1, tm, GATE_W), lambda bi, i: (bi, i, 0)),
                  pl.BlockSpec((1, 1, d), lambda bi, i: (bi, 0, 0)),
                  full((BRANCH_W, BRANCH_W)), full((N_BRANCH, BRANCH_W, d)), full((d, d)),
                  full((1, d)), full((1, d))],
        out_specs=pl.BlockSpec((1, tm, d), lambda bi, i: (bi, i, 0)),
        out_shape=jax.ShapeDtypeStruct((b, l, d), F32),
        scratch_shapes=[pltpu.VMEM((BRANCH_W // LANES, tm, LANES), F32)],
        compiler_params=_cparams(("parallel", "parallel")), name="merge",
    )(x, s5y, hy, na, sw, gates, g_a, w_glu, w_branch, w_out, ln_g, ln_b)


def _mlp_kernel(x_ref, sh_ref, sc_ref, gm_ref, w1_ref, w2_ref, lg_ref, lb_ref, o_ref, h_scr, acc_scr):
    j = pl.program_id(2)

    @pl.when(j == 0)
    def _():
        h = _layernorm(x_ref[0]) * (1.0 + sc_ref[0]) + sh_ref[0]
        h_scr[...] = h.astype(BF16)
        acc_scr[...] = jnp.zeros_like(acc_scr)

    a = jnp.dot(h_scr[...], w1_ref[...], preferred_element_type=F32)
    a = jnp.square(jnp.maximum(a, 0.0))
    acc_scr[...] += jnp.dot(a.astype(BF16), w2_ref[...], preferred_element_type=F32)

    @pl.when(j == pl.num_programs(2) - 1)
    def _():
        y = np.float32(DEEPNORM_ALPHA) * x_ref[0] + gm_ref[0] * acc_scr[...]
        o_ref[0] = _layernorm(y) * lg_ref[...] + lb_ref[...]


def _mlp(x, sh, sc, g_m, w1, w2, ln_g, ln_b, *, tm, th):
    b, l, d = x.shape
    hdim = w1.shape[1]
    mod = pl.BlockSpec((1, 1, d), lambda bi, i, j: (bi, 0, 0))
    vec = pl.BlockSpec((1, d), lambda bi, i, j: (0, 0))
    return pl.pallas_call(
        _mlp_kernel,
        grid=(b, l // tm, hdim // th),
        in_specs=[pl.BlockSpec((1, tm, d), lambda bi, i, j: (bi, i, 0)), mod, mod, mod,
                  pl.BlockSpec((d, th), lambda bi, i, j: (0, j)),
                  pl.BlockSpec((th, d), lambda bi, i, j: (j, 0)), vec, vec],
        out_specs=pl.BlockSpec((1, tm, d), lambda bi, i, j: (bi, i, 0)),
        out_shape=jax.ShapeDtypeStruct((b, l, d), F32),
        scratch_shapes=[pltpu.VMEM((tm, d), BF16), pltpu.VMEM((tm, d), F32)],
        compiler_params=_cparams(("parallel", "parallel", "arbitrary")), name="mlp",
    )(x, sh, sc, g_m, w1, w2, ln_g, ln_b)


def kernel(x, c, ctx, c_ctx, w_ada, b_ada, w_in, s5_lambda_re, s5_lambda_im, s5_log_dt, s5_b_re, s5_b_im, s5_c_re,
           s5_c_im, s5_d, s5_w_glu, hy_conv_w, hy_conv_b, hy_freq, hy_w1, hy_b1, hy_w2, hy_b2, hy_w3, hy_bias,
           na_rpb, sw_sink, w_branch, w_out, ln1_g, ln1_b, w_mlp1, w_mlp2, ln2_g, ln2_b):
    b, l, d = x.shape
    n_ctx = ctx.shape[1]
    depth = w_ada.shape[0]
    cc = jnp.zeros((8, d), F32).at[:b].set(c.astype(F32)).at[b].set(c_ctx.astype(F32))
    mod_all = _ada(cc, w_ada.astype(F32), b_ada.astype(F32))
    xc = ctx
    for layer in range(depth):
        need_ctx_out = layer < depth - 1
        mod = mod_all[layer, :b].reshape(b, 1, 6, d)
        mod_c = mod_all[layer, b].reshape(1, 1, 6, d)
        sh_a, sc_a, g_a, sh_m, sc_m, g_m = [mod[:, :, i] for i in range(6)]
        csh_a, csc_a, cg_a, csh_m, csc_m, cg_m = [mod_c[:, :, i] for i in range(6)]
        w_in_l = w_in[layer].astype(BF16)
        w_mix, w_gate = w_in_l[:, :MIX_W], w_in_l[:, MIX_W:]

        lat = _in_proj(x, sh_a, sc_a, w_mix, w_gate, tm=1024, with_rope=True)
        z_mix, gates = lat["z"], lat["gates"]
        flat = lambda a: a.reshape(1, b * a.shape[1], a.shape[2])
        unflat = lambda a: a.reshape(b, a.shape[1] // b, a.shape[2])
        con = _in_proj(flat(xc), csh_a, csc_a, w_mix, w_gate if need_ctx_out else None, tm=b * n_ctx)
        zc_mix = unflat(con["z"])

        ops = _s5_operators(s5_lambda_re[layer], s5_lambda_im[layer], s5_log_dt[layer], s5_b_re[layer],
                            s5_b_im[layer], s5_c_re[layer], s5_c_im[layer], s5_d[layer])
        yc_s5, e_ctx = _s5_mix(unflat(con["u"]), ops, [jnp.zeros((b, 1, S5_STATE_W // 4), F32)] * 4)
        y_s5, _ = _s5_mix(lat["u"], ops, e_ctx)

        hy_args = (hy_freq[layer], hy_w1[layer], hy_b1[layer], hy_w2[layer], hy_b2[layer], hy_w3[layer])
        sp_re, sp_im = _hy_spectra(_hy_filters(l, *hy_args), l)
        hy_l = _hyena(z_mix, hy_conv_w[layer], hy_conv_b[layer], sp_re, sp_im, hy_bias[layer])

        na_l = _na_attention(z_mix, zc_mix, na_rpb[layer])
        sw_l = _sw_attention(lat["q_rope"], lat["kv_rope"], zc_mix, sw_sink[layer])

        w_glu = s5_w_glu[layer].astype(BF16)
        w_br = w_branch[layer].astype(BF16)
        w_o = w_out[layer].astype(BF16)
        lg1, lb1 = ln1_g[layer].astype(F32)[None], ln1_b[layer].astype(F32)[None]
        lg2, lb2 = ln2_g[layer].astype(F32)[None], ln2_b[layer].astype(F32)[None]
        w1 = w_mlp1[layer].astype(BF16)
        w2 = w_mlp2[layer].astype(BF16)

        x_new = _merge(x, y_s5, hy_l, na_l, sw_l, gates, g_a, w_glu, w_br, w_o, lg1, lb1, tm=1024)
        x_new = _mlp(x_new, sh_m, sc_m, g_m, w1, w2, lg2, lb2, tm=1024, th=2048)

        if need_ctx_out:
            spc_re, spc_im = _hy_spectra(_hy_filters(n_ctx, *hy_args), n_ctx)
            hy_c = _hyena(zc_mix, hy_conv_w[layer], hy_conv_b[layer], spc_re, spc_im, hy_bias[layer])
            na_c = _ctx_attention(zc_mix, NA_COL0, NA_HEADS, NA_HEADS, NA_HEAD_DIM, None)
            sw_c = _ctx_attention(zc_mix, SW_COL0, SW_HEADS, SW_KV_HEADS, SW_HEAD_DIM, sw_sink[layer])
            xc_new = _merge(flat(xc), flat(yc_s5), flat(hy_c), flat(na_c), flat(sw_c), con["gates"], cg_a,
                            w_glu, w_br, w_o, lg1, lb1, tm=512)
            xc = unflat(_mlp(xc_new, csh_m, csc_m, cg_m, w1, w2, lg2, lb2, tm=b * n_ctx, th=1024))
        x = x_new
    return x
```

```python
import functools
import math

import numpy as np
import jax
import jax.numpy as jnp
from jax import lax
from jax.experimental import pallas as pl
from jax.experimental.pallas import tpu as pltpu

F32 = jnp.float32
BF16 = jnp.bfloat16

D_MODEL = 1024
GRID_W = 64
BRANCH_W = 256
N_BRANCH = 4
S5_GROUP = 16
S5_GROUPS = 16
S5_STATE = 64
HY_WIDTH = 256
HY_ORDER = 2
HY_BANDS = 16
HY_EMB = 2 * HY_BANDS + 1
HY_FFN = 64
HY_MIN_DECAY = math.log(1e-2) / 1.5
HY_MAX_DECAY = math.log(1e-2) / 0.3
NA_HEADS = 4
NA_HEAD_DIM = 64
NA_WIN_H = 8
NA_WIN_W = 16
SW_HEADS = 4
SW_KV_HEADS = 2
SW_HEAD_DIM = 64
SW_WINDOW = 128
SW_BLOCK = 128
MLP_HIDDEN = 4 * D_MODEL
ROPE_BASE = 10000.0
LN_EPS = 1e-6
NEG_INF = -1e30
DEPTH = 2
DEEPNORM_ALPHA = (2 * DEPTH) ** 0.25

MIX_W = 2304
NA_COL0 = BRANCH_W + 3 * HY_WIDTH
SW_COL0 = NA_COL0 + 3 * BRANCH_W
GATE_W = N_BRANCH * D_MODEL
S5_T = 8
S5_STATE_W = 4 * S5_GROUPS * S5_STATE
SUBLANES = 8
LANES = 128
VMEM_LIMIT = 56 * 1024 * 1024

HI = lax.Precision.HIGHEST


def _cparams(sem):
    return pltpu.CompilerParams(dimension_semantics=sem, vmem_limit_bytes=VMEM_LIMIT)


def _layernorm(x):
    mu = jnp.mean(x, axis=-1, keepdims=True)
    xc = x - mu
    var = jnp.mean(xc * xc, axis=-1, keepdims=True)
    return xc * lax.rsqrt(var + LN_EPS)


def _ada_kernel(c_ref, w_ref, b_ref, o_ref):
    c = c_ref[...]
    a = c * jax.nn.sigmoid(c)
    o_ref[0] = jnp.dot(a, w_ref[0], preferred_element_type=F32, precision=HI) + b_ref[0]


def _ada(cc, w_ada, b_ada):
    depth, d, n = w_ada.shape
    tn = 512
    return pl.pallas_call(
        _ada_kernel,
        grid=(depth, n // tn),
        in_specs=[pl.BlockSpec((8, d), lambda l, j: (0, 0)),
                  pl.BlockSpec((1, d, tn), lambda l, j: (l, 0, j)),
                  pl.BlockSpec((1, 1, tn), lambda l, j: (l, 0, j))],
        out_specs=pl.BlockSpec((1, 8, tn), lambda l, j: (l, 0, j)),
        out_shape=jax.ShapeDtypeStruct((depth, 8, n), F32),
        compiler_params=_cparams(("parallel", "parallel")), name="ada",
    )(cc, w_ada, b_ada.reshape(depth, 1, n))


IN_MIX_TN = MIX_W // 2
IN_GATE_TN = GATE_W // 2


def _in_proj_kernel(x_ref, sh_ref, sc_ref, wm_ref, *rest, n_mix, with_gates, with_rope):
    rest = list(rest)
    wg_ref = rest.pop(0) if with_gates else None
    cos_ref, sin_ref = (rest.pop(0), rest.pop(0)) if with_rope else (None, None)
    z_ref, u_ref = rest.pop(0), rest.pop(0)
    g_ref = rest.pop(0) if with_gates else None
    qr_ref, kvr_ref = (rest.pop(0), rest.pop(0)) if with_rope else (None, None)
    tok_scr, h_scr = rest
    j = pl.program_id(2)

    @pl.when(j == 0)
    def _():
        h = _layernorm(x_ref[0]) * (1.0 + sc_ref[0]) + sh_ref[0]
        h_scr[...] = h.astype(BF16)

    @pl.when(j < n_mix)
    def _():
        r = jnp.dot(h_scr[...], wm_ref[...], preferred_element_type=F32)
        z_ref[0] = r.astype(z_ref.dtype)

        @pl.when(j == 0)
        def _():
            rows = tok_scr.shape[1] // S5_T
            for c in range(BRANCH_W // LANES):
                tok_scr[c] = r[:, c * LANES:(c + 1) * LANES]
                for t in range(S5_T):
                    col = t * BRANCH_W + c * LANES
                    u_ref[0, :, col:col + LANES] = tok_scr[c, pl.ds(t, rows, stride=S5_T), :].astype(u_ref.dtype)

        if with_rope:
            @pl.when(j == n_mix - 1)
            def _():
                qw = SW_HEADS * SW_HEAD_DIM
                q0 = SW_COL0 - (n_mix - 1) * IN_MIX_TN
                cos_t, sin_t = cos_ref[...], sin_ref[...]
                q = _rope(r[:, q0:q0 + qw], cos_t, sin_t) * np.float32(SW_HEAD_DIM ** -0.5)
                qr_ref[0] = q.astype(BF16)
                kv = r[:, q0 + qw:q0 + 2 * qw]
                lane = lax.broadcasted_iota(jnp.int32, kv.shape, 1)
                kvr_ref[0] = jnp.where(lane < SW_KV_HEADS * SW_HEAD_DIM, _rope(kv, cos_t, sin_t), kv).astype(BF16)

    if with_gates:
        @pl.when(j >= n_mix)
        def _():
            r = jnp.dot(h_scr[...], wg_ref[...], preferred_element_type=F32)
            g_ref[0] = jax.nn.sigmoid(r).astype(g_ref.dtype)


def _in_proj(x, sh, sc, w_in, with_gates, *, tm, with_rope=False):
    b, l, d = x.shape
    n_mix = MIX_W // IN_MIX_TN
    n_gate = GATE_W // IN_GATE_TN if with_gates else 0
    mix_j = lambda j: jnp.minimum(j, n_mix - 1)
    gate_j = lambda j: jnp.maximum(j - n_mix, 0)
    in_specs = [pl.BlockSpec((1, tm, d), lambda bi, i, j: (bi, i, 0)),
                pl.BlockSpec((1, 1, d), lambda bi, i, j: (bi, 0, 0)),
                pl.BlockSpec((1, 1, d), lambda bi, i, j: (bi, 0, 0)),
                pl.BlockSpec((pl.Element(d), pl.Element(IN_MIX_TN)), lambda bi, i, j: (0, pl.multiple_of(mix_j(j) * IN_MIX_TN, LANES)))]
    out_specs = [pl.BlockSpec((1, tm, IN_MIX_TN), lambda bi, i, j: (bi, i, mix_j(j))),
                 pl.BlockSpec((1, tm // S5_T, S5_T * BRANCH_W), lambda bi, i, j: (bi, i, 0))]
    out_shape = [jax.ShapeDtypeStruct((b, l, MIX_W), BF16),
                 jax.ShapeDtypeStruct((b, l // S5_T, S5_T * BRANCH_W), BF16)]
    args = [x, sh, sc, w_in]
    names = ["z", "u"]
    if with_gates:
        in_specs.append(pl.BlockSpec((pl.Element(d), pl.Element(IN_GATE_TN)),
                                     lambda bi, i, j: (0, pl.multiple_of(MIX_W + gate_j(j) * IN_GATE_TN, LANES))))
        out_specs.append(pl.BlockSpec((1, tm, IN_GATE_TN), lambda bi, i, j: (bi, i, gate_j(j))))
        out_shape.append(jax.ShapeDtypeStruct((b, l, GATE_W), BF16))
        args.append(w_in)
        names.append("gates")
    if with_rope:
        qw = SW_HEADS * SW_HEAD_DIM
        in_specs += [pl.BlockSpec((tm, qw), lambda bi, i, j: (i, 0))] * 2
        out_specs += [pl.BlockSpec((1, tm, qw), lambda bi, i, j: (bi, i, 0))] * 2
        out_shape += [jax.ShapeDtypeStruct((b, l, qw), BF16)] * 2
        args += list(_rope_tables(l, SW_HEAD_DIM, SW_HEADS))
        names += ["q_rope", "kv_rope"]
    outs = pl.pallas_call(
        functools.partial(_in_proj_kernel, n_mix=n_mix, with_gates=with_gates, with_rope=with_rope),
        grid=(b, l // tm, n_mix + n_gate),
        in_specs=in_specs,
        out_specs=out_specs,
        out_shape=out_shape,
        scratch_shapes=[pltpu.VMEM((BRANCH_W // LANES, tm, LANES), F32), pltpu.VMEM((tm, d), BF16)],
        compiler_params=_cparams(("parallel", "parallel", "arbitrary")), name="in_proj",
    )(*args)
    return dict(zip(names, outs))


def _s5_inc_kernel(u_ref, q_ref, o_ref):
    o_ref[...] = jnp.dot(u_ref[...].astype(BF16), q_ref[...], preferred_element_type=F32)


def _s5_increments(uu, q_all, *, tm, tn):
    m, k = uu.shape
    n = q_all.shape[1]
    return pl.pallas_call(
        _s5_inc_kernel,
        grid=(m // tm, n // tn),
        in_specs=[pl.BlockSpec((tm, k), lambda i, j: (i, 0)), pl.BlockSpec((k, tn), lambda i, j: (0, j))],
        out_specs=pl.BlockSpec((tm, tn), lambda i, j: (i, j)),
        out_shape=jax.ShapeDtypeStruct((m, n), F32),
        compiler_params=_cparams(("parallel", "parallel")), name="s5_inc",
    )(uu, q_all)


def _s5_out_kernel(u_ref, h0, h1, h2, h3, m_ref, n0, n1, n2, n3, o_ref):
    nt = (((1,), (1,)), ((), ()))
    acc = jnp.dot(u_ref[...].astype(BF16), m_ref[...], preferred_element_type=F32)
    for h_ref, n_ref in ((h0, n0), (h1, n1), (h2, n2), (h3, n3)):
        acc += lax.dot_general(h_ref[...].astype(BF16), n_ref[...], nt, preferred_element_type=F32)
    o_ref[...] = acc


def _s5_outputs(uu, h, m_tot, n_t, *, tm, tn):
    m, k = uu.shape
    gp = h[0].shape[1]
    n = m_tot.shape[1]
    return pl.pallas_call(
        _s5_out_kernel,
        grid=(m // tm, n // tn),
        in_specs=[pl.BlockSpec((tm, k), lambda i, j: (i, 0))]
        + [pl.BlockSpec((tm, gp), lambda i, j: (i, 0))] * 4
        + [pl.BlockSpec((k, tn), lambda i, j: (0, j))]
        + [pl.BlockSpec((tn, gp), lambda i, j, c=c: (j, c)) for c in range(4)],
        out_specs=pl.BlockSpec((tm, tn), lambda i, j: (i, j)),
        out_shape=jax.ShapeDtypeStruct((m, n), F32),
        compiler_params=_cparams(("parallel", "parallel")), name="s5_out",
    )(uu, *h, m_tot, n_t, n_t, n_t, n_t)


def _s5_scan_kernel(gfr, gfi, gbr, gbi, afr, afi, abr, abi, h0fr, h0fi, h0br, h0bi,
                    hfr, hfi, hbr, hbi, efr, efi, ebr, ebi, *, n_chunks):
    a_fr = afr[...][None]
    a_fi = afi[...][None]
    a_br = abr[...][None]
    a_bi = abi[...][None]

    def body(k, carry):
        sfr, sfi, sbr, sbi = carry
        kb = n_chunks - 1 - k
        hfr[:, pl.ds(k, 1), :] = sfr
        hfi[:, pl.ds(k, 1), :] = sfi
        hbr[:, pl.ds(kb, 1), :] = sbr
        hbi[:, pl.ds(kb, 1), :] = sbi
        nfr = a_fr * sfr - a_fi * sfi + gfr[:, pl.ds(k, 1), :]
        nfi = a_fr * sfi + a_fi * sfr + gfi[:, pl.ds(k, 1), :]
        nbr = a_br * sbr - a_bi * sbi + gbr[:, pl.ds(kb, 1), :]
        nbi = a_br * sbi + a_bi * sbr + gbi[:, pl.ds(kb, 1), :]
        return nfr, nfi, nbr, nbi

    sfr, sfi, sbr, sbi = lax.fori_loop(0, n_chunks, body, (h0fr[...], h0fi[...], h0br[...], h0bi[...]), unroll=4)
    efr[...] = sfr
    efi[...] = sfi
    ebr[...] = sbr
    ebi[...] = sbi


def _s5_scan(g, a_t, h0):
    b, k, w4 = g.shape
    w = 2 * LANES
    q = w4 // 4
    nb = q // w

    def comp(c):
        return pl.BlockSpec((b, k, w), lambda j, c=c: (0, 0, c * nb + j))

    def comp_a(c):
        return pl.BlockSpec((1, w), lambda j, c=c: (0, c * nb + j))

    state = pl.BlockSpec((b, k, w), lambda j: (0, 0, j))
    edge = pl.BlockSpec((b, 1, w), lambda j: (0, 0, j))
    outs = pl.pallas_call(
        functools.partial(_s5_scan_kernel, n_chunks=k),
        grid=(nb,),
        in_specs=[comp(c) for c in range(4)] + [comp_a(c) for c in range(4)] + [edge] * 4,
        out_specs=[state] * 4 + [edge] * 4,
        out_shape=[jax.ShapeDtypeStruct((b, k, q), F32)] * 4 + [jax.ShapeDtypeStruct((b, 1, q), F32)] * 4,
        compiler_params=_cparams(("parallel",)), name="s5_scan",
    )(g, g, g, g, a_t, a_t, a_t, a_t, *h0)
    return outs[:4], outs[4:]


def _cmul(ar, ai, br, bi):
    return ar * br - ai * bi, ar * bi + ai * br


def _s5_operators(lam_re, lam_im, log_dt, b_re, b_im, c_re, c_im, d):
    t_ = S5_T
    g_, p_, n_ = S5_GROUPS, S5_STATE, S5_GROUP
    gp, w_ = g_ * p_, g_ * n_
    lr, li = lam_re.astype(F32), lam_im.astype(F32)
    dt = jnp.exp(log_dt.astype(F32))[..., None]
    ks = jnp.arange(t_ + 1, dtype=F32)[:, None, None, None]
    mag = jnp.exp(ks * lr * dt)
    pw_re = mag * jnp.cos(ks * li * dt)
    pw_im = mag * jnp.sin(ks * li * dt)
    a_re, a_im = pw_re[1], pw_im[1]
    den = lr ** 2 + li ** 2
    f_re = ((a_re - 1.0) * lr + a_im * li) / den
    f_im = (a_im * lr - (a_re - 1.0) * li) / den
    br, bi = b_re.astype(F32), b_im.astype(F32)
    bb_re = f_re[..., None] * br - f_im[..., None] * bi
    bb_im = f_re[..., None] * bi + f_im[..., None] * br
    cr, ci = c_re.astype(F32), c_im.astype(F32)
    wb_re, wb_im = _cmul(pw_re[:t_, ..., None], pw_im[:t_, ..., None], bb_re[None], bb_im[None])
    kern = jnp.einsum('dgnp,kdgpm->dkgnm', cr, wb_re, precision=HI) \
        - jnp.einsum('dgnp,kdgpm->dkgnm', ci, wb_im, precision=HI)
    k_tile = jnp.tile(kern.transpose(0, 1, 2, 4, 3).reshape(2, t_, w_, n_), (1, 1, 1, g_))

    def rows_tiled(x):
        return jnp.tile(x.transpose(0, 3, 1, 2).reshape(2, n_, gp), (1, g_, 1))

    bt_re, bt_im = rows_tiled(bb_re), rows_tiled(bb_im)
    ct_re, ct_im = rows_tiled(cr.transpose(0, 1, 3, 2)), rows_tiled(ci.transpose(0, 1, 3, 2))
    pr = pw_re.transpose(1, 0, 2, 3).reshape(2, t_ + 1, gp)
    pi_ = pw_im.transpose(1, 0, 2, 3).reshape(2, t_ + 1, gp)
    full = lambda s: pl.BlockSpec(s, lambda t: (0,) * len(s))
    m_tot, q_all, n_t = pl.pallas_call(
        _s5_ops_kernel,
        grid=(t_,),
        in_specs=[full((2, t_ + 1, gp))] * 2 + [full((2, w_, gp))] * 4 + [full((2, t_, w_, w_)), full((1, w_))],
        out_specs=[pl.BlockSpec((w_, t_ * w_), lambda t: (t, 0)),
                   pl.BlockSpec((w_, 4 * gp), lambda t: (t, 0)),
                   pl.BlockSpec((w_, 4 * gp), lambda t: (t, 0))],
        out_shape=[jax.ShapeDtypeStruct((t_ * w_, t_ * w_), BF16),
                   jax.ShapeDtypeStruct((t_ * w_, 4 * gp), BF16),
                   jax.ShapeDtypeStruct((t_ * w_, 4 * gp), BF16)],
        compiler_params=_cparams(("parallel",)), name="s5_ops",
    )(pr, pi_, bt_re, bt_im, ct_re, ct_im, k_tile, d.astype(F32)[None])
    a_t = jnp.concatenate([pr[0, t_], pi_[0, t_], pr[1, t_], pi_[1, t_]])[None]
    return m_tot, q_all, n_t, a_t


def _s5_ops_kernel(pr_ref, pi_ref, btr_ref, bti_ref, ctr_ref, cti_ref, kt_ref, d_ref, m_ref, q_ref, nt_ref):
    t = pl.program_id(0)
    t_ = S5_T
    w_, gp = btr_ref.shape[1], btr_ref.shape[2]
    row_g = lax.broadcasted_iota(jnp.int32, (w_, gp), 0) // S5_GROUP
    col_g = lax.broadcasted_iota(jnp.int32, (w_, gp), 1) // S5_STATE
    same = row_g == col_g
    for dr in range(2):
        e_q = (t_ - 1 - t) if dr == 0 else t
        e_n = (t + 1) if dr == 0 else (t_ - t)
        b_re = jnp.where(same, btr_ref[dr], 0.0)
        b_im = jnp.where(same, bti_ref[dr], 0.0)
        q_re, q_im = _cmul(b_re, b_im, pr_ref[dr, pl.ds(e_q, 1), :], pi_ref[dr, pl.ds(e_q, 1), :])
        c_re = jnp.where(same, ctr_ref[dr], 0.0)
        c_im = jnp.where(same, cti_ref[dr], 0.0)
        n_re, n_im = _cmul(c_re, c_im, pr_ref[dr, pl.ds(e_n, 1), :], pi_ref[dr, pl.ds(e_n, 1), :])
        base = 2 * dr * gp
        q_ref[:, base:base + gp] = q_re.astype(BF16)
        q_ref[:, base + gp:base + 2 * gp] = q_im.astype(BF16)
        nt_ref[:, base:base + gp] = n_re.astype(BF16)
        nt_ref[:, base + gp:base + 2 * gp] = (-n_im).astype(BF16)
    r2 = lax.broadcasted_iota(jnp.int32, (w_, w_), 0)
    c2 = lax.broadcasted_iota(jnp.int32, (w_, w_), 1)
    same2 = (r2 // S5_GROUP) == (c2 // S5_GROUP)
    skip = jnp.where(r2 == c2, d_ref[...], 0.0)
    for i in range(t_):
        k_f = kt_ref[0, jnp.maximum(i - t, 0)] * jnp.where(i >= t, 1.0, 0.0)
        k_b = kt_ref[1, jnp.maximum(t - i, 0)] * jnp.where(t >= i, 1.0, 0.0)
        blk = jnp.where(same2, k_f + k_b, 0.0) + skip * jnp.where(t == i, 1.0, 0.0)
        m_ref[:, i * w_:(i + 1) * w_] = blk.astype(BF16)


def _s5_mix(u, ops, h0):
    m_tot, q_all, n_t, a_t = ops
    b, k, tw = u.shape
    rows = b * k
    uu = u.reshape(rows, tw)
    tm = min(rows, 512)
    g = _s5_increments(uu, q_all, tm=tm, tn=2048)
    h, e = _s5_scan(g.reshape(b, k, S5_STATE_W), a_t, h0)
    y = _s5_outputs(uu, [hc.reshape(rows, -1) for hc in h], m_tot, n_t, tm=tm, tn=512)
    return y.reshape(b, k, tw), e


def _hy_filter_kernel(f_ref, w1_ref, b1_ref, w2_ref, b2_ref, w3_ref, fr_ref, dl_ref, o_ref, *, half_len, tile):
    feats = f_ref[...]
    h = jnp.dot(feats, w1_ref[...], preferred_element_type=F32, precision=HI) + b1_ref[...]
    h = jnp.sin(fr_ref[0:1, :] * h)
    h = jnp.dot(h, w2_ref[...], preferred_element_type=F32, precision=HI) + b2_ref[...]
    h = jnp.sin(fr_ref[1:2, :] * h)
    o = jnp.dot(h.astype(BF16), w3_ref[...].astype(BF16), preferred_element_type=F32)
    t = feats[:, 0:1]
    o = o * jnp.exp(-t * dl_ref[...])
    nw = HY_ORDER * HY_WIDTH
    n = pl.program_id(0) * tile + lax.broadcasted_iota(jnp.int32, (tile, 1), 0)
    k = jnp.where(n < half_len, o[:, :nw], jnp.where(n == half_len, 0.0, o[:, nw:]))
    o_ref[...] = k.reshape(o_ref.shape)


def _hy_filters(n_tokens, freq, w1, b1, w2, b2, w3):
    n2 = 2 * n_tokens
    idx = np.arange(n2, dtype=np.float64)
    pos = np.where(idx <= n_tokens, idx, n2 - idx)[:, None]
    t = pos / max(n_tokens - 1, 1)
    bands = np.linspace(1e-4, HY_BANDS - 1, HY_BANDS)[None]
    ang = 2.0 * math.pi * bands * pos / n_tokens
    feats = np.concatenate([t, np.cos(ang), -np.sin(ang)], axis=-1)
    kpad = LANES - HY_EMB
    feats = jnp.asarray(np.pad(feats, ((0, 0), (0, kpad))), F32)
    w1p = jnp.pad(w1.astype(F32), ((0, kpad), (0, 0)))
    deltas = jnp.abs(jnp.linspace(HY_MIN_DECAY, HY_MAX_DECAY, HY_WIDTH, dtype=F32))
    dl = jnp.tile(deltas, 2 * HY_ORDER)[None]
    tile = min(n2, 1024)
    nw = 2 * HY_ORDER * HY_WIDTH
    full = lambda s: pl.BlockSpec(s, lambda i: (0,) * len(s))
    return pl.pallas_call(
        functools.partial(_hy_filter_kernel, half_len=n_tokens, tile=tile),
        grid=(n2 // tile,),
        in_specs=[pl.BlockSpec((tile, LANES), lambda i: (i, 0)),
                  full((LANES, HY_FFN)), full((1, HY_FFN)), full((HY_FFN, HY_FFN)), full((1, HY_FFN)),
                  full((HY_FFN, nw)), full((2, HY_FFN)), full((1, nw))],
        out_specs=pl.BlockSpec((tile // SUBLANES, SUBLANES, HY_ORDER * HY_WIDTH), lambda i: (i, 0, 0)),
        out_shape=jax.ShapeDtypeStruct((n2 // SUBLANES, SUBLANES, HY_ORDER * HY_WIDTH), F32),
        compiler_params=_cparams(("parallel",)), name="hy_filter",
    )(feats, w1p, b1.astype(F32)[None], w2.astype(F32), b2.astype(F32)[None], w3.astype(F32),
      freq.astype(F32), dl)


FFT_G = 4
FFT_UNROLL = 8


def _fft_tables(n):
    s = n // SUBLANES
    nst = int(round(math.log2(s)))
    runs = []
    half = s // 2
    while half >= FFT_G:
        runs.append(-2 * np.pi * np.arange(half) / (2 * half))
        half //= 2
    ang_s = np.concatenate(runs)
    tw_slab = np.stack([np.cos(ang_s), np.sin(ang_s)]).astype(np.float32)
    tw_slab = np.broadcast_to(tw_slab[..., None, None], (2, ang_s.size, SUBLANES, LANES)).copy()
    pos = np.arange(s)
    rev = np.zeros(s, np.int64)
    for bit in range(nst):
        rev |= ((pos >> bit) & 1) << (nst - 1 - bit)
    ang = -2 * np.pi * (rev[:, None] * np.arange(SUBLANES)[None, :]) / n
    tw_mid = np.stack([np.cos(ang), np.sin(ang)]).astype(np.float32)
    tw_mid = np.broadcast_to(tw_mid[..., None], (2, s, SUBLANES, LANES)).copy()
    t = np.arange(n)
    ang2 = -2 * np.pi * t / (2 * n)
    mod = np.stack([np.cos(ang2), np.sin(ang2)]).astype(np.float32).reshape(2, s, SUBLANES)
    mod = np.broadcast_to(mod[..., None], (2, s, SUBLANES, LANES)).copy()
    return jnp.asarray(tw_slab), jnp.asarray(tw_mid), jnp.asarray(mod)


def _sub_patterns():
    sub = lax.broadcasted_iota(jnp.int32, (SUBLANES, LANES), 0)

    def table(vals):
        out = jnp.full((SUBLANES, LANES), vals[0], F32)
        for k in range(1, SUBLANES):
            out = jnp.where(sub == k, np.float32(vals[k]), out)
        return out

    pats = {}
    for dist in (4, 2, 1):
        lo = (sub & dist) == 0
        sgn = jnp.where(lo, 1.0, -1.0).astype(F32)
        wr = [1.0] * SUBLANES
        wi = [0.0] * SUBLANES
        for k in range(SUBLANES):
            if k & dist:
                e = (k % dist) * (SUBLANES // (2 * dist))
                wr[k] = math.cos(-2 * math.pi * e / SUBLANES)
                wi[k] = math.sin(-2 * math.pi * e / SUBLANES)
        pats[dist] = (lo, sgn, table(wr), table(wi))
    pats["quarter"] = (sub & 3) == 3
    return pats


def _dft8_fwd(vr, vi, pats):
    for dist in (4, 2, 1):
        lo, sgn, wr, wi = pats[dist]
        up_r = pltpu.roll(vr, SUBLANES - dist, 0)
        up_i = pltpu.roll(vi, SUBLANES - dist, 0)
        if dist == 4:
            pr, pi_ = up_r, up_i
        else:
            pr = jnp.where(lo, up_r, pltpu.roll(vr, dist, 0))
            pi_ = jnp.where(lo, up_i, pltpu.roll(vi, dist, 0))
        tr = pr + sgn * vr
        ti = pi_ + sgn * vi
        if dist == 1:
            vr, vi = tr, ti
        elif dist == 2:
            qt = pats["quarter"]
            vr, vi = jnp.where(qt, ti, tr), jnp.where(qt, -tr, ti)
        else:
            vr, vi = _cmul(tr, ti, wr, wi)
    return vr, vi


def _dft8_inv(vr, vi, pats):
    for dist in (1, 2, 4):
        lo, sgn, wr, wi = pats[dist]
        if dist == 2:
            qt = pats["quarter"]
            vr, vi = jnp.where(qt, -vi, vr), jnp.where(qt, vr, vi)
        elif dist == 4:
            vr, vi = _cmul(vr, vi, wr, -wi)
        up_r = pltpu.roll(vr, SUBLANES - dist, 0)
        up_i = pltpu.roll(vi, SUBLANES - dist, 0)
        if dist == 4:
            pr, pi_ = up_r, up_i
        else:
            pr = jnp.where(lo, up_r, pltpu.roll(vr, dist, 0))
            pi_ = jnp.where(lo, up_i, pltpu.roll(vi, dist, 0))
        vr = pr + sgn * vr
        vi = pi_ + sgn * vi
    return vr, vi


def _slab_stage(re, im, tw, half, slabs, inverse):
    per_block = half // FFT_G
    nblk = slabs // (2 * half)
    unroll = min(FFT_UNROLL, nblk * per_block)
    tw_off = slabs - 2 * half
    if nblk >= unroll:
        blocks_per_it = unroll // per_block
        trips = nblk // blocks_per_it
        offsets = [(b * 2 * half + jc * FFT_G, jc * FFT_G) for b in range(blocks_per_it) for jc in range(per_block)]
        data_step, tw_step = blocks_per_it * 2 * half, 0
    else:
        chunks_per_it = unroll // nblk
        trips = per_block // chunks_per_it
        offsets = [(b * 2 * half + k * FFT_G, k * FFT_G) for b in range(nblk) for k in range(chunks_per_it)]
        data_step = tw_step = chunks_per_it * FFT_G

    def body(c, carry):
        d0 = pl.multiple_of(c * data_step, FFT_G)
        t0 = pl.multiple_of(tw_off + c * tw_step, FFT_G)
        twiddles = {}
        for d_off, t_off in offsets:
            if t_off not in twiddles:
                tws = pl.ds(t0 + t_off, FFT_G)
                twiddles[t_off] = (tw[0, tws], tw[1, tws])
            wr, wi = twiddles[t_off]
            lo = pl.ds(d0 + d_off, FFT_G)
            hi = pl.ds(d0 + d_off + half, FFT_G)
            ar, ai, br, bi = re[lo], im[lo], re[hi], im[hi]
            if inverse:
                br, bi = br * wr + bi * wi, bi * wr - br * wi
                re[lo] = ar + br
                im[lo] = ai + bi
                re[hi] = ar - br
                im[hi] = ai - bi
            else:
                re[lo] = ar + br
                im[lo] = ai + bi
                dr, di = ar - br, ai - bi
                re[hi] = dr * wr - di * wi
                im[hi] = dr * wi + di * wr
        return carry

    lax.fori_loop(0, trips, body, 0)


def _fft_forward_big(re, im, tw, slabs):
    half = slabs // 2
    while half >= 4:
        _slab_stage(re, im, tw, half, slabs, inverse=False)
        half //= 2


def _fft_inverse_big(re, im, tw, slabs):
    half = 4
    while half <= slabs // 2:
        _slab_stage(re, im, tw, half, slabs, inverse=True)
        half *= 2


def _radix4_fwd(x):
    (x0r, x0i), (x1r, x1i), (x2r, x2i), (x3r, x3i) = x
    y0r, y0i = x0r + x2r, x0i + x2i
    y2r, y2i = x0r - x2r, x0i - x2i
    y1r, y1i = x1r + x3r, x1i + x3i
    dr, di = x1r - x3r, x1i - x3i
    y3r, y3i = di, -dr
    return [(y0r + y1r, y0i + y1i), (y0r - y1r, y0i - y1i), (y2r + y3r, y2i + y3i), (y2r - y3r, y2i - y3i)]


def _radix4_inv(z):
    (z0r, z0i), (z1r, z1i), (z2r, z2i), (z3r, z3i) = z
    y0r, y0i = z0r + z1r, z0i + z1i
    y1r, y1i = z0r - z1r, z0i - z1i
    y2r, y2i = z2r + z3r, z2i + z3i
    y3r, y3i = z2r - z3r, z2i - z3i
    qr, qi = -y3i, y3r
    return [(y0r + y2r, y0i + y2i), (y1r + qr, y1i + qi), (y0r - y2r, y0i - y2i), (y1r - qr, y1i - qi)]


def _fft_middle(re, im, twm_ref, slabs, pats, spec=None, out=None):
    def body(q, carry):
        p0 = q * FFT_G
        x = [(re[p0 + g], im[p0 + g]) for g in range(FFT_G)]
        z = _radix4_fwd(x)
        res = []
        for g in range(FFT_G):
            twr = twm_ref[0, p0 + g]
            twi = twm_ref[1, p0 + g]
            vr, vi = _cmul(z[g][0], z[g][1], twr, twi)
            vr, vi = _dft8_fwd(vr, vi, pats)
            if spec is None:
                out[0][0, 0, p0 + g] = vr * out[2]
                out[1][0, 0, p0 + g] = vi * out[2]
            else:
                vr, vi = _cmul(vr, vi, spec[0][0, 0, p0 + g], spec[1][0, 0, p0 + g])
                vr, vi = _dft8_inv(vr, vi, pats)
                res.append(_cmul(vr, vi, twr, -twi))
        if spec is not None:
            x = _radix4_inv(res)
            for g in range(FFT_G):
                re[p0 + g] = x[g][0]
                im[p0 + g] = x[g][1]
        return carry

    lax.fori_loop(0, slabs // FFT_G, body, 0)


def _hy_spec_kernel(tw_ref, k_ref, twm_ref, mod_ref, sr_ref, si_ref, re, im, *, slabs):
    h = pl.program_id(2)
    pats = _sub_patterns()
    a = k_ref[0:slabs]
    b = k_ref[slabs:2 * slabs]

    @pl.when(h == 0)
    def _():
        re[...] = a + b
        im[...] = jnp.zeros_like(a)

    @pl.when(h == 1)
    def _():
        dlt = a - b
        re[...] = dlt * mod_ref[0]
        im[...] = dlt * mod_ref[1]

    _fft_forward_big(re, im, tw_ref, slabs)
    scale = np.float32(1.0 / (2 * SUBLANES * slabs))
    _fft_middle(re, im, twm_ref, slabs, pats, out=(sr_ref, si_ref, scale))


def _hy_spectra(k, n_tokens):
    slabs = n_tokens // SUBLANES
    tw, twm, mod = _fft_tables(n_tokens)
    nt = HY_WIDTH // LANES
    shape = jax.ShapeDtypeStruct((HY_ORDER, 2, slabs, SUBLANES, HY_WIDTH), F32)
    spec_out = pl.BlockSpec((1, 1, slabs, SUBLANES, LANES), lambda o, j, h: (o, h, 0, 0, j))
    tab = pl.BlockSpec((2, slabs, SUBLANES, LANES), lambda o, j, h: (0, 0, 0, 0), pipeline_mode=pl.Buffered(1))
    return pl.pallas_call(
        functools.partial(_hy_spec_kernel, slabs=slabs),
        grid=(HY_ORDER, nt, 2),
        in_specs=[pl.BlockSpec(tw.shape, lambda o, j, h: (0, 0, 0, 0), pipeline_mode=pl.Buffered(1)),
                  pl.BlockSpec((2 * slabs, SUBLANES, LANES), lambda o, j, h: (0, 0, o * nt + j)),
                  tab, tab],
        out_specs=[spec_out, spec_out],
        out_shape=[shape, shape],
        scratch_shapes=[pltpu.VMEM((slabs, SUBLANES, LANES), F32)] * 2,
        compiler_params=_cparams(("parallel", "parallel", "arbitrary")), name="hy_spec",
    )(tw, k, twm, mod)


def _hy_conv_kernel(tw_ref, u_ref, g_ref, sr_ref, si_ref, twm_ref, mod_ref, b_ref, o_ref, re, im, *, slabs):
    h = pl.program_id(2)
    pats = _sub_patterns()
    slab_shape = (slabs, SUBLANES, LANES)
    n = slabs * SUBLANES

    @pl.when(h == 0)
    def _():
        re[...] = u_ref[0].reshape(slab_shape)
        im[...] = u_ref[1].reshape(slab_shape)

    @pl.when(h == 1)
    def _():
        ur, ui = u_ref[0].reshape(slab_shape), u_ref[1].reshape(slab_shape)
        mr, mi = mod_ref[0], mod_ref[1]
        re[...] = ur * mr - ui * mi
        im[...] = ur * mi + ui * mr

    _fft_forward_big(re, im, tw_ref, slabs)
    _fft_middle(re, im, twm_ref, slabs, pats, spec=(sr_ref, si_ref))
    _fft_inverse_big(re, im, tw_ref, slabs)

    @pl.when(h == 0)
    def _():
        o_ref[0] = re[...].reshape(n, LANES)
        o_ref[1] = im[...].reshape(n, LANES)

    @pl.when(h == 1)
    def _():
        yr, yi = re[...], im[...]
        mr, mi = mod_ref[0], mod_ref[1]
        y0 = o_ref[0] + (yr * mr + yi * mi).reshape(n, LANES)
        y1 = o_ref[1] + (yi * mr - yr * mi).reshape(n, LANES)
        o_ref[0] = g_ref[0].astype(F32) * (y0 + b_ref[...] * u_ref[0])
        o_ref[1] = g_ref[1].astype(F32) * (y1 + b_ref[...] * u_ref[1])


def _hy_order(u, u_blk, gate, gate_blk, spec_re, spec_im, order, bias):
    b, n, _ = u.shape
    w = HY_WIDTH
    slabs = n // SUBLANES
    tw, twm, mod = _fft_tables(n)
    nt = w // LANES
    tab = pl.BlockSpec((2, slabs, SUBLANES, LANES), lambda j, p, h: (0, 0, 0, 0), pipeline_mode=pl.Buffered(1))
    spec_in = pl.BlockSpec((1, 1, slabs, SUBLANES, LANES), lambda j, p, h: (order, h, 0, 0, j))

    def io(blk):
        return pl.BlockSpec((2, n, LANES), lambda j, p, h: (p, 0, blk * nt + j))

    return pl.pallas_call(
        functools.partial(_hy_conv_kernel, slabs=slabs),
        grid=(nt, b // 2, 2),
        in_specs=[pl.BlockSpec(tw.shape, lambda j, p, h: (0, 0, 0, 0), pipeline_mode=pl.Buffered(1)),
                  io(u_blk), io(gate_blk), spec_in, spec_in, tab, tab,
                  pl.BlockSpec((1, LANES), lambda j, p, h: (0, j))],
        out_specs=io(0),
        out_shape=jax.ShapeDtypeStruct((b, n, w), F32),
        scratch_shapes=[pltpu.VMEM((slabs, SUBLANES, LANES), F32)] * 2,
        compiler_params=_cparams(("parallel", "parallel", "arbitrary")), name="hy_conv",
    )(tw, u, gate, spec_re, spec_im, twm, mod, bias.astype(F32)[None])


HY_SHORT_ROWS = 64


def _hy_short_kernel(z_ref, w_ref, b_ref, v_ref, g_ref, *, n):
    j = pl.program_id(1)
    rb = min(HY_SHORT_ROWS, n)
    grp = 2 * SUBLANES
    row = lax.broadcasted_iota(jnp.int32, (rb, z_ref.shape[2]), 0)
    zero_row = jnp.zeros((1, z_ref.shape[2]), F32)
    w0, w1, w2, bias = w_ref[0:1, :], w_ref[1:2, :], w_ref[2:3, :], b_ref[...]
    for c in range(n // rb):
        r0 = c * rb
        x = z_ref[0, r0:r0 + rb].astype(F32)
        before = z_ref[0, r0 - grp:r0].astype(F32)[grp - 1:grp] if c > 0 else zero_row
        after = z_ref[0, r0 + rb:r0 + rb + grp].astype(F32)[0:1] if r0 + rb < n else zero_row
        prev = jnp.where(row == 0, before, pltpu.roll(x, 1, 0))
        nxt = jnp.where(row == rb - 1, after, pltpu.roll(x, rb - 1, 0))
        y = prev * w0 + x * w1 + nxt * w2 + bias

        @pl.when(j == 0)
        def _():
            v_ref[0, r0:r0 + rb] = y

        @pl.when(j > 0)
        def _():
            g_ref[0, r0:r0 + rb] = y.astype(g_ref.dtype)


def _hy_short(z_mix, conv_w, conv_b):
    b, n, _ = z_mix.shape
    w_ = HY_WIDTH
    w = conv_w.reshape(3, 3 * w_).astype(F32)
    col0 = BRANCH_W // w_
    return pl.pallas_call(
        functools.partial(_hy_short_kernel, n=n),
        grid=(b, 3),
        in_specs=[pl.BlockSpec((1, n, w_), lambda bi, j: (bi, 0, col0 + j)),
                  pl.BlockSpec((3, w_), lambda bi, j: (0, j)),
                  pl.BlockSpec((1, w_), lambda bi, j: (0, j))],
        out_specs=[pl.BlockSpec((1, n, w_), lambda bi, j: (bi, 0, 0)),
                   pl.BlockSpec((1, n, w_), lambda bi, j: (bi, 0, jnp.maximum(j - 1, 0)))],
        out_shape=[jax.ShapeDtypeStruct((b, n, w_), F32), jax.ShapeDtypeStruct((b, n, 2 * w_), BF16)],
        compiler_params=_cparams(("parallel", "arbitrary")), name="hy_short",
    )(z_mix, w, conv_b.astype(F32)[None])


def _hyena(z_mix, conv_w, conv_b, spec_re, spec_im, bias):
    v, gates = _hy_short(z_mix, conv_w, conv_b)
    v1 = _hy_order(v, 0, gates, 0, spec_re, spec_im, 0, bias[0])
    return _hy_order(v1, 0, gates, 1, spec_re, spec_im, 1, bias[1])


NA_QROWS = 4
NA_KROWS = NA_QROWS + NA_WIN_H
NA_KPART = 256


def _softmax_parts(parts):
    m = None
    for s in parts:
        mm = jnp.max(s, axis=-1, keepdims=True)
        m = mm if m is None else jnp.maximum(m, mm)
    return m


def _na_kernel(*refs, kparts):
    q_ref, k_refs, v_refs = refs[0], refs[1:1 + kparts], refs[1 + kparts:1 + 2 * kparts]
    kc_ref, vc_ref, bias_ref, o_ref = refs[1 + 2 * kparts:]
    q = q_ref[0] * jnp.asarray(NA_HEAD_DIM ** -0.5, q_ref.dtype)
    k = jnp.concatenate([r[0] for r in k_refs], axis=0).astype(BF16)
    v = jnp.concatenate([r[0] for r in v_refs], axis=0).astype(BF16)
    kc = kc_ref[0].astype(BF16)
    vc = vc_ref[0].astype(BF16)
    nt = (((1,), (1,)), ((), ()))
    outs = []
    for h in range(NA_HEADS):
        sl = slice(h * NA_HEAD_DIM, (h + 1) * NA_HEAD_DIM)
        qh = q[:, sl].astype(BF16)
        s_lat = lax.dot_general(qh, k[:, sl], nt, preferred_element_type=F32) + bias_ref[0, h]
        s_ctx = lax.dot_general(qh, kc[:, sl], nt, preferred_element_type=F32)
        m = _softmax_parts([s_lat, s_ctx])
        p_lat = jnp.exp(s_lat - m)
        p_ctx = jnp.exp(s_ctx - m)
        den = jnp.sum(p_lat, axis=-1, keepdims=True) + jnp.sum(p_ctx, axis=-1, keepdims=True)
        o = jnp.dot(p_lat.astype(BF16), v[:, sl], preferred_element_type=F32) \
            + jnp.dot(p_ctx.astype(BF16), vc[:, sl], preferred_element_type=F32)
        outs.append(o / den)
    o_ref[0] = jnp.concatenate(outs, axis=-1)


def _na_bias(rpb, rows):
    kh = min(NA_WIN_H, rows)
    col = np.arange(GRID_W)
    col_start = np.clip(col - NA_WIN_W // 2, 0, GRID_W - NA_WIN_W)
    col_ok = (col[None] >= col_start[:, None]) & (col[None] < col_start[:, None] + NA_WIN_W)
    off_c = np.clip(col[None] - col[:, None], -(NA_WIN_W - 1), NA_WIN_W - 1) + (NA_WIN_W - 1)
    nblk = rows // NA_QROWS
    n_r, n_c = 2 * NA_WIN_H - 1, 2 * NA_WIN_W - 1
    table = jnp.pad(rpb.astype(F32), ((0, 0), (0, 1), (0, 1)), constant_values=NEG_INF)
    sel_c = np.eye(n_c + 1, dtype=np.float32)[np.where(col_ok, off_c, n_c)]
    blocks = jnp.einsum('hab,qcb->hqac', table, sel_c, precision=HI).reshape(NA_HEADS, GRID_W, (n_r + 1) * GRID_W)
    blocks_odd = jnp.roll(blocks, -GRID_W, axis=-1)
    row_off = []
    for j in (0, 1, nblk - 1):
        qr = j * NA_QROWS + np.arange(NA_QROWS)
        ws = int(np.clip(j * NA_QROWS - NA_WIN_H // 2, 0, rows - NA_KROWS))
        kr = ws + np.arange(NA_KROWS)
        start = np.clip(qr - kh // 2, 0, rows - kh)
        row_ok = (kr[None] >= start[:, None]) & (kr[None] < start[:, None] + kh)
        off_r = kr[None] - qr[:, None] + (NA_WIN_H - 1)
        row_off.append(np.where(row_ok, off_r, n_r))
    row_off = np.stack(row_off)
    wide = (n_r + 1) * GRID_W
    return pl.pallas_call(
        functools.partial(_na_bias_kernel, row_off=row_off),
        grid=(NA_HEADS,),
        in_specs=[pl.BlockSpec((1, GRID_W, wide), lambda h: (h, 0, 0))] * 2,
        out_specs=pl.BlockSpec((3, 1, NA_QROWS * GRID_W, NA_KROWS * GRID_W), lambda h: (0, h, 0, 0)),
        out_shape=jax.ShapeDtypeStruct((3, NA_HEADS, NA_QROWS * GRID_W, NA_KROWS * GRID_W), F32),
        compiler_params=_cparams(("parallel",)), name="na_bias",
    )(blocks, blocks_odd)


def _na_bias_kernel(even_ref, odd_ref, o_ref, *, row_off):
    n_cls, n_q, n_k = row_off.shape
    masked = 2 * NA_WIN_H - 1
    pair = 2 * GRID_W
    lane = lax.broadcasted_iota(jnp.int32, (GRID_W, pair), 1)

    def left(a):
        src, start = (even_ref, a) if a % 2 == 0 else (odd_ref, a - 1)
        return src[0, :, start * GRID_W:start * GRID_W + pair]

    def right(a):
        assert a % 2 == 1
        return even_ref[0, :, (a - 1) * GRID_W:(a - 1) * GRID_W + pair]

    for t in range(n_cls):
        for r in range(n_q):
            for kp in range(n_k // 2):
                a0, a1 = int(row_off[t, r, 2 * kp]), int(row_off[t, r, 2 * kp + 1])
                if a0 != masked and a1 == a0 + 1:
                    blk = left(a0)
                elif a0 == masked and a1 == masked:
                    blk = jnp.full((GRID_W, pair), NEG_INF, F32)
                elif a1 == masked:
                    blk = jnp.where(lane < GRID_W, left(a0), NEG_INF)
                else:
                    blk = jnp.where(lane >= GRID_W, right(a1), NEG_INF)
                o_ref[t, 0, r * GRID_W:(r + 1) * GRID_W, kp * pair:(kp + 1) * pair] = blk


def _na_attention(z_mix, zc_mix, rpb):
    b, n, _ = z_mix.shape
    n_ctx = zc_mix.shape[1]
    rows = n // GRID_W
    nblk = rows // NA_QROWS
    tq = NA_QROWS * GRID_W
    bias = _na_bias(rpb, rows)
    qb, kb, vb = NA_COL0 // BRANCH_W, NA_COL0 // BRANCH_W + 1, NA_COL0 // BRANCH_W + 2
    kparts = NA_KROWS * GRID_W // NA_KPART
    max_k0 = (rows - NA_KROWS) * GRID_W // NA_KPART

    def k0_of(j):
        per_step = NA_QROWS * GRID_W // NA_KPART
        lead = (NA_WIN_H // 2) * GRID_W // NA_KPART
        return jnp.clip(j * per_step - lead, 0, max_k0)

    def kspec(part, blk):
        return pl.BlockSpec((1, NA_KPART, BRANCH_W), lambda bi, j: (bi, k0_of(j) + part, blk))

    def cls(j):
        return jnp.where(j == 0, 0, jnp.where(j == nblk - 1, 2, 1))

    return pl.pallas_call(
        functools.partial(_na_kernel, kparts=kparts),
        grid=(b, nblk),
        in_specs=[pl.BlockSpec((1, tq, BRANCH_W), lambda bi, j: (bi, j, qb))]
        + [kspec(p, kb) for p in range(kparts)] + [kspec(p, vb) for p in range(kparts)]
        + [pl.BlockSpec((1, n_ctx, BRANCH_W), lambda bi, j: (bi, 0, kb)),
           pl.BlockSpec((1, n_ctx, BRANCH_W), lambda bi, j: (bi, 0, vb)),
           pl.BlockSpec((1, NA_HEADS, tq, NA_KROWS * GRID_W), lambda bi, j: (cls(j), 0, 0, 0))],
        out_specs=pl.BlockSpec((1, tq, BRANCH_W), lambda bi, j: (bi, j, 0)),
        out_shape=jax.ShapeDtypeStruct((b, n, BRANCH_W), F32),
        compiler_params=_cparams(("parallel", "arbitrary")), name="na_attn",
    )(z_mix, *([z_mix] * (2 * kparts)), zc_mix, zc_mix, bias)


def _rope_tables(n_tokens, head_dim, heads):
    t = np.arange(n_tokens)
    row = (t // GRID_W).astype(np.float64)
    col = (t % GRID_W).astype(np.float64)
    half = head_dim // 2
    inv = ROPE_BASE ** (-(np.arange(0, half, 2, dtype=np.float64) / half))
    ang = np.concatenate([row[:, None] * inv, col[:, None] * inv], axis=-1)
    cos, sin = np.cos(ang), np.sin(ang)
    cos_t = np.tile(np.concatenate([cos, cos], axis=-1), (1, heads))
    sin_t = np.tile(np.concatenate([-sin, sin], axis=-1), (1, heads))
    return jnp.asarray(cos_t, F32), jnp.asarray(sin_t, F32)


def _rope(x, cos_t, sin_t):
    w = x.shape[-1]
    half = SW_HEAD_DIM // 2
    lane = lax.broadcasted_iota(jnp.int32, x.shape, 1)
    first = (lane % SW_HEAD_DIM) < half
    partner = jnp.where(first, pltpu.roll(x, w - half, 1), pltpu.roll(x, half, 1))
    return x * cos_t + partner * sin_t


SW_QBLK = 2 * SW_BLOCK
SW_KPARTS = SW_QBLK // SW_BLOCK + 2


def _sw_kernel(q_ref, kv0, kv1, kv2, kv3, kc_ref, vc_ref, sink_ref, o_ref, *, nblk):
    i = pl.program_id(1)
    kvw = SW_KV_HEADS * SW_HEAD_DIM
    q = q_ref[0]
    kv = jnp.concatenate([kv0[0], kv1[0], kv2[0], kv3[0]], axis=0)
    kc = kc_ref[0].astype(BF16)
    vc = vc_ref[0].astype(BF16)
    g_ = SW_HEADS // SW_KV_HEADS
    rows = g_ * SW_QBLK
    span = SW_KPARTS * SW_BLOCK
    r = lax.broadcasted_iota(jnp.int32, (rows, span), 0) % SW_QBLK
    c = lax.broadcasted_iota(jnp.int32, (rows, span), 1)
    diff = c - SW_BLOCK - r
    blk = i * (SW_QBLK // SW_BLOCK) - 1 + c // SW_BLOCK
    ok = (jnp.abs(diff) <= SW_WINDOW) & (blk >= 0) & (blk < nblk)
    nt = (((1,), (1,)), ((), ()))
    outs = []
    for kvh in range(SW_KV_HEADS):
        ksl = slice(kvh * SW_HEAD_DIM, (kvh + 1) * SW_HEAD_DIM)
        vsl = slice(kvw + kvh * SW_HEAD_DIM, kvw + (kvh + 1) * SW_HEAD_DIM)
        qg = jnp.concatenate([q[:, (kvh * g_ + g) * SW_HEAD_DIM:(kvh * g_ + g + 1) * SW_HEAD_DIM] for g in range(g_)],
                             axis=0)
        s_lat = jnp.where(ok, lax.dot_general(qg, kv[:, ksl], nt, preferred_element_type=F32), NEG_INF)
        s_ctx = lax.dot_general(qg, kc[:, ksl], nt, preferred_element_type=F32)
        rr = lax.broadcasted_iota(jnp.int32, (rows, 1), 0)
        s_sink = jnp.zeros((rows, 1), F32)
        for g in range(g_):
            s_sink = jnp.where(rr // SW_QBLK == g, sink_ref[kvh * g_ + g], s_sink)
        m = jnp.maximum(_softmax_parts([s_lat, s_ctx]), s_sink)
        p_lat = jnp.exp(s_lat - m)
        p_ctx = jnp.exp(s_ctx - m)
        den = jnp.sum(p_lat, axis=-1, keepdims=True) + jnp.sum(p_ctx, axis=-1, keepdims=True) + jnp.exp(s_sink - m)
        o = jnp.dot(p_lat.astype(BF16), kv[:, vsl], preferred_element_type=F32) \
            + jnp.dot(p_ctx.astype(BF16), vc[:, ksl], preferred_element_type=F32)
        o = o / den
        outs += [o[g * SW_QBLK:(g + 1) * SW_QBLK] for g in range(g_)]
    o_ref[0] = jnp.concatenate(outs, axis=-1)


def _sw_attention(q_r, kv_r, zc_mix, sink):
    b, n, _ = q_r.shape
    n_ctx = zc_mix.shape[1]
    nblk = n // SW_BLOCK
    qw = SW_HEADS * SW_HEAD_DIM
    kvw = SW_KV_HEADS * SW_HEAD_DIM
    k_blk = (SW_COL0 + qw) // kvw
    per_q = SW_QBLK // SW_BLOCK

    def kpart(part):
        return pl.BlockSpec((1, SW_BLOCK, qw),
                            lambda bi, i: (bi, jnp.clip(i * per_q - 1 + part, 0, nblk - 1), 0))

    return pl.pallas_call(
        functools.partial(_sw_kernel, nblk=nblk),
        grid=(b, n // SW_QBLK),
        in_specs=[pl.BlockSpec((1, SW_QBLK, qw), lambda bi, i: (bi, i, 0))]
        + [kpart(p) for p in range(SW_KPARTS)]
        + [pl.BlockSpec((1, n_ctx, kvw), lambda bi, i: (bi, 0, k_blk)),
           pl.BlockSpec((1, n_ctx, kvw), lambda bi, i: (bi, 0, k_blk + 1)),
           pl.BlockSpec(memory_space=pltpu.SMEM)],
        out_specs=pl.BlockSpec((1, SW_QBLK, qw), lambda bi, i: (bi, i, 0)),
        out_shape=jax.ShapeDtypeStruct((b, n, qw), F32),
        compiler_params=_cparams(("parallel", "arbitrary")), name="sw_attn",
    )(q_r, *([kv_r] * SW_KPARTS), zc_mix, zc_mix, sink.astype(F32))


def _ctx_attn_kernel(q_ref, k_ref, v_ref, sink_ref, o_ref, *, heads, kv_heads, dh, use_sink):
    q = q_ref[0] * np.float32(dh ** -0.5)
    k = k_ref[0].astype(BF16)
    v = v_ref[0].astype(BF16)
    g_ = heads // kv_heads
    nt = (((1,), (1,)), ((), ()))
    outs = []
    for h in range(heads):
        kv = h // g_
        s = lax.dot_general(q[:, h * dh:(h + 1) * dh].astype(BF16), k[:, kv * dh:(kv + 1) * dh], nt,
                            preferred_element_type=F32)
        m = jnp.max(s, axis=-1, keepdims=True)
        if use_sink:
            m = jnp.maximum(m, sink_ref[h])
        p = jnp.exp(s - m)
        den = jnp.sum(p, axis=-1, keepdims=True)
        if use_sink:
            den = den + jnp.exp(sink_ref[h] - m)
        outs.append(jnp.dot(p.astype(BF16), v[:, kv * dh:(kv + 1) * dh], preferred_element_type=F32) / den)
    o_ref[0] = jnp.concatenate(outs, axis=-1)


def _ctx_attention(zc_mix, col0, heads, kv_heads, dh, sink):
    b, n, _ = zc_mix.shape
    qw, kvw = heads * dh, kv_heads * dh
    use_sink = sink is not None
    sink_arr = sink.astype(F32) if use_sink else jnp.zeros((heads,), F32)
    return pl.pallas_call(
        functools.partial(_ctx_attn_kernel, heads=heads, kv_heads=kv_heads, dh=dh, use_sink=use_sink),
        grid=(b,),
        in_specs=[pl.BlockSpec((1, n, qw), lambda bi: (bi, 0, col0 // qw)),
                  pl.BlockSpec((1, n, kvw), lambda bi: (bi, 0, (col0 + qw) // kvw)),
                  pl.BlockSpec((1, n, kvw), lambda bi: (bi, 0, (col0 + qw) // kvw + 1)),
                  pl.BlockSpec(memory_space=pltpu.SMEM)],
        out_specs=pl.BlockSpec((1, n, qw), lambda bi: (bi, 0, 0)),
        out_shape=jax.ShapeDtypeStruct((b, n, qw), F32),
        compiler_params=_cparams(("parallel",)), name="ctx_attn",
    )(zc_mix, zc_mix, zc_mix, sink_arr)


def _merge_kernel(x_ref, s5_ref, hy_ref, na_ref, sw_ref, gt_ref, ga_ref, wglu_ref, wb_ref, wo_ref,
                  lg_ref, lb_ref, o_ref, tok_scr):
    rows = tok_scr.shape[1] // S5_T
    for c in range(BRANCH_W // LANES):
        for t in range(S5_T):
            col = t * BRANCH_W + c * LANES
            tok_scr[c, pl.ds(t, rows, stride=S5_T), :] = s5_ref[0, :, col:col + LANES]
    g = jax.nn.gelu(jnp.concatenate([tok_scr[c] for c in range(BRANCH_W // LANES)], axis=-1))
    s5 = g * jax.nn.sigmoid(jnp.dot(g.astype(BF16), wglu_ref[...], preferred_element_type=F32))
    branches = (s5, hy_ref[0], na_ref[0], sw_ref[0])
    acc = None
    for n in range(N_BRANCH):
        proj = jnp.dot(branches[n].astype(BF16), wb_ref[n], preferred_element_type=F32)
        t = gt_ref[0, :, n * D_MODEL:(n + 1) * D_MODEL].astype(F32) * proj
        acc = t if acc is None else acc + t
    mix = jnp.dot(acc.astype(BF16), wo_ref[...], preferred_element_type=F32)
    y = np.float32(DEEPNORM_ALPHA) * x_ref[0] + ga_ref[0] * mix
    o_ref[0] = _layernorm(y) * lg_ref[...] + lb_ref[...]


def _merge(x, s5y, hy, na, sw, gates, g_a, w_glu, w_branch, w_out, ln_g, ln_b, *, tm):
    b, l, d = x.shape
    br = pl.BlockSpec((1, tm, BRANCH_W), lambda bi, i: (bi, i, 0))
    full = lambda s: pl.BlockSpec(s, lambda bi, i: (0,) * len(s), pipeline_mode=pl.Buffered(1))
    return pl.pallas_call(
        _merge_kernel,
        grid=(b, l // tm),
        in_specs=[pl.BlockSpec((1, tm, d), lambda bi, i: (bi, i, 0)),
                  pl.BlockSpec((1, tm // S5_T, S5_T * BRANCH_W), lambda bi, i: (bi, i, 0)), br, br, br,
                  pl.BlockSpec((1, tm, GATE_W), lambda bi, i: (bi, i, 0)),
                  pl.BlockSpec((1, 1, d), lambda bi, i: (bi, 0, 0)),
                  full((BRANCH_W, BRANCH_W)), full((N_BRANCH, BRANCH_W, d)), full((d, d)),
                  full((1, d)), full((1, d))],
        out_specs=pl.BlockSpec((1, tm, d), lambda bi, i: (bi, i, 0)),
        out_shape=jax.ShapeDtypeStruct((b, l, d), F32),
        scratch_shapes=[pltpu.VMEM((BRANCH_W // LANES, tm, LANES), F32)],
        compiler_params=_cparams(("parallel", "parallel")), name="merge",
    )(x, s5y, hy, na, sw, gates, g_a, w_glu, w_branch, w_out, ln_g, ln_b)


def _mlp_kernel(x_ref, sh_ref, sc_ref, gm_ref, w1_ref, w2_ref, lg_ref, lb_ref, o_ref, h_scr, acc_scr):
    j = pl.program_id(2)

    @pl.when(j == 0)
    def _():
        h = _layernorm(x_ref[0]) * (1.0 + sc_ref[0]) + sh_ref[0]
        h_scr[...] = h.astype(BF16)
        acc_scr[...] = jnp.zeros_like(acc_scr)

    a = jnp.dot(h_scr[...], w1_ref[...], preferred_element_type=F32)
    a = jnp.square(jnp.maximum(a, 0.0))
    acc_scr[...] += jnp.dot(a.astype(BF16), w2_ref[...], preferred_element_type=F32)

    @pl.when(j == pl.num_programs(2) - 1)
    def _():
        y = np.float32(DEEPNORM_ALPHA) * x_ref[0] + gm_ref[0] * acc_scr[...]
        o_ref[0] = _layernorm(y) * lg_ref[...] + lb_ref[...]


def _mlp(x, sh, sc, g_m, w1, w2, ln_g, ln_b, *, tm, th):
    b, l, d = x.shape
    hdim = w1.shape[1]
    mod = pl.BlockSpec((1, 1, d), lambda bi, i, j: (bi, 0, 0))
    vec = pl.BlockSpec((1, d), lambda bi, i, j: (0, 0))
    return pl.pallas_call(
        _mlp_kernel,
        grid=(b, l // tm, hdim // th),
        in_specs=[pl.BlockSpec((1, tm, d), lambda bi, i, j: (bi, i, 0)), mod, mod, mod,
                  pl.BlockSpec((d, th), lambda bi, i, j: (0, j)),
                  pl.BlockSpec((th, d), lambda bi, i, j: (j, 0)), vec, vec],
        out_specs=pl.BlockSpec((1, tm, d), lambda bi, i, j: (bi, i, 0)),
        out_shape=jax.ShapeDtypeStruct((b, l, d), F32),
        scratch_shapes=[pltpu.VMEM((tm, d), BF16), pltpu.VMEM((tm, d), F32)],
        compiler_params=_cparams(("parallel", "parallel", "arbitrary")), name="mlp",
    )(x, sh, sc, g_m, w1, w2, ln_g, ln_b)


def kernel(x, c, ctx, c_ctx, w_ada, b_ada, w_in, s5_lambda_re, s5_lambda_im, s5_log_dt, s5_b_re, s5_b_im, s5_c_re,
           s5_c_im, s5_d, s5_w_glu, hy_conv_w, hy_conv_b, hy_freq, hy_w1, hy_b1, hy_w2, hy_b2, hy_w3, hy_bias,
           na_rpb, sw_sink, w_branch, w_out, ln1_g, ln1_b, w_mlp1, w_mlp2, ln2_g, ln2_b):
    b, l, d = x.shape
    n_ctx = ctx.shape[1]
    depth = w_ada.shape[0]
    cc = jnp.zeros((8, d), F32).at[:b].set(c.astype(F32)).at[b].set(c_ctx.astype(F32))
    mod_all = _ada(cc, w_ada.astype(F32), b_ada.astype(F32))
    xc = ctx
    for layer in range(depth):
        need_ctx_out = layer < depth - 1
        mod = mod_all[layer, :b].reshape(b, 1, 6, d)
        mod_c = mod_all[layer, b].reshape(1, 1, 6, d)
        sh_a, sc_a, g_a, sh_m, sc_m, g_m = [mod[:, :, i] for i in range(6)]
        csh_a, csc_a, cg_a, csh_m, csc_m, cg_m = [mod_c[:, :, i] for i in range(6)]
        w_in_l = w_in[layer].astype(BF16)

        lat = _in_proj(x, sh_a, sc_a, w_in_l, True, tm=1024, with_rope=True)
        z_mix, gates = lat["z"], lat["gates"]
        flat = lambda a: a.reshape(1, b * a.shape[1], a.shape[2])
        unflat = lambda a: a.reshape(b, a.shape[1] // b, a.shape[2])
        con = _in_proj(flat(xc), csh_a, csc_a, w_in_l, need_ctx_out, tm=b * n_ctx)
        zc_mix = unflat(con["z"])

        ops = _s5_operators(s5_lambda_re[layer], s5_lambda_im[layer], s5_log_dt[layer], s5_b_re[layer],
                            s5_b_im[layer], s5_c_re[layer], s5_c_im[layer], s5_d[layer])
        yc_s5, e_ctx = _s5_mix(unflat(con["u"]), ops, [jnp.zeros((b, 1, S5_STATE_W // 4), F32)] * 4)
        y_s5, _ = _s5_mix(lat["u"], ops, e_ctx)

        hy_args = (hy_freq[layer], hy_w1[layer], hy_b1[layer], hy_w2[layer], hy_b2[layer], hy_w3[layer])
        sp_re, sp_im = _hy_spectra(_hy_filters(l, *hy_args), l)
        hy_l = _hyena(z_mix, hy_conv_w[layer], hy_conv_b[layer], sp_re, sp_im, hy_bias[layer])

        na_l = _na_attention(z_mix, zc_mix, na_rpb[layer])
        sw_l = _sw_attention(lat["q_rope"], lat["kv_rope"], zc_mix, sw_sink[layer])

        w_glu = s5_w_glu[layer].astype(BF16)
        w_br = w_branch[layer].astype(BF16)
        w_o = w_out[layer].astype(BF16)
        lg1, lb1 = ln1_g[layer].astype(F32)[None], ln1_b[layer].astype(F32)[None]
        lg2, lb2 = ln2_g[layer].astype(F32)[None], ln2_b[layer].astype(F32)[None]
        w1 = w_mlp1[layer].astype(BF16)
        w2 = w_mlp2[layer].astype(BF16)

        x_new = _merge(x, y_s5, hy_l, na_l, sw_l, gates, g_a, w_glu, w_br, w_o, lg1, lb1, tm=1024)
        x_new = _mlp(x_new, sh_m, sc_m, g_m, w1, w2, lg2, lb2, tm=1024, th=2048)

        if need_ctx_out:
            spc_re, spc_im = _hy_spectra(_hy_filters(n_ctx, *hy_args), n_ctx)
            hy_c = _hyena(zc_mix, hy_conv_w[layer], hy_conv_b[layer], spc_re, spc_im, hy_bias[layer])
            na_c = _ctx_attention(zc_mix, NA_COL0, NA_HEADS, NA_HEADS, NA_HEAD_DIM, None)
            sw_c = _ctx_attention(zc_mix, SW_COL0, SW_HEADS, SW_KV_HEADS, SW_HEAD_DIM, sw_sink[layer])
            xc_new = _merge(flat(xc), flat(yc_s5), flat(hy_c), flat(na_c), flat(sw_c), con["gates"], cg_a,
                            w_glu, w_br, w_o, lg1, lb1, tm=512)
            xc = unflat(_mlp(xc_new, csh_m, csc_m, cg_m, w1, w2, lg2, lb2, tm=b * n_ctx, th=1024))
        x = x_new
    return x
```

```python
import functools
import math

import numpy as np
import jax
import jax.numpy as jnp
from jax import lax
from jax.experimental import pallas as pl
from jax.experimental.pallas import tpu as pltpu

F32 = jnp.float32
BF16 = jnp.bfloat16

D_MODEL = 1024
GRID_W = 64
BRANCH_W = 256
N_BRANCH = 4
S5_GROUP = 16
S5_GROUPS = 16
S5_STATE = 64
HY_WIDTH = 256
HY_ORDER = 2
HY_BANDS = 16
HY_EMB = 2 * HY_BANDS + 1
HY_FFN = 64
HY_MIN_DECAY = math.log(1e-2) / 1.5
HY_MAX_DECAY = math.log(1e-2) / 0.3
NA_HEADS = 4
NA_HEAD_DIM = 64
NA_WIN_H = 8
NA_WIN_W = 16
SW_HEADS = 4
SW_KV_HEADS = 2
SW_HEAD_DIM = 64
SW_WINDOW = 128
SW_BLOCK = 128
MLP_HIDDEN = 4 * D_MODEL
ROPE_BASE = 10000.0
LN_EPS = 1e-6
NEG_INF = -1e30
DEPTH = 2
DEEPNORM_ALPHA = (2 * DEPTH) ** 0.25

MIX_W = 2304
NA_COL0 = BRANCH_W + 3 * HY_WIDTH
SW_COL0 = NA_COL0 + 3 * BRANCH_W
GATE_W = N_BRANCH * D_MODEL
S5_T = 8
S5_STATE_W = 4 * S5_GROUPS * S5_STATE
SUBLANES = 8
LANES = 128
VMEM_LIMIT = 56 * 1024 * 1024
MLP_ROW_PARTS = 2

HI = lax.Precision.HIGHEST


def _cparams(sem):
    return pltpu.CompilerParams(dimension_semantics=sem, vmem_limit_bytes=VMEM_LIMIT)


def _layernorm(x):
    mu = jnp.mean(x, axis=-1, keepdims=True)
    xc = x - mu
    var = jnp.mean(xc * xc, axis=-1, keepdims=True)
    return xc * lax.rsqrt(var + LN_EPS)


def _ada_kernel(c_ref, w_ref, b_ref, o_ref):
    c = c_ref[...]
    a = c * jax.nn.sigmoid(c)
    o_ref[0] = jnp.dot(a, w_ref[0], preferred_element_type=F32, precision=HI) + b_ref[0]


def _ada(cc, w_ada, b_ada):
    depth, d, n = w_ada.shape
    tn = 512
    return pl.pallas_call(
        _ada_kernel,
        grid=(depth, n // tn),
        in_specs=[pl.BlockSpec((8, d), lambda l, j: (0, 0)),
                  pl.BlockSpec((1, d, tn), lambda l, j: (l, 0, j)),
                  pl.BlockSpec((1, 1, tn), lambda l, j: (l, 0, j))],
        out_specs=pl.BlockSpec((1, 8, tn), lambda l, j: (l, 0, j)),
        out_shape=jax.ShapeDtypeStruct((depth, 8, n), F32),
        compiler_params=_cparams(("parallel", "parallel")), name="ada",
    )(cc, w_ada, b_ada.reshape(depth, 1, n))


IN_MIX_TN = MIX_W // 2
IN_GATE_TN = GATE_W // 2


def _in_proj_kernel(x_ref, sh_ref, sc_ref, wm_ref, *rest, n_mix, with_gates, with_rope):
    rest = list(rest)
    wg_ref = rest.pop(0) if with_gates else None
    cos_ref, sin_ref = (rest.pop(0), rest.pop(0)) if with_rope else (None, None)
    z_ref, u_ref = rest.pop(0), rest.pop(0)
    g_ref = rest.pop(0) if with_gates else None
    qr_ref, kvr_ref = (rest.pop(0), rest.pop(0)) if with_rope else (None, None)
    defer = with_gates and with_rope
    rope_scr = rest.pop(0) if defer else None
    tok_scr, h_scr = rest
    j = pl.program_id(2)
    qw = SW_HEADS * SW_HEAD_DIM

    def emit_chunks():
        rows = tok_scr.shape[1] // S5_T
        for c in range(BRANCH_W // LANES):
            for t in range(S5_T):
                col = t * BRANCH_W + c * LANES
                u_ref[0, :, col:col + LANES] = tok_scr[c, pl.ds(t, rows, stride=S5_T), :].astype(u_ref.dtype)

    def emit_rope(q, kv):
        cos_t, sin_t = cos_ref[...], sin_ref[...]
        qr_ref[0] = (_rope(q, cos_t, sin_t) * np.float32(SW_HEAD_DIM ** -0.5)).astype(BF16)
        lane = lax.broadcasted_iota(jnp.int32, kv.shape, 1)
        kvr_ref[0] = jnp.where(lane < SW_KV_HEADS * SW_HEAD_DIM, _rope(kv, cos_t, sin_t), kv).astype(BF16)

    @pl.when(j == 0)
    def _():
        h = _layernorm(x_ref[0]) * (1.0 + sc_ref[0]) + sh_ref[0]
        h_scr[...] = h.astype(BF16)

    @pl.when(j < n_mix)
    def _():
        r = jnp.dot(h_scr[...], wm_ref[...], preferred_element_type=F32)
        z_ref[0] = r.astype(z_ref.dtype)

        @pl.when(j == 0)
        def _():
            for c in range(BRANCH_W // LANES):
                tok_scr[c] = r[:, c * LANES:(c + 1) * LANES]
            if not defer:
                emit_chunks()

        if with_rope:
            @pl.when(j == n_mix - 1)
            def _():
                q0 = SW_COL0 - (n_mix - 1) * IN_MIX_TN
                if defer:
                    rope_scr[...] = r[:, q0:q0 + 2 * qw]
                else:
                    emit_rope(r[:, q0:q0 + qw], r[:, q0 + qw:q0 + 2 * qw])

    if with_gates:
        @pl.when(j >= n_mix)
        def _():
            r = jnp.dot(h_scr[...], wg_ref[...], preferred_element_type=F32)
            g_ref[0] = jax.nn.sigmoid(r).astype(g_ref.dtype)
            if defer:
                emit_chunks()
                emit_rope(rope_scr[:, :qw], rope_scr[:, qw:])


def _in_proj(x, sh, sc, w_in, with_gates, *, tm, with_rope=False):
    b, l, d = x.shape
    n_mix = MIX_W // IN_MIX_TN
    n_gate = GATE_W // IN_GATE_TN if with_gates else 0
    mix_j = lambda j: jnp.minimum(j, n_mix - 1)
    gate_j = lambda j: jnp.maximum(j - n_mix, 0)
    in_specs = [pl.BlockSpec((1, tm, d), lambda bi, i, j: (bi, i, 0)),
                pl.BlockSpec((1, 1, d), lambda bi, i, j: (bi, 0, 0)),
                pl.BlockSpec((1, 1, d), lambda bi, i, j: (bi, 0, 0)),
                pl.BlockSpec((pl.Element(d), pl.Element(IN_MIX_TN)), lambda bi, i, j: (0, pl.multiple_of(mix_j(j) * IN_MIX_TN, LANES)))]
    out_specs = [pl.BlockSpec((1, tm, IN_MIX_TN), lambda bi, i, j: (bi, i, mix_j(j))),
                 pl.BlockSpec((1, tm // S5_T, S5_T * BRANCH_W), lambda bi, i, j: (bi, i, 0))]
    out_shape = [jax.ShapeDtypeStruct((b, l, MIX_W), BF16),
                 jax.ShapeDtypeStruct((b, l // S5_T, S5_T * BRANCH_W), BF16)]
    args = [x, sh, sc, w_in]
    names = ["z", "u"]
    if with_gates:
        in_specs.append(pl.BlockSpec((pl.Element(d), pl.Element(IN_GATE_TN)),
                                     lambda bi, i, j: (0, pl.multiple_of(MIX_W + gate_j(j) * IN_GATE_TN, LANES))))
        out_specs.append(pl.BlockSpec((1, tm, IN_GATE_TN), lambda bi, i, j: (bi, i, gate_j(j))))
        out_shape.append(jax.ShapeDtypeStruct((b, l, GATE_W), BF16))
        args.append(w_in)
        names.append("gates")
    if with_rope:
        qw = SW_HEADS * SW_HEAD_DIM
        in_specs += [pl.BlockSpec((tm, qw), lambda bi, i, j: (i, 0))] * 2
        out_specs += [pl.BlockSpec((1, tm, qw), lambda bi, i, j: (bi, i, 0))] * 2
        out_shape += [jax.ShapeDtypeStruct((b, l, qw), BF16)] * 2
        args += list(_rope_tables(l, SW_HEAD_DIM, SW_HEADS))
        names += ["q_rope", "kv_rope"]
    scratch = [pltpu.VMEM((BRANCH_W // LANES, tm, LANES), F32), pltpu.VMEM((tm, d), BF16)]
    if with_gates and with_rope:
        scratch.insert(0, pltpu.VMEM((tm, 2 * SW_HEADS * SW_HEAD_DIM), F32))
    outs = pl.pallas_call(
        functools.partial(_in_proj_kernel, n_mix=n_mix, with_gates=with_gates, with_rope=with_rope),
        grid=(b, l // tm, n_mix + n_gate),
        in_specs=in_specs,
        out_specs=out_specs,
        out_shape=out_shape,
        scratch_shapes=scratch,
        compiler_params=_cparams(("parallel", "parallel", "arbitrary")), name="in_proj",
    )(*args)
    return dict(zip(names, outs))


def _s5_inc_kernel(u_ref, q_ref, o_ref):
    o_ref[...] = jnp.dot(u_ref[...].astype(BF16), q_ref[...], preferred_element_type=F32)


def _s5_increments(uu, q_all, *, tm, tn):
    m, k = uu.shape
    n = q_all.shape[1]
    return pl.pallas_call(
        _s5_inc_kernel,
        grid=(m // tm, n // tn),
        in_specs=[pl.BlockSpec((tm, k), lambda i, j: (i, 0)), pl.BlockSpec((k, tn), lambda i, j: (0, j))],
        out_specs=pl.BlockSpec((tm, tn), lambda i, j: (i, j)),
        out_shape=jax.ShapeDtypeStruct((m, n), F32),
        compiler_params=_cparams(("parallel", "parallel")), name="s5_inc",
    )(uu, q_all)


def _s5_out_kernel(u_ref, h0, h1, h2, h3, m_ref, n0, n1, n2, n3, o_ref):
    nt = (((1,), (1,)), ((), ()))
    acc = jnp.dot(u_ref[...].astype(BF16), m_ref[...], preferred_element_type=F32)
    for h_ref, n_ref in ((h0, n0), (h1, n1), (h2, n2), (h3, n3)):
        acc += lax.dot_general(h_ref[...].astype(BF16), n_ref[...], nt, preferred_element_type=F32)
    o_ref[...] = acc


def _s5_outputs(uu, h, m_tot, n_t, *, tm, tn):
    m, k = uu.shape
    gp = h[0].shape[1]
    n = m_tot.shape[1]
    return pl.pallas_call(
        _s5_out_kernel,
        grid=(m // tm, n // tn),
        in_specs=[pl.BlockSpec((tm, k), lambda i, j: (i, 0))]
        + [pl.BlockSpec((tm, gp), lambda i, j: (i, 0))] * 4
        + [pl.BlockSpec((k, tn), lambda i, j: (0, j))]
        + [pl.BlockSpec((tn, gp), lambda i, j, c=c: (j, c)) for c in range(4)],
        out_specs=pl.BlockSpec((tm, tn), lambda i, j: (i, j)),
        out_shape=jax.ShapeDtypeStruct((m, n), F32),
        compiler_params=_cparams(("parallel", "parallel")), name="s5_out",
    )(uu, *h, m_tot, n_t, n_t, n_t, n_t)


def _s5_scan_kernel(gfr, gfi, gbr, gbi, afr, afi, abr, abi, h0fr, h0fi, h0br, h0bi,
                    hfr, hfi, hbr, hbi, efr, efi, ebr, ebi, *, n_chunks):
    a_fr = afr[...][None]
    a_fi = afi[...][None]
    a_br = abr[...][None]
    a_bi = abi[...][None]

    def body(k, carry):
        sfr, sfi, sbr, sbi = carry
        kb = n_chunks - 1 - k
        hfr[:, pl.ds(k, 1), :] = sfr
        hfi[:, pl.ds(k, 1), :] = sfi
        hbr[:, pl.ds(kb, 1), :] = sbr
        hbi[:, pl.ds(kb, 1), :] = sbi
        nfr = a_fr * sfr - a_fi * sfi + gfr[:, pl.ds(k, 1), :]
        nfi = a_fr * sfi + a_fi * sfr + gfi[:, pl.ds(k, 1), :]
        nbr = a_br * sbr - a_bi * sbi + gbr[:, pl.ds(kb, 1), :]
        nbi = a_br * sbi + a_bi * sbr + gbi[:, pl.ds(kb, 1), :]
        return nfr, nfi, nbr, nbi

    sfr, sfi, sbr, sbi = lax.fori_loop(0, n_chunks, body, (h0fr[...], h0fi[...], h0br[...], h0bi[...]), unroll=4)
    efr[...] = sfr
    efi[...] = sfi
    ebr[...] = sbr
    ebi[...] = sbi


def _s5_scan(g, a_t, h0):
    b, k, w4 = g.shape
    w = 2 * LANES
    q = w4 // 4
    nb = q // w

    def comp(c):
        return pl.BlockSpec((b, k, w), lambda j, c=c: (0, 0, c * nb + j))

    def comp_a(c):
        return pl.BlockSpec((1, w), lambda j, c=c: (0, c * nb + j))

    state = pl.BlockSpec((b, k, w), lambda j: (0, 0, j))
    edge = pl.BlockSpec((b, 1, w), lambda j: (0, 0, j))
    outs = pl.pallas_call(
        functools.partial(_s5_scan_kernel, n_chunks=k),
        grid=(nb,),
        in_specs=[comp(c) for c in range(4)] + [comp_a(c) for c in range(4)] + [edge] * 4,
        out_specs=[state] * 4 + [edge] * 4,
        out_shape=[jax.ShapeDtypeStruct((b, k, q), F32)] * 4 + [jax.ShapeDtypeStruct((b, 1, q), F32)] * 4,
        compiler_params=_cparams(("parallel",)), name="s5_scan",
    )(g, g, g, g, a_t, a_t, a_t, a_t, *h0)
    return outs[:4], outs[4:]


def _cmul(ar, ai, br, bi):
    return ar * br - ai * bi, ar * bi + ai * br


def _s5_operators(lam_re, lam_im, log_dt, b_re, b_im, c_re, c_im, d):
    t_ = S5_T
    g_, p_, n_ = S5_GROUPS, S5_STATE, S5_GROUP
    gp, w_ = g_ * p_, g_ * n_
    lr, li = lam_re.astype(F32), lam_im.astype(F32)
    dt = jnp.exp(log_dt.astype(F32))[..., None]
    ks = jnp.arange(t_ + 1, dtype=F32)[:, None, None, None]
    mag = jnp.exp(ks * lr * dt)
    pw_re = mag * jnp.cos(ks * li * dt)
    pw_im = mag * jnp.sin(ks * li * dt)
    a_re, a_im = pw_re[1], pw_im[1]
    den = lr ** 2 + li ** 2
    f_re = ((a_re - 1.0) * lr + a_im * li) / den
    f_im = (a_im * lr - (a_re - 1.0) * li) / den
    br, bi = b_re.astype(F32), b_im.astype(F32)
    bb_re = f_re[..., None] * br - f_im[..., None] * bi
    bb_im = f_re[..., None] * bi + f_im[..., None] * br
    cr, ci = c_re.astype(F32), c_im.astype(F32)
    wb_re, wb_im = _cmul(pw_re[:t_, ..., None], pw_im[:t_, ..., None], bb_re[None], bb_im[None])
    kern = jnp.einsum('dgnp,kdgpm->dkgnm', cr, wb_re, precision=HI) \
        - jnp.einsum('dgnp,kdgpm->dkgnm', ci, wb_im, precision=HI)
    k_tile = jnp.tile(kern.transpose(0, 1, 2, 4, 3).reshape(2, t_, w_, n_), (1, 1, 1, g_))

    def rows_tiled(x):
        return jnp.tile(x.transpose(0, 3, 1, 2).reshape(2, n_, gp), (1, g_, 1))

    bt_re, bt_im = rows_tiled(bb_re), rows_tiled(bb_im)
    ct_re, ct_im = rows_tiled(cr.transpose(0, 1, 3, 2)), rows_tiled(ci.transpose(0, 1, 3, 2))
    pr = pw_re.transpose(1, 0, 2, 3).reshape(2, t_ + 1, gp)
    pi_ = pw_im.transpose(1, 0, 2, 3).reshape(2, t_ + 1, gp)
    full = lambda s: pl.BlockSpec(s, lambda t: (0,) * len(s))
    m_tot, q_all, n_t = pl.pallas_call(
        _s5_ops_kernel,
        grid=(t_,),
        in_specs=[full((2, t_ + 1, gp))] * 2 + [full((2, w_, gp))] * 4 + [full((2, t_, w_, w_)), full((1, w_))],
        out_specs=[pl.BlockSpec((w_, t_ * w_), lambda t: (t, 0)),
                   pl.BlockSpec((w_, 4 * gp), lambda t: (t, 0)),
                   pl.BlockSpec((w_, 4 * gp), lambda t: (t, 0))],
        out_shape=[jax.ShapeDtypeStruct((t_ * w_, t_ * w_), BF16),
                   jax.ShapeDtypeStruct((t_ * w_, 4 * gp), BF16),
                   jax.ShapeDtypeStruct((t_ * w_, 4 * gp), BF16)],
        compiler_params=_cparams(("parallel",)), name="s5_ops",
    )(pr, pi_, bt_re, bt_im, ct_re, ct_im, k_tile, d.astype(F32)[None])
    a_t = jnp.concatenate([pr[0, t_], pi_[0, t_], pr[1, t_], pi_[1, t_]])[None]
    return m_tot, q_all, n_t, a_t


def _s5_ops_kernel(pr_ref, pi_ref, btr_ref, bti_ref, ctr_ref, cti_ref, kt_ref, d_ref, m_ref, q_ref, nt_ref):
    t = pl.program_id(0)
    t_ = S5_T
    w_, gp = btr_ref.shape[1], btr_ref.shape[2]
    row_g = lax.broadcasted_iota(jnp.int32, (w_, gp), 0) // S5_GROUP
    col_g = lax.broadcasted_iota(jnp.int32, (w_, gp), 1) // S5_STATE
    same = row_g == col_g
    for dr in range(2):
        e_q = (t_ - 1 - t) if dr == 0 else t
        e_n = (t + 1) if dr == 0 else (t_ - t)
        b_re = jnp.where(same, btr_ref[dr], 0.0)
        b_im = jnp.where(same, bti_ref[dr], 0.0)
        q_re, q_im = _cmul(b_re, b_im, pr_ref[dr, pl.ds(e_q, 1), :], pi_ref[dr, pl.ds(e_q, 1), :])
        c_re = jnp.where(same, ctr_ref[dr], 0.0)
        c_im = jnp.where(same, cti_ref[dr], 0.0)
        n_re, n_im = _cmul(c_re, c_im, pr_ref[dr, pl.ds(e_n, 1), :], pi_ref[dr, pl.ds(e_n, 1), :])
        base = 2 * dr * gp
        q_ref[:, base:base + gp] = q_re.astype(BF16)
        q_ref[:, base + gp:base + 2 * gp] = q_im.astype(BF16)
        nt_ref[:, base:base + gp] = n_re.astype(BF16)
        nt_ref[:, base + gp:base + 2 * gp] = (-n_im).astype(BF16)
    r2 = lax.broadcasted_iota(jnp.int32, (w_, w_), 0)
    c2 = lax.broadcasted_iota(jnp.int32, (w_, w_), 1)
    same2 = (r2 // S5_GROUP) == (c2 // S5_GROUP)
    skip = jnp.where(r2 == c2, d_ref[...], 0.0)
    for i in range(t_):
        k_f = kt_ref[0, jnp.maximum(i - t, 0)] * jnp.where(i >= t, 1.0, 0.0)
        k_b = kt_ref[1, jnp.maximum(t - i, 0)] * jnp.where(t >= i, 1.0, 0.0)
        blk = jnp.where(same2, k_f + k_b, 0.0) + skip * jnp.where(t == i, 1.0, 0.0)
        m_ref[:, i * w_:(i + 1) * w_] = blk.astype(BF16)


def _s5_mix(u, ops, h0):
    m_tot, q_all, n_t, a_t = ops
    b, k, tw = u.shape
    rows = b * k
    uu = u.reshape(rows, tw)
    tm = min(rows, 512)
    g = _s5_increments(uu, q_all, tm=tm, tn=2048)
    h, e = _s5_scan(g.reshape(b, k, S5_STATE_W), a_t, h0)
    y = _s5_outputs(uu, [hc.reshape(rows, -1) for hc in h], m_tot, n_t, tm=tm, tn=512)
    return y.reshape(b, k, tw), e


def _hy_filter_kernel(f_ref, w1_ref, b1_ref, w2_ref, b2_ref, w3_ref, fr_ref, dl_ref, o_ref, *, half_len, tile):
    feats = f_ref[...]
    h = jnp.dot(feats, w1_ref[...], preferred_element_type=F32, precision=HI) + b1_ref[...]
    h = jnp.sin(fr_ref[0:1, :] * h)
    h = jnp.dot(h, w2_ref[...], preferred_element_type=F32, precision=HI) + b2_ref[...]
    h = jnp.sin(fr_ref[1:2, :] * h)
    o = jnp.dot(h.astype(BF16), w3_ref[...].astype(BF16), preferred_element_type=F32)
    t = feats[:, 0:1]
    o = o * jnp.exp(-t * dl_ref[...])
    nw = HY_ORDER * HY_WIDTH
    n = pl.program_id(0) * tile + lax.broadcasted_iota(jnp.int32, (tile, 1), 0)
    k = jnp.where(n < half_len, o[:, :nw], jnp.where(n == half_len, 0.0, o[:, nw:]))
    o_ref[...] = k.reshape(o_ref.shape)


def _hy_filters(n_tokens, freq, w1, b1, w2, b2, w3):
    n2 = 2 * n_tokens
    idx = np.arange(n2, dtype=np.float64)
    pos = np.where(idx <= n_tokens, idx, n2 - idx)[:, None]
    t = pos / max(n_tokens - 1, 1)
    bands = np.linspace(1e-4, HY_BANDS - 1, HY_BANDS)[None]
    ang = 2.0 * math.pi * bands * pos / n_tokens
    feats = np.concatenate([t, np.cos(ang), -np.sin(ang)], axis=-1)
    kpad = LANES - HY_EMB
    feats = jnp.asarray(np.pad(feats, ((0, 0), (0, kpad))), F32)
    w1p = jnp.pad(w1.astype(F32), ((0, kpad), (0, 0)))
    deltas = jnp.abs(jnp.linspace(HY_MIN_DECAY, HY_MAX_DECAY, HY_WIDTH, dtype=F32))
    dl = jnp.tile(deltas, 2 * HY_ORDER)[None]
    tile = min(n2, 1024)
    nw = 2 * HY_ORDER * HY_WIDTH
    full = lambda s: pl.BlockSpec(s, lambda i: (0,) * len(s))
    return pl.pallas_call(
        functools.partial(_hy_filter_kernel, half_len=n_tokens, tile=tile),
        grid=(n2 // tile,),
        in_specs=[pl.BlockSpec((tile, LANES), lambda i: (i, 0)),
                  full((LANES, HY_FFN)), full((1, HY_FFN)), full((HY_FFN, HY_FFN)), full((1, HY_FFN)),
                  full((HY_FFN, nw)), full((2, HY_FFN)), full((1, nw))],
        out_specs=pl.BlockSpec((tile // SUBLANES, SUBLANES, HY_ORDER * HY_WIDTH), lambda i: (i, 0, 0)),
        out_shape=jax.ShapeDtypeStruct((n2 // SUBLANES, SUBLANES, HY_ORDER * HY_WIDTH), F32),
        compiler_params=_cparams(("parallel",)), name="hy_filter",
    )(feats, w1p, b1.astype(F32)[None], w2.astype(F32), b2.astype(F32)[None], w3.astype(F32),
      freq.astype(F32), dl)


FFT_G = 4
FFT_UNROLL = 8


def _fft_tables(n):
    s = n // SUBLANES
    nst = int(round(math.log2(s)))
    runs = []
    half = s // 2
    while half >= FFT_G:
        runs.append(-2 * np.pi * np.arange(half) / (2 * half))
        half //= 2
    ang_s = np.concatenate(runs)
    tw_slab = np.stack([np.cos(ang_s), np.sin(ang_s)]).astype(np.float32)
    tw_slab = np.broadcast_to(tw_slab[..., None, None], (2, ang_s.size, SUBLANES, LANES)).copy()
    pos = np.arange(s)
    rev = np.zeros(s, np.int64)
    for bit in range(nst):
        rev |= ((pos >> bit) & 1) << (nst - 1 - bit)
    ang = -2 * np.pi * (rev[:, None] * np.arange(SUBLANES)[None, :]) / n
    tw_mid = np.stack([np.cos(ang), np.sin(ang)]).astype(np.float32)
    tw_mid = np.broadcast_to(tw_mid[..., None], (2, s, SUBLANES, LANES)).copy()
    t = np.arange(n)
    ang2 = -2 * np.pi * t / (2 * n)
    mod = np.stack([np.cos(ang2), np.sin(ang2)]).astype(np.float32).reshape(2, s, SUBLANES)
    mod = np.broadcast_to(mod[..., None], (2, s, SUBLANES, LANES)).copy()
    return jnp.asarray(tw_slab), jnp.asarray(tw_mid), jnp.asarray(mod)


def _sub_patterns():
    sub = lax.broadcasted_iota(jnp.int32, (SUBLANES, LANES), 0)

    def table(vals):
        out = jnp.full((SUBLANES, LANES), vals[0], F32)
        for k in range(1, SUBLANES):
            out = jnp.where(sub == k, np.float32(vals[k]), out)
        return out

    pats = {}
    for dist in (4, 2, 1):
        lo = (sub & dist) == 0
        sgn = jnp.where(lo, 1.0, -1.0).astype(F32)
        wr = [1.0] * SUBLANES
        wi = [0.0] * SUBLANES
        for k in range(SUBLANES):
            if k & dist:
                e = (k % dist) * (SUBLANES // (2 * dist))
                wr[k] = math.cos(-2 * math.pi * e / SUBLANES)
                wi[k] = math.sin(-2 * math.pi * e / SUBLANES)
        pats[dist] = (lo, sgn, table(wr), table(wi))
    pats["quarter"] = (sub & 3) == 3
    return pats


def _dft8_fwd(vr, vi, pats):
    for dist in (4, 2, 1):
        lo, sgn, wr, wi = pats[dist]
        up_r = pltpu.roll(vr, SUBLANES - dist, 0)
        up_i = pltpu.roll(vi, SUBLANES - dist, 0)
        if dist == 4:
            pr, pi_ = up_r, up_i
        else:
            pr = jnp.where(lo, up_r, pltpu.roll(vr, dist, 0))
            pi_ = jnp.where(lo, up_i, pltpu.roll(vi, dist, 0))
        tr = pr + sgn * vr
        ti = pi_ + sgn * vi
        if dist == 1:
            vr, vi = tr, ti
        elif dist == 2:
            qt = pats["quarter"]
            vr, vi = jnp.where(qt, ti, tr), jnp.where(qt, -tr, ti)
        else:
            vr, vi = _cmul(tr, ti, wr, wi)
    return vr, vi


def _dft8_inv(vr, vi, pats):
    for dist in (1, 2, 4):
        lo, sgn, wr, wi = pats[dist]
        if dist == 2:
            qt = pats["quarter"]
            vr, vi = jnp.where(qt, -vi, vr), jnp.where(qt, vr, vi)
        elif dist == 4:
            vr, vi = _cmul(vr, vi, wr, -wi)
        up_r = pltpu.roll(vr, SUBLANES - dist, 0)
        up_i = pltpu.roll(vi, SUBLANES - dist, 0)
        if dist == 4:
            pr, pi_ = up_r, up_i
        else:
            pr = jnp.where(lo, up_r, pltpu.roll(vr, dist, 0))
            pi_ = jnp.where(lo, up_i, pltpu.roll(vi, dist, 0))
        vr = pr + sgn * vr
        vi = pi_ + sgn * vi
    return vr, vi


def _slab_stage(re, im, tw, half, slabs, inverse):
    per_block = half // FFT_G
    nblk = slabs // (2 * half)
    unroll = min(FFT_UNROLL, nblk * per_block)
    tw_off = slabs - 2 * half
    if nblk >= unroll:
        blocks_per_it = unroll // per_block
        trips = nblk // blocks_per_it
        offsets = [(b * 2 * half + jc * FFT_G, jc * FFT_G) for b in range(blocks_per_it) for jc in range(per_block)]
        data_step, tw_step = blocks_per_it * 2 * half, 0
    else:
        chunks_per_it = unroll // nblk
        trips = per_block // chunks_per_it
        offsets = [(b * 2 * half + k * FFT_G, k * FFT_G) for b in range(nblk) for k in range(chunks_per_it)]
        data_step = tw_step = chunks_per_it * FFT_G

    def body(c, carry):
        d0 = pl.multiple_of(c * data_step, FFT_G)
        t0 = pl.multiple_of(tw_off + c * tw_step, FFT_G)
        twiddles = {}
        for d_off, t_off in offsets:
            if t_off not in twiddles:
                tws = pl.ds(t0 + t_off, FFT_G)
                twiddles[t_off] = (tw[0, tws], tw[1, tws])
            wr, wi = twiddles[t_off]
            lo = pl.ds(d0 + d_off, FFT_G)
            hi = pl.ds(d0 + d_off + half, FFT_G)
            ar, ai, br, bi = re[lo], im[lo], re[hi], im[hi]
            if inverse:
                br, bi = br * wr + bi * wi, bi * wr - br * wi
                re[lo] = ar + br
                im[lo] = ai + bi
                re[hi] = ar - br
                im[hi] = ai - bi
            else:
                re[lo] = ar + br
                im[lo] = ai + bi
                dr, di = ar - br, ai - bi
                re[hi] = dr * wr - di * wi
                im[hi] = dr * wi + di * wr
        return carry

    lax.fori_loop(0, trips, body, 0)


def _fft_forward_big(re, im, tw, slabs):
    half = slabs // 2
    while half >= 4:
        _slab_stage(re, im, tw, half, slabs, inverse=False)
        half //= 2


def _fft_inverse_big(re, im, tw, slabs):
    half = 4
    while half <= slabs // 2:
        _slab_stage(re, im, tw, half, slabs, inverse=True)
        half *= 2


def _radix4_fwd(x):
    (x0r, x0i), (x1r, x1i), (x2r, x2i), (x3r, x3i) = x
    y0r, y0i = x0r + x2r, x0i + x2i
    y2r, y2i = x0r - x2r, x0i - x2i
    y1r, y1i = x1r + x3r, x1i + x3i
    dr, di = x1r - x3r, x1i - x3i
    y3r, y3i = di, -dr
    return [(y0r + y1r, y0i + y1i), (y0r - y1r, y0i - y1i), (y2r + y3r, y2i + y3i), (y2r - y3r, y2i - y3i)]


def _radix4_inv(z):
    (z0r, z0i), (z1r, z1i), (z2r, z2i), (z3r, z3i) = z
    y0r, y0i = z0r + z1r, z0i + z1i
    y1r, y1i = z0r - z1r, z0i - z1i
    y2r, y2i = z2r + z3r, z2i + z3i
    y3r, y3i = z2r - z3r, z2i - z3i
    qr, qi = -y3i, y3r
    return [(y0r + y2r, y0i + y2i), (y1r + qr, y1i + qi), (y0r - y2r, y0i - y2i), (y1r - qr, y1i - qi)]


def _fft_middle(re, im, twm_ref, slabs, pats, spec=None, out=None):
    def body(q, carry):
        p0 = q * FFT_G
        x = [(re[p0 + g], im[p0 + g]) for g in range(FFT_G)]
        z = _radix4_fwd(x)
        res = []
        for g in range(FFT_G):
            twr = twm_ref[0, p0 + g]
            twi = twm_ref[1, p0 + g]
            vr, vi = _cmul(z[g][0], z[g][1], twr, twi)
            vr, vi = _dft8_fwd(vr, vi, pats)
            if spec is None:
                out[0][0, 0, p0 + g] = vr * out[2]
                out[1][0, 0, p0 + g] = vi * out[2]
            else:
                vr, vi = _cmul(vr, vi, spec[0][0, 0, p0 + g], spec[1][0, 0, p0 + g])
                vr, vi = _dft8_inv(vr, vi, pats)
                res.append(_cmul(vr, vi, twr, -twi))
        if spec is not None:
            x = _radix4_inv(res)
            for g in range(FFT_G):
                re[p0 + g] = x[g][0]
                im[p0 + g] = x[g][1]
        return carry

    lax.fori_loop(0, slabs // FFT_G, body, 0)


def _hy_spec_kernel(tw_ref, k_ref, twm_ref, mod_ref, sr_ref, si_ref, re, im, *, slabs):
    h = pl.program_id(2)
    pats = _sub_patterns()
    a = k_ref[0:slabs]
    b = k_ref[slabs:2 * slabs]

    @pl.when(h == 0)
    def _():
        re[...] = a + b
        im[...] = jnp.zeros_like(a)

    @pl.when(h == 1)
    def _():
        dlt = a - b
        re[...] = dlt * mod_ref[0]
        im[...] = dlt * mod_ref[1]

    _fft_forward_big(re, im, tw_ref, slabs)
    scale = np.float32(1.0 / (2 * SUBLANES * slabs))
    _fft_middle(re, im, twm_ref, slabs, pats, out=(sr_ref, si_ref, scale))


def _hy_spectra(k, n_tokens):
    slabs = n_tokens // SUBLANES
    tw, twm, mod = _fft_tables(n_tokens)
    nt = HY_WIDTH // LANES
    shape = jax.ShapeDtypeStruct((HY_ORDER, 2, slabs, SUBLANES, HY_WIDTH), F32)
    spec_out = pl.BlockSpec((1, 1, slabs, SUBLANES, LANES), lambda o, j, h: (o, h, 0, 0, j))
    tab = pl.BlockSpec((2, slabs, SUBLANES, LANES), lambda o, j, h: (0, 0, 0, 0), pipeline_mode=pl.Buffered(1))
    return pl.pallas_call(
        functools.partial(_hy_spec_kernel, slabs=slabs),
        grid=(HY_ORDER, nt, 2),
        in_specs=[pl.BlockSpec(tw.shape, lambda o, j, h: (0, 0, 0, 0), pipeline_mode=pl.Buffered(1)),
                  pl.BlockSpec((2 * slabs, SUBLANES, LANES), lambda o, j, h: (0, 0, o * nt + j)),
                  tab, tab],
        out_specs=[spec_out, spec_out],
        out_shape=[shape, shape],
        scratch_shapes=[pltpu.VMEM((slabs, SUBLANES, LANES), F32)] * 2,
        compiler_params=_cparams(("parallel", "parallel", "arbitrary")), name="hy_spec",
    )(tw, k, twm, mod)


def _hy_conv_kernel(tw_ref, u_ref, g_ref, sr_ref, si_ref, twm_ref, mod_ref, b_ref, o_ref, re, im, *, slabs):
    h = pl.program_id(2)
    pats = _sub_patterns()
    slab_shape = (slabs, SUBLANES, LANES)
    n = slabs * SUBLANES

    @pl.when(h == 0)
    def _():
        re[...] = u_ref[0].reshape(slab_shape)
        im[...] = u_ref[1].reshape(slab_shape)

    @pl.when(h == 1)
    def _():
        ur, ui = u_ref[0].reshape(slab_shape), u_ref[1].reshape(slab_shape)
        mr, mi = mod_ref[0], mod_ref[1]
        re[...] = ur * mr - ui * mi
        im[...] = ur * mi + ui * mr

    _fft_forward_big(re, im, tw_ref, slabs)
    _fft_middle(re, im, twm_ref, slabs, pats, spec=(sr_ref, si_ref))
    _fft_inverse_big(re, im, tw_ref, slabs)

    @pl.when(h == 0)
    def _():
        o_ref[0] = re[...].reshape(n, LANES)
        o_ref[1] = im[...].reshape(n, LANES)

    @pl.when(h == 1)
    def _():
        yr, yi = re[...], im[...]
        mr, mi = mod_ref[0], mod_ref[1]
        y0 = o_ref[0] + (yr * mr + yi * mi).reshape(n, LANES)
        y1 = o_ref[1] + (yi * mr - yr * mi).reshape(n, LANES)
        o_ref[0] = g_ref[0].astype(F32) * (y0 + b_ref[...] * u_ref[0])
        o_ref[1] = g_ref[1].astype(F32) * (y1 + b_ref[...] * u_ref[1])


def _hy_order(u, u_blk, gate, gate_blk, spec_re, spec_im, order, bias):
    b, n, _ = u.shape
    w = HY_WIDTH
    slabs = n // SUBLANES
    tw, twm, mod = _fft_tables(n)
    nt = w // LANES
    tab = pl.BlockSpec((2, slabs, SUBLANES, LANES), lambda j, p, h: (0, 0, 0, 0), pipeline_mode=pl.Buffered(1))
    spec_in = pl.BlockSpec((1, 1, slabs, SUBLANES, LANES), lambda j, p, h: (order, h, 0, 0, j))

    def io(blk):
        return pl.BlockSpec((2, n, LANES), lambda j, p, h: (p, 0, blk * nt + j))

    return pl.pallas_call(
        functools.partial(_hy_conv_kernel, slabs=slabs),
        grid=(nt, b // 2, 2),
        in_specs=[pl.BlockSpec(tw.shape, lambda j, p, h: (0, 0, 0, 0), pipeline_mode=pl.Buffered(1)),
                  io(u_blk), io(gate_blk), spec_in, spec_in, tab, tab,
                  pl.BlockSpec((1, LANES), lambda j, p, h: (0, j))],
        out_specs=io(0),
        out_shape=jax.ShapeDtypeStruct((b, n, w), F32),
        scratch_shapes=[pltpu.VMEM((slabs, SUBLANES, LANES), F32)] * 2,
        compiler_params=_cparams(("parallel", "parallel", "arbitrary")), name="hy_conv",
    )(tw, u, gate, spec_re, spec_im, twm, mod, bias.astype(F32)[None])


HY_SHORT_ROWS = 64


def _hy_short_kernel(z_ref, w_ref, b_ref, v_ref, g_ref, *, n):
    j = pl.program_id(1)
    rb = min(HY_SHORT_ROWS, n)
    grp = 2 * SUBLANES
    row = lax.broadcasted_iota(jnp.int32, (rb, z_ref.shape[2]), 0)
    zero_row = jnp.zeros((1, z_ref.shape[2]), F32)
    w0, w1, w2, bias = w_ref[0:1, :], w_ref[1:2, :], w_ref[2:3, :], b_ref[...]
    for c in range(n // rb):
        r0 = c * rb
        x = z_ref[0, r0:r0 + rb].astype(F32)
        before = z_ref[0, r0 - grp:r0].astype(F32)[grp - 1:grp] if c > 0 else zero_row
        after = z_ref[0, r0 + rb:r0 + rb + grp].astype(F32)[0:1] if r0 + rb < n else zero_row
        prev = jnp.where(row == 0, before, pltpu.roll(x, 1, 0))
        nxt = jnp.where(row == rb - 1, after, pltpu.roll(x, rb - 1, 0))
        y = prev * w0 + x * w1 + nxt * w2 + bias

        @pl.when(j == 0)
        def _():
            v_ref[0, r0:r0 + rb] = y

        @pl.when(j > 0)
        def _():
            g_ref[0, r0:r0 + rb] = y.astype(g_ref.dtype)


def _hy_short(z_mix, conv_w, conv_b):
    b, n, _ = z_mix.shape
    w_ = HY_WIDTH
    w = conv_w.reshape(3, 3 * w_).astype(F32)
    col0 = BRANCH_W // w_
    return pl.pallas_call(
        functools.partial(_hy_short_kernel, n=n),
        grid=(b, 3),
        in_specs=[pl.BlockSpec((1, n, w_), lambda bi, j: (bi, 0, col0 + j)),
                  pl.BlockSpec((3, w_), lambda bi, j: (0, j)),
                  pl.BlockSpec((1, w_), lambda bi, j: (0, j))],
        out_specs=[pl.BlockSpec((1, n, w_), lambda bi, j: (bi, 0, 0)),
                   pl.BlockSpec((1, n, w_), lambda bi, j: (bi, 0, jnp.maximum(j - 1, 0)))],
        out_shape=[jax.ShapeDtypeStruct((b, n, w_), F32), jax.ShapeDtypeStruct((b, n, 2 * w_), BF16)],
        compiler_params=_cparams(("parallel", "arbitrary")), name="hy_short",
    )(z_mix, w, conv_b.astype(F32)[None])


def _hyena(z_mix, conv_w, conv_b, spec_re, spec_im, bias):
    v, gates = _hy_short(z_mix, conv_w, conv_b)
    v1 = _hy_order(v, 0, gates, 0, spec_re, spec_im, 0, bias[0])
    return _hy_order(v1, 0, gates, 1, spec_re, spec_im, 1, bias[1])


NA_QROWS = 4
NA_KROWS = NA_QROWS + NA_WIN_H
NA_KPART = 256


def _softmax_parts(parts):
    m = None
    for s in parts:
        mm = jnp.max(s, axis=-1, keepdims=True)
        m = mm if m is None else jnp.maximum(m, mm)
    return m


def _na_kernel(*refs, kparts):
    q_ref, k_refs, v_refs = refs[0], refs[1:1 + kparts], refs[1 + kparts:1 + 2 * kparts]
    kc_ref, vc_ref, bias_ref, o_ref = refs[1 + 2 * kparts:]
    q = q_ref[0] * jnp.asarray(NA_HEAD_DIM ** -0.5, q_ref.dtype)
    k = jnp.concatenate([r[0] for r in k_refs], axis=0).astype(BF16)
    v = jnp.concatenate([r[0] for r in v_refs], axis=0).astype(BF16)
    kc = kc_ref[0].astype(BF16)
    vc = vc_ref[0].astype(BF16)
    nt = (((1,), (1,)), ((), ()))
    outs = []
    for h in range(NA_HEADS):
        sl = slice(h * NA_HEAD_DIM, (h + 1) * NA_HEAD_DIM)
        qh = q[:, sl].astype(BF16)
        s_lat = lax.dot_general(qh, k[:, sl], nt, preferred_element_type=F32) + bias_ref[0, h]
        s_ctx = lax.dot_general(qh, kc[:, sl], nt, preferred_element_type=F32)
        m = _softmax_parts([s_lat, s_ctx])
        p_lat = jnp.exp(s_lat - m)
        p_ctx = jnp.exp(s_ctx - m)
        den = jnp.sum(p_lat, axis=-1, keepdims=True) + jnp.sum(p_ctx, axis=-1, keepdims=True)
        o = jnp.dot(p_lat.astype(BF16), v[:, sl], preferred_element_type=F32) \
            + jnp.dot(p_ctx.astype(BF16), vc[:, sl], preferred_element_type=F32)
        outs.append(o / den)
    o_ref[0] = jnp.concatenate(outs, axis=-1)


def _na_bias(rpb, rows):
    kh = min(NA_WIN_H, rows)
    col = np.arange(GRID_W)
    col_start = np.clip(col - NA_WIN_W // 2, 0, GRID_W - NA_WIN_W)
    col_ok = (col[None] >= col_start[:, None]) & (col[None] < col_start[:, None] + NA_WIN_W)
    off_c = np.clip(col[None] - col[:, None], -(NA_WIN_W - 1), NA_WIN_W - 1) + (NA_WIN_W - 1)
    nblk = rows // NA_QROWS
    n_r, n_c = 2 * NA_WIN_H - 1, 2 * NA_WIN_W - 1
    table = jnp.pad(rpb.astype(F32), ((0, 0), (0, 1), (0, 1)), constant_values=NEG_INF)
    sel_c = np.eye(n_c + 1, dtype=np.float32)[np.where(col_ok, off_c, n_c)]
    blocks = jnp.einsum('hab,qcb->hqac', table, sel_c, precision=HI).reshape(NA_HEADS, GRID_W, (n_r + 1) * GRID_W)
    blocks_odd = jnp.roll(blocks, -GRID_W, axis=-1)
    row_off = []
    for j in (0, 1, nblk - 1):
        qr = j * NA_QROWS + np.arange(NA_QROWS)
        ws = int(np.clip(j * NA_QROWS - NA_WIN_H // 2, 0, rows - NA_KROWS))
        kr = ws + np.arange(NA_KROWS)
        start = np.clip(qr - kh // 2, 0, rows - kh)
        row_ok = (kr[None] >= start[:, None]) & (kr[None] < start[:, None] + kh)
        off_r = kr[None] - qr[:, None] + (NA_WIN_H - 1)
        row_off.append(np.where(row_ok, off_r, n_r))
    row_off = np.stack(row_off)
    wide = (n_r + 1) * GRID_W
    return pl.pallas_call(
        functools.partial(_na_bias_kernel, row_off=row_off),
        grid=(NA_HEADS,),
        in_specs=[pl.BlockSpec((1, GRID_W, wide), lambda h: (h, 0, 0))] * 2,
        out_specs=pl.BlockSpec((3, 1, NA_QROWS * GRID_W, NA_KROWS * GRID_W), lambda h: (0, h, 0, 0)),
        out_shape=jax.ShapeDtypeStruct((3, NA_HEADS, NA_QROWS * GRID_W, NA_KROWS * GRID_W), F32),
        compiler_params=_cparams(("parallel",)), name="na_bias",
    )(blocks, blocks_odd)


def _na_bias_kernel(even_ref, odd_ref, o_ref, *, row_off):
    n_cls, n_q, n_k = row_off.shape
    masked = 2 * NA_WIN_H - 1
    pair = 2 * GRID_W
    lane = lax.broadcasted_iota(jnp.int32, (GRID_W, pair), 1)

    def left(a):
        src, start = (even_ref, a) if a % 2 == 0 else (odd_ref, a - 1)
        return src[0, :, start * GRID_W:start * GRID_W + pair]

    def right(a):
        assert a % 2 == 1
        return even_ref[0, :, (a - 1) * GRID_W:(a - 1) * GRID_W + pair]

    for t in range(n_cls):
        for r in range(n_q):
            for kp in range(n_k // 2):
                a0, a1 = int(row_off[t, r, 2 * kp]), int(row_off[t, r, 2 * kp + 1])
                if a0 != masked and a1 == a0 + 1:
                    blk = left(a0)
                elif a0 == masked and a1 == masked:
                    blk = jnp.full((GRID_W, pair), NEG_INF, F32)
                elif a1 == masked:
                    blk = jnp.where(lane < GRID_W, left(a0), NEG_INF)
                else:
                    blk = jnp.where(lane >= GRID_W, right(a1), NEG_INF)
                o_ref[t, 0, r * GRID_W:(r + 1) * GRID_W, kp * pair:(kp + 1) * pair] = blk


def _na_attention(z_mix, zc_mix, rpb):
    b, n, _ = z_mix.shape
    n_ctx = zc_mix.shape[1]
    rows = n // GRID_W
    nblk = rows // NA_QROWS
    tq = NA_QROWS * GRID_W
    bias = _na_bias(rpb, rows)
    qb, kb, vb = NA_COL0 // BRANCH_W, NA_COL0 // BRANCH_W + 1, NA_COL0 // BRANCH_W + 2
    kparts = NA_KROWS * GRID_W // NA_KPART
    max_k0 = (rows - NA_KROWS) * GRID_W // NA_KPART

    def k0_of(j):
        per_step = NA_QROWS * GRID_W // NA_KPART
        lead = (NA_WIN_H // 2) * GRID_W // NA_KPART
        return jnp.clip(j * per_step - lead, 0, max_k0)

    def kspec(part, blk):
        return pl.BlockSpec((1, NA_KPART, BRANCH_W), lambda bi, j: (bi, k0_of(j) + part, blk))

    def cls(j):
        return jnp.where(j == 0, 0, jnp.where(j == nblk - 1, 2, 1))

    return pl.pallas_call(
        functools.partial(_na_kernel, kparts=kparts),
        grid=(b, nblk),
        in_specs=[pl.BlockSpec((1, tq, BRANCH_W), lambda bi, j: (bi, j, qb))]
        + [kspec(p, kb) for p in range(kparts)] + [kspec(p, vb) for p in range(kparts)]
        + [pl.BlockSpec((1, n_ctx, BRANCH_W), lambda bi, j: (bi, 0, kb)),
           pl.BlockSpec((1, n_ctx, BRANCH_W), lambda bi, j: (bi, 0, vb)),
           pl.BlockSpec((1, NA_HEADS, tq, NA_KROWS * GRID_W), lambda bi, j: (cls(j), 0, 0, 0))],
        out_specs=pl.BlockSpec((1, tq, BRANCH_W), lambda bi, j: (bi, j, 0)),
        out_shape=jax.ShapeDtypeStruct((b, n, BRANCH_W), F32),
        compiler_params=_cparams(("parallel", "arbitrary")), name="na_attn",
    )(z_mix, *([z_mix] * (2 * kparts)), zc_mix, zc_mix, bias)


def _rope_tables(n_tokens, head_dim, heads):
    t = np.arange(n_tokens)
    row = (t // GRID_W).astype(np.float64)
    col = (t % GRID_W).astype(np.float64)
    half = head_dim // 2
    inv = ROPE_BASE ** (-(np.arange(0, half, 2, dtype=np.float64) / half))
    ang = np.concatenate([row[:, None] * inv, col[:, None] * inv], axis=-1)
    cos, sin = np.cos(ang), np.sin(ang)
    cos_t = np.tile(np.concatenate([cos, cos], axis=-1), (1, heads))
    sin_t = np.tile(np.concatenate([-sin, sin], axis=-1), (1, heads))
    return jnp.asarray(cos_t, F32), jnp.asarray(sin_t, F32)


def _rope(x, cos_t, sin_t):
    w = x.shape[-1]
    half = SW_HEAD_DIM // 2
    lane = lax.broadcasted_iota(jnp.int32, x.shape, 1)
    first = (lane % SW_HEAD_DIM) < half
    partner = jnp.where(first, pltpu.roll(x, w - half, 1), pltpu.roll(x, half, 1))
    return x * cos_t + partner * sin_t


SW_QBLK = 2 * SW_BLOCK
SW_KPARTS = SW_QBLK // SW_BLOCK + 2


def _sw_kernel(q_ref, kv0, kv1, kv2, kv3, kc_ref, vc_ref, sink_ref, o_ref, *, nblk):
    i = pl.program_id(1)
    kvw = SW_KV_HEADS * SW_HEAD_DIM
    q = q_ref[0]
    kv = jnp.concatenate([kv0[0], kv1[0], kv2[0], kv3[0]], axis=0)
    kc = kc_ref[0].astype(BF16)
    vc = vc_ref[0].astype(BF16)
    g_ = SW_HEADS // SW_KV_HEADS
    rows = g_ * SW_QBLK
    span = SW_KPARTS * SW_BLOCK
    r = lax.broadcasted_iota(jnp.int32, (rows, span), 0) % SW_QBLK
    c = lax.broadcasted_iota(jnp.int32, (rows, span), 1)
    diff = c - SW_BLOCK - r
    blk = i * (SW_QBLK // SW_BLOCK) - 1 + c // SW_BLOCK
    ok = (jnp.abs(diff) <= SW_WINDOW) & (blk >= 0) & (blk < nblk)
    nt = (((1,), (1,)), ((), ()))
    outs = []
    for kvh in range(SW_KV_HEADS):
        ksl = slice(kvh * SW_HEAD_DIM, (kvh + 1) * SW_HEAD_DIM)
        vsl = slice(kvw + kvh * SW_HEAD_DIM, kvw + (kvh + 1) * SW_HEAD_DIM)
        qg = jnp.concatenate([q[:, (kvh * g_ + g) * SW_HEAD_DIM:(kvh * g_ + g + 1) * SW_HEAD_DIM] for g in range(g_)],
                             axis=0)
        s_lat = jnp.where(ok, lax.dot_general(qg, kv[:, ksl], nt, preferred_element_type=F32), NEG_INF)
        s_ctx = lax.dot_general(qg, kc[:, ksl], nt, preferred_element_type=F32)
        rr = lax.broadcasted_iota(jnp.int32, (rows, 1), 0)
        s_sink = jnp.zeros((rows, 1), F32)
        for g in range(g_):
            s_sink = jnp.where(rr // SW_QBLK == g, sink_ref[kvh * g_ + g], s_sink)
        m = jnp.maximum(_softmax_parts([s_lat, s_ctx]), s_sink)
        p_lat = jnp.exp(s_lat - m)
        p_ctx = jnp.exp(s_ctx - m)
        den = jnp.sum(p_lat, axis=-1, keepdims=True) + jnp.sum(p_ctx, axis=-1, keepdims=True) + jnp.exp(s_sink - m)
        o = jnp.dot(p_lat.astype(BF16), kv[:, vsl], preferred_element_type=F32) \
            + jnp.dot(p_ctx.astype(BF16), vc[:, ksl], preferred_element_type=F32)
        o = o / den
        outs += [o[g * SW_QBLK:(g + 1) * SW_QBLK] for g in range(g_)]
    o_ref[0] = jnp.concatenate(outs, axis=-1)


def _sw_attention(q_r, kv_r, zc_mix, sink):
    b, n, _ = q_r.shape
    n_ctx = zc_mix.shape[1]
    nblk = n // SW_BLOCK
    qw = SW_HEADS * SW_HEAD_DIM
    kvw = SW_KV_HEADS * SW_HEAD_DIM
    k_blk = (SW_COL0 + qw) // kvw
    per_q = SW_QBLK // SW_BLOCK

    def kpart(part):
        return pl.BlockSpec((1, SW_BLOCK, qw),
                            lambda bi, i: (bi, jnp.clip(i * per_q - 1 + part, 0, nblk - 1), 0))

    return pl.pallas_call(
        functools.partial(_sw_kernel, nblk=nblk),
        grid=(b, n // SW_QBLK),
        in_specs=[pl.BlockSpec((1, SW_QBLK, qw), lambda bi, i: (bi, i, 0))]
        + [kpart(p) for p in range(SW_KPARTS)]
        + [pl.BlockSpec((1, n_ctx, kvw), lambda bi, i: (bi, 0, k_blk)),
           pl.BlockSpec((1, n_ctx, kvw), lambda bi, i: (bi, 0, k_blk + 1)),
           pl.BlockSpec(memory_space=pltpu.SMEM)],
        out_specs=pl.BlockSpec((1, SW_QBLK, qw), lambda bi, i: (bi, i, 0)),
        out_shape=jax.ShapeDtypeStruct((b, n, qw), F32),
        compiler_params=_cparams(("parallel", "arbitrary")), name="sw_attn",
    )(q_r, *([kv_r] * SW_KPARTS), zc_mix, zc_mix, sink.astype(F32))


def _ctx_attn_kernel(q_ref, k_ref, v_ref, sink_ref, o_ref, *, heads, kv_heads, dh, use_sink):
    q = q_ref[0] * np.float32(dh ** -0.5)
    k = k_ref[0].astype(BF16)
    v = v_ref[0].astype(BF16)
    g_ = heads // kv_heads
    nt = (((1,), (1,)), ((), ()))
    outs = []
    for h in range(heads):
        kv = h // g_
        s = lax.dot_general(q[:, h * dh:(h + 1) * dh].astype(BF16), k[:, kv * dh:(kv + 1) * dh], nt,
                            preferred_element_type=F32)
        m = jnp.max(s, axis=-1, keepdims=True)
        if use_sink:
            m = jnp.maximum(m, sink_ref[h])
        p = jnp.exp(s - m)
        den = jnp.sum(p, axis=-1, keepdims=True)
        if use_sink:
            den = den + jnp.exp(sink_ref[h] - m)
        outs.append(jnp.dot(p.astype(BF16), v[:, kv * dh:(kv + 1) * dh], preferred_element_type=F32) / den)
    o_ref[0] = jnp.concatenate(outs, axis=-1)


def _ctx_attention(zc_mix, col0, heads, kv_heads, dh, sink):
    b, n, _ = zc_mix.shape
    qw, kvw = heads * dh, kv_heads * dh
    use_sink = sink is not None
    sink_arr = sink.astype(F32) if use_sink else jnp.zeros((heads,), F32)
    return pl.pallas_call(
        functools.partial(_ctx_attn_kernel, heads=heads, kv_heads=kv_heads, dh=dh, use_sink=use_sink),
        grid=(b,),
        in_specs=[pl.BlockSpec((1, n, qw), lambda bi: (bi, 0, col0 // qw)),
                  pl.BlockSpec((1, n, kvw), lambda bi: (bi, 0, (col0 + qw) // kvw)),
                  pl.BlockSpec((1, n, kvw), lambda bi: (bi, 0, (col0 + qw) // kvw + 1)),
                  pl.BlockSpec(memory_space=pltpu.SMEM)],
        out_specs=pl.BlockSpec((1, n, qw), lambda bi: (bi, 0, 0)),
        out_shape=jax.ShapeDtypeStruct((b, n, qw), F32),
        compiler_params=_cparams(("parallel",)), name="ctx_attn",
    )(zc_mix, zc_mix, zc_mix, sink_arr)


def _merge_kernel(x_ref, s5_ref, hy_ref, na_ref, sw_ref, gt_ref, ga_ref, wglu_ref, wb_ref, wo_ref,
                  lg_ref, lb_ref, o_ref, tok_scr):
    rows = tok_scr.shape[1] // S5_T
    for c in range(BRANCH_W // LANES):
        for t in range(S5_T):
            col = t * BRANCH_W + c * LANES
            tok_scr[c, pl.ds(t, rows, stride=S5_T), :] = s5_ref[0, :, col:col + LANES]
    part_rows = x_ref.shape[1] // MLP_ROW_PARTS
    for part in range(MLP_ROW_PARTS):
        sl = slice(part * part_rows, (part + 1) * part_rows)
        g = jax.nn.gelu(jnp.concatenate([tok_scr[c, sl] for c in range(BRANCH_W // LANES)], axis=-1))
        s5 = g * jax.nn.sigmoid(jnp.dot(g.astype(BF16), wglu_ref[...], preferred_element_type=F32))
        branches = (s5, hy_ref[0, sl], na_ref[0, sl], sw_ref[0, sl])
        acc = None
        for n in range(N_BRANCH):
            proj = jnp.dot(branches[n].astype(BF16), wb_ref[n], preferred_element_type=F32)
            t = gt_ref[0, sl, n * D_MODEL:(n + 1) * D_MODEL].astype(F32) * proj
            acc = t if acc is None else acc + t
        mix = jnp.dot(acc.astype(BF16), wo_ref[...], preferred_element_type=F32)
        y = np.float32(DEEPNORM_ALPHA) * x_ref[0, sl] + ga_ref[0] * mix
        o_ref[0, sl] = _layernorm(y) * lg_ref[...] + lb_ref[...]


def _merge(x, s5y, hy, na, sw, gates, g_a, w_glu, w_branch, w_out, ln_g, ln_b, *, tm):
    b, l, d = x.shape
    br = pl.BlockSpec((1, tm, BRANCH_W), lambda bi, i: (bi, i, 0))
    full = lambda s: pl.BlockSpec(s, lambda bi, i: (0,) * len(s), pipeline_mode=pl.Buffered(1))
    return pl.pallas_call(
        _merge_kernel,
        grid=(b, l // tm),
        in_specs=[pl.BlockSpec((1, tm, d), lambda bi, i: (bi, i, 0)),
                  pl.BlockSpec((1, tm // S5_T, S5_T * BRANCH_W), lambda bi, i: (bi, i, 0)), br, br, br,
                  pl.BlockSpec((1, tm, GATE_W), lambda bi, i: (bi, i, 0)),
                  pl.BlockSpec((1, 1, d), lambda bi, i: (bi, 0, 0)),
                  full((BRANCH_W, BRANCH_W)), full((N_BRANCH, BRANCH_W, d)), full((d, d)),
                  full((1, d)), full((1, d))],
        out_specs=pl.BlockSpec((1, tm, d), lambda bi, i: (bi, i, 0)),
        out_shape=jax.ShapeDtypeStruct((b, l, d), F32),
        scratch_shapes=[pltpu.VMEM((BRANCH_W // LANES, tm, LANES), F32)],
        compiler_params=_cparams(("parallel", "parallel")), name="merge",
    )(x, s5y, hy, na, sw, gates, g_a, w_glu, w_branch, w_out, ln_g, ln_b)


def _mlp_kernel(x_ref, sh_ref, sc_ref, gm_ref, w1_ref, w2_ref, lg_ref, lb_ref, o_ref, *, th):
    tm = x_ref.shape[1]
    hdim = w1_ref.shape[1]
    rows = tm // MLP_ROW_PARTS
    for part in range(MLP_ROW_PARTS):
        sl = slice(part * rows, (part + 1) * rows)
        xs = x_ref[0, sl]
        h = (_layernorm(xs) * (1.0 + sc_ref[0]) + sh_ref[0]).astype(BF16)
        acc = None
        for c in range(hdim // th):
            a = jnp.dot(h, w1_ref[:, c * th:(c + 1) * th], preferred_element_type=F32)
            a = jnp.square(jnp.maximum(a, 0.0))
            t = jnp.dot(a.astype(BF16), w2_ref[c * th:(c + 1) * th, :], preferred_element_type=F32)
            acc = t if acc is None else acc + t
        y = np.float32(DEEPNORM_ALPHA) * xs + gm_ref[0] * acc
        o_ref[0, sl] = _layernorm(y) * lg_ref[...] + lb_ref[...]


def _mlp(x, sh, sc, g_m, w1, w2, ln_g, ln_b, *, tm, th):
    b, l, d = x.shape
    hdim = w1.shape[1]
    mod = pl.BlockSpec((1, 1, d), lambda bi, i: (bi, 0, 0))
    once = lambda s: pl.BlockSpec(s, lambda bi, i: (0,) * len(s), pipeline_mode=pl.Buffered(1))
    return pl.pallas_call(
        functools.partial(_mlp_kernel, th=th),
        grid=(b, l // tm),
        in_specs=[pl.BlockSpec((1, tm, d), lambda bi, i: (bi, i, 0)), mod, mod, mod,
                  once((d, hdim)), once((hdim, d)), once((1, d)), once((1, d))],
        out_specs=pl.BlockSpec((1, tm, d), lambda bi, i: (bi, i, 0)),
        out_shape=jax.ShapeDtypeStruct((b, l, d), F32),
        compiler_params=_cparams(("parallel", "parallel")), name="mlp",
    )(x, sh, sc, g_m, w1, w2, ln_g, ln_b)


def kernel(x, c, ctx, c_ctx, w_ada, b_ada, w_in, s5_lambda_re, s5_lambda_im, s5_log_dt, s5_b_re, s5_b_im, s5_c_re,
           s5_c_im, s5_d, s5_w_glu, hy_conv_w, hy_conv_b, hy_freq, hy_w1, hy_b1, hy_w2, hy_b2, hy_w3, hy_bias,
           na_rpb, sw_sink, w_branch, w_out, ln1_g, ln1_b, w_mlp1, w_mlp2, ln2_g, ln2_b):
    b, l, d = x.shape
    n_ctx = ctx.shape[1]
    depth = w_ada.shape[0]
    cc = jnp.zeros((8, d), F32).at[:b].set(c.astype(F32)).at[b].set(c_ctx.astype(F32))
    mod_all = _ada(cc, w_ada.astype(F32), b_ada.astype(F32))
    xc = ctx
    for layer in range(depth):
        need_ctx_out = layer < depth - 1
        mod = mod_all[layer, :b].reshape(b, 1, 6, d)
        mod_c = mod_all[layer, b].reshape(1, 1, 6, d)
        sh_a, sc_a, g_a, sh_m, sc_m, g_m = [mod[:, :, i] for i in range(6)]
        csh_a, csc_a, cg_a, csh_m, csc_m, cg_m = [mod_c[:, :, i] for i in range(6)]
        w_in_l = w_in[layer].astype(BF16)

        lat = _in_proj(x, sh_a, sc_a, w_in_l, True, tm=1024, with_rope=True)
        z_mix, gates = lat["z"], lat["gates"]
        flat = lambda a: a.reshape(1, b * a.shape[1], a.shape[2])
        unflat = lambda a: a.reshape(b, a.shape[1] // b, a.shape[2])
        con = _in_proj(flat(xc), csh_a, csc_a, w_in_l, need_ctx_out, tm=b * n_ctx)
        zc_mix = unflat(con["z"])

        ops = _s5_operators(s5_lambda_re[layer], s5_lambda_im[layer], s5_log_dt[layer], s5_b_re[layer],
                            s5_b_im[layer], s5_c_re[layer], s5_c_im[layer], s5_d[layer])
        yc_s5, e_ctx = _s5_mix(unflat(con["u"]), ops, [jnp.zeros((b, 1, S5_STATE_W // 4), F32)] * 4)
        y_s5, _ = _s5_mix(lat["u"], ops, e_ctx)

        hy_args = (hy_freq[layer], hy_w1[layer], hy_b1[layer], hy_w2[layer], hy_b2[layer], hy_w3[layer])
        sp_re, sp_im = _hy_spectra(_hy_filters(l, *hy_args), l)
        hy_l = _hyena(z_mix, hy_conv_w[layer], hy_conv_b[layer], sp_re, sp_im, hy_bias[layer])

        na_l = _na_attention(z_mix, zc_mix, na_rpb[layer])
        sw_l = _sw_attention(lat["q_rope"], lat["kv_rope"], zc_mix, sw_sink[layer])

        w_glu = s5_w_glu[layer].astype(BF16)
        w_br = w_branch[layer].astype(BF16)
        w_o = w_out[layer].astype(BF16)
        lg1, lb1 = ln1_g[layer].astype(F32)[None], ln1_b[layer].astype(F32)[None]
        lg2, lb2 = ln2_g[layer].astype(F32)[None], ln2_b[layer].astype(F32)[None]
        w1 = w_mlp1[layer].astype(BF16)
        w2 = w_mlp2[layer].astype(BF16)

        x_new = _merge(x, y_s5, hy_l, na_l, sw_l, gates, g_a, w_glu, w_br, w_o, lg1, lb1, tm=1024)
        x_new = _mlp(x_new, sh_m, sc_m, g_m, w1, w2, lg2, lb2, tm=1024, th=2048)

        if need_ctx_out:
            spc_re, spc_im = _hy_spectra(_hy_filters(n_ctx, *hy_args), n_ctx)
            hy_c = _hyena(zc_mix, hy_conv_w[layer], hy_conv_b[layer], spc_re, spc_im, hy_bias[layer])
            na_c = _ctx_attention(zc_mix, NA_COL0, NA_HEADS, NA_HEADS, NA_HEAD_DIM, None)
            sw_c = _ctx_attention(zc_mix, SW_COL0, SW_HEADS, SW_KV_HEADS, SW_HEAD_DIM, sw_sink[layer])
            xc_new = _merge(flat(xc), flat(yc_s5), flat(hy_c), flat(na_c), flat(sw_c), con["gates"], cg_a,
                            w_glu, w_br, w_o, lg1, lb1, tm=512)
            xc = unflat(_mlp(xc_new, csh_m, csc_m, cg_m, w1, w2, lg2, lb2, tm=b * n_ctx, th=1024))
        x = x_new
    return x
```

```python
import functools
import math

import numpy as np
import jax
import jax.numpy as jnp
from jax import lax
from jax.experimental import pallas as pl
from jax.experimental.pallas import tpu as pltpu

F32 = jnp.float32
BF16 = jnp.bfloat16

D_MODEL = 1024
GRID_W = 64
BRANCH_W = 256
N_BRANCH = 4
S5_GROUP = 16
S5_GROUPS = 16
S5_STATE = 64
HY_WIDTH = 256
HY_ORDER = 2
HY_BANDS = 16
HY_EMB = 2 * HY_BANDS + 1
HY_FFN = 64
HY_MIN_DECAY = math.log(1e-2) / 1.5
HY_MAX_DECAY = math.log(1e-2) / 0.3
NA_HEADS = 4
NA_HEAD_DIM = 64
NA_WIN_H = 8
NA_WIN_W = 16
SW_HEADS = 4
SW_KV_HEADS = 2
SW_HEAD_DIM = 64
SW_WINDOW = 128
SW_BLOCK = 128
MLP_HIDDEN = 4 * D_MODEL
ROPE_BASE = 10000.0
LN_EPS = 1e-6
NEG_INF = -1e30
DEPTH = 2
DEEPNORM_ALPHA = (2 * DEPTH) ** 0.25

MIX_W = 2304
NA_COL0 = BRANCH_W + 3 * HY_WIDTH
SW_COL0 = NA_COL0 + 3 * BRANCH_W
GATE_W = N_BRANCH * D_MODEL
S5_T = 8
S5_STATE_W = 4 * S5_GROUPS * S5_STATE
SUBLANES = 8
LANES = 128
VMEM_LIMIT = 56 * 1024 * 1024
MLP_ROW_PARTS = 2

HI = lax.Precision.HIGHEST


def _cparams(sem):
    return pltpu.CompilerParams(dimension_semantics=sem, vmem_limit_bytes=VMEM_LIMIT)


def _layernorm(x):
    mu = jnp.mean(x, axis=-1, keepdims=True)
    xc = x - mu
    var = jnp.mean(xc * xc, axis=-1, keepdims=True)
    return xc * lax.rsqrt(var + LN_EPS)


def _ada_kernel(c_ref, w_ref, b_ref, o_ref):
    c = c_ref[...]
    a = c * jax.nn.sigmoid(c)
    o_ref[0] = jnp.dot(a, w_ref[0], preferred_element_type=F32, precision=HI) + b_ref[0]


def _ada(cc, w_ada, b_ada):
    depth, d, n = w_ada.shape
    tn = 512
    return pl.pallas_call(
        _ada_kernel,
        grid=(depth, n // tn),
        in_specs=[pl.BlockSpec((8, d), lambda l, j: (0, 0)),
                  pl.BlockSpec((1, d, tn), lambda l, j: (l, 0, j)),
                  pl.BlockSpec((1, 1, tn), lambda l, j: (l, 0, j))],
        out_specs=pl.BlockSpec((1, 8, tn), lambda l, j: (l, 0, j)),
        out_shape=jax.ShapeDtypeStruct((depth, 8, n), F32),
        compiler_params=_cparams(("parallel", "parallel")), name="ada",
    )(cc, w_ada, b_ada.reshape(depth, 1, n))


IN_MIX_TN = MIX_W // 2
IN_GATE_TN = GATE_W // 2


def _in_proj_kernel(x_ref, sh_ref, sc_ref, wm_ref, *rest, n_mix, with_gates, with_rope):
    rest = list(rest)
    wg_ref = rest.pop(0) if with_gates else None
    cos_ref, sin_ref = (rest.pop(0), rest.pop(0)) if with_rope else (None, None)
    z_ref, u_ref = rest.pop(0), rest.pop(0)
    g_ref = rest.pop(0) if with_gates else None
    qr_ref, kvr_ref = (rest.pop(0), rest.pop(0)) if with_rope else (None, None)
    defer = with_gates and with_rope
    rope_scr = rest.pop(0) if defer else None
    tok_scr, h_scr = rest
    j = pl.program_id(2)
    qw = SW_HEADS * SW_HEAD_DIM

    def emit_chunks():
        rows = tok_scr.shape[1] // S5_T
        for c in range(BRANCH_W // LANES):
            for t in range(S5_T):
                col = t * BRANCH_W + c * LANES
                u_ref[0, :, col:col + LANES] = tok_scr[c, pl.ds(t, rows, stride=S5_T), :].astype(u_ref.dtype)

    def emit_rope(q, kv):
        cos_t, sin_t = cos_ref[...], sin_ref[...]
        qr_ref[0] = (_rope(q, cos_t, sin_t) * np.float32(SW_HEAD_DIM ** -0.5)).astype(BF16)
        lane = lax.broadcasted_iota(jnp.int32, kv.shape, 1)
        kvr_ref[0] = jnp.where(lane < SW_KV_HEADS * SW_HEAD_DIM, _rope(kv, cos_t, sin_t), kv).astype(BF16)

    @pl.when(j == 0)
    def _():
        part_rows = x_ref.shape[1] // MLP_ROW_PARTS
        for part in range(MLP_ROW_PARTS):
            sl = slice(part * part_rows, (part + 1) * part_rows)
            h = (_layernorm(x_ref[0, sl]) * (1.0 + sc_ref[0]) + sh_ref[0]).astype(BF16)
            h_scr[sl] = h
            r = jnp.dot(h, wm_ref[...], preferred_element_type=F32)
            z_ref[0, sl] = r.astype(z_ref.dtype)
            for c in range(BRANCH_W // LANES):
                tok_scr[c, sl] = r[:, c * LANES:(c + 1) * LANES]
        if not defer:
            emit_chunks()

    @pl.when((j > 0) & (j < n_mix))
    def _():
        r = jnp.dot(h_scr[...], wm_ref[...], preferred_element_type=F32)
        z_ref[0] = r.astype(z_ref.dtype)

        if with_rope:
            @pl.when(j == n_mix - 1)
            def _():
                q0 = SW_COL0 - (n_mix - 1) * IN_MIX_TN
                if defer:
                    rope_scr[...] = r[:, q0:q0 + 2 * qw]
                else:
                    emit_rope(r[:, q0:q0 + qw], r[:, q0 + qw:q0 + 2 * qw])

    if with_gates:
        @pl.when(j >= n_mix)
        def _():
            r = jnp.dot(h_scr[...], wg_ref[...], preferred_element_type=F32)
            g_ref[0] = jax.nn.sigmoid(r).astype(g_ref.dtype)
            if defer:
                emit_chunks()
                emit_rope(rope_scr[:, :qw], rope_scr[:, qw:])


def _in_proj(x, sh, sc, w_in, with_gates, *, tm, with_rope=False):
    b, l, d = x.shape
    n_mix = MIX_W // IN_MIX_TN
    n_gate = GATE_W // IN_GATE_TN if with_gates else 0
    mix_j = lambda j: jnp.minimum(j, n_mix - 1)
    gate_j = lambda j: jnp.maximum(j - n_mix, 0)
    in_specs = [pl.BlockSpec((1, tm, d), lambda bi, i, j: (bi, i, 0)),
                pl.BlockSpec((1, 1, d), lambda bi, i, j: (bi, 0, 0)),
                pl.BlockSpec((1, 1, d), lambda bi, i, j: (bi, 0, 0)),
                pl.BlockSpec((pl.Element(d), pl.Element(IN_MIX_TN)), lambda bi, i, j: (0, pl.multiple_of(mix_j(j) * IN_MIX_TN, LANES)))]
    out_specs = [pl.BlockSpec((1, tm, IN_MIX_TN), lambda bi, i, j: (bi, i, mix_j(j))),
                 pl.BlockSpec((1, tm // S5_T, S5_T * BRANCH_W), lambda bi, i, j: (bi, i, 0))]
    out_shape = [jax.ShapeDtypeStruct((b, l, MIX_W), BF16),
                 jax.ShapeDtypeStruct((b, l // S5_T, S5_T * BRANCH_W), BF16)]
    args = [x, sh, sc, w_in]
    names = ["z", "u"]
    if with_gates:
        in_specs.append(pl.BlockSpec((pl.Element(d), pl.Element(IN_GATE_TN)),
                                     lambda bi, i, j: (0, pl.multiple_of(MIX_W + gate_j(j) * IN_GATE_TN, LANES))))
        out_specs.append(pl.BlockSpec((1, tm, IN_GATE_TN), lambda bi, i, j: (bi, i, gate_j(j))))
        out_shape.append(jax.ShapeDtypeStruct((b, l, GATE_W), BF16))
        args.append(w_in)
        names.append("gates")
    if with_rope:
        qw = SW_HEADS * SW_HEAD_DIM
        in_specs += [pl.BlockSpec((tm, qw), lambda bi, i, j: (i, 0))] * 2
        out_specs += [pl.BlockSpec((1, tm, qw), lambda bi, i, j: (bi, i, 0))] * 2
        out_shape += [jax.ShapeDtypeStruct((b, l, qw), BF16)] * 2
        args += list(_rope_tables(l, SW_HEAD_DIM, SW_HEADS))
        names += ["q_rope", "kv_rope"]
    scratch = [pltpu.VMEM((BRANCH_W // LANES, tm, LANES), F32), pltpu.VMEM((tm, d), BF16)]
    if with_gates and with_rope:
        scratch.insert(0, pltpu.VMEM((tm, 2 * SW_HEADS * SW_HEAD_DIM), F32))
    outs = pl.pallas_call(
        functools.partial(_in_proj_kernel, n_mix=n_mix, with_gates=with_gates, with_rope=with_rope),
        grid=(b, l // tm, n_mix + n_gate),
        in_specs=in_specs,
        out_specs=out_specs,
        out_shape=out_shape,
        scratch_shapes=scratch,
        compiler_params=_cparams(("parallel", "parallel", "arbitrary")), name="in_proj",
    )(*args)
    return dict(zip(names, outs))


def _s5_inc_kernel(u_ref, q_ref, o_ref):
    o_ref[...] = jnp.dot(u_ref[...].astype(BF16), q_ref[...], preferred_element_type=F32)


def _s5_increments(uu, q_all, *, tm, tn):
    m, k = uu.shape
    n = q_all.shape[1]
    return pl.pallas_call(
        _s5_inc_kernel,
        grid=(m // tm, n // tn),
        in_specs=[pl.BlockSpec((tm, k), lambda i, j: (i, 0)), pl.BlockSpec((k, tn), lambda i, j: (0, j))],
        out_specs=pl.BlockSpec((tm, tn), lambda i, j: (i, j)),
        out_shape=jax.ShapeDtypeStruct((m, n), F32),
        compiler_params=_cparams(("parallel", "parallel")), name="s5_inc",
    )(uu, q_all)


def _s5_out_kernel(u_ref, h0, h1, h2, h3, m_ref, n0, n1, n2, n3, o_ref):
    nt = (((1,), (1,)), ((), ()))
    acc = jnp.dot(u_ref[...].astype(BF16), m_ref[...], preferred_element_type=F32)
    for h_ref, n_ref in ((h0, n0), (h1, n1), (h2, n2), (h3, n3)):
        acc += lax.dot_general(h_ref[...].astype(BF16), n_ref[...], nt, preferred_element_type=F32)
    o_ref[...] = acc


def _s5_outputs(uu, h, m_tot, n_t, *, tm, tn):
    m, k = uu.shape
    gp = h[0].shape[1]
    n = m_tot.shape[1]
    return pl.pallas_call(
        _s5_out_kernel,
        grid=(m // tm, n // tn),
        in_specs=[pl.BlockSpec((tm, k), lambda i, j: (i, 0))]
        + [pl.BlockSpec((tm, gp), lambda i, j: (i, 0))] * 4
        + [pl.BlockSpec((k, tn), lambda i, j: (0, j))]
        + [pl.BlockSpec((tn, gp), lambda i, j, c=c: (j, c)) for c in range(4)],
        out_specs=pl.BlockSpec((tm, tn), lambda i, j: (i, j)),
        out_shape=jax.ShapeDtypeStruct((m, n), F32),
        compiler_params=_cparams(("parallel", "parallel")), name="s5_out",
    )(uu, *h, m_tot, n_t, n_t, n_t, n_t)


def _s5_scan_kernel(gfr, gfi, gbr, gbi, afr, afi, abr, abi, h0fr, h0fi, h0br, h0bi,
                    hfr, hfi, hbr, hbi, efr, efi, ebr, ebi, *, n_chunks):
    a_fr = afr[...][None]
    a_fi = afi[...][None]
    a_br = abr[...][None]
    a_bi = abi[...][None]

    def body(k, carry):
        sfr, sfi, sbr, sbi = carry
        kb = n_chunks - 1 - k
        hfr[:, pl.ds(k, 1), :] = sfr
        hfi[:, pl.ds(k, 1), :] = sfi
        hbr[:, pl.ds(kb, 1), :] = sbr
        hbi[:, pl.ds(kb, 1), :] = sbi
        nfr = a_fr * sfr - a_fi * sfi + gfr[:, pl.ds(k, 1), :]
        nfi = a_fr * sfi + a_fi * sfr + gfi[:, pl.ds(k, 1), :]
        nbr = a_br * sbr - a_bi * sbi + gbr[:, pl.ds(kb, 1), :]
        nbi = a_br * sbi + a_bi * sbr + gbi[:, pl.ds(kb, 1), :]
        return nfr, nfi, nbr, nbi

    sfr, sfi, sbr, sbi = lax.fori_loop(0, n_chunks, body, (h0fr[...], h0fi[...], h0br[...], h0bi[...]), unroll=4)
    efr[...] = sfr
    efi[...] = sfi
    ebr[...] = sbr
    ebi[...] = sbi


def _s5_scan(g, a_t, h0):
    b, k, w4 = g.shape
    w = 2 * LANES
    q = w4 // 4
    nb = q // w

    def comp(c):
        return pl.BlockSpec((b, k, w), lambda j, c=c: (0, 0, c * nb + j))

    def comp_a(c):
        return pl.BlockSpec((1, w), lambda j, c=c: (0, c * nb + j))

    state = pl.BlockSpec((b, k, w), lambda j: (0, 0, j))
    edge = pl.BlockSpec((b, 1, w), lambda j: (0, 0, j))
    outs = pl.pallas_call(
        functools.partial(_s5_scan_kernel, n_chunks=k),
        grid=(nb,),
        in_specs=[comp(c) for c in range(4)] + [comp_a(c) for c in range(4)] + [edge] * 4,
        out_specs=[state] * 4 + [edge] * 4,
        out_shape=[jax.ShapeDtypeStruct((b, k, q), F32)] * 4 + [jax.ShapeDtypeStruct((b, 1, q), F32)] * 4,
        compiler_params=_cparams(("parallel",)), name="s5_scan",
    )(g, g, g, g, a_t, a_t, a_t, a_t, *h0)
    return outs[:4], outs[4:]


def _cmul(ar, ai, br, bi):
    return ar * br - ai * bi, ar * bi + ai * br


def _s5_operators(lam_re, lam_im, log_dt, b_re, b_im, c_re, c_im, d):
    t_ = S5_T
    g_, p_, n_ = S5_GROUPS, S5_STATE, S5_GROUP
    gp, w_ = g_ * p_, g_ * n_
    lr, li = lam_re.astype(F32), lam_im.astype(F32)
    dt = jnp.exp(log_dt.astype(F32))[..., None]
    ks = jnp.arange(t_ + 1, dtype=F32)[:, None, None, None]
    mag = jnp.exp(ks * lr * dt)
    pw_re = mag * jnp.cos(ks * li * dt)
    pw_im = mag * jnp.sin(ks * li * dt)
    a_re, a_im = pw_re[1], pw_im[1]
    den = lr ** 2 + li ** 2
    f_re = ((a_re - 1.0) * lr + a_im * li) / den
    f_im = (a_im * lr - (a_re - 1.0) * li) / den
    br, bi = b_re.astype(F32), b_im.astype(F32)
    bb_re = f_re[..., None] * br - f_im[..., None] * bi
    bb_im = f_re[..., None] * bi + f_im[..., None] * br
    cr, ci = c_re.astype(F32), c_im.astype(F32)
    wb_re, wb_im = _cmul(pw_re[:t_, ..., None], pw_im[:t_, ..., None], bb_re[None], bb_im[None])
    kern = jnp.einsum('dgnp,kdgpm->dkgnm', cr, wb_re, precision=HI) \
        - jnp.einsum('dgnp,kdgpm->dkgnm', ci, wb_im, precision=HI)
    k_tile = jnp.tile(kern.transpose(0, 1, 2, 4, 3).reshape(2, t_, w_, n_), (1, 1, 1, g_))

    def rows_tiled(x):
        return jnp.tile(x.transpose(0, 3, 1, 2).reshape(2, n_, gp), (1, g_, 1))

    bt_re, bt_im = rows_tiled(bb_re), rows_tiled(bb_im)
    ct_re, ct_im = rows_tiled(cr.transpose(0, 1, 3, 2)), rows_tiled(ci.transpose(0, 1, 3, 2))
    pr = pw_re.transpose(1, 0, 2, 3).reshape(2, t_ + 1, gp)
    pi_ = pw_im.transpose(1, 0, 2, 3).reshape(2, t_ + 1, gp)
    full = lambda s: pl.BlockSpec(s, lambda t: (0,) * len(s))
    m_tot, q_all, n_t = pl.pallas_call(
        _s5_ops_kernel,
        grid=(t_,),
        in_specs=[full((2, t_ + 1, gp))] * 2 + [full((2, w_, gp))] * 4 + [full((2, t_, w_, w_)), full((1, w_))],
        out_specs=[pl.BlockSpec((w_, t_ * w_), lambda t: (t, 0)),
                   pl.BlockSpec((w_, 4 * gp), lambda t: (t, 0)),
                   pl.BlockSpec((w_, 4 * gp), lambda t: (t, 0))],
        out_shape=[jax.ShapeDtypeStruct((t_ * w_, t_ * w_), BF16),
                   jax.ShapeDtypeStruct((t_ * w_, 4 * gp), BF16),
                   jax.ShapeDtypeStruct((t_ * w_, 4 * gp), BF16)],
        compiler_params=_cparams(("parallel",)), name="s5_ops",
    )(pr, pi_, bt_re, bt_im, ct_re, ct_im, k_tile, d.astype(F32)[None])
    a_t = jnp.concatenate([pr[0, t_], pi_[0, t_], pr[1, t_], pi_[1, t_]])[None]
    return m_tot, q_all, n_t, a_t


def _s5_ops_kernel(pr_ref, pi_ref, btr_ref, bti_ref, ctr_ref, cti_ref, kt_ref, d_ref, m_ref, q_ref, nt_ref):
    t = pl.program_id(0)
    t_ = S5_T
    w_, gp = btr_ref.shape[1], btr_ref.shape[2]
    row_g = lax.broadcasted_iota(jnp.int32, (w_, gp), 0) // S5_GROUP
    col_g = lax.broadcasted_iota(jnp.int32, (w_, gp), 1) // S5_STATE
    same = row_g == col_g
    for dr in range(2):
        e_q = (t_ - 1 - t) if dr == 0 else t
        e_n = (t + 1) if dr == 0 else (t_ - t)
        b_re = jnp.where(same, btr_ref[dr], 0.0)
        b_im = jnp.where(same, bti_ref[dr], 0.0)
        q_re, q_im = _cmul(b_re, b_im, pr_ref[dr, pl.ds(e_q, 1), :], pi_ref[dr, pl.ds(e_q, 1), :])
        c_re = jnp.where(same, ctr_ref[dr], 0.0)
        c_im = jnp.where(same, cti_ref[dr], 0.0)
        n_re, n_im = _cmul(c_re, c_im, pr_ref[dr, pl.ds(e_n, 1), :], pi_ref[dr, pl.ds(e_n, 1), :])
        base = 2 * dr * gp
        q_ref[:, base:base + gp] = q_re.astype(BF16)
        q_ref[:, base + gp:base + 2 * gp] = q_im.astype(BF16)
        nt_ref[:, base:base + gp] = n_re.astype(BF16)
        nt_ref[:, base + gp:base + 2 * gp] = (-n_im).astype(BF16)
    r2 = lax.broadcasted_iota(jnp.int32, (w_, w_), 0)
    c2 = lax.broadcasted_iota(jnp.int32, (w_, w_), 1)
    same2 = (r2 // S5_GROUP) == (c2 // S5_GROUP)
    skip = jnp.where(r2 == c2, d_ref[...], 0.0)
    for i in range(t_):
        k_f = kt_ref[0, jnp.maximum(i - t, 0)] * jnp.where(i >= t, 1.0, 0.0)
        k_b = kt_ref[1, jnp.maximum(t - i, 0)] * jnp.where(t >= i, 1.0, 0.0)
        blk = jnp.where(same2, k_f + k_b, 0.0) + skip * jnp.where(t == i, 1.0, 0.0)
        m_ref[:, i * w_:(i + 1) * w_] = blk.astype(BF16)


def _s5_mix(u, ops, h0):
    m_tot, q_all, n_t, a_t = ops
    b, k, tw = u.shape
    rows = b * k
    uu = u.reshape(rows, tw)
    tm = min(rows, 512)
    g = _s5_increments(uu, q_all, tm=tm, tn=2048)
    h, e = _s5_scan(g.reshape(b, k, S5_STATE_W), a_t, h0)
    y = _s5_outputs(uu, [hc.reshape(rows, -1) for hc in h], m_tot, n_t, tm=tm, tn=512)
    return y.reshape(b, k, tw), e


def _hy_filter_kernel(f_ref, w1_ref, b1_ref, w2_ref, b2_ref, w3_ref, fr_ref, dl_ref, o_ref, *, half_len, tile):
    feats = f_ref[...]
    h = jnp.dot(feats, w1_ref[...], preferred_element_type=F32, precision=HI) + b1_ref[...]
    h = jnp.sin(fr_ref[0:1, :] * h)
    h = jnp.dot(h, w2_ref[...], preferred_element_type=F32, precision=HI) + b2_ref[...]
    h = jnp.sin(fr_ref[1:2, :] * h)
    o = jnp.dot(h.astype(BF16), w3_ref[...].astype(BF16), preferred_element_type=F32)
    t = feats[:, 0:1]
    o = o * jnp.exp(-t * dl_ref[...])
    nw = HY_ORDER * HY_WIDTH
    n = pl.program_id(0) * tile + lax.broadcasted_iota(jnp.int32, (tile, 1), 0)
    k = jnp.where(n < half_len, o[:, :nw], jnp.where(n == half_len, 0.0, o[:, nw:]))
    o_ref[...] = k.reshape(o_ref.shape)


def _hy_filters(n_tokens, freq, w1, b1, w2, b2, w3):
    n2 = 2 * n_tokens
    idx = np.arange(n2, dtype=np.float64)
    pos = np.where(idx <= n_tokens, idx, n2 - idx)[:, None]
    t = pos / max(n_tokens - 1, 1)
    bands = np.linspace(1e-4, HY_BANDS - 1, HY_BANDS)[None]
    ang = 2.0 * math.pi * bands * pos / n_tokens
    feats = np.concatenate([t, np.cos(ang), -np.sin(ang)], axis=-1)
    kpad = LANES - HY_EMB
    feats = jnp.asarray(np.pad(feats, ((0, 0), (0, kpad))), F32)
    w1p = jnp.pad(w1.astype(F32), ((0, kpad), (0, 0)))
    deltas = jnp.abs(jnp.linspace(HY_MIN_DECAY, HY_MAX_DECAY, HY_WIDTH, dtype=F32))
    dl = jnp.tile(deltas, 2 * HY_ORDER)[None]
    tile = min(n2, 1024)
    nw = 2 * HY_ORDER * HY_WIDTH
    full = lambda s: pl.BlockSpec(s, lambda i: (0,) * len(s))
    return pl.pallas_call(
        functools.partial(_hy_filter_kernel, half_len=n_tokens, tile=tile),
        grid=(n2 // tile,),
        in_specs=[pl.BlockSpec((tile, LANES), lambda i: (i, 0)),
                  full((LANES, HY_FFN)), full((1, HY_FFN)), full((HY_FFN, HY_FFN)), full((1, HY_FFN)),
                  full((HY_FFN, nw)), full((2, HY_FFN)), full((1, nw))],
        out_specs=pl.BlockSpec((tile // SUBLANES, SUBLANES, HY_ORDER * HY_WIDTH), lambda i: (i, 0, 0)),
        out_shape=jax.ShapeDtypeStruct((n2 // SUBLANES, SUBLANES, HY_ORDER * HY_WIDTH), F32),
        compiler_params=_cparams(("parallel",)), name="hy_filter",
    )(feats, w1p, b1.astype(F32)[None], w2.astype(F32), b2.astype(F32)[None], w3.astype(F32),
      freq.astype(F32), dl)


FFT_G = 4
FFT_UNROLL = 8


def _fft_tables(n):
    s = n // SUBLANES
    nst = int(round(math.log2(s)))
    runs = []
    half = s // 2
    while half >= FFT_G:
        runs.append(-2 * np.pi * np.arange(half) / (2 * half))
        half //= 2
    ang_s = np.concatenate(runs)
    tw_slab = np.stack([np.cos(ang_s), np.sin(ang_s)]).astype(np.float32)
    tw_slab = np.broadcast_to(tw_slab[..., None, None], (2, ang_s.size, SUBLANES, LANES)).copy()
    pos = np.arange(s)
    rev = np.zeros(s, np.int64)
    for bit in range(nst):
        rev |= ((pos >> bit) & 1) << (nst - 1 - bit)
    ang = -2 * np.pi * (rev[:, None] * np.arange(SUBLANES)[None, :]) / n
    tw_mid = np.stack([np.cos(ang), np.sin(ang)]).astype(np.float32)
    tw_mid = np.broadcast_to(tw_mid[..., None], (2, s, SUBLANES, LANES)).copy()
    t = np.arange(n)
    ang2 = -2 * np.pi * t / (2 * n)
    mod = np.stack([np.cos(ang2), np.sin(ang2)]).astype(np.float32).reshape(2, s, SUBLANES)
    mod = np.broadcast_to(mod[..., None], (2, s, SUBLANES, LANES)).copy()
    return jnp.asarray(tw_slab), jnp.asarray(tw_mid), jnp.asarray(mod)


def _sub_patterns():
    sub = lax.broadcasted_iota(jnp.int32, (SUBLANES, LANES), 0)

    def table(vals):
        out = jnp.full((SUBLANES, LANES), vals[0], F32)
        for k in range(1, SUBLANES):
            out = jnp.where(sub == k, np.float32(vals[k]), out)
        return out

    pats = {}
    for dist in (4, 2, 1):
        lo = (sub & dist) == 0
        sgn = jnp.where(lo, 1.0, -1.0).astype(F32)
        wr = [1.0] * SUBLANES
        wi = [0.0] * SUBLANES
        for k in range(SUBLANES):
            if k & dist:
                e = (k % dist) * (SUBLANES // (2 * dist))
                wr[k] = math.cos(-2 * math.pi * e / SUBLANES)
                wi[k] = math.sin(-2 * math.pi * e / SUBLANES)
        pats[dist] = (lo, sgn, table(wr), table(wi))
    pats["quarter"] = (sub & 3) == 3
    return pats


def _dft8_fwd(vr, vi, pats):
    for dist in (4, 2, 1):
        lo, sgn, wr, wi = pats[dist]
        up_r = pltpu.roll(vr, SUBLANES - dist, 0)
        up_i = pltpu.roll(vi, SUBLANES - dist, 0)
        if dist == 4:
            pr, pi_ = up_r, up_i
        else:
            pr = jnp.where(lo, up_r, pltpu.roll(vr, dist, 0))
            pi_ = jnp.where(lo, up_i, pltpu.roll(vi, dist, 0))
        tr = pr + sgn * vr
        ti = pi_ + sgn * vi
        if dist == 1:
            vr, vi = tr, ti
        elif dist == 2:
            qt = pats["quarter"]
            vr, vi = jnp.where(qt, ti, tr), jnp.where(qt, -tr, ti)
        else:
            vr, vi = _cmul(tr, ti, wr, wi)
    return vr, vi


def _dft8_inv(vr, vi, pats):
    for dist in (1, 2, 4):
        lo, sgn, wr, wi = pats[dist]
        if dist == 2:
            qt = pats["quarter"]
            vr, vi = jnp.where(qt, -vi, vr), jnp.where(qt, vr, vi)
        elif dist == 4:
            vr, vi = _cmul(vr, vi, wr, -wi)
        up_r = pltpu.roll(vr, SUBLANES - dist, 0)
        up_i = pltpu.roll(vi, SUBLANES - dist, 0)
        if dist == 4:
            pr, pi_ = up_r, up_i
        else:
            pr = jnp.where(lo, up_r, pltpu.roll(vr, dist, 0))
            pi_ = jnp.where(lo, up_i, pltpu.roll(vi, dist, 0))
        vr = pr + sgn * vr
        vi = pi_ + sgn * vi
    return vr, vi


def _slab_stage(re, im, tw, half, slabs, inverse):
    per_block = half // FFT_G
    nblk = slabs // (2 * half)
    unroll = min(FFT_UNROLL, nblk * per_block)
    tw_off = slabs - 2 * half
    if nblk >= unroll:
        blocks_per_it = unroll // per_block
        trips = nblk // blocks_per_it
        offsets = [(b * 2 * half + jc * FFT_G, jc * FFT_G) for b in range(blocks_per_it) for jc in range(per_block)]
        data_step, tw_step = blocks_per_it * 2 * half, 0
    else:
        chunks_per_it = unroll // nblk
        trips = per_block // chunks_per_it
        offsets = [(b * 2 * half + k * FFT_G, k * FFT_G) for b in range(nblk) for k in range(chunks_per_it)]
        data_step = tw_step = chunks_per_it * FFT_G

    def body(c, carry):
        d0 = pl.multiple_of(c * data_step, FFT_G)
        t0 = pl.multiple_of(tw_off + c * tw_step, FFT_G)
        twiddles = {}
        for d_off, t_off in offsets:
            if t_off not in twiddles:
                tws = pl.ds(t0 + t_off, FFT_G)
                twiddles[t_off] = (tw[0, tws], tw[1, tws])
            wr, wi = twiddles[t_off]
            lo = pl.ds(d0 + d_off, FFT_G)
            hi = pl.ds(d0 + d_off + half, FFT_G)
            ar, ai, br, bi = re[lo], im[lo], re[hi], im[hi]
            if inverse:
                br, bi = br * wr + bi * wi, bi * wr - br * wi
                re[lo] = ar + br
                im[lo] = ai + bi
                re[hi] = ar - br
                im[hi] = ai - bi
            else:
                re[lo] = ar + br
                im[lo] = ai + bi
                dr, di = ar - br, ai - bi
                re[hi] = dr * wr - di * wi
                im[hi] = dr * wi + di * wr
        return carry

    lax.fori_loop(0, trips, body, 0)


def _fft_forward_big(re, im, tw, slabs):
    half = slabs // 2
    while half >= 4:
        _slab_stage(re, im, tw, half, slabs, inverse=False)
        half //= 2


def _fft_inverse_big(re, im, tw, slabs):
    half = 4
    while half <= slabs // 2:
        _slab_stage(re, im, tw, half, slabs, inverse=True)
        half *= 2


def _radix4_fwd(x):
    (x0r, x0i), (x1r, x1i), (x2r, x2i), (x3r, x3i) = x
    y0r, y0i = x0r + x2r, x0i + x2i
    y2r, y2i = x0r - x2r, x0i - x2i
    y1r, y1i = x1r + x3r, x1i + x3i
    dr, di = x1r - x3r, x1i - x3i
    y3r, y3i = di, -dr
    return [(y0r + y1r, y0i + y1i), (y0r - y1r, y0i - y1i), (y2r + y3r, y2i + y3i), (y2r - y3r, y2i - y3i)]


def _radix4_inv(z):
    (z0r, z0i), (z1r, z1i), (z2r, z2i), (z3r, z3i) = z
    y0r, y0i = z0r + z1r, z0i + z1i
    y1r, y1i = z0r - z1r, z0i - z1i
    y2r, y2i = z2r + z3r, z2i + z3i
    y3r, y3i = z2r - z3r, z2i - z3i
    qr, qi = -y3i, y3r
    return [(y0r + y2r, y0i + y2i), (y1r + qr, y1i + qi), (y0r - y2r, y0i - y2i), (y1r - qr, y1i - qi)]


def _fft_middle(re, im, twm_ref, slabs, pats, spec=None, out=None):
    def body(q, carry):
        p0 = q * FFT_G
        x = [(re[p0 + g], im[p0 + g]) for g in range(FFT_G)]
        z = _radix4_fwd(x)
        res = []
        for g in range(FFT_G):
            twr = twm_ref[0, p0 + g]
            twi = twm_ref[1, p0 + g]
            vr, vi = _cmul(z[g][0], z[g][1], twr, twi)
            vr, vi = _dft8_fwd(vr, vi, pats)
            if spec is None:
                out[0][0, 0, p0 + g] = vr * out[2]
                out[1][0, 0, p0 + g] = vi * out[2]
            else:
                vr, vi = _cmul(vr, vi, spec[0][0, 0, p0 + g], spec[1][0, 0, p0 + g])
                vr, vi = _dft8_inv(vr, vi, pats)
                res.append(_cmul(vr, vi, twr, -twi))
        if spec is not None:
            x = _radix4_inv(res)
            for g in range(FFT_G):
                re[p0 + g] = x[g][0]
                im[p0 + g] = x[g][1]
        return carry

    lax.fori_loop(0, slabs // FFT_G, body, 0)


def _hy_spec_kernel(tw_ref, k_ref, twm_ref, mod_ref, sr_ref, si_ref, re, im, *, slabs):
    h = pl.program_id(2)
    pats = _sub_patterns()
    a = k_ref[0:slabs]
    b = k_ref[slabs:2 * slabs]

    @pl.when(h == 0)
    def _():
        re[...] = a + b
        im[...] = jnp.zeros_like(a)

    @pl.when(h == 1)
    def _():
        dlt = a - b
        re[...] = dlt * mod_ref[0]
        im[...] = dlt * mod_ref[1]

    _fft_forward_big(re, im, tw_ref, slabs)
    scale = np.float32(1.0 / (2 * SUBLANES * slabs))
    _fft_middle(re, im, twm_ref, slabs, pats, out=(sr_ref, si_ref, scale))


def _hy_spectra(k, n_tokens):
    slabs = n_tokens // SUBLANES
    tw, twm, mod = _fft_tables(n_tokens)
    nt = HY_WIDTH // LANES
    shape = jax.ShapeDtypeStruct((HY_ORDER, 2, slabs, SUBLANES, HY_WIDTH), F32)
    spec_out = pl.BlockSpec((1, 1, slabs, SUBLANES, LANES), lambda o, j, h: (o, h, 0, 0, j))
    tab = pl.BlockSpec((2, slabs, SUBLANES, LANES), lambda o, j, h: (0, 0, 0, 0), pipeline_mode=pl.Buffered(1))
    return pl.pallas_call(
        functools.partial(_hy_spec_kernel, slabs=slabs),
        grid=(HY_ORDER, nt, 2),
        in_specs=[pl.BlockSpec(tw.shape, lambda o, j, h: (0, 0, 0, 0), pipeline_mode=pl.Buffered(1)),
                  pl.BlockSpec((2 * slabs, SUBLANES, LANES), lambda o, j, h: (0, 0, o * nt + j)),
                  tab, tab],
        out_specs=[spec_out, spec_out],
        out_shape=[shape, shape],
        scratch_shapes=[pltpu.VMEM((slabs, SUBLANES, LANES), F32)] * 2,
        compiler_params=_cparams(("parallel", "parallel", "arbitrary")), name="hy_spec",
    )(tw, k, twm, mod)


def _hy_conv_kernel(tw_ref, u_ref, g_ref, sr_ref, si_ref, twm_ref, mod_ref, b_ref, o_ref, re, im, *, slabs):
    h = pl.program_id(2)
    pats = _sub_patterns()
    slab_shape = (slabs, SUBLANES, LANES)
    n = slabs * SUBLANES

    @pl.when(h == 0)
    def _():
        re[...] = u_ref[0].reshape(slab_shape)
        im[...] = u_ref[1].reshape(slab_shape)

    @pl.when(h == 1)
    def _():
        ur, ui = u_ref[0].reshape(slab_shape), u_ref[1].reshape(slab_shape)
        mr, mi = mod_ref[0], mod_ref[1]
        re[...] = ur * mr - ui * mi
        im[...] = ur * mi + ui * mr

    _fft_forward_big(re, im, tw_ref, slabs)
    _fft_middle(re, im, twm_ref, slabs, pats, spec=(sr_ref, si_ref))
    _fft_inverse_big(re, im, tw_ref, slabs)

    @pl.when(h == 0)
    def _():
        o_ref[0] = re[...].reshape(n, LANES)
        o_ref[1] = im[...].reshape(n, LANES)

    @pl.when(h == 1)
    def _():
        yr, yi = re[...], im[...]
        mr, mi = mod_ref[0], mod_ref[1]
        y0 = o_ref[0] + (yr * mr + yi * mi).reshape(n, LANES)
        y1 = o_ref[1] + (yi * mr - yr * mi).reshape(n, LANES)
        o_ref[0] = g_ref[0].astype(F32) * (y0 + b_ref[...] * u_ref[0])
        o_ref[1] = g_ref[1].astype(F32) * (y1 + b_ref[...] * u_ref[1])


def _hy_order(u, u_blk, gate, gate_blk, spec_re, spec_im, order, bias):
    b, n, _ = u.shape
    w = HY_WIDTH
    slabs = n // SUBLANES
    tw, twm, mod = _fft_tables(n)
    nt = w // LANES
    tab = pl.BlockSpec((2, slabs, SUBLANES, LANES), lambda j, p, h: (0, 0, 0, 0), pipeline_mode=pl.Buffered(1))
    spec_in = pl.BlockSpec((1, 1, slabs, SUBLANES, LANES), lambda j, p, h: (order, h, 0, 0, j))

    def io(blk):
        return pl.BlockSpec((2, n, LANES), lambda j, p, h: (p, 0, blk * nt + j))

    return pl.pallas_call(
        functools.partial(_hy_conv_kernel, slabs=slabs),
        grid=(nt, b // 2, 2),
        in_specs=[pl.BlockSpec(tw.shape, lambda j, p, h: (0, 0, 0, 0), pipeline_mode=pl.Buffered(1)),
                  io(u_blk), io(gate_blk), spec_in, spec_in, tab, tab,
                  pl.BlockSpec((1, LANES), lambda j, p, h: (0, j))],
        out_specs=io(0),
        out_shape=jax.ShapeDtypeStruct((b, n, w), F32),
        scratch_shapes=[pltpu.VMEM((slabs, SUBLANES, LANES), F32)] * 2,
        compiler_params=_cparams(("parallel", "parallel", "arbitrary")), name="hy_conv",
    )(tw, u, gate, spec_re, spec_im, twm, mod, bias.astype(F32)[None])


HY_SHORT_ROWS = 64


def _hy_short_kernel(z_ref, w_ref, b_ref, v_ref, g_ref, *, n):
    j = pl.program_id(1)
    rb = min(HY_SHORT_ROWS, n)
    grp = 2 * SUBLANES
    row = lax.broadcasted_iota(jnp.int32, (rb, z_ref.shape[2]), 0)
    zero_row = jnp.zeros((1, z_ref.shape[2]), F32)
    w0, w1, w2, bias = w_ref[0:1, :], w_ref[1:2, :], w_ref[2:3, :], b_ref[...]
    for c in range(n // rb):
        r0 = c * rb
        x = z_ref[0, r0:r0 + rb].astype(F32)
        before = z_ref[0, r0 - grp:r0].astype(F32)[grp - 1:grp] if c > 0 else zero_row
        after = z_ref[0, r0 + rb:r0 + rb + grp].astype(F32)[0:1] if r0 + rb < n else zero_row
        prev = jnp.where(row == 0, before, pltpu.roll(x, 1, 0))
        nxt = jnp.where(row == rb - 1, after, pltpu.roll(x, rb - 1, 0))
        y = prev * w0 + x * w1 + nxt * w2 + bias

        @pl.when(j == 0)
        def _():
            v_ref[0, r0:r0 + rb] = y

        @pl.when(j > 0)
        def _():
            g_ref[0, r0:r0 + rb] = y.astype(g_ref.dtype)


def _hy_short(z_mix, conv_w, conv_b):
    b, n, _ = z_mix.shape
    w_ = HY_WIDTH
    w = conv_w.reshape(3, 3 * w_).astype(F32)
    col0 = BRANCH_W // w_
    return pl.pallas_call(
        functools.partial(_hy_short_kernel, n=n),
        grid=(b, 3),
        in_specs=[pl.BlockSpec((1, n, w_), lambda bi, j: (bi, 0, col0 + j)),
                  pl.BlockSpec((3, w_), lambda bi, j: (0, j)),
                  pl.BlockSpec((1, w_), lambda bi, j: (0, j))],
        out_specs=[pl.BlockSpec((1, n, w_), lambda bi, j: (bi, 0, 0)),
                   pl.BlockSpec((1, n, w_), lambda bi, j: (bi, 0, jnp.maximum(j - 1, 0)))],
        out_shape=[jax.ShapeDtypeStruct((b, n, w_), F32), jax.ShapeDtypeStruct((b, n, 2 * w_), BF16)],
        compiler_params=_cparams(("parallel", "arbitrary")), name="hy_short",
    )(z_mix, w, conv_b.astype(F32)[None])


def _hyena(z_mix, conv_w, conv_b, spec_re, spec_im, bias):
    v, gates = _hy_short(z_mix, conv_w, conv_b)
    v1 = _hy_order(v, 0, gates, 0, spec_re, spec_im, 0, bias[0])
    return _hy_order(v1, 0, gates, 1, spec_re, spec_im, 1, bias[1])


NA_QROWS = 4
NA_KROWS = NA_QROWS + NA_WIN_H
NA_KPART = 256


def _softmax_parts(parts):
    m = None
    for s in parts:
        mm = jnp.max(s, axis=-1, keepdims=True)
        m = mm if m is None else jnp.maximum(m, mm)
    return m


def _na_kernel(*refs, kparts):
    q_ref, k_refs, v_refs = refs[0], refs[1:1 + kparts], refs[1 + kparts:1 + 2 * kparts]
    kc_ref, vc_ref, bias_ref, o_ref = refs[1 + 2 * kparts:]
    q = q_ref[0] * jnp.asarray(NA_HEAD_DIM ** -0.5, q_ref.dtype)
    k = jnp.concatenate([r[0] for r in k_refs], axis=0).astype(BF16)
    v = jnp.concatenate([r[0] for r in v_refs], axis=0).astype(BF16)
    kc = kc_ref[0].astype(BF16)
    vc = vc_ref[0].astype(BF16)
    nt = (((1,), (1,)), ((), ()))
    outs = []
    for h in range(NA_HEADS):
        sl = slice(h * NA_HEAD_DIM, (h + 1) * NA_HEAD_DIM)
        qh = q[:, sl].astype(BF16)
        s_lat = lax.dot_general(qh, k[:, sl], nt, preferred_element_type=F32) + bias_ref[0, h]
        s_ctx = lax.dot_general(qh, kc[:, sl], nt, preferred_element_type=F32)
        m = _softmax_parts([s_lat, s_ctx])
        p_lat = jnp.exp(s_lat - m)
        p_ctx = jnp.exp(s_ctx - m)
        den = jnp.sum(p_lat, axis=-1, keepdims=True) + jnp.sum(p_ctx, axis=-1, keepdims=True)
        o = jnp.dot(p_lat.astype(BF16), v[:, sl], preferred_element_type=F32) \
            + jnp.dot(p_ctx.astype(BF16), vc[:, sl], preferred_element_type=F32)
        outs.append(o / den)
    o_ref[0] = jnp.concatenate(outs, axis=-1)


def _na_bias(rpb, rows):
    kh = min(NA_WIN_H, rows)
    col = np.arange(GRID_W)
    col_start = np.clip(col - NA_WIN_W // 2, 0, GRID_W - NA_WIN_W)
    col_ok = (col[None] >= col_start[:, None]) & (col[None] < col_start[:, None] + NA_WIN_W)
    off_c = np.clip(col[None] - col[:, None], -(NA_WIN_W - 1), NA_WIN_W - 1) + (NA_WIN_W - 1)
    nblk = rows // NA_QROWS
    n_r, n_c = 2 * NA_WIN_H - 1, 2 * NA_WIN_W - 1
    table = jnp.pad(rpb.astype(F32), ((0, 0), (0, 1), (0, 1)), constant_values=NEG_INF)
    sel_c = np.eye(n_c + 1, dtype=np.float32)[np.where(col_ok, off_c, n_c)]
    blocks = jnp.einsum('hab,qcb->hqac', table, sel_c, precision=HI).reshape(NA_HEADS, GRID_W, (n_r + 1) * GRID_W)
    blocks_odd = jnp.roll(blocks, -GRID_W, axis=-1)
    row_off = []
    for j in (0, 1, nblk - 1):
        qr = j * NA_QROWS + np.arange(NA_QROWS)
        ws = int(np.clip(j * NA_QROWS - NA_WIN_H // 2, 0, rows - NA_KROWS))
        kr = ws + np.arange(NA_KROWS)
        start = np.clip(qr - kh // 2, 0, rows - kh)
        row_ok = (kr[None] >= start[:, None]) & (kr[None] < start[:, None] + kh)
        off_r = kr[None] - qr[:, None] + (NA_WIN_H - 1)
        row_off.append(np.where(row_ok, off_r, n_r))
    row_off = np.stack(row_off)
    wide = (n_r + 1) * GRID_W
    return pl.pallas_call(
        functools.partial(_na_bias_kernel, row_off=row_off),
        grid=(NA_HEADS,),
        in_specs=[pl.BlockSpec((1, GRID_W, wide), lambda h: (h, 0, 0))] * 2,
        out_specs=pl.BlockSpec((3, 1, NA_QROWS * GRID_W, NA_KROWS * GRID_W), lambda h: (0, h, 0, 0)),
        out_shape=jax.ShapeDtypeStruct((3, NA_HEADS, NA_QROWS * GRID_W, NA_KROWS * GRID_W), F32),
        compiler_params=_cparams(("parallel",)), name="na_bias",
    )(blocks, blocks_odd)


def _na_bias_kernel(even_ref, odd_ref, o_ref, *, row_off):
    n_cls, n_q, n_k = row_off.shape
    masked = 2 * NA_WIN_H - 1
    pair = 2 * GRID_W
    lane = lax.broadcasted_iota(jnp.int32, (GRID_W, pair), 1)

    def left(a):
        src, start = (even_ref, a) if a % 2 == 0 else (odd_ref, a - 1)
        return src[0, :, start * GRID_W:start * GRID_W + pair]

    def right(a):
        assert a % 2 == 1
        return even_ref[0, :, (a - 1) * GRID_W:(a - 1) * GRID_W + pair]

    for t in range(n_cls):
        for r in range(n_q):
            for kp in range(n_k // 2):
                a0, a1 = int(row_off[t, r, 2 * kp]), int(row_off[t, r, 2 * kp + 1])
                if a0 != masked and a1 == a0 + 1:
                    blk = left(a0)
                elif a0 == masked and a1 == masked:
                    blk = jnp.full((GRID_W, pair), NEG_INF, F32)
                elif a1 == masked:
                    blk = jnp.where(lane < GRID_W, left(a0), NEG_INF)
                else:
                    blk = jnp.where(lane >= GRID_W, right(a1), NEG_INF)
                o_ref[t, 0, r * GRID_W:(r + 1) * GRID_W, kp * pair:(kp + 1) * pair] = blk


def _na_attention(z_mix, zc_mix, rpb):
    b, n, _ = z_mix.shape
    n_ctx = zc_mix.shape[1]
    rows = n // GRID_W
    nblk = rows // NA_QROWS
    tq = NA_QROWS * GRID_W
    bias = _na_bias(rpb, rows)
    qb, kb, vb = NA_COL0 // BRANCH_W, NA_COL0 // BRANCH_W + 1, NA_COL0 // BRANCH_W + 2
    kparts = NA_KROWS * GRID_W // NA_KPART
    max_k0 = (rows - NA_KROWS) * GRID_W // NA_KPART

    def k0_of(j):
        per_step = NA_QROWS * GRID_W // NA_KPART
        lead = (NA_WIN_H // 2) * GRID_W // NA_KPART
        return jnp.clip(j * per_step - lead, 0, max_k0)

    def kspec(part, blk):
        return pl.BlockSpec((1, NA_KPART, BRANCH_W), lambda bi, j: (bi, k0_of(j) + part, blk))

    def cls(j):
        return jnp.where(j == 0, 0, jnp.where(j == nblk - 1, 2, 1))

    return pl.pallas_call(
        functools.partial(_na_kernel, kparts=kparts),
        grid=(b, nblk),
        in_specs=[pl.BlockSpec((1, tq, BRANCH_W), lambda bi, j: (bi, j, qb))]
        + [kspec(p, kb) for p in range(kparts)] + [kspec(p, vb) for p in range(kparts)]
        + [pl.BlockSpec((1, n_ctx, BRANCH_W), lambda bi, j: (bi, 0, kb)),
           pl.BlockSpec((1, n_ctx, BRANCH_W), lambda bi, j: (bi, 0, vb)),
           pl.BlockSpec((1, NA_HEADS, tq, NA_KROWS * GRID_W), lambda bi, j: (cls(j), 0, 0, 0))],
        out_specs=pl.BlockSpec((1, tq, BRANCH_W), lambda bi, j: (bi, j, 0)),
        out_shape=jax.ShapeDtypeStruct((b, n, BRANCH_W), F32),
        compiler_params=_cparams(("parallel", "arbitrary")), name="na_attn",
    )(z_mix, *([z_mix] * (2 * kparts)), zc_mix, zc_mix, bias)


def _rope_tables(n_tokens, head_dim, heads):
    t = np.arange(n_tokens)
    row = (t // GRID_W).astype(np.float64)
    col = (t % GRID_W).astype(np.float64)
    half = head_dim // 2
    inv = ROPE_BASE ** (-(np.arange(0, half, 2, dtype=np.float64) / half))
    ang = np.concatenate([row[:, None] * inv, col[:, None] * inv], axis=-1)
    cos, sin = np.cos(ang), np.sin(ang)
    cos_t = np.tile(np.concatenate([cos, cos], axis=-1), (1, heads))
    sin_t = np.tile(np.concatenate([-sin, sin], axis=-1), (1, heads))
    return jnp.asarray(cos_t, F32), jnp.asarray(sin_t, F32)


def _rope(x, cos_t, sin_t):
    w = x.shape[-1]
    half = SW_HEAD_DIM // 2
    lane = lax.broadcasted_iota(jnp.int32, x.shape, 1)
    first = (lane % SW_HEAD_DIM) < half
    partner = jnp.where(first, pltpu.roll(x, w - half, 1), pltpu.roll(x, half, 1))
    return x * cos_t + partner * sin_t


SW_QBLK = 2 * SW_BLOCK
SW_KPARTS = SW_QBLK // SW_BLOCK + 2


def _sw_kernel(q_ref, kv0, kv1, kv2, kv3, kc_ref, vc_ref, sink_ref, o_ref, *, nblk):
    i = pl.program_id(1)
    kvw = SW_KV_HEADS * SW_HEAD_DIM
    q = q_ref[0]
    kv = jnp.concatenate([kv0[0], kv1[0], kv2[0], kv3[0]], axis=0)
    kc = kc_ref[0].astype(BF16)
    vc = vc_ref[0].astype(BF16)
    g_ = SW_HEADS // SW_KV_HEADS
    rows = g_ * SW_QBLK
    span = SW_KPARTS * SW_BLOCK
    r = lax.broadcasted_iota(jnp.int32, (rows, span), 0) % SW_QBLK
    c = lax.broadcasted_iota(jnp.int32, (rows, span), 1)
    diff = c - SW_BLOCK - r
    blk = i * (SW_QBLK // SW_BLOCK) - 1 + c // SW_BLOCK
    ok = (jnp.abs(diff) <= SW_WINDOW) & (blk >= 0) & (blk < nblk)
    nt = (((1,), (1,)), ((), ()))
    outs = []
    for kvh in range(SW_KV_HEADS):
        ksl = slice(kvh * SW_HEAD_DIM, (kvh + 1) * SW_HEAD_DIM)
        vsl = slice(kvw + kvh * SW_HEAD_DIM, kvw + (kvh + 1) * SW_HEAD_DIM)
        qg = jnp.concatenate([q[:, (kvh * g_ + g) * SW_HEAD_DIM:(kvh * g_ + g + 1) * SW_HEAD_DIM] for g in range(g_)],
                             axis=0)
        s_lat = jnp.where(ok, lax.dot_general(qg, kv[:, ksl], nt, preferred_element_type=F32), NEG_INF)
        s_ctx = lax.dot_general(qg, kc[:, ksl], nt, preferred_element_type=F32)
        rr = lax.broadcasted_iota(jnp.int32, (rows, 1), 0)
        s_sink = jnp.zeros((rows, 1), F32)
        for g in range(g_):
            s_sink = jnp.where(rr // SW_QBLK == g, sink_ref[kvh * g_ + g], s_sink)
        m = jnp.maximum(_softmax_parts([s_lat, s_ctx]), s_sink)
        p_lat = jnp.exp(s_lat - m)
        p_ctx = jnp.exp(s_ctx - m)
        den = jnp.sum(p_lat, axis=-1, keepdims=True) + jnp.sum(p_ctx, axis=-1, keepdims=True) + jnp.exp(s_sink - m)
        o = jnp.dot(p_lat.astype(BF16), kv[:, vsl], preferred_element_type=F32) \
            + jnp.dot(p_ctx.astype(BF16), vc[:, ksl], preferred_element_type=F32)
        o = o / den
        outs += [o[g * SW_QBLK:(g + 1) * SW_QBLK] for g in range(g_)]
    o_ref[0] = jnp.concatenate(outs, axis=-1)


def _sw_attention(q_r, kv_r, zc_mix, sink):
    b, n, _ = q_r.shape
    n_ctx = zc_mix.shape[1]
    nblk = n // SW_BLOCK
    qw = SW_HEADS * SW_HEAD_DIM
    kvw = SW_KV_HEADS * SW_HEAD_DIM
    k_blk = (SW_COL0 + qw) // kvw
    per_q = SW_QBLK // SW_BLOCK

    def kpart(part):
        return pl.BlockSpec((1, SW_BLOCK, qw),
                            lambda bi, i: (bi, jnp.clip(i * per_q - 1 + part, 0, nblk - 1), 0))

    return pl.pallas_call(
        functools.partial(_sw_kernel, nblk=nblk),
        grid=(b, n // SW_QBLK),
        in_specs=[pl.BlockSpec((1, SW_QBLK, qw), lambda bi, i: (bi, i, 0))]
        + [kpart(p) for p in range(SW_KPARTS)]
        + [pl.BlockSpec((1, n_ctx, kvw), lambda bi, i: (bi, 0, k_blk)),
           pl.BlockSpec((1, n_ctx, kvw), lambda bi, i: (bi, 0, k_blk + 1)),
           pl.BlockSpec(memory_space=pltpu.SMEM)],
        out_specs=pl.BlockSpec((1, SW_QBLK, qw), lambda bi, i: (bi, i, 0)),
        out_shape=jax.ShapeDtypeStruct((b, n, qw), F32),
        compiler_params=_cparams(("parallel", "arbitrary")), name="sw_attn",
    )(q_r, *([kv_r] * SW_KPARTS), zc_mix, zc_mix, sink.astype(F32))


def _ctx_attn_kernel(q_ref, k_ref, v_ref, sink_ref, o_ref, *, heads, kv_heads, dh, use_sink):
    q = q_ref[0] * np.float32(dh ** -0.5)
    k = k_ref[0].astype(BF16)
    v = v_ref[0].astype(BF16)
    g_ = heads // kv_heads
    nt = (((1,), (1,)), ((), ()))
    outs = []
    for h in range(heads):
        kv = h // g_
        s = lax.dot_general(q[:, h * dh:(h + 1) * dh].astype(BF16), k[:, kv * dh:(kv + 1) * dh], nt,
                            preferred_element_type=F32)
        m = jnp.max(s, axis=-1, keepdims=True)
        if use_sink:
            m = jnp.maximum(m, sink_ref[h])
        p = jnp.exp(s - m)
        den = jnp.sum(p, axis=-1, keepdims=True)
        if use_sink:
            den = den + jnp.exp(sink_ref[h] - m)
        outs.append(jnp.dot(p.astype(BF16), v[:, kv * dh:(kv + 1) * dh], preferred_element_type=F32) / den)
    o_ref[0] = jnp.concatenate(outs, axis=-1)


def _ctx_attention(zc_mix, col0, heads, kv_heads, dh, sink):
    b, n, _ = zc_mix.shape
    qw, kvw = heads * dh, kv_heads * dh
    use_sink = sink is not None
    sink_arr = sink.astype(F32) if use_sink else jnp.zeros((heads,), F32)
    return pl.pallas_call(
        functools.partial(_ctx_attn_kernel, heads=heads, kv_heads=kv_heads, dh=dh, use_sink=use_sink),
        grid=(b,),
        in_specs=[pl.BlockSpec((1, n, qw), lambda bi: (bi, 0, col0 // qw)),
                  pl.BlockSpec((1, n, kvw), lambda bi: (bi, 0, (col0 + qw) // kvw)),
                  pl.BlockSpec((1, n, kvw), lambda bi: (bi, 0, (col0 + qw) // kvw + 1)),
                  pl.BlockSpec(memory_space=pltpu.SMEM)],
        out_specs=pl.BlockSpec((1, n, qw), lambda bi: (bi, 0, 0)),
        out_shape=jax.ShapeDtypeStruct((b, n, qw), F32),
        compiler_params=_cparams(("parallel",)), name="ctx_attn",
    )(zc_mix, zc_mix, zc_mix, sink_arr)


def _merge_kernel(x_ref, s5_ref, hy_ref, na_ref, sw_ref, gt_ref, ga_ref, wglu_ref, wb_ref, wo_ref,
                  lg_ref, lb_ref, o_ref, tok_scr):
    rows = tok_scr.shape[1] // S5_T
    for c in range(BRANCH_W // LANES):
        for t in range(S5_T):
            col = t * BRANCH_W + c * LANES
            tok_scr[c, pl.ds(t, rows, stride=S5_T), :] = s5_ref[0, :, col:col + LANES]
    part_rows = x_ref.shape[1] // MLP_ROW_PARTS
    for part in range(MLP_ROW_PARTS):
        sl = slice(part * part_rows, (part + 1) * part_rows)
        g = jax.nn.gelu(jnp.concatenate([tok_scr[c, sl] for c in range(BRANCH_W // LANES)], axis=-1))
        s5 = g * jax.nn.sigmoid(jnp.dot(g.astype(BF16), wglu_ref[...], preferred_element_type=F32))
        branches = (s5, hy_ref[0, sl], na_ref[0, sl], sw_ref[0, sl])
        acc = None
        for n in range(N_BRANCH):
            proj = jnp.dot(branches[n].astype(BF16), wb_ref[n], preferred_element_type=F32)
            t = gt_ref[0, sl, n * D_MODEL:(n + 1) * D_MODEL].astype(F32) * proj
            acc = t if acc is None else acc + t
        mix = jnp.dot(acc.astype(BF16), wo_ref[...], preferred_element_type=F32)
        y = np.float32(DEEPNORM_ALPHA) * x_ref[0, sl] + ga_ref[0] * mix
        o_ref[0, sl] = _layernorm(y) * lg_ref[...] + lb_ref[...]


def _merge(x, s5y, hy, na, sw, gates, g_a, w_glu, w_branch, w_out, ln_g, ln_b, *, tm):
    b, l, d = x.shape
    br = pl.BlockSpec((1, tm, BRANCH_W), lambda bi, i: (bi, i, 0))
    full = lambda s: pl.BlockSpec(s, lambda bi, i: (0,) * len(s), pipeline_mode=pl.Buffered(1))
    return pl.pallas_call(
        _merge_kernel,
        grid=(b, l // tm),
        in_specs=[pl.BlockSpec((1, tm, d), lambda bi, i: (bi, i, 0)),
                  pl.BlockSpec((1, tm // S5_T, S5_T * BRANCH_W), lambda bi, i: (bi, i, 0)), br, br, br,
                  pl.BlockSpec((1, tm, GATE_W), lambda bi, i: (bi, i, 0)),
                  pl.BlockSpec((1, 1, d), lambda bi, i: (bi, 0, 0)),
                  full((BRANCH_W, BRANCH_W)), full((N_BRANCH, BRANCH_W, d)), full((d, d)),
                  full((1, d)), full((1, d))],
        out_specs=pl.BlockSpec((1, tm, d), lambda bi, i: (bi, i, 0)),
        out_shape=jax.ShapeDtypeStruct((b, l, d), F32),
        scratch_shapes=[pltpu.VMEM((BRANCH_W // LANES, tm, LANES), F32)],
        compiler_params=_cparams(("parallel", "parallel")), name="merge",
    )(x, s5y, hy, na, sw, gates, g_a, w_glu, w_branch, w_out, ln_g, ln_b)


def _mlp_kernel(x_ref, sh_ref, sc_ref, gm_ref, w1_ref, w2_ref, lg_ref, lb_ref, o_ref, *, th):
    tm = x_ref.shape[1]
    hdim = w1_ref.shape[1]
    rows = tm // MLP_ROW_PARTS
    for part in range(MLP_ROW_PARTS):
        sl = slice(part * rows, (part + 1) * rows)
        xs = x_ref[0, sl]
        h = (_layernorm(xs) * (1.0 + sc_ref[0]) + sh_ref[0]).astype(BF16)
        acc = None
        for c in range(hdim // th):
            a = jnp.dot(h, w1_ref[:, c * th:(c + 1) * th], preferred_element_type=F32)
            a = jnp.square(jnp.maximum(a, 0.0))
            t = jnp.dot(a.astype(BF16), w2_ref[c * th:(c + 1) * th, :], preferred_element_type=F32)
            acc = t if acc is None else acc + t
        y = np.float32(DEEPNORM_ALPHA) * xs + gm_ref[0] * acc
        o_ref[0, sl] = _layernorm(y) * lg_ref[...] + lb_ref[...]


def _mlp(x, sh, sc, g_m, w1, w2, ln_g, ln_b, *, tm, th):
    b, l, d = x.shape
    hdim = w1.shape[1]
    mod = pl.BlockSpec((1, 1, d), lambda bi, i: (bi, 0, 0))
    once = lambda s: pl.BlockSpec(s, lambda bi, i: (0,) * len(s), pipeline_mode=pl.Buffered(1))
    return pl.pallas_call(
        functools.partial(_mlp_kernel, th=th),
        grid=(b, l // tm),
        in_specs=[pl.BlockSpec((1, tm, d), lambda bi, i: (bi, i, 0)), mod, mod, mod,
                  once((d, hdim)), once((hdim, d)), once((1, d)), once((1, d))],
        out_specs=pl.BlockSpec((1, tm, d), lambda bi, i: (bi, i, 0)),
        out_shape=jax.ShapeDtypeStruct((b, l, d), F32),
        compiler_params=_cparams(("parallel", "parallel")), name="mlp",
    )(x, sh, sc, g_m, w1, w2, ln_g, ln_b)


def kernel(x, c, ctx, c_ctx, w_ada, b_ada, w_in, s5_lambda_re, s5_lambda_im, s5_log_dt, s5_b_re, s5_b_im, s5_c_re,
           s5_c_im, s5_d, s5_w_glu, hy_conv_w, hy_conv_b, hy_freq, hy_w1, hy_b1, hy_w2, hy_b2, hy_w3, hy_bias,
           na_rpb, sw_sink, w_branch, w_out, ln1_g, ln1_b, w_mlp1, w_mlp2, ln2_g, ln2_b):
    b, l, d = x.shape
    n_ctx = ctx.shape[1]
    depth = w_ada.shape[0]
    cc = jnp.zeros((8, d), F32).at[:b].set(c.astype(F32)).at[b].set(c_ctx.astype(F32))
    mod_all = _ada(cc, w_ada.astype(F32), b_ada.astype(F32))
    xc = ctx
    for layer in range(depth):
        need_ctx_out = layer < depth - 1
        mod = mod_all[layer, :b].reshape(b, 1, 6, d)
        mod_c = mod_all[layer, b].reshape(1, 1, 6, d)
        sh_a, sc_a, g_a, sh_m, sc_m, g_m = [mod[:, :, i] for i in range(6)]
        csh_a, csc_a, cg_a, csh_m, csc_m, cg_m = [mod_c[:, :, i] for i in range(6)]
        w_in_l = w_in[layer].astype(BF16)

        lat = _in_proj(x, sh_a, sc_a, w_in_l, True, tm=1024, with_rope=True)
        z_mix, gates = lat["z"], lat["gates"]
        flat = lambda a: a.reshape(1, b * a.shape[1], a.shape[2])
        unflat = lambda a: a.reshape(b, a.shape[1] // b, a.shape[2])
        con = _in_proj(flat(xc), csh_a, csc_a, w_in_l, need_ctx_out, tm=b * n_ctx)
        zc_mix = unflat(con["z"])

        ops = _s5_operators(s5_lambda_re[layer], s5_lambda_im[layer], s5_log_dt[layer], s5_b_re[layer],
                            s5_b_im[layer], s5_c_re[layer], s5_c_im[layer], s5_d[layer])
        yc_s5, e_ctx = _s5_mix(unflat(con["u"]), ops, [jnp.zeros((b, 1, S5_STATE_W // 4), F32)] * 4)
        y_s5, _ = _s5_mix(lat["u"], ops, e_ctx)

        hy_args = (hy_freq[layer], hy_w1[layer], hy_b1[layer], hy_w2[layer], hy_b2[layer], hy_w3[layer])
        sp_re, sp_im = _hy_spectra(_hy_filters(l, *hy_args), l)
        hy_l = _hyena(z_mix, hy_conv_w[layer], hy_conv_b[layer], sp_re, sp_im, hy_bias[layer])

        na_l = _na_attention(z_mix, zc_mix, na_rpb[layer])
        sw_l = _sw_attention(lat["q_rope"], lat["kv_rope"], zc_mix, sw_sink[layer])

        w_glu = s5_w_glu[layer].astype(BF16)
        w_br = w_branch[layer].astype(BF16)
        w_o = w_out[layer].astype(BF16)
        lg1, lb1 = ln1_g[layer].astype(F32)[None], ln1_b[layer].astype(F32)[None]
        lg2, lb2 = ln2_g[layer].astype(F32)[None], ln2_b[layer].astype(F32)[None]
        w1 = w_mlp1[layer].astype(BF16)
        w2 = w_mlp2[layer].astype(BF16)

        x_new = _merge(x, y_s5, hy_l, na_l, sw_l, gates, g_a, w_glu, w_br, w_o, lg1, lb1, tm=1024)
        x_new = _mlp(x_new, sh_m, sc_m, g_m, w1, w2, lg2, lb2, tm=1024, th=2048)

        if need_ctx_out:
            spc_re, spc_im = _hy_spectra(_hy_filters(n_ctx, *hy_args), n_ctx)
            hy_c = _hyena(zc_mix, hy_conv_w[layer], hy_conv_b[layer], spc_re, spc_im, hy_bias[layer])
            na_c = _ctx_attention(zc_mix, NA_COL0, NA_HEADS, NA_HEADS, NA_HEAD_DIM, None)
            sw_c = _ctx_attention(zc_mix, SW_COL0, SW_HEADS, SW_KV_HEADS, SW_HEAD_DIM, sw_sink[layer])
            xc_new = _merge(flat(xc), flat(yc_s5), flat(hy_c), flat(na_c), flat(sw_c), con["gates"], cg_a,
                            w_glu, w_br, w_o, lg1, lb1, tm=512)
            xc = unflat(_mlp(xc_new, csh_m, csc_m, cg_m, w1, w2, lg2, lb2, tm=b * n_ctx, th=1024))
        x = x_new
    return x
```
